```python
import math
import jax, jax.numpy as jnp
from jax import lax
import numpy as np

D_MODEL = 1024
BATCH = 16
SEQ = 2048
DEPTH = 2

DEEPNORM_ALPHA = (2 * DEPTH) ** 0.25
DEEPNORM_BETA = (8 * DEPTH) ** -0.25
LN_EPS = 1e-5
RMS_EPS = 1e-6
ADA_INIT = 0.1

CONV_CH = D_MODEL // 2
CONV_WIDTH = 31

GLA_HEADS = 4
GLA_V_WIDTH = D_MODEL // 2
GLA_K_WIDTH = GLA_V_WIDTH // 2
GLA_HEAD_K = GLA_K_WIDTH // GLA_HEADS
GLA_HEAD_V = GLA_V_WIDTH // GLA_HEADS
GLA_GATE_RANK = 16
GLA_GATE_TAU = 16.0
GLA_CHUNK = 64

AB_SPLITS = [2 * CONV_CH,
             2 * CONV_CH + GLA_K_WIDTH,
             2 * CONV_CH + 2 * GLA_K_WIDTH,
             2 * CONV_CH + 2 * GLA_K_WIDTH + GLA_V_WIDTH,
             2 * CONV_CH + 2 * GLA_K_WIDTH + 2 * GLA_V_WIDTH]
AB_IN = AB_SPLITS[-1] + GLA_GATE_RANK
AB_OUT = CONV_CH + GLA_V_WIDTH

MLA_HEADS = 8
MLA_NOPE = 128
MLA_ROPE = 64
MLA_V = 128
MLA_Q_LORA = 384
MLA_KV_LORA = 256
MLA_IN = MLA_Q_LORA + MLA_KV_LORA + MLA_ROPE
MLA_SCALE = (MLA_NOPE + MLA_ROPE) ** -0.5
ROPE_THETA = 10000.0
Q_BLOCK = 128

MOE_GROUPS = 4
MOE_EXPERTS_PER_GROUP = 8
MOE_EXPERTS = MOE_GROUPS * MOE_EXPERTS_PER_GROUP
MOE_TOPK = 2
MOE_FF = 256

kernel_name = "hybrid_conv_gla_mla_hmoe_deepnorm"


def layer_norm(x, g, b):
    xf = x.astype(jnp.float32)
    mu = jnp.mean(xf, axis=-1, keepdims=True)
    var = jnp.mean(jnp.square(xf - mu), axis=-1, keepdims=True)
    return ((xf - mu) * lax.rsqrt(var + LN_EPS) * g + b).astype(x.dtype)


def rms_norm(x, g):
    xf = x.astype(jnp.float32)
    return (xf * lax.rsqrt(jnp.mean(jnp.square(xf), axis=-1, keepdims=True) + RMS_EPS) * g).astype(x.dtype)


def apply_rope(x, cos, sin):
    xf = x.astype(jnp.float32)
    x1, x2 = jnp.split(xf, 2, axis=-1)
    return jnp.concatenate([x1 * cos - x2 * sin, x1 * sin + x2 * cos], axis=-1).astype(x.dtype)


def conformer_conv(u, conv_w, conv_b, ln_g, ln_b):
    a, gate = jnp.split(u, 2, axis=-1)
    h = a * jax.nn.sigmoid(gate)
    h = lax.conv_general_dilated(
        h, conv_w[:, None, :], window_strides=(1,), padding=[(CONV_WIDTH - 1, 0)],
        dimension_numbers=('NWC', 'WIO', 'NWC'), feature_group_count=CONV_CH) + conv_b
    return jax.nn.silu(layer_norm(h, ln_g, ln_b))


def gla_chunked(q, k, v, glog):
    B, S, H, DK = q.shape
    DV = v.shape[-1]
    C = GLA_CHUNK
    NC = S // C

    def chunks(t):
        return t.astype(jnp.float32).reshape(B, NC, C, H, t.shape[-1]).transpose(1, 0, 3, 2, 4)

    qc = chunks(q) * (DK ** -0.5)
    kc = chunks(k)
    vc = chunks(v)
    bc = jnp.cumsum(chunks(glog), axis=3)
    causal = jnp.tril(jnp.ones((C, C), dtype=bool))

    def step(state, inp):
        qi, ki, vi, bi = inp
        o_inter = jnp.einsum('bhik,bhkv->bhiv', qi * jnp.exp(bi), state)
        diff = bi[:, :, :, None, :] - bi[:, :, None, :, :]
        decay = jnp.exp(jnp.where(causal[:, :, None], diff, -jnp.inf))
        att = jnp.einsum('bhijk,bhjk->bhij', qi[:, :, :, None, :] * decay, ki)
        o = o_inter + jnp.einsum('bhij,bhjv->bhiv', att, vi)
        b_last = bi[:, :, -1, :]
        k_dec = ki * jnp.exp(b_last[:, :, None, :] - bi)
        state = jnp.exp(b_last)[..., None] * state + jnp.einsum('bhjk,bhjv->bhkv', k_dec, vi)
        return state, o

    state0 = jnp.zeros((B, H, DK, DV), jnp.float32)
    _, o = lax.scan(step, state0, (qc, kc, vc, bc))
    return o.transpose(1, 0, 3, 2, 4).reshape(B, S, H, DV)


def conv_gla_mixer(h, w_in, conv_w, conv_b, conv_ln_g, conv_ln_b, gate_w, gate_b, norm_g, w_out):
    B, S, _ = h.shape
    u = h @ w_in
    u_conv, q, k, v, r, g_low = jnp.split(u, AB_SPLITS, axis=-1)
    y_a = conformer_conv(u_conv, conv_w, conv_b, conv_ln_g, conv_ln_b)
    glog = jax.nn.log_sigmoid((g_low @ gate_w + gate_b).astype(jnp.float32)) / GLA_GATE_TAU
    o = gla_chunked(q.reshape(B, S, GLA_HEADS, GLA_HEAD_K),
                    k.reshape(B, S, GLA_HEADS, GLA_HEAD_K),
                    v.reshape(B, S, GLA_HEADS, GLA_HEAD_V),
                    glog.reshape(B, S, GLA_HEADS, GLA_HEAD_K))
    o = rms_norm(o, norm_g.reshape(GLA_HEADS, GLA_HEAD_V)).reshape(B, S, GLA_V_WIDTH)
    y_b = (o * jax.nn.silu(r.astype(jnp.float32))).astype(h.dtype)
    return jnp.concatenate([y_a.astype(h.dtype), y_b], axis=-1) @ w_out


def mla_mixer(h, cos, sin, w_in, q_norm_g, kv_norm_g, w_uq, w_ukv, w_out):
    B, S, _ = h.shape
    u = h @ w_in
    cq, ckv, k_rope = jnp.split(u, [MLA_Q_LORA, MLA_Q_LORA + MLA_KV_LORA], axis=-1)
    q = (rms_norm(cq, q_norm_g) @ w_uq).reshape(B, S, MLA_HEADS, MLA_NOPE + MLA_ROPE)
    kv = (rms_norm(ckv, kv_norm_g) @ w_ukv).reshape(B, S, MLA_HEADS, MLA_NOPE + MLA_V)
    q_nope, q_rope = jnp.split(q, [MLA_NOPE], axis=-1)
    k_nope, v = jnp.split(kv, [MLA_NOPE], axis=-1)
    q_rope = apply_rope(q_rope, cos[:, :, None, :], sin[:, :, None, :])
    k_rope = apply_rope(k_rope, cos, sin)
    nb = S // Q_BLOCK

    def blocks(t):
        return t.reshape(B, nb, Q_BLOCK, *t.shape[2:]).swapaxes(0, 1)

    key_idx = jnp.arange(S)

    def attend(args):
        qn, qr, blk = args
        s = (jnp.einsum('bqhd,bkhd->bhqk', qn, k_nope, preferred_element_type=jnp.float32)
             + jnp.einsum('bqhr,bkr->bhqk', qr, k_rope, preferred_element_type=jnp.float32)) * MLA_SCALE
        q_idx = blk * Q_BLOCK + jnp.arange(Q_BLOCK)
        s = jnp.where(key_idx[None, :] <= q_idx[:, None], s, -jnp.inf)
        p = jax.nn.softmax(s, axis=-1).astype(v.dtype)
        return jnp.einsum('bhqk,bkhv->bqhv', p, v)

    o = lax.map(attend, (blocks(q_nope), blocks(q_rope), jnp.arange(nb)))
    o = o.swapaxes(0, 1).reshape(B, S, MLA_HEADS * MLA_V)
    return o @ w_out


def hier_moe(h, w_group, b_group, w_router, b_router, w_gate, w_up, w_down):
    B, S, D = h.shape
    t = h.reshape(-1, D)
    T = t.shape[0]
    g_logits = jnp.dot(t, w_group, preferred_element_type=jnp.float32) + b_group
    g_prob = jax.nn.softmax(g_logits, axis=-1)
    g_idx = jnp.argmax(g_logits, axis=-1)
    g_w = jnp.take_along_axis(g_prob, g_idx[:, None], axis=-1)[:, 0]
    e_logits = (jnp.dot(t, w_router, preferred_element_type=jnp.float32) + b_router
                ).reshape(T, MOE_GROUPS, MOE_EXPERTS_PER_GROUP)
    e_sel = jnp.take_along_axis(e_logits, g_idx[:, None, None], axis=1)[:, 0]
    top_v, top_i = lax.top_k(e_sel, MOE_TOPK)
    top_w = jax.nn.softmax(top_v, axis=-1) * g_w[:, None]
    e_id = g_idx[:, None] * MOE_EXPERTS_PER_GROUP + top_i
    comb = jnp.sum(jax.nn.one_hot(e_id, MOE_EXPERTS, dtype=jnp.float32) * top_w[..., None], axis=1)
    y = jnp.zeros((T, D), jnp.float32)
    for e in range(MOE_EXPERTS):
        hid = jax.nn.silu(t @ w_gate[e]) * (t @ w_up[e])
        y = y + comb[:, e:e + 1] * jnp.dot(hid, w_down[e], preferred_element_type=jnp.float32)
    return y.astype(h.dtype).reshape(B, S, D)


def setup_inputs(seed: int = 0) -> dict:
    key = jax.random.key(seed)
    ks = iter(jax.random.split(key, 40))
    D = D_MODEL
    ne = (DEPTH + 1) // 2
    no = DEPTH // 2

    def nrm(shape, scale):
        return jax.random.normal(next(ks), shape, jnp.float32) * scale

    def gain(shape):
        return 1.0 + nrm(shape, 0.02)

    x = nrm((BATCH, SEQ, D), 1.0)
    c = nrm((BATCH, D), 1.0)
    offsets = jax.random.randint(next(ks), (BATCH, 1), 0, 4096)
    positions = (offsets + jnp.arange(SEQ)[None, :]).astype(jnp.int32)
    return {
        "x": x,
        "c": c,
        "positions": positions,
        "ada_w": nrm((DEPTH, D, 6 * D), ADA_INIT * D ** -0.5),
        "ada_b": nrm((DEPTH, 6 * D), 0.01),
        "ln_mix_g": gain((DEPTH, D)),
        "ln_mix_b": nrm((DEPTH, D), 0.02),
        "ln_ffn_g": gain((DEPTH, D)),
        "ln_ffn_b": nrm((DEPTH, D), 0.02),
        "ab_w_in": nrm((ne, D, AB_IN), D ** -0.5),
        "conv_w": nrm((ne, CONV_WIDTH, CONV_CH), CONV_WIDTH ** -0.5),
        "conv_b": nrm((ne, CONV_CH), 0.02),
        "conv_ln_g": gain((ne, CONV_CH)),
        "conv_ln_b": nrm((ne, CONV_CH), 0.02),
        "gla_gate_w": nrm((ne, GLA_GATE_RANK, GLA_K_WIDTH), GLA_GATE_RANK ** -0.5),
        "gla_gate_b": nrm((ne, GLA_K_WIDTH), 0.02),
        "gla_norm_g": gain((ne, GLA_V_WIDTH)),
        "ab_w_out": nrm((ne, AB_OUT, D), AB_OUT ** -0.5 * DEEPNORM_BETA),
        "mla_w_in": nrm((no, D, MLA_IN), D ** -0.5),
        "mla_q_norm_g": gain((no, MLA_Q_LORA)),
        "mla_kv_norm_g": gain((no, MLA_KV_LORA)),
        "mla_w_uq": nrm((no, MLA_Q_LORA, MLA_HEADS * (MLA_NOPE + MLA_ROPE)), MLA_Q_LORA ** -0.5),
        "mla_w_ukv": nrm((no, MLA_KV_LORA, MLA_HEADS * (MLA_NOPE + MLA_V)), MLA_KV_LORA ** -0.5),
        "mla_w_out": nrm((no, MLA_HEADS * MLA_V, D), (MLA_HEADS * MLA_V) ** -0.5 * DEEPNORM_BETA),
        "moe_w_group": nrm((DEPTH, D, MOE_GROUPS), D ** -0.5),
        "moe_b_group": nrm((DEPTH, MOE_GROUPS), 0.01),
        "moe_w_router": nrm((DEPTH, D, MOE_EXPERTS), D ** -0.5),
        "moe_b_router": nrm((DEPTH, MOE_EXPERTS), 0.01),
        "moe_w_gate": nrm((DEPTH, MOE_EXPERTS, D, MOE_FF), D ** -0.5),
        "moe_w_up": nrm((DEPTH, MOE_EXPERTS, D, MOE_FF), D ** -0.5),
        "moe_w_down": nrm((DEPTH, MOE_EXPERTS, MOE_FF, D), MOE_FF ** -0.5 * DEEPNORM_BETA),
    }


def reference(x, c, positions, ada_w, ada_b, ln_mix_g, ln_mix_b, ln_ffn_g, ln_ffn_b,
              ab_w_in, conv_w, conv_b, conv_ln_g, conv_ln_b, gla_gate_w, gla_gate_b,
              gla_norm_g, ab_w_out, mla_w_in, mla_q_norm_g, mla_kv_norm_g, mla_w_uq,
              mla_w_ukv, mla_w_out, moe_w_group, moe_b_group, moe_w_router, moe_b_router,
              moe_w_gate, moe_w_up, moe_w_down):
    inv_freq = 1.0 / (ROPE_THETA ** (jnp.arange(0, MLA_ROPE, 2, dtype=jnp.float32) / MLA_ROPE))
    ang = positions.astype(jnp.float32)[..., None] * inv_freq
    cos, sin = jnp.cos(ang), jnp.sin(ang)
    c_act = jax.nn.silu(c)

    for layer in range(DEPTH):
        mod = (c_act @ ada_w[layer] + ada_b[layer])[:, None, :]
        sh_m, sc_m, g_m, sh_f, sc_f, g_f = jnp.split(mod, 6, axis=-1)
        i = layer // 2

        h = x * (1.0 + sc_m) + sh_m
        if layer % 2 == 0:
            mix = conv_gla_mixer(h, ab_w_in[i], conv_w[i], conv_b[i], conv_ln_g[i], conv_ln_b[i],
                                 gla_gate_w[i], gla_gate_b[i], gla_norm_g[i], ab_w_out[i])
        else:
            mix = mla_mixer(h, cos, sin, mla_w_in[i], mla_q_norm_g[i], mla_kv_norm_g[i],
                            mla_w_uq[i], mla_w_ukv[i], mla_w_out[i])
        x = layer_norm(DEEPNORM_ALPHA * x + (1.0 + g_m) * mix, ln_mix_g[layer], ln_mix_b[layer])

        h = x * (1.0 + sc_f) + sh_f
        ffn = hier_moe(h, moe_w_group[layer], moe_b_group[layer], moe_w_router[layer],
                       moe_b_router[layer], moe_w_gate[layer], moe_w_up[layer], moe_w_down[layer])
        x = layer_norm(DEEPNORM_ALPHA * x + (1.0 + g_f) * ffn, ln_ffn_g[layer], ln_ffn_b[layer])
    return x
```

```python
import functools

import jax
import jax.numpy as jnp
from jax import lax
from jax.experimental import pallas as pl
from jax.experimental.pallas import tpu as pltpu

F32 = jnp.float32
BF16 = jnp.bfloat16
I32 = jnp.int32
HIGHEST = lax.Precision.HIGHEST

LN_EPS = 1e-5
RMS_EPS = 1e-6
CONV_WIDTH = 31
GLA_HEADS = 4
GLA_GATE_TAU = 16.0
MLA_HEADS = 8
MLA_NOPE = 128
MLA_ROPE = 64
MLA_V = 128
ROPE_THETA = 10000.0
MOE_GROUPS = 4
MOE_EXPERTS_PER_GROUP = 8
MOE_EXPERTS = MOE_GROUPS * MOE_EXPERTS_PER_GROUP

LANES = 128
TOKEN_TILE = 512
GLA_CHUNK = 128
GLA_BLOCK = 512
CONV_ROWS = 32
CONV_HALO = 32
ATTN_TQ = 512
ATTN_TK = 512
MOE_ROW_TILE = 256
COMBINE_TILE = 256
VMEM_LIMIT = 48 * 1024 * 1024


def _cparams(sem):
    return pltpu.CompilerParams(dimension_semantics=sem, vmem_limit_bytes=VMEM_LIMIT)


def _sigmoid(x):
    return 1.0 / (1.0 + jnp.exp(-x))


def _dot(a, b):
    return jnp.dot(a, b, preferred_element_type=F32)


def _dot_nt(a, b):
    return lax.dot_general(a, b, (((1,), (1,)), ((), ())), preferred_element_type=F32)


def _dot_tn(a, b):
    return lax.dot_general(a, b, (((0,), (0,)), ((), ())), preferred_element_type=F32)


def _layer_norm(y, g, b):
    mu = jnp.mean(y, axis=-1, keepdims=True)
    d = y - mu
    var = jnp.mean(d * d, axis=-1, keepdims=True)
    return d * lax.rsqrt(var + LN_EPS) * g + b


def _ada_kernel(c_ref, w_ref, b_ref, o_ref):
    c = c_ref[...]
    o_ref[0] = jnp.dot(c * _sigmoid(c), w_ref[0], precision=HIGHEST, preferred_element_type=F32) + b_ref[0]


def _ada(c, ada_w, ada_b):
    depth, d, n = ada_w.shape
    bsz = c.shape[0]
    tn = 1536
    return pl.pallas_call(
        _ada_kernel,
        out_shape=jax.ShapeDtypeStruct((depth, bsz, n), F32),
        grid=(depth, n // tn),
        in_specs=[pl.BlockSpec((bsz, d), lambda l, j: (0, 0)),
                  pl.BlockSpec((1, d, tn), lambda l, j: (l, 0, j)),
                  pl.BlockSpec((1, 1, tn), lambda l, j: (l, 0, j))],
        out_specs=pl.BlockSpec((1, bsz, tn), lambda l, j: (l, 0, j)),
        compiler_params=_cparams(("parallel", "parallel")),
        name="ada",
    )(c, ada_w, ada_b.reshape(depth, 1, n))


def _ab_in_kernel(x_ref, sc_ref, sh_ref, wc_ref, wq_ref, wk_ref, wv_ref, wr_ref, wg_ref, gw_ref, gb_ref,
                  uc_ref, q_ref, k_ref, v_ref, r_ref, gl_ref):
    h = (x_ref[...] * (1.0 + sc_ref[0]) + sh_ref[0]).astype(BF16)
    uc_ref[...] = _dot(h, wc_ref[...])
    q_ref[...] = _dot(h, wq_ref[...])
    k_ref[...] = _dot(h, wk_ref[...])
    v_ref[...] = _dot(h, wv_ref[...])
    r_ref[...] = _dot(h, wr_ref[...])
    g_low = _dot(h, wg_ref[...])
    z = jnp.dot(g_low, gw_ref[...], precision=HIGHEST, preferred_element_type=F32) + gb_ref[...]
    gl_ref[...] = (jnp.minimum(z, 0.0) - jnp.log(1.0 + jnp.exp(-jnp.abs(z)))) * (1.0 / GLA_GATE_TAU)


def _ab_in(x2, sc, sh, w_in, gate_w, gate_b, tiles_per_b):
    t, d = x2.shape
    cc2 = d
    kw = d // 4
    vw = d // 2
    rank = gate_w.shape[0]
    splits = [cc2, cc2 + kw, cc2 + 2 * kw, cc2 + 2 * kw + vw, cc2 + 2 * kw + 2 * vw]
    wb = w_in.astype(BF16)
    ws = [wb[:, :splits[0]], wb[:, splits[0]:splits[1]], wb[:, splits[1]:splits[2]],
          wb[:, splits[2]:splits[3]], wb[:, splits[3]:splits[4]], wb[:, splits[4]:]]
    tm = TOKEN_TILE
    full = lambda a: pl.BlockSpec(a.shape, lambda i: (0,) * a.ndim)
    row = lambda n: pl.BlockSpec((tm, n), lambda i: (i, 0))
    mod = pl.BlockSpec((1, 1, d), lambda i: (i // tiles_per_b, 0, 0))
    gb2 = gate_b.reshape(1, kw)
    widths = [cc2, kw, kw, vw, vw, kw]
    return pl.pallas_call(
        _ab_in_kernel,
        out_shape=[jax.ShapeDtypeStruct((t, n), F32) for n in widths],
        grid=(t // tm,),
        in_specs=[row(d), mod, mod] + [full(w) for w in ws] + [full(gate_w), full(gb2)],
        out_specs=[row(n) for n in widths],
        compiler_params=_cparams(("parallel",)),
        name="ab_in",
    )(x2, sc, sh, *ws, gate_w, gb2)


def _conv_kernel(u_ref, halo_ref, cw_ref, cb_ref, lg_ref, lb_ref, o_ref, hp_ref, *, ts, cc):
    j = pl.program_id(1)

    def glu(u):
        return u[:, :cc] * _sigmoid(u[:, cc:])

    hp_ref[0:CONV_HALO, :] = jnp.where(j > 0, glu(halo_ref[0]), 0.0)
    hp_ref[CONV_HALO:CONV_HALO + ts, :] = glu(u_ref[0])
    shift = CONV_HALO - (CONV_WIDTH - 1)
    for rb in range(ts // CONV_ROWS):
        r0 = rb * CONV_ROWS
        acc = jnp.zeros((CONV_ROWS, cc), F32)
        for tap in range(CONV_WIDTH):
            acc = acc + cw_ref[tap:tap + 1, :] * hp_ref[r0 + shift + tap:r0 + shift + tap + CONV_ROWS, :]
        y = _layer_norm(acc + cb_ref[...], lg_ref[...], lb_ref[...])
        o_ref[0, r0:r0 + CONV_ROWS, :] = (y * _sigmoid(y)).astype(o_ref.dtype)


def _conv(u3, conv_w, conv_b, ln_g, ln_b):
    bsz, s, cc2 = u3.shape
    cc = cc2 // 2
    ts = TOKEN_TILE
    hb = ts // CONV_HALO
    vec = lambda a: pl.BlockSpec((1, cc), lambda b, j: (0, 0))
    return pl.pallas_call(
        functools.partial(_conv_kernel, ts=ts, cc=cc),
        out_shape=jax.ShapeDtypeStruct((bsz, s, cc), BF16),
        grid=(bsz, s // ts),
        in_specs=[pl.BlockSpec((1, ts, cc2), lambda b, j: (b, j, 0)),
                  pl.BlockSpec((1, CONV_HALO, cc2), lambda b, j: (b, jnp.maximum(j * hb - 1, 0), 0)),
                  pl.BlockSpec((CONV_WIDTH, cc), lambda b, j: (0, 0)),
                  vec(conv_b), vec(ln_g), vec(ln_b)],
        out_specs=pl.BlockSpec((1, ts, cc), lambda b, j: (b, j, 0)),
        scratch_shapes=[pltpu.VMEM((CONV_HALO + ts, cc), F32)],
        compiler_params=_cparams(("parallel", "parallel")),
        name="conv",
    )(u3, u3, conv_w, conv_b.reshape(1, cc), ln_g.reshape(1, cc), ln_b.reshape(1, cc))


def _gla_kernel(q_ref, k_ref, v_ref, gl_ref, r_ref, ng_ref, o_ref, st_ref, *, nh, dk, dv, gc, nchunks):
    @pl.when(pl.program_id(1) == 0)
    def _():
        st_ref[...] = jnp.zeros_like(st_ref)

    row = lax.broadcasted_iota(I32, (gc, gc), 0)
    col = lax.broadcasted_iota(I32, (gc, gc), 1)
    causal = col <= row
    tri = jnp.where(causal, 1.0, 0.0).astype(F32)
    scale = dk ** -0.5

    def chunk(c, carry):
        r0 = pl.multiple_of(c * gc, gc)
        rows = pl.ds(r0, gc)
        g = gl_ref[0, rows, :]
        b = jnp.dot(tri, g, precision=HIGHEST, preferred_element_type=F32)
        b_last = b[gc - 1:gc, :]
        mid = 0.5 * b_last
        q = q_ref[0, rows, :] * scale
        k = k_ref[0, rows, :]
        v = v_ref[0, rows, :].astype(BF16)
        q_in = (q * jnp.exp(b - mid)).astype(BF16)
        k_in = (k * jnp.exp(mid - b)).astype(BF16)
        q_st = (q * jnp.exp(b)).astype(BF16)
        k_st = (k * jnp.exp(b_last - b)).astype(BF16)
        decay = jnp.exp(b_last)
        r = r_ref[0, rows, :]
        gate = r * _sigmoid(r)
        for h in range(nh):
            ks = slice(h * dk, (h + 1) * dk)
            vs = slice(h * dv, (h + 1) * dv)
            att = jnp.where(causal, _dot_nt(q_in[:, ks], k_in[:, ks]), 0.0).astype(BF16)
            st = st_ref[h]
            o = _dot(att, v[:, vs]) + _dot_nt(q_st[:, ks], st.astype(BF16))
            st_ref[h] = st * decay[:, ks] + _dot_tn(v[:, vs], k_st[:, ks])
            o = o * lax.rsqrt(jnp.mean(o * o, axis=-1, keepdims=True) + RMS_EPS) * ng_ref[:, vs]
            o_ref[0, rows, vs] = (o * gate[:, vs]).astype(o_ref.dtype)
        return carry

    lax.fori_loop(0, nchunks, chunk, 0)


def _gla(q3, k3, v3, gl3, r3, norm_g):
    bsz, s, kw = q3.shape
    vw = v3.shape[-1]
    nh = GLA_HEADS
    dk, dv = kw // nh, vw // nh
    cb = GLA_BLOCK
    gc = GLA_CHUNK
    blk = lambda n: pl.BlockSpec((1, cb, n), lambda b, j: (b, j, 0))
    return pl.pallas_call(
        functools.partial(_gla_kernel, nh=nh, dk=dk, dv=dv, gc=gc, nchunks=cb // gc),
        out_shape=jax.ShapeDtypeStruct((bsz, s, vw), BF16),
        grid=(bsz, s // cb),
        in_specs=[blk(kw), blk(kw), blk(vw), blk(kw), blk(vw), pl.BlockSpec((1, vw), lambda b, j: (0, 0))],
        out_specs=blk(vw),
        scratch_shapes=[pltpu.VMEM((nh, dv, dk), F32)],
        compiler_params=_cparams(("parallel", "arbitrary")),
        name="gla",
    )(q3, k3, v3, gl3, r3, norm_g.reshape(1, vw))


def _proj_ln_kernel(*refs, n_in, alpha):
    a_refs, w_refs = refs[:n_in], refs[n_in:2 * n_in]
    x_ref, gate_ref, lg_ref, lb_ref, o_ref = refs[2 * n_in:]
    mix = _dot(a_refs[0][...], w_refs[0][...])
    for a_ref, w_ref in zip(a_refs[1:], w_refs[1:]):
        mix = mix + _dot(a_ref[...], w_ref[...])
    y = alpha * x_ref[...] + (1.0 + gate_ref[0]) * mix
    o_ref[...] = _layer_norm(y, lg_ref[...], lb_ref[...])


def _proj_ln(acts, weights, x2, gate, ln_g, ln_b, tiles_per_b, alpha):
    t, d = x2.shape
    tm = TOKEN_TILE
    n_in = len(acts)
    row = lambda n: pl.BlockSpec((tm, n), lambda i: (i, 0))
    full = lambda a: pl.BlockSpec(a.shape, lambda i: (0,) * a.ndim)
    vec = pl.BlockSpec((1, d), lambda i: (0, 0))
    return pl.pallas_call(
        functools.partial(_proj_ln_kernel, n_in=n_in, alpha=alpha),
        out_shape=jax.ShapeDtypeStruct((t, d), F32),
        grid=(t // tm,),
        in_specs=[row(a.shape[1]) for a in acts] + [full(w) for w in weights]
                 + [row(d), pl.BlockSpec((1, 1, d), lambda i: (i // tiles_per_b, 0, 0)), vec, vec],
        out_specs=row(d),
        compiler_params=_cparams(("parallel",)),
        name="proj_ln",
    )(*acts, *weights, x2, gate, ln_g.reshape(1, d), ln_b.reshape(1, d))


def _mla_in_kernel(x_ref, sc_ref, sh_ref, pos_ref, invf_ref, sign_ref, win_ref, gq_ref, gkv_ref,
                   wqa_ref, wqb_ref, wkv_ref, q_ref, k_ref, v_ref, *, nh, q_lora, kv_lora, scale):
    h = (x_ref[...] * (1.0 + sc_ref[0]) + sh_ref[0]).astype(BF16)
    u = _dot(h, win_ref[...])
    cq = u[:, :q_lora]
    ckv = u[:, q_lora:q_lora + kv_lora]
    kr = u[:, q_lora + kv_lora:q_lora + kv_lora + LANES]
    kr_sw = u[:, q_lora + kv_lora + LANES:]
    cqn = (cq * lax.rsqrt(jnp.mean(cq * cq, axis=-1, keepdims=True) + RMS_EPS) * gq_ref[...]).astype(BF16)
    kvn = (ckv * lax.rsqrt(jnp.mean(ckv * ckv, axis=-1, keepdims=True) + RMS_EPS) * gkv_ref[...]).astype(BF16)
    ang = pos_ref[...].astype(F32) * invf_ref[...]
    cos = jnp.cos(ang)
    sin = jnp.sin(ang) * sign_ref[...]
    kr_rot = (kr * cos + kr_sw * sin).astype(BF16)
    qa = _dot(cqn, wqa_ref[...])
    qb = _dot(cqn, wqb_ref[...])
    kv = _dot(kvn, wkv_ref[...])
    hw = 2 * LANES
    for hd in range(nh):
        q_ref[:, hd * hw:hd * hw + LANES] = (qa[:, hd * hw:hd * hw + LANES] * scale).astype(BF16)
        rope = qa[:, hd * hw + LANES:(hd + 1) * hw] * cos + qb[:, hd * LANES:(hd + 1) * LANES] * sin
        q_ref[:, hd * hw + LANES:(hd + 1) * hw] = (rope * scale).astype(BF16)
        k_ref[:, hd * hw:hd * hw + LANES] = kv[:, hd * LANES:(hd + 1) * LANES].astype(BF16)
        k_ref[:, hd * hw + LANES:(hd + 1) * hw] = kr_rot
    v_ref[...] = kv[:, nh * LANES:].astype(BF16)


def _mla_in(x2, sc, sh, pos2, w_in, gq, gkv, w_uq, w_ukv, tiles_per_b):
    t, d = x2.shape
    nh = MLA_HEADS
    q_lora, kv_lora = gq.shape[0], gkv.shape[0]
    half = MLA_ROPE // 2
    pad = LANES - MLA_ROPE
    kr_w = w_in[:, q_lora + kv_lora:]
    kr_sw = jnp.concatenate([kr_w[:, half:], kr_w[:, :half]], axis=1)
    zpad = jnp.zeros((d, pad), w_in.dtype)
    win_ext = jnp.concatenate([w_in[:, :q_lora + kv_lora], kr_w, zpad, kr_sw, zpad], axis=1).astype(BF16)
    wq = w_uq.reshape(q_lora, nh, MLA_NOPE + MLA_ROPE)
    q_nope, q_rope = wq[:, :, :MLA_NOPE], wq[:, :, MLA_NOPE:]
    q_rope_sw = jnp.concatenate([q_rope[:, :, half:], q_rope[:, :, :half]], axis=2)
    zq = jnp.zeros((q_lora, nh, pad), w_uq.dtype)
    wqa = jnp.concatenate([q_nope, q_rope, zq], axis=2).reshape(q_lora, nh * 2 * LANES).astype(BF16)
    wqb = jnp.concatenate([q_rope_sw, zq], axis=2).reshape(q_lora, nh * LANES).astype(BF16)
    wkv = w_ukv.reshape(kv_lora, nh, MLA_NOPE + MLA_V)
    wkv_p = jnp.concatenate([wkv[:, :, :MLA_NOPE].reshape(kv_lora, nh * MLA_NOPE),
                             wkv[:, :, MLA_NOPE:].reshape(kv_lora, nh * MLA_V)], axis=1).astype(BF16)
    inv_freq = 1.0 / (ROPE_THETA ** (jnp.arange(0, MLA_ROPE, 2, dtype=F32) / MLA_ROPE))
    invf = jnp.concatenate([inv_freq, inv_freq, jnp.zeros((pad,), F32)]).reshape(1, LANES)
    sign = jnp.concatenate([-jnp.ones((half,), F32), jnp.ones((half,), F32), jnp.zeros((pad,), F32)]).reshape(1, LANES)
    tm = TOKEN_TILE
    full = lambda a: pl.BlockSpec(a.shape, lambda i: (0,) * a.ndim)
    row = lambda n: pl.BlockSpec((tm, n), lambda i: (i, 0))
    mod = pl.BlockSpec((1, 1, d), lambda i: (i // tiles_per_b, 0, 0))
    gq2, gkv2 = gq.reshape(1, q_lora), gkv.reshape(1, kv_lora)
    scale = (MLA_NOPE + MLA_ROPE) ** -0.5
    return pl.pallas_call(
        functools.partial(_mla_in_kernel, nh=nh, q_lora=q_lora, kv_lora=kv_lora, scale=scale),
        out_shape=[jax.ShapeDtypeStruct((t, nh * 2 * LANES), BF16), jax.ShapeDtypeStruct((t, nh * 2 * LANES), BF16),
                   jax.ShapeDtypeStruct((t, nh * MLA_V), BF16)],
        grid=(t // tm,),
        in_specs=[row(d), mod, mod, row(1), full(invf), full(sign), full(win_ext), full(gq2), full(gkv2),
                  full(wqa), full(wqb), full(wkv_p)],
        out_specs=[row(nh * 2 * LANES), row(nh * 2 * LANES), row(nh * MLA_V)],
        compiler_params=_cparams(("parallel",)),
        name="mla_in",
    )(x2, sc, sh, pos2, invf, sign, win_ext, gq2, gkv2, wqa, wqb, wkv_p)


def _attn_kernel(q_ref, k_ref, v_ref, o_ref, m_ref, l_ref, acc_ref, *, tq, tk):
    i = pl.program_id(2)
    q = q_ref[...]
    m_ref[...] = jnp.full_like(m_ref, -jnp.inf)
    l_ref[...] = jnp.zeros_like(l_ref)
    acc_ref[...] = jnp.zeros_like(acc_ref)

    def step(s):
        m_old = m_ref[...]
        m_new = jnp.maximum(m_old, jnp.max(s, axis=-1, keepdims=True))
        p = jnp.exp(s - m_new)
        alpha = jnp.exp(m_old - m_new)
        l_ref[...] = alpha * l_ref[...] + jnp.sum(p, axis=-1, keepdims=True)
        m_ref[...] = m_new
        return p.astype(BF16), alpha

    def full_tile(j, carry):
        rows = pl.ds(pl.multiple_of(j * tk, tk), tk)
        p, alpha = step(_dot_nt(q, k_ref[rows, :]))
        acc_ref[...] = alpha * acc_ref[...] + _dot(p, v_ref[rows, :])
        return carry

    lax.fori_loop(0, i * (tq // tk), full_tile, 0)
    rows = pl.ds(pl.multiple_of(i * tq, tq), tk)
    s = _dot_nt(q, k_ref[rows, :])
    qi = lax.broadcasted_iota(I32, (tq, tk), 0)
    ki = lax.broadcasted_iota(I32, (tq, tk), 1)
    p, alpha = step(jnp.where(ki <= qi, s, -jnp.inf))
    acc = alpha * acc_ref[...] + _dot(p, v_ref[rows, :])
    o_ref[...] = (acc / l_ref[...]).astype(o_ref.dtype)


def _attn(q, k, v, bsz, s):
    nh = MLA_HEADS
    tq, tk = ATTN_TQ, ATTN_TK
    assert tq == tk
    nq = s // tq
    hw = 2 * LANES
    return pl.pallas_call(
        functools.partial(_attn_kernel, tq=tq, tk=tk),
        out_shape=jax.ShapeDtypeStruct((bsz * s, nh * MLA_V), BF16),
        grid=(bsz, nh, nq),
        in_specs=[pl.BlockSpec((tq, hw), lambda b, h, i: (b * nq + i, h)),
                  pl.BlockSpec((s, hw), lambda b, h, i: (b, h)),
                  pl.BlockSpec((s, MLA_V), lambda b, h, i: (b, h))],
        out_specs=pl.BlockSpec((tq, MLA_V), lambda b, h, i: (b * nq + i, h)),
        scratch_shapes=[pltpu.VMEM((tq, 1), F32), pltpu.VMEM((tq, 1), F32), pltpu.VMEM((tq, MLA_V), F32)],
        compiler_params=_cparams(("parallel", "parallel", "arbitrary")),
        name="attn",
    )(q, k, v)


def _route_kernel(x_ref, sc_ref, sh_ref, w_ref, b_ref, er_ref, wt_ref, offs_ref, te_ref, na_ref,
                  upper_ref, carry_ref, *, tm, ne, ng, row_tile, nt_pad):
    i = pl.program_id(0)
    epg = ne // ng

    @pl.when(i == 0)
    def _():
        r = lax.broadcasted_iota(I32, (tm, tm), 0)
        c = lax.broadcasted_iota(I32, (tm, tm), 1)
        upper_ref[...] = jnp.where(r < c, 1.0, 0.0).astype(BF16)
        carry_ref[...] = jnp.zeros_like(carry_ref)

    h = x_ref[...] * (1.0 + sc_ref[0]) + sh_ref[0]
    logits = jnp.dot(h, w_ref[...], precision=HIGHEST, preferred_element_type=F32) + b_ref[...]
    lt = logits.T
    lr = lt[0:ne]
    grow = lax.broadcasted_iota(I32, (8, tm), 0).astype(F32)
    lg = jnp.where(grow < ng, lt[ne:ne + 8], -jnp.inf)
    gmax = jnp.max(lg, axis=0, keepdims=True)
    g_idx = jnp.min(jnp.where(lg == gmax, grow, 1e9), axis=0, keepdims=True)
    g_w = 1.0 / jnp.sum(jnp.exp(lg - gmax), axis=0, keepdims=True)
    erow = lax.broadcasted_iota(I32, (ne, tm), 0).astype(F32)
    in_group = jnp.floor(erow * (1.0 / epg)) == g_idx
    sel = jnp.where(in_group, lr, -jnp.inf)
    v1 = jnp.max(sel, axis=0, keepdims=True)
    i1 = jnp.min(jnp.where(sel == v1, erow, 1e9), axis=0, keepdims=True)
    sel2 = jnp.where(erow == i1, -jnp.inf, sel)
    v2 = jnp.max(sel2, axis=0, keepdims=True)
    i2 = jnp.min(jnp.where(sel2 == v2, erow, 1e9), axis=0, keepdims=True)
    t = jnp.exp(v2 - v1)
    w1 = g_w / (1.0 + t)
    w2 = g_w * t / (1.0 + t)
    oh1 = erow == i1
    oh2 = erow == i2
    member = jnp.where(oh1 | oh2, 1.0, 0.0)
    rank = _dot(member.astype(BF16), upper_ref[...]) + carry_ref[...]
    r1 = jnp.sum(jnp.where(oh1, rank, 0.0), axis=0, keepdims=True)
    r2 = jnp.sum(jnp.where(oh2, rank, 0.0), axis=0, keepdims=True)
    carry_ref[...] = carry_ref[...] + jnp.sum(member, axis=1, keepdims=True)

    orow = lax.broadcasted_iota(I32, (8, tm), 0)
    er = jnp.where(orow == 0, i1, jnp.where(orow == 1, i2, jnp.where(orow == 2, r1, jnp.where(orow == 3, r2, 0.0))))
    er_ref[...] = er.astype(I32)
    wrow = lax.broadcasted_iota(I32, (LANES, tm), 0)
    wt_ref[...] = jnp.where(wrow == 0, w1, jnp.where(wrow == 1, w2, 0.0)).T

    @pl.when(i == pl.num_programs(0) - 1)
    def _():
        cnt = carry_ref[...]
        ntl = jnp.floor((cnt + (row_tile - 1)) * (1.0 / row_tile))
        er_ = lax.broadcasted_iota(I32, (ne, ne), 0)
        ec_ = lax.broadcasted_iota(I32, (ne, ne), 1)
        incl = jnp.where(ec_ <= er_, 1.0, 0.0).astype(F32)
        ends = jnp.dot(incl, jnp.broadcast_to(ntl, (ne, LANES)), precision=HIGHEST,
                       preferred_element_type=F32)
        starts = ends - ntl
        offs_ref[...] = jnp.concatenate([starts * row_tile, ends * row_tile, jnp.broadcast_to(ntl, (ne, LANES)),
                                         jnp.zeros((8, LANES), F32)], axis=0).astype(I32)
        tile = lax.broadcasted_iota(I32, (ne, nt_pad), 1).astype(F32)
        te = jnp.sum(jnp.where(ends[:, 0:1] <= tile, 1.0, 0.0), axis=0, keepdims=True)
        te_ref[...] = jnp.broadcast_to(jnp.minimum(te, ne - 1.0), (8, nt_pad)).astype(I32)
        na_ref[...] = jnp.broadcast_to(ends[ne - 1:ne, :], (8, LANES)).astype(I32)


def _moe_tiles(t):
    nt_max = (2 * t) // MOE_ROW_TILE + MOE_EXPERTS
    nt_pad = -(-nt_max // LANES) * LANES
    return nt_max, nt_pad


def _route(x2, sc, sh, w_group, b_group, w_router, b_router, tiles_per_b):
    t, d = x2.shape
    ne, ng = MOE_EXPERTS, MOE_GROUPS
    tm = TOKEN_TILE
    _, nt_pad = _moe_tiles(t)
    wcat = jnp.concatenate([w_router, w_group, jnp.zeros((d, LANES - ne - ng), F32)], axis=1)
    bcat = jnp.concatenate([b_router, b_group, jnp.zeros((LANES - ne - ng,), F32)]).reshape(1, LANES)
    const = lambda shp: pl.BlockSpec(shp, lambda i: (0,) * len(shp))
    return pl.pallas_call(
        functools.partial(_route_kernel, tm=tm, ne=ne, ng=ng, row_tile=MOE_ROW_TILE, nt_pad=nt_pad),
        out_shape=[jax.ShapeDtypeStruct((8, t), I32), jax.ShapeDtypeStruct((t, LANES), F32),
                   jax.ShapeDtypeStruct((3 * ne + 8, LANES), I32), jax.ShapeDtypeStruct((8, nt_pad), I32),
                   jax.ShapeDtypeStruct((8, LANES), I32)],
        grid=(t // tm,),
        in_specs=[pl.BlockSpec((tm, d), lambda i: (i, 0)),
                  pl.BlockSpec((1, 1, d), lambda i: (i // tiles_per_b, 0, 0)),
                  pl.BlockSpec((1, 1, d), lambda i: (i // tiles_per_b, 0, 0)),
                  const((d, LANES)), const((1, LANES))],
        out_specs=[pl.BlockSpec((8, tm), lambda i: (0, i)), pl.BlockSpec((tm, LANES), lambda i: (i, 0)),
                   const((3 * ne + 8, LANES)), const((8, nt_pad)), const((8, LANES))],
        scratch_shapes=[pltpu.VMEM((tm, tm), BF16), pltpu.VMEM((ne, 1), F32)],
        compiler_params=_cparams(("arbitrary",)),
        name="moe_route",
    )(x2, sc, sh, wcat, bcat)


def _rows_to_tiles(ref, val):
    n, d = val.shape
    nchunk = d // LANES
    for s in range(nchunk):
        ref[pl.ds(s, n, stride=nchunk), :] = val[:, s * LANES:(s + 1) * LANES]


def _tiles_to_rows(ref, n, d):
    nchunk = d // LANES
    return jnp.concatenate([ref[pl.ds(s, n, stride=nchunk), :] for s in range(nchunk)], axis=1)


def _dispatch_kernel(offs_ref, gend_ref, ntl_ref, x_ref, sc_ref, sh_ref, er_ref, xs_ref,
                     h_ref, z_ref, idx_ref, sem_i, sem_r, *, tm, ne, row_tile, nchunk, nt_max):
    i = pl.program_id(0)

    @pl.when(i == 0)
    def _():
        z_ref[...] = jnp.zeros_like(z_ref)
        for e in range(ne):
            @pl.when(ntl_ref[e] > 0)
            def _():
                start = pl.multiple_of((gend_ref[e] - row_tile) * nchunk, row_tile * nchunk)
                cp = pltpu.make_async_copy(z_ref, xs_ref.at[pl.ds(start, row_tile * nchunk), :], sem_r)
                cp.start()
                cp.wait()
        for back in range(1, ne + 1):
            @pl.when(nt_max - back >= gend_ref[ne - 1] // row_tile)
            def _():
                start = (nt_max - back) * row_tile * nchunk
                cp = pltpu.make_async_copy(z_ref, xs_ref.at[pl.ds(start, row_tile * nchunk), :], sem_r)
                cp.start()
                cp.wait()

    @pl.when(i > 0)
    def _():
        _rows_to_tiles(h_ref, x_ref[...] * (1.0 + sc_ref[0]) + sh_ref[0])
        cp = pltpu.make_async_copy(er_ref, idx_ref, sem_i)
        cp.start()
        cp.wait()

        def row_copy(tok, k):
            dst = offs_ref[idx_ref[k, tok]] + idx_ref[2 + k, tok]
            return pltpu.make_async_copy(h_ref.at[pl.ds(pl.multiple_of(tok * nchunk, nchunk), nchunk), :],
                                         xs_ref.at[pl.ds(pl.multiple_of(dst * nchunk, nchunk), nchunk), :], sem_r)

        def issue(tok, carry):
            row_copy(tok, 0).start()
            row_copy(tok, 1).start()
            return carry

        def drain(tok, carry):
            row_copy(tok, 0).wait()
            row_copy(tok, 1).wait()
            return carry

        lax.fori_loop(0, tm, issue, 0)
        lax.fori_loop(0, tm, drain, 0)


def _dispatch(x2, sc, sh, er, offs, gend, ntl, tiles_per_b):
    t, d = x2.shape
    tm = TOKEN_TILE
    nt_max, _ = _moe_tiles(t)
    nchunk = d // LANES
    prev = lambda i: jnp.maximum(i - 1, 0)
    grid_spec = pltpu.PrefetchScalarGridSpec(
        num_scalar_prefetch=3,
        grid=(t // tm + 1,),
        in_specs=[pl.BlockSpec((tm, d), lambda i, *_: (prev(i), 0)),
                  pl.BlockSpec((1, 1, d), lambda i, *_: (prev(i) // tiles_per_b, 0, 0)),
                  pl.BlockSpec((1, 1, d), lambda i, *_: (prev(i) // tiles_per_b, 0, 0)),
                  pl.BlockSpec((8, tm), lambda i, *_: (0, prev(i)))],
        out_specs=pl.BlockSpec(memory_space=pl.ANY),
        scratch_shapes=[pltpu.VMEM((tm * nchunk, LANES), F32), pltpu.VMEM((MOE_ROW_TILE * nchunk, LANES), F32),
                        pltpu.SMEM((8, tm), I32), pltpu.SemaphoreType.DMA, pltpu.SemaphoreType.DMA])
    return pl.pallas_call(
        functools.partial(_dispatch_kernel, tm=tm, ne=MOE_EXPERTS, row_tile=MOE_ROW_TILE, nchunk=nchunk,
                          nt_max=nt_max),
        out_shape=jax.ShapeDtypeStruct((nt_max * MOE_ROW_TILE * nchunk, LANES), F32),
        grid_spec=grid_spec,
        compiler_params=_cparams(("arbitrary",)),
        name="moe_dispatch",
    )(offs, gend, ntl, x2, sc, sh, er)


def _gmm_kernel(te_ref, na_ref, xs_ref, wg_ref, wu_ref, wd_ref, ys_ref, wgu_buf, wd_buf, *, ff, tr, d):
    j = pl.program_id(0)

    @pl.when(j < na_ref[0])
    def _():
        changed = (j == 0) | (te_ref[j] != te_ref[jnp.maximum(j - 1, 0)])

        @pl.when(changed)
        def _():
            wgu_buf[:, :ff] = wg_ref[0, 0].astype(BF16)
            wgu_buf[:, ff:] = wu_ref[0, 0].astype(BF16)
            wd_buf[...] = wd_ref[0, 0].astype(BF16)

        gu = _dot(_tiles_to_rows(xs_ref, tr, d).astype(BF16), wgu_buf[...])
        g, u = gu[:, :ff], gu[:, ff:]
        hid = (g * _sigmoid(g) * u).astype(BF16)
        _rows_to_tiles(ys_ref, _dot(hid, wd_buf[...]))

    @pl.when(j >= na_ref[0])
    def _():
        ys_ref[...] = jnp.zeros_like(ys_ref)


def _gmm(xs, te, na, w_gate, w_up, w_down, layer):
    d, ff = w_gate.shape[-2:]
    nchunk = d // LANES
    tr = MOE_ROW_TILE
    ns = xs.shape[0] // nchunk
    act = lambda j, te_ref, na_ref: jnp.minimum(j, na_ref[0] - 1)
    grid_spec = pltpu.PrefetchScalarGridSpec(
        num_scalar_prefetch=2,
        grid=(ns // tr,),
        in_specs=[pl.BlockSpec((tr * nchunk, LANES), lambda j, te_ref, na_ref: (act(j, te_ref, na_ref), 0)),
                  pl.BlockSpec((1, 1, d, ff), lambda j, te_ref, na_ref: (layer, te_ref[act(j, te_ref, na_ref)], 0, 0)),
                  pl.BlockSpec((1, 1, d, ff), lambda j, te_ref, na_ref: (layer, te_ref[act(j, te_ref, na_ref)], 0, 0)),
                  pl.BlockSpec((1, 1, ff, d), lambda j, te_ref, na_ref: (layer, te_ref[act(j, te_ref, na_ref)], 0, 0))],
        out_specs=pl.BlockSpec((tr * nchunk, LANES), lambda j, te_ref, na_ref: (j, 0)),
        scratch_shapes=[pltpu.VMEM((d, 2 * ff), BF16), pltpu.VMEM((ff, d), BF16)])
    return pl.pallas_call(
        functools.partial(_gmm_kernel, ff=ff, tr=tr, d=d),
        out_shape=jax.ShapeDtypeStruct((ns * nchunk, LANES), F32),
        grid_spec=grid_spec,
        compiler_params=_cparams(("arbitrary",)),
        name="moe_gmm",
    )(te, na, xs, w_gate, w_up, w_down)


def _combine_kernel(offs_ref, er_ref, wt_ref, x_ref, gate_ref, lg_ref, lb_ref, ys_ref, o_ref,
                    buf0_ref, buf1_ref, idx_ref, sem_i, sem_r, *, tm, d, alpha):
    nchunk = d // LANES
    bufs = (buf0_ref, buf1_ref)
    cp = pltpu.make_async_copy(er_ref, idx_ref, sem_i)
    cp.start()
    cp.wait()

    def row_copy(tok, k):
        src = offs_ref[idx_ref[k, tok]] + idx_ref[2 + k, tok]
        return pltpu.make_async_copy(ys_ref.at[pl.ds(pl.multiple_of(src * nchunk, nchunk), nchunk), :],
                                     bufs[k].at[pl.ds(pl.multiple_of(tok * nchunk, nchunk), nchunk), :], sem_r)

    def issue(tok, carry):
        row_copy(tok, 0).start()
        row_copy(tok, 1).start()
        return carry

    def drain(tok, carry):
        row_copy(tok, 0).wait()
        row_copy(tok, 1).wait()
        return carry

    lax.fori_loop(0, tm, issue, 0)
    lax.fori_loop(0, tm, drain, 0)
    w = wt_ref[...]
    ffn = w[:, 0:1] * _tiles_to_rows(buf0_ref, tm, d) + w[:, 1:2] * _tiles_to_rows(buf1_ref, tm, d)
    y = alpha * x_ref[...] + (1.0 + gate_ref[0]) * ffn
    o_ref[...] = _layer_norm(y, lg_ref[...], lb_ref[...])


def _combine(ys, er, wt, x2, gate, ln_g, ln_b, offs, tiles_per_b, alpha):
    t, d = x2.shape
    tm = COMBINE_TILE
    ratio = TOKEN_TILE // tm
    vec = pl.BlockSpec((1, d), lambda i, *_: (0, 0))
    grid_spec = pltpu.PrefetchScalarGridSpec(
        num_scalar_prefetch=1,
        grid=(t // tm,),
        in_specs=[pl.BlockSpec((8, tm), lambda i, *_: (0, i)),
                  pl.BlockSpec((tm, LANES), lambda i, *_: (i, 0)),
                  pl.BlockSpec((tm, d), lambda i, *_: (i, 0)),
                  pl.BlockSpec((1, 1, d), lambda i, *_: (i // (tiles_per_b * ratio), 0, 0)),
                  vec, vec,
                  pl.BlockSpec(memory_space=pl.ANY)],
        out_specs=pl.BlockSpec((tm, d), lambda i, *_: (i, 0)),
        scratch_shapes=[pltpu.VMEM((tm * d // LANES, LANES), F32), pltpu.VMEM((tm * d // LANES, LANES), F32),
                        pltpu.SMEM((8, tm), I32), pltpu.SemaphoreType.DMA, pltpu.SemaphoreType.DMA])
    return pl.pallas_call(
        functools.partial(_combine_kernel, tm=tm, d=d, alpha=alpha),
        out_shape=jax.ShapeDtypeStruct((t, d), F32),
        grid_spec=grid_spec,
        compiler_params=_cparams(("arbitrary",)),
        name="moe_combine",
    )(offs, er, wt, x2, gate, ln_g.reshape(1, d), ln_b.reshape(1, d), ys)


def _moe_block(x2, sc, sh, gate, ln_g, ln_b, w_group, b_group, w_router, b_router, w_gate, w_up, w_down,
               layer, tiles_per_b, alpha):
    ne = MOE_EXPERTS
    nt_max, _ = _moe_tiles(x2.shape[0])
    er, wt, meta, te, na = _route(x2, sc, sh, w_group, b_group, w_router, b_router, tiles_per_b)
    offs, gend, ntl = meta[:ne, 0], meta[ne:2 * ne, 0], meta[2 * ne:3 * ne, 0]
    xs = _dispatch(x2, sc, sh, er, offs, gend, ntl, tiles_per_b)
    ys = _gmm(xs, te[0, :nt_max], na[0, :1], w_gate, w_up, w_down, layer)
    return _combine(ys, er, wt, x2, gate, ln_g, ln_b, offs, tiles_per_b, alpha)


def kernel(x, c, positions, ada_w, ada_b, ln_mix_g, ln_mix_b, ln_ffn_g, ln_ffn_b, ab_w_in, conv_w, conv_b, conv_ln_g, conv_ln_b, gla_gate_w, gla_gate_b, gla_norm_g, ab_w_out, mla_w_in, mla_q_norm_g, mla_kv_norm_g, mla_w_uq, mla_w_ukv, mla_w_out, moe_w_group, moe_b_group, moe_w_router, moe_b_router, moe_w_gate, moe_w_up, moe_w_down):
    bsz, s, d = x.shape
    depth = ada_w.shape[0]
    t = bsz * s
    tiles_per_b = s // TOKEN_TILE
    alpha = (2 * depth) ** 0.25
    mod = _ada(c, ada_w, ada_b).reshape(depth, bsz, 6, 1, d)
    x2 = x.reshape(t, d)
    for layer in range(depth):
        sh_m, sc_m, g_m, sh_f, sc_f, g_f = (mod[layer, :, n] for n in range(6))
        i = layer // 2
        if layer % 2 == 0:
            uc, q, k, v, r, gl = _ab_in(x2, sc_m, sh_m, ab_w_in[i], gla_gate_w[i], gla_gate_b[i], tiles_per_b)
            y_a = _conv(uc.reshape(bsz, s, -1), conv_w[i], conv_b[i], conv_ln_g[i], conv_ln_b[i])
            b3 = lambda a: a.reshape(bsz, s, -1)
            y_b = _gla(b3(q), b3(k), b3(v), b3(gl), b3(r), gla_norm_g[i])
            w_out = ab_w_out[i].astype(BF16)
            cc = y_a.shape[-1]
            acts = [y_a.reshape(t, cc), y_b.reshape(t, -1)]
            weights = [w_out[:cc], w_out[cc:]]
        else:
            qc, kc, vv = _mla_in(x2, sc_m, sh_m, positions.reshape(t, 1), mla_w_in[i], mla_q_norm_g[i],
                                 mla_kv_norm_g[i], mla_w_uq[i], mla_w_ukv[i], tiles_per_b)
            acts = [_attn(qc, kc, vv, bsz, s)]
            weights = [mla_w_out[i].astype(BF16)]
        x2 = _proj_ln(acts, weights, x2, g_m, ln_mix_g[layer], ln_mix_b[layer], tiles_per_b, alpha)
        x2 = _moe_block(x2, sc_f, sh_f, g_f, ln_ffn_g[layer], ln_ffn_b[layer], moe_w_group[layer], moe_b_group[layer],
                        moe_w_router[layer], moe_b_router[layer], moe_w_gate, moe_w_up, moe_w_down,
                        layer, tiles_per_b, alpha)
    return x2.reshape(bsz, s, d)
```

```python
import functools

import jax
import jax.numpy as jnp
from jax import lax
from jax.experimental import pallas as pl
from jax.experimental.pallas import tpu as pltpu

F32 = jnp.float32
BF16 = jnp.bfloat16
I32 = jnp.int32
HIGHEST = lax.Precision.HIGHEST

LN_EPS = 1e-5
RMS_EPS = 1e-6
CONV_WIDTH = 31
GLA_HEADS = 4
GLA_GATE_TAU = 16.0
MLA_HEADS = 8
MLA_NOPE = 128
MLA_ROPE = 64
MLA_V = 128
ROPE_THETA = 10000.0
MOE_GROUPS = 4
MOE_EXPERTS_PER_GROUP = 8
MOE_EXPERTS = MOE_GROUPS * MOE_EXPERTS_PER_GROUP

LANES = 128
TOKEN_TILE = 512
GLA_CHUNK = 128
GLA_BLOCK = 512
CONV_ROWS = 32
CONV_HALO = 32
ATTN_TQ = 512
ATTN_TK = 256
ATTN_HEADS_PER_STEP = 4
MOE_ROW_TILE = 256
RUN_PIECE = 16
VMEM_LIMIT = 48 * 1024 * 1024


def _cparams(sem):
    return pltpu.CompilerParams(dimension_semantics=sem, vmem_limit_bytes=VMEM_LIMIT)


def _sigmoid(x):
    return 1.0 / (1.0 + jnp.exp(-x))


def _dot(a, b):
    return jnp.dot(a, b, preferred_element_type=F32)


def _dot_nt(a, b):
    return lax.dot_general(a, b, (((1,), (1,)), ((), ())), preferred_element_type=F32)


def _dot_tn(a, b):
    return lax.dot_general(a, b, (((0,), (0,)), ((), ())), preferred_element_type=F32)


def _layer_norm(y, g, b):
    mu = jnp.mean(y, axis=-1, keepdims=True)
    d = y - mu
    var = jnp.mean(d * d, axis=-1, keepdims=True)
    return d * lax.rsqrt(var + LN_EPS) * g + b


def _ada_kernel(c_ref, w_ref, b_ref, o_ref):
    c = c_ref[...]
    o_ref[0] = jnp.dot(c * _sigmoid(c), w_ref[0], precision=HIGHEST, preferred_element_type=F32) + b_ref[0]


def _ada(c, ada_w, ada_b):
    depth, d, n = ada_w.shape
    bsz = c.shape[0]
    tn = 1536
    return pl.pallas_call(
        _ada_kernel,
        out_shape=jax.ShapeDtypeStruct((depth, bsz, n), F32),
        grid=(depth, n // tn),
        in_specs=[pl.BlockSpec((bsz, d), lambda l, j: (0, 0)),
                  pl.BlockSpec((1, d, tn), lambda l, j: (l, 0, j)),
                  pl.BlockSpec((1, 1, tn), lambda l, j: (l, 0, j))],
        out_specs=pl.BlockSpec((1, bsz, tn), lambda l, j: (l, 0, j)),
        compiler_params=_cparams(("parallel", "parallel")),
        name="ada",
    )(c, ada_w, ada_b.reshape(depth, 1, n))


def _ab_in_kernel(x_ref, sc_ref, sh_ref, wc_ref, wq_ref, wk_ref, wv_ref, wr_ref, wg_ref, gw_ref, gb_ref,
                  uc_ref, q_ref, k_ref, v_ref, r_ref, gl_ref):
    h = (x_ref[...] * (1.0 + sc_ref[0]) + sh_ref[0]).astype(BF16)
    uc_ref[...] = _dot(h, wc_ref[...])
    q_ref[...] = _dot(h, wq_ref[...])
    k_ref[...] = _dot(h, wk_ref[...])
    v_ref[...] = _dot(h, wv_ref[...])
    r_ref[...] = _dot(h, wr_ref[...])
    g_low = _dot(h, wg_ref[...])
    z = jnp.dot(g_low, gw_ref[...], precision=HIGHEST, preferred_element_type=F32) + gb_ref[...]
    gl_ref[...] = (jnp.minimum(z, 0.0) - jnp.log(1.0 + jnp.exp(-jnp.abs(z)))) * (1.0 / GLA_GATE_TAU)


def _ab_in(x2, sc, sh, w_in, gate_w, gate_b, tiles_per_b):
    t, d = x2.shape
    cc2 = d
    kw = d // 4
    vw = d // 2
    rank = gate_w.shape[0]
    splits = [cc2, cc2 + kw, cc2 + 2 * kw, cc2 + 2 * kw + vw, cc2 + 2 * kw + 2 * vw]
    wb = w_in.astype(BF16)
    ws = [wb[:, :splits[0]], wb[:, splits[0]:splits[1]], wb[:, splits[1]:splits[2]],
          wb[:, splits[2]:splits[3]], wb[:, splits[3]:splits[4]], wb[:, splits[4]:]]
    tm = TOKEN_TILE
    full = lambda a: pl.BlockSpec(a.shape, lambda i: (0,) * a.ndim)
    row = lambda n: pl.BlockSpec((tm, n), lambda i: (i, 0))
    mod = pl.BlockSpec((1, 1, d), lambda i: (i // tiles_per_b, 0, 0))
    gb2 = gate_b.reshape(1, kw)
    widths = [cc2, kw, kw, vw, vw, kw]
    return pl.pallas_call(
        _ab_in_kernel,
        out_shape=[jax.ShapeDtypeStruct((t, n), F32) for n in widths],
        grid=(t // tm,),
        in_specs=[row(d), mod, mod] + [full(w) for w in ws] + [full(gate_w), full(gb2)],
        out_specs=[row(n) for n in widths],
        compiler_params=_cparams(("parallel",)),
        name="ab_in",
    )(x2, sc, sh, *ws, gate_w, gb2)


def _conv_kernel(u_ref, halo_ref, cw_ref, cb_ref, lg_ref, lb_ref, o_ref, hp_ref, *, ts, cc):
    j = pl.program_id(1)

    def glu(u):
        return u[:, :cc] * _sigmoid(u[:, cc:])

    hp_ref[0:CONV_HALO, :] = jnp.where(j > 0, glu(halo_ref[0]), 0.0)
    hp_ref[CONV_HALO:CONV_HALO + ts, :] = glu(u_ref[0])
    shift = CONV_HALO - (CONV_WIDTH - 1)
    for rb in range(ts // CONV_ROWS):
        r0 = rb * CONV_ROWS
        acc = jnp.zeros((CONV_ROWS, cc), F32)
        for tap in range(CONV_WIDTH):
            acc = acc + cw_ref[tap:tap + 1, :] * hp_ref[r0 + shift + tap:r0 + shift + tap + CONV_ROWS, :]
        y = _layer_norm(acc + cb_ref[...], lg_ref[...], lb_ref[...])
        o_ref[0, r0:r0 + CONV_ROWS, :] = (y * _sigmoid(y)).astype(o_ref.dtype)


def _conv(u3, conv_w, conv_b, ln_g, ln_b):
    bsz, s, cc2 = u3.shape
    cc = cc2 // 2
    ts = TOKEN_TILE
    hb = ts // CONV_HALO
    vec = lambda a: pl.BlockSpec((1, cc), lambda b, j: (0, 0))
    return pl.pallas_call(
        functools.partial(_conv_kernel, ts=ts, cc=cc),
        out_shape=jax.ShapeDtypeStruct((bsz, s, cc), BF16),
        grid=(bsz, s // ts),
        in_specs=[pl.BlockSpec((1, ts, cc2), lambda b, j: (b, j, 0)),
                  pl.BlockSpec((1, CONV_HALO, cc2), lambda b, j: (b, jnp.maximum(j * hb - 1, 0), 0)),
                  pl.BlockSpec((CONV_WIDTH, cc), lambda b, j: (0, 0)),
                  vec(conv_b), vec(ln_g), vec(ln_b)],
        out_specs=pl.BlockSpec((1, ts, cc), lambda b, j: (b, j, 0)),
        scratch_shapes=[pltpu.VMEM((CONV_HALO + ts, cc), F32)],
        compiler_params=_cparams(("parallel", "parallel")),
        name="conv",
    )(u3, u3, conv_w, conv_b.reshape(1, cc), ln_g.reshape(1, cc), ln_b.reshape(1, cc))


def _gla_kernel(q_ref, k_ref, v_ref, gl_ref, r_ref, ng_ref, o_ref, st_ref, *, nh, dk, dv, gc, nchunks):
    @pl.when(pl.program_id(1) == 0)
    def _():
        st_ref[...] = jnp.zeros_like(st_ref)

    row = lax.broadcasted_iota(I32, (gc, gc), 0)
    col = lax.broadcasted_iota(I32, (gc, gc), 1)
    causal = col <= row
    tri = jnp.where(causal, 1.0, 0.0).astype(F32)
    scale = dk ** -0.5

    def chunk(c, carry):
        r0 = pl.multiple_of(c * gc, gc)
        rows = pl.ds(r0, gc)
        g = gl_ref[0, rows, :]
        b = jnp.dot(tri, g, precision=HIGHEST, preferred_element_type=F32)
        b_last = b[gc - 1:gc, :]
        mid = 0.5 * b_last
        q = q_ref[0, rows, :] * scale
        k = k_ref[0, rows, :]
        v = v_ref[0, rows, :].astype(BF16)
        q_in = (q * jnp.exp(b - mid)).astype(BF16)
        k_in = (k * jnp.exp(mid - b)).astype(BF16)
        q_st = (q * jnp.exp(b)).astype(BF16)
        k_st = (k * jnp.exp(b_last - b)).astype(BF16)
        decay = jnp.exp(b_last)
        r = r_ref[0, rows, :]
        gate = r * _sigmoid(r)
        for h in range(nh):
            ks = slice(h * dk, (h + 1) * dk)
            vs = slice(h * dv, (h + 1) * dv)
            att = jnp.where(causal, _dot_nt(q_in[:, ks], k_in[:, ks]), 0.0).astype(BF16)
            st = st_ref[h]
            o = _dot(att, v[:, vs]) + _dot_nt(q_st[:, ks], st.astype(BF16))
            st_ref[h] = st * decay[:, ks] + _dot_tn(v[:, vs], k_st[:, ks])
            o = o * lax.rsqrt(jnp.mean(o * o, axis=-1, keepdims=True) + RMS_EPS) * ng_ref[:, vs]
            o_ref[0, rows, vs] = (o * gate[:, vs]).astype(o_ref.dtype)
        return carry

    lax.fori_loop(0, nchunks, chunk, 0)


def _gla(q3, k3, v3, gl3, r3, norm_g):
    bsz, s, kw = q3.shape
    vw = v3.shape[-1]
    nh = GLA_HEADS
    dk, dv = kw // nh, vw // nh
    cb = GLA_BLOCK
    gc = GLA_CHUNK
    blk = lambda n: pl.BlockSpec((1, cb, n), lambda b, j: (b, j, 0))
    return pl.pallas_call(
        functools.partial(_gla_kernel, nh=nh, dk=dk, dv=dv, gc=gc, nchunks=cb // gc),
        out_shape=jax.ShapeDtypeStruct((bsz, s, vw), BF16),
        grid=(bsz, s // cb),
        in_specs=[blk(kw), blk(kw), blk(vw), blk(kw), blk(vw), pl.BlockSpec((1, vw), lambda b, j: (0, 0))],
        out_specs=blk(vw),
        scratch_shapes=[pltpu.VMEM((nh, dv, dk), F32)],
        compiler_params=_cparams(("parallel", "arbitrary")),
        name="gla",
    )(q3, k3, v3, gl3, r3, norm_g.reshape(1, vw))


def _proj_ln_kernel(*refs, n_in, alpha):
    a_refs, w_refs = refs[:n_in], refs[n_in:2 * n_in]
    x_ref, gate_ref, lg_ref, lb_ref, o_ref = refs[2 * n_in:]
    mix = _dot(a_refs[0][...], w_refs[0][...])
    for a_ref, w_ref in zip(a_refs[1:], w_refs[1:]):
        mix = mix + _dot(a_ref[...], w_ref[...])
    y = alpha * x_ref[...] + (1.0 + gate_ref[0]) * mix
    o_ref[...] = _layer_norm(y, lg_ref[...], lb_ref[...])


def _proj_ln(acts, weights, x2, gate, ln_g, ln_b, tiles_per_b, alpha):
    t, d = x2.shape
    tm = TOKEN_TILE
    n_in = len(acts)
    row = lambda n: pl.BlockSpec((tm, n), lambda i: (i, 0))
    full = lambda a: pl.BlockSpec(a.shape, lambda i: (0,) * a.ndim)
    vec = pl.BlockSpec((1, d), lambda i: (0, 0))
    return pl.pallas_call(
        functools.partial(_proj_ln_kernel, n_in=n_in, alpha=alpha),
        out_shape=jax.ShapeDtypeStruct((t, d), F32),
        grid=(t // tm,),
        in_specs=[row(a.shape[1]) for a in acts] + [full(w) for w in weights]
                 + [row(d), pl.BlockSpec((1, 1, d), lambda i: (i // tiles_per_b, 0, 0)), vec, vec],
        out_specs=row(d),
        compiler_params=_cparams(("parallel",)),
        name="proj_ln",
    )(*acts, *weights, x2, gate, ln_g.reshape(1, d), ln_b.reshape(1, d))


def _mla_in_kernel(x_ref, sc_ref, sh_ref, pos_ref, invf_ref, sign_ref, win_ref, gq_ref, gkv_ref,
                   wqa_ref, wqb_ref, wk_ref, wvt_ref, q_ref, k_ref, vt_ref, *, nh, q_lora, kv_lora, scale, tk):
    h = (x_ref[...] * (1.0 + sc_ref[0]) + sh_ref[0]).astype(BF16)
    u = _dot(h, win_ref[...])
    cq = u[:, :q_lora]
    ckv = u[:, q_lora:q_lora + kv_lora]
    kr = u[:, q_lora + kv_lora:q_lora + kv_lora + LANES]
    kr_sw = u[:, q_lora + kv_lora + LANES:]
    cqn = (cq * lax.rsqrt(jnp.mean(cq * cq, axis=-1, keepdims=True) + RMS_EPS) * gq_ref[...]).astype(BF16)
    kvn = (ckv * lax.rsqrt(jnp.mean(ckv * ckv, axis=-1, keepdims=True) + RMS_EPS) * gkv_ref[...]).astype(BF16)
    ang = pos_ref[...].astype(F32) * invf_ref[...]
    cos = jnp.cos(ang)
    sin = jnp.sin(ang) * sign_ref[...]
    kr_rot = (kr * cos + kr_sw * sin).astype(BF16)
    qa = _dot(cqn, wqa_ref[...])
    qb = _dot(cqn, wqb_ref[...])
    kv = _dot(kvn, wk_ref[...])
    hw = 2 * LANES
    for hd in range(nh):
        q_ref[:, hd * hw:hd * hw + LANES] = (qa[:, hd * hw:hd * hw + LANES] * scale).astype(BF16)
        rope = qa[:, hd * hw + LANES:(hd + 1) * hw] * cos + qb[:, hd * LANES:(hd + 1) * LANES] * sin
        q_ref[:, hd * hw + LANES:(hd + 1) * hw] = (rope * scale).astype(BF16)
        k_ref[:, hd * hw:hd * hw + LANES] = kv[:, hd * LANES:(hd + 1) * LANES].astype(BF16)
        k_ref[:, hd * hw + LANES:(hd + 1) * hw] = kr_rot
    vt = _dot_nt(wvt_ref[...], kvn).astype(BF16)
    for c in range(vt.shape[1] // tk):
        vt_ref[0, c] = vt[:, c * tk:(c + 1) * tk]


def _mla_in(x2, sc, sh, pos2, w_in, gq, gkv, w_uq, w_ukv, tiles_per_b):
    t, d = x2.shape
    nh = MLA_HEADS
    q_lora, kv_lora = gq.shape[0], gkv.shape[0]
    half = MLA_ROPE // 2
    pad = LANES - MLA_ROPE
    kr_w = w_in[:, q_lora + kv_lora:]
    kr_sw = jnp.concatenate([kr_w[:, half:], kr_w[:, :half]], axis=1)
    zpad = jnp.zeros((d, pad), w_in.dtype)
    win_ext = jnp.concatenate([w_in[:, :q_lora + kv_lora], kr_w, zpad, kr_sw, zpad], axis=1).astype(BF16)
    wq = w_uq.reshape(q_lora, nh, MLA_NOPE + MLA_ROPE)
    q_nope, q_rope = wq[:, :, :MLA_NOPE], wq[:, :, MLA_NOPE:]
    q_rope_sw = jnp.concatenate([q_rope[:, :, half:], q_rope[:, :, :half]], axis=2)
    zq = jnp.zeros((q_lora, nh, pad), w_uq.dtype)
    wqa = jnp.concatenate([q_nope, q_rope, zq], axis=2).reshape(q_lora, nh * 2 * LANES).astype(BF16)
    wqb = jnp.concatenate([q_rope_sw, zq], axis=2).reshape(q_lora, nh * LANES).astype(BF16)
    wkv = w_ukv.reshape(kv_lora, nh, MLA_NOPE + MLA_V)
    wk = wkv[:, :, :MLA_NOPE].reshape(kv_lora, nh * MLA_NOPE).astype(BF16)
    wvt = wkv[:, :, MLA_NOPE:].reshape(kv_lora, nh * MLA_V).T.astype(BF16)
    inv_freq = 1.0 / (ROPE_THETA ** (jnp.arange(0, MLA_ROPE, 2, dtype=F32) / MLA_ROPE))
    invf = jnp.concatenate([inv_freq, inv_freq, jnp.zeros((pad,), F32)]).reshape(1, LANES)
    sign = jnp.concatenate([-jnp.ones((half,), F32), jnp.ones((half,), F32), jnp.zeros((pad,), F32)]).reshape(1, LANES)
    tm = TOKEN_TILE
    full = lambda a: pl.BlockSpec(a.shape, lambda i: (0,) * a.ndim)
    row = lambda n: pl.BlockSpec((tm, n), lambda i: (i, 0))
    mod = pl.BlockSpec((1, 1, d), lambda i: (i // tiles_per_b, 0, 0))
    gq2, gkv2 = gq.reshape(1, q_lora), gkv.reshape(1, kv_lora)
    scale = (MLA_NOPE + MLA_ROPE) ** -0.5
    tk = ATTN_TK
    kt_per_tile = tm // tk
    s = tiles_per_b * tm
    return pl.pallas_call(
        functools.partial(_mla_in_kernel, nh=nh, q_lora=q_lora, kv_lora=kv_lora, scale=scale, tk=tk),
        out_shape=[jax.ShapeDtypeStruct((t, nh * 2 * LANES), BF16), jax.ShapeDtypeStruct((t, nh * 2 * LANES), BF16),
                   jax.ShapeDtypeStruct((t // s, s // tk, nh * MLA_V, tk), BF16)],
        grid=(t // tm,),
        in_specs=[row(d), mod, mod, row(1), full(invf), full(sign), full(win_ext), full(gq2), full(gkv2),
                  full(wqa), full(wqb), full(wk), full(wvt)],
        out_specs=[row(nh * 2 * LANES), row(nh * 2 * LANES),
                   pl.BlockSpec((1, kt_per_tile, nh * MLA_V, tk),
                                lambda i: (i // tiles_per_b, i % tiles_per_b, 0, 0))],
        compiler_params=_cparams(("parallel",)),
        name="mla_in",
    )(x2, sc, sh, pos2, invf, sign, win_ext, gq2, gkv2, wqa, wqb, wk, wvt)


def _attn_kernel(q_ref, k_ref, vt_ref, o_ref, acc_ref, *, s, tq, tk, hb):
    kpq = tq // tk
    hw = 2 * LANES
    ones = jnp.ones((8, tk), BF16)

    def q_block(qi, carry):
        qrows = pl.ds(pl.multiple_of(qi * tq, tq), tq)
        acc_ref[...] = jnp.zeros_like(acc_ref)

        def tile(j, stats, masked):
            krows = pl.ds(pl.multiple_of(j * tk, tk), tk)
            sts = [_dot_nt(k_ref[krows, h * hw:(h + 1) * hw], q_ref[qrows, h * hw:(h + 1) * hw])
                   for h in range(hb)]
            ps, alphas, out = [], [], []
            for h in range(hb):
                m, st = stats[2 * h], sts[h]
                if masked:
                    key = j * tk + lax.broadcasted_iota(I32, (tk, tq), 0)
                    qry = qi * tq + lax.broadcasted_iota(I32, (tk, tq), 1)
                    st = jnp.where(key <= qry, st, -jnp.inf)
                m_new = jnp.maximum(m, jnp.max(st, axis=0, keepdims=True))
                ps.append(jnp.exp(st - m_new).astype(BF16))
                alphas.append(jnp.exp(m - m_new))
                out.append(m_new)
            for h in range(hb):
                acc_ref[h] = alphas[h] * acc_ref[h] + _dot(vt_ref[0, j, h * MLA_V:(h + 1) * MLA_V, :], ps[h])
                out.insert(2 * h + 1, alphas[h] * stats[2 * h + 1] + _dot(ones, ps[h])[0:1])
            return tuple(out)

        stats = (jnp.full((1, tq), -jnp.inf, F32), jnp.zeros((1, tq), F32)) * hb
        stats = lax.fori_loop(0, qi * kpq, lambda j, c: tile(j, c, False), stats)
        stats = lax.fori_loop(qi * kpq, (qi + 1) * kpq, lambda j, c: tile(j, c, True), stats)
        for h in range(hb):
            o_ref[qrows, h * MLA_V:(h + 1) * MLA_V] = (acc_ref[h] / stats[2 * h + 1]).T.astype(o_ref.dtype)
        return carry

    lax.fori_loop(0, s // tq, q_block, 0)


def _attn(q, k, vt, bsz, s):
    nh = MLA_HEADS
    tq, tk = ATTN_TQ, ATTN_TK
    hb = ATTN_HEADS_PER_STEP
    hw = 2 * LANES
    return pl.pallas_call(
        functools.partial(_attn_kernel, s=s, tq=tq, tk=tk, hb=hb),
        out_shape=jax.ShapeDtypeStruct((bsz * s, nh * MLA_V), BF16),
        grid=(bsz, nh // hb),
        in_specs=[pl.BlockSpec((s, hb * hw), lambda b, h: (b, h)),
                  pl.BlockSpec((s, hb * hw), lambda b, h: (b, h)),
                  pl.BlockSpec((1, s // tk, hb * MLA_V, tk), lambda b, h: (b, 0, h, 0))],
        out_specs=pl.BlockSpec((s, hb * MLA_V), lambda b, h: (b, h)),
        scratch_shapes=[pltpu.VMEM((hb, MLA_V, tq), F32)],
        compiler_params=_cparams(("parallel", "parallel")),
        name="attn",
    )(q, k, vt)


def _route_kernel(x_ref, sc_ref, sh_ref, w_ref, b_ref, lp_ref, wt_ref, tm_ref, offs_ref, te_ref, na_ref,
                  upper_ref, carry_ref, *, tm, ne, ng, row_tile, nt_pad):
    i = pl.program_id(0)
    epg = ne // ng

    @pl.when(i == 0)
    def _():
        r = lax.broadcasted_iota(I32, (tm, tm), 0)
        c = lax.broadcasted_iota(I32, (tm, tm), 1)
        upper_ref[...] = jnp.where(r < c, 1.0, 0.0).astype(BF16)
        carry_ref[...] = jnp.zeros_like(carry_ref)

    h = x_ref[...] * (1.0 + sc_ref[0]) + sh_ref[0]
    logits = jnp.dot(h, w_ref[...], precision=HIGHEST, preferred_element_type=F32) + b_ref[...]
    lt = logits.T
    lr = lt[0:ne]
    grow = lax.broadcasted_iota(I32, (8, tm), 0).astype(F32)
    lg = jnp.where(grow < ng, lt[ne:ne + 8], -jnp.inf)
    gmax = jnp.max(lg, axis=0, keepdims=True)
    g_idx = jnp.min(jnp.where(lg == gmax, grow, 1e9), axis=0, keepdims=True)
    g_w = 1.0 / jnp.sum(jnp.exp(lg - gmax), axis=0, keepdims=True)
    erow = lax.broadcasted_iota(I32, (ne, tm), 0).astype(F32)
    in_group = jnp.floor(erow * (1.0 / epg)) == g_idx
    sel = jnp.where(in_group, lr, -jnp.inf)
    v1 = jnp.max(sel, axis=0, keepdims=True)
    i1 = jnp.min(jnp.where(sel == v1, erow, 1e9), axis=0, keepdims=True)
    sel2 = jnp.where(erow == i1, -jnp.inf, sel)
    v2 = jnp.max(sel2, axis=0, keepdims=True)
    i2 = jnp.min(jnp.where(sel2 == v2, erow, 1e9), axis=0, keepdims=True)
    t = jnp.exp(v2 - v1)
    w1 = g_w / (1.0 + t)
    w2 = g_w * t / (1.0 + t)
    oh1 = erow == i1
    oh2 = erow == i2
    member = jnp.where(oh1 | oh2, 1.0, 0.0)
    lcnt = jnp.sum(member, axis=1, keepdims=True)
    er_ = lax.broadcasted_iota(I32, (ne, ne), 0)
    ec_ = lax.broadcasted_iota(I32, (ne, ne), 1)
    lstart = jnp.dot(jnp.where(ec_ < er_, 1.0, 0.0).astype(F32), jnp.broadcast_to(lcnt, (ne, LANES)),
                     precision=HIGHEST, preferred_element_type=F32)[:, 0:1]
    lrank = _dot(member.astype(BF16), upper_ref[...]) + lstart
    p1 = jnp.sum(jnp.where(oh1, lrank, 0.0), axis=0, keepdims=True)
    p2 = jnp.sum(jnp.where(oh2, lrank, 0.0), axis=0, keepdims=True)

    orow = lax.broadcasted_iota(I32, (8, tm), 0)
    lp_ref[...] = jnp.where(orow == 0, p1, jnp.where(orow == 1, p2, 0.0)).astype(I32)
    wrow = lax.broadcasted_iota(I32, (LANES, tm), 0)
    wt_ref[...] = jnp.where(wrow == 0, w1, jnp.where(wrow == 1, w2,
                                                     jnp.where(wrow == 2, p1, jnp.where(wrow == 3, p2, 0.0)))).T
    mr = lax.broadcasted_iota(I32, (ne, LANES), 0)
    mc = lax.broadcasted_iota(I32, (ne, LANES), 1)
    to_row = lambda col: jnp.sum(jnp.where(mr == mc, col, 0.0), axis=0, keepdims=True)
    trow = lax.broadcasted_iota(I32, (8, LANES), 0)
    tm_ref[...] = jnp.where(trow == 0, to_row(lstart), jnp.where(trow == 1, to_row(lcnt),
                            jnp.where(trow == 2, to_row(carry_ref[...]), 0.0))).astype(I32)
    carry_ref[...] = carry_ref[...] + lcnt

    @pl.when(i == pl.num_programs(0) - 1)
    def _():
        cnt = carry_ref[...]
        ntl = jnp.floor((cnt + (row_tile - 1)) * (1.0 / row_tile))
        incl = jnp.where(ec_ <= er_, 1.0, 0.0).astype(F32)
        ends = jnp.dot(incl, jnp.broadcast_to(ntl, (ne, LANES)), precision=HIGHEST,
                       preferred_element_type=F32)
        starts = ends - ntl
        offs_ref[...] = jnp.concatenate([starts * row_tile, ends * row_tile, jnp.broadcast_to(ntl, (ne, LANES)),
                                         jnp.zeros((8, LANES), F32)], axis=0).astype(I32)
        tile = lax.broadcasted_iota(I32, (ne, nt_pad), 1).astype(F32)
        te = jnp.sum(jnp.where(ends[:, 0:1] <= tile, 1.0, 0.0), axis=0, keepdims=True)
        te_ref[...] = jnp.broadcast_to(jnp.minimum(te, ne - 1.0), (8, nt_pad)).astype(I32)
        na_ref[...] = jnp.broadcast_to(ends[ne - 1:ne, :], (8, LANES)).astype(I32)


def _moe_tiles(t):
    nt_max = (2 * t) // MOE_ROW_TILE + MOE_EXPERTS
    nt_pad = -(-nt_max // LANES) * LANES
    return nt_max, nt_pad


def _route(x2, sc, sh, w_group, b_group, w_router, b_router, tiles_per_b):
    t, d = x2.shape
    ne, ng = MOE_EXPERTS, MOE_GROUPS
    tm = TOKEN_TILE
    _, nt_pad = _moe_tiles(t)
    wcat = jnp.concatenate([w_router, w_group, jnp.zeros((d, LANES - ne - ng), F32)], axis=1)
    bcat = jnp.concatenate([b_router, b_group, jnp.zeros((LANES - ne - ng,), F32)]).reshape(1, LANES)
    const = lambda shp: pl.BlockSpec(shp, lambda i: (0,) * len(shp))
    return pl.pallas_call(
        functools.partial(_route_kernel, tm=tm, ne=ne, ng=ng, row_tile=MOE_ROW_TILE, nt_pad=nt_pad),
        out_shape=[jax.ShapeDtypeStruct((8, t), I32), jax.ShapeDtypeStruct((t, LANES), F32),
                   jax.ShapeDtypeStruct((8 * (t // tm), LANES), I32),
                   jax.ShapeDtypeStruct((3 * ne + 8, LANES), I32), jax.ShapeDtypeStruct((8, nt_pad), I32),
                   jax.ShapeDtypeStruct((8, LANES), I32)],
        grid=(t // tm,),
        in_specs=[pl.BlockSpec((tm, d), lambda i: (i, 0)),
                  pl.BlockSpec((1, 1, d), lambda i: (i // tiles_per_b, 0, 0)),
                  pl.BlockSpec((1, 1, d), lambda i: (i // tiles_per_b, 0, 0)),
                  const((d, LANES)), const((1, LANES))],
        out_specs=[pl.BlockSpec((8, tm), lambda i: (0, i)), pl.BlockSpec((tm, LANES), lambda i: (i, 0)),
                   pl.BlockSpec((8, LANES), lambda i: (i, 0)),
                   const((3 * ne + 8, LANES)), const((8, nt_pad)), const((8, LANES))],
        scratch_shapes=[pltpu.VMEM((tm, tm), BF16), pltpu.VMEM((ne, 1), F32)],
        compiler_params=_cparams(("arbitrary",)),
        name="moe_route",
    )(x2, sc, sh, wcat, bcat)


def _rows_to_tiles(ref, val):
    n, d = val.shape
    nchunk = d // LANES
    for s in range(nchunk):
        ref[pl.ds(s, n, stride=nchunk), :] = val[:, s * LANES:(s + 1) * LANES]


def _tiles_to_rows(ref, n, d):
    nchunk = d // LANES
    return jnp.concatenate([ref[pl.ds(s, n, stride=nchunk), :] for s in range(nchunk)], axis=1)


def _dispatch_kernel(offs_ref, gend_ref, ntl_ref, x_ref, sc_ref, sh_ref, lp_ref, tm_ref, xs_ref,
                     h_ref, z_ref, idx_ref, sem_i, sem_r, *, tm, ne, row_tile, nchunk, nt_max):
    i = pl.program_id(0)

    @pl.when(i == 0)
    def _():
        z_ref[...] = jnp.zeros_like(z_ref)
        for e in range(ne):
            @pl.when(ntl_ref[e] > 0)
            def _():
                start = pl.multiple_of((gend_ref[e] - row_tile) * nchunk, row_tile * nchunk)
                cp = pltpu.make_async_copy(z_ref, xs_ref.at[pl.ds(start, row_tile * nchunk), :], sem_r)
                cp.start()
                cp.wait()
        for back in range(1, ne + 1):
            @pl.when(nt_max - back >= gend_ref[ne - 1] // row_tile)
            def _():
                start = (nt_max - back) * row_tile * nchunk
                cp = pltpu.make_async_copy(z_ref, xs_ref.at[pl.ds(start, row_tile * nchunk), :], sem_r)
                cp.start()
                cp.wait()

    @pl.when(i > 0)
    def _():
        cp = pltpu.make_async_copy(tm_ref, idx_ref, sem_i)
        cp.start()
        h = (x_ref[...] * (1.0 + sc_ref[0]) + sh_ref[0]).astype(BF16)
        slot = lax.broadcasted_iota(I32, (2 * tm, tm), 0)
        lp = lp_ref[...]
        onehot = jnp.where((slot == lp[0:1, :]) | (slot == lp[1:2, :]), 1.0, 0.0).astype(BF16)
        _rows_to_tiles(h_ref, _dot(onehot, h))
        cp.wait()

        def expert_run(e, carry):
            def copy(off, rows):
                src = pl.multiple_of((idx_ref[0, e] + off) * nchunk, nchunk)
                dst = pl.multiple_of((offs_ref[e] + idx_ref[2, e] + off) * nchunk, nchunk)
                return pltpu.make_async_copy(h_ref.at[pl.ds(src, rows * nchunk), :],
                                             xs_ref.at[pl.ds(dst, rows * nchunk), :], sem_r)
            _start_run(idx_ref[1, e], copy)
            return carry

        lax.fori_loop(0, ne, expert_run, 0)
        pltpu.make_async_copy(h_ref, xs_ref.at[pl.ds(0, 2 * tm * nchunk), :], sem_r).wait()


def _start_run(n, copy):
    nfull = n // RUN_PIECE

    def piece(c, carry):
        copy(c * RUN_PIECE, RUN_PIECE).start()
        return carry

    lax.fori_loop(0, nfull, piece, 0)
    off = nfull * RUN_PIECE
    p = RUN_PIECE // 2
    while p >= 1:
        @pl.when((n & p) != 0)
        def _():
            copy(off, p).start()
        off = off + (n & p)
        p //= 2


def _dispatch(x2, sc, sh, lp, tmeta, offs, gend, ntl, tiles_per_b):
    t, d = x2.shape
    tm = TOKEN_TILE
    nt_max, _ = _moe_tiles(t)
    nchunk = d // LANES
    prev = lambda i: jnp.maximum(i - 1, 0)
    grid_spec = pltpu.PrefetchScalarGridSpec(
        num_scalar_prefetch=3,
        grid=(t // tm + 1,),
        in_specs=[pl.BlockSpec((tm, d), lambda i, *_: (prev(i), 0)),
                  pl.BlockSpec((1, 1, d), lambda i, *_: (prev(i) // tiles_per_b, 0, 0)),
                  pl.BlockSpec((1, 1, d), lambda i, *_: (prev(i) // tiles_per_b, 0, 0)),
                  pl.BlockSpec((8, tm), lambda i, *_: (0, prev(i))),
                  pl.BlockSpec((8, LANES), lambda i, *_: (prev(i), 0))],
        out_specs=pl.BlockSpec(memory_space=pl.ANY),
        scratch_shapes=[pltpu.VMEM((2 * tm * nchunk, LANES), F32), pltpu.VMEM((MOE_ROW_TILE * nchunk, LANES), F32),
                        pltpu.SMEM((8, LANES), I32), pltpu.SemaphoreType.DMA, pltpu.SemaphoreType.DMA])
    return pl.pallas_call(
        functools.partial(_dispatch_kernel, tm=tm, ne=MOE_EXPERTS, row_tile=MOE_ROW_TILE, nchunk=nchunk,
                          nt_max=nt_max),
        out_shape=jax.ShapeDtypeStruct((nt_max * MOE_ROW_TILE * nchunk, LANES), F32),
        grid_spec=grid_spec,
        compiler_params=_cparams(("arbitrary",)),
        name="moe_dispatch",
    )(offs, gend, ntl, x2, sc, sh, lp, tmeta)


def _gmm_kernel(te_ref, na_ref, xs_ref, wg_ref, wu_ref, wd_ref, ys_ref, wgu_buf, wd_buf, *, ff, tr, d):
    j = pl.program_id(0)

    @pl.when(j < na_ref[0])
    def _():
        changed = (j == 0) | (te_ref[j] != te_ref[jnp.maximum(j - 1, 0)])

        @pl.when(changed)
        def _():
            wgu_buf[:, :ff] = wg_ref[0, 0].astype(BF16)
            wgu_buf[:, ff:] = wu_ref[0, 0].astype(BF16)
            wd_buf[...] = wd_ref[0, 0].astype(BF16)

        gu = _dot(_tiles_to_rows(xs_ref, tr, d).astype(BF16), wgu_buf[...])
        g, u = gu[:, :ff], gu[:, ff:]
        hid = (g * _sigmoid(g) * u).astype(BF16)
        _rows_to_tiles(ys_ref, _dot(hid, wd_buf[...]))

    @pl.when(j >= na_ref[0])
    def _():
        ys_ref[...] = jnp.zeros_like(ys_ref)


def _gmm(xs, te, na, w_gate, w_up, w_down, layer):
    d, ff = w_gate.shape[-2:]
    nchunk = d // LANES
    tr = MOE_ROW_TILE
    ns = xs.shape[0] // nchunk
    act = lambda j, te_ref, na_ref: jnp.minimum(j, na_ref[0] - 1)
    grid_spec = pltpu.PrefetchScalarGridSpec(
        num_scalar_prefetch=2,
        grid=(ns // tr,),
        in_specs=[pl.BlockSpec((tr * nchunk, LANES), lambda j, te_ref, na_ref: (act(j, te_ref, na_ref), 0)),
                  pl.BlockSpec((1, 1, d, ff), lambda j, te_ref, na_ref: (layer, te_ref[act(j, te_ref, na_ref)], 0, 0)),
                  pl.BlockSpec((1, 1, d, ff), lambda j, te_ref, na_ref: (layer, te_ref[act(j, te_ref, na_ref)], 0, 0)),
                  pl.BlockSpec((1, 1, ff, d), lambda j, te_ref, na_ref: (layer, te_ref[act(j, te_ref, na_ref)], 0, 0))],
        out_specs=pl.BlockSpec((tr * nchunk, LANES), lambda j, te_ref, na_ref: (j, 0)),
        scratch_shapes=[pltpu.VMEM((d, 2 * ff), BF16), pltpu.VMEM((ff, d), BF16)])
    return pl.pallas_call(
        functools.partial(_gmm_kernel, ff=ff, tr=tr, d=d),
        out_shape=jax.ShapeDtypeStruct((ns * nchunk, LANES), F32),
        grid_spec=grid_spec,
        compiler_params=_cparams(("arbitrary",)),
        name="moe_gmm",
    )(te, na, xs, w_gate, w_up, w_down)


def _combine_kernel(offs_ref, tm_ref, wt_ref, x_ref, gate_ref, lg_ref, lb_ref, ys_ref, o_ref,
                    buf_ref, idx_ref, sem_i, sem_r, *, tm, ne, d, alpha):
    nchunk = d // LANES
    cp = pltpu.make_async_copy(tm_ref, idx_ref, sem_i)
    cp.start()
    cp.wait()

    def expert_run(e, carry):
        def copy(off, rows):
            src = pl.multiple_of((offs_ref[e] + idx_ref[2, e] + off) * nchunk, nchunk)
            dst = pl.multiple_of((idx_ref[0, e] + off) * nchunk, nchunk)
            return pltpu.make_async_copy(ys_ref.at[pl.ds(src, rows * nchunk), :],
                                         buf_ref.at[pl.ds(dst, rows * nchunk), :], sem_r)
        _start_run(idx_ref[1, e], copy)
        return carry

    lax.fori_loop(0, ne, expert_run, 0)
    pltpu.make_async_copy(ys_ref.at[pl.ds(0, 2 * tm * nchunk), :], buf_ref, sem_r).wait()
    w = wt_ref[...]
    rows = _tiles_to_rows(buf_ref, 2 * tm, d).astype(BF16)
    slot = lax.broadcasted_iota(I32, (tm, 2 * tm), 1).astype(F32)
    pick = lambda k: _dot(jnp.where(slot == w[:, 2 + k:3 + k], 1.0, 0.0).astype(BF16), rows)
    ffn = w[:, 0:1] * pick(0) + w[:, 1:2] * pick(1)
    y = alpha * x_ref[...] + (1.0 + gate_ref[0]) * ffn
    o_ref[...] = _layer_norm(y, lg_ref[...], lb_ref[...])


def _combine(ys, tmeta, wt, x2, gate, ln_g, ln_b, offs, tiles_per_b, alpha):
    t, d = x2.shape
    tm = TOKEN_TILE
    vec = pl.BlockSpec((1, d), lambda i, *_: (0, 0))
    grid_spec = pltpu.PrefetchScalarGridSpec(
        num_scalar_prefetch=1,
        grid=(t // tm,),
        in_specs=[pl.BlockSpec((8, LANES), lambda i, *_: (i, 0)),
                  pl.BlockSpec((tm, LANES), lambda i, *_: (i, 0)),
                  pl.BlockSpec((tm, d), lambda i, *_: (i, 0)),
                  pl.BlockSpec((1, 1, d), lambda i, *_: (i // tiles_per_b, 0, 0)),
                  vec, vec,
                  pl.BlockSpec(memory_space=pl.ANY)],
        out_specs=pl.BlockSpec((tm, d), lambda i, *_: (i, 0)),
        scratch_shapes=[pltpu.VMEM((2 * tm * d // LANES, LANES), F32), pltpu.SMEM((8, LANES), I32),
                        pltpu.SemaphoreType.DMA, pltpu.SemaphoreType.DMA])
    return pl.pallas_call(
        functools.partial(_combine_kernel, tm=tm, ne=MOE_EXPERTS, d=d, alpha=alpha),
        out_shape=jax.ShapeDtypeStruct((t, d), F32),
        grid_spec=grid_spec,
        compiler_params=_cparams(("arbitrary",)),
        name="moe_combine",
    )(offs, tmeta, wt, x2, gate, ln_g.reshape(1, d), ln_b.reshape(1, d), ys)


def _moe_block(x2, sc, sh, gate, ln_g, ln_b, w_group, b_group, w_router, b_router, w_gate, w_up, w_down,
               layer, tiles_per_b, alpha):
    ne = MOE_EXPERTS
    nt_max, _ = _moe_tiles(x2.shape[0])
    lp, wt, tmeta, meta, te, na = _route(x2, sc, sh, w_group, b_group, w_router, b_router, tiles_per_b)
    offs, gend, ntl = meta[:ne, 0], meta[ne:2 * ne, 0], meta[2 * ne:3 * ne, 0]
    xs = _dispatch(x2, sc, sh, lp, tmeta, offs, gend, ntl, tiles_per_b)
    ys = _gmm(xs, te[0, :nt_max], na[0, :1], w_gate, w_up, w_down, layer)
    return _combine(ys, tmeta, wt, x2, gate, ln_g, ln_b, offs, tiles_per_b, alpha)


def kernel(x, c, positions, ada_w, ada_b, ln_mix_g, ln_mix_b, ln_ffn_g, ln_ffn_b, ab_w_in, conv_w, conv_b, conv_ln_g, conv_ln_b, gla_gate_w, gla_gate_b, gla_norm_g, ab_w_out, mla_w_in, mla_q_norm_g, mla_kv_norm_g, mla_w_uq, mla_w_ukv, mla_w_out, moe_w_group, moe_b_group, moe_w_router, moe_b_router, moe_w_gate, moe_w_up, moe_w_down):
    bsz, s, d = x.shape
    depth = ada_w.shape[0]
    t = bsz * s
    tiles_per_b = s // TOKEN_TILE
    alpha = (2 * depth) ** 0.25
    mod = _ada(c, ada_w, ada_b).reshape(depth, bsz, 6, 1, d)
    x2 = x.reshape(t, d)
    for layer in range(depth):
        sh_m, sc_m, g_m, sh_f, sc_f, g_f = (mod[layer, :, n] for n in range(6))
        i = layer // 2
        if layer % 2 == 0:
            uc, q, k, v, r, gl = _ab_in(x2, sc_m, sh_m, ab_w_in[i], gla_gate_w[i], gla_gate_b[i], tiles_per_b)
            y_a = _conv(uc.reshape(bsz, s, -1), conv_w[i], conv_b[i], conv_ln_g[i], conv_ln_b[i])
            b3 = lambda a: a.reshape(bsz, s, -1)
            y_b = _gla(b3(q), b3(k), b3(v), b3(gl), b3(r), gla_norm_g[i])
            w_out = ab_w_out[i].astype(BF16)
            cc = y_a.shape[-1]
            acts = [y_a.reshape(t, cc), y_b.reshape(t, -1)]
            weights = [w_out[:cc], w_out[cc:]]
        else:
            qc, kc, vv = _mla_in(x2, sc_m, sh_m, positions.reshape(t, 1), mla_w_in[i], mla_q_norm_g[i],
                                 mla_kv_norm_g[i], mla_w_uq[i], mla_w_ukv[i], tiles_per_b)
            acts = [_attn(qc, kc, vv, bsz, s)]
            weights = [mla_w_out[i].astype(BF16)]
        x2 = _proj_ln(acts, weights, x2, g_m, ln_mix_g[layer], ln_mix_b[layer], tiles_per_b, alpha)
        x2 = _moe_block(x2, sc_f, sh_f, g_f, ln_ffn_g[layer], ln_ffn_b[layer], moe_w_group[layer], moe_b_group[layer],
                        moe_w_router[layer], moe_b_router[layer], moe_w_gate, moe_w_up, moe_w_down,
                        layer, tiles_per_b, alpha)
    return x2.reshape(bsz, s, d)
```

```python
import functools

import jax
import jax.numpy as jnp
from jax import lax
from jax.experimental import pallas as pl
from jax.experimental.pallas import tpu as pltpu

F32 = jnp.float32
BF16 = jnp.bfloat16
I32 = jnp.int32
HIGHEST = lax.Precision.HIGHEST

LN_EPS = 1e-5
RMS_EPS = 1e-6
CONV_WIDTH = 31
GLA_HEADS = 4
GLA_GATE_TAU = 16.0
MLA_HEADS = 8
MLA_NOPE = 128
MLA_ROPE = 64
MLA_V = 128
ROPE_THETA = 10000.0
MOE_GROUPS = 4
MOE_EXPERTS_PER_GROUP = 8
MOE_EXPERTS = MOE_GROUPS * MOE_EXPERTS_PER_GROUP

LANES = 128
SUBLANES = 8
TOKEN_TILE = 512
GLA_CHUNK = 128
GLA_BLOCK = 512
CONV_ROWS = 32
CONV_HALO = 32
ATTN_TQ = 512
ATTN_TK = 256
ATTN_HEADS_PER_STEP = 4
MOE_ROW_TILE = 256
RUN_PIECE = 32
VMEM_LIMIT = 48 * 1024 * 1024


def _cparams(sem):
    return pltpu.CompilerParams(dimension_semantics=sem, vmem_limit_bytes=VMEM_LIMIT)


def _sigmoid(x):
    return 1.0 / (1.0 + jnp.exp(-x))


def _dot(a, b):
    return jnp.dot(a, b, preferred_element_type=F32)


def _dot_nt(a, b):
    return lax.dot_general(a, b, (((1,), (1,)), ((), ())), preferred_element_type=F32)


def _dot_tn(a, b):
    return lax.dot_general(a, b, (((0,), (0,)), ((), ())), preferred_element_type=F32)


def _layer_norm(y, g, b):
    mu = jnp.mean(y, axis=-1, keepdims=True)
    d = y - mu
    var = jnp.mean(d * d, axis=-1, keepdims=True)
    return d * lax.rsqrt(var + LN_EPS) * g + b


def _ada_kernel(c_ref, w_ref, b_ref, o_ref):
    c = c_ref[...]
    o_ref[0] = jnp.dot(c * _sigmoid(c), w_ref[0], precision=HIGHEST, preferred_element_type=F32) + b_ref[0]


def _ada(c, ada_w, ada_b):
    depth, d, n = ada_w.shape
    bsz = c.shape[0]
    tn = 1536
    return pl.pallas_call(
        _ada_kernel,
        out_shape=jax.ShapeDtypeStruct((depth, bsz, n), F32),
        grid=(depth, n // tn),
        in_specs=[pl.BlockSpec((bsz, d), lambda l, j: (0, 0)),
                  pl.BlockSpec((1, d, tn), lambda l, j: (l, 0, j)),
                  pl.BlockSpec((1, 1, tn), lambda l, j: (l, 0, j))],
        out_specs=pl.BlockSpec((1, bsz, tn), lambda l, j: (l, 0, j)),
        compiler_params=_cparams(("parallel", "parallel")),
        name="ada",
    )(c, ada_w, ada_b.reshape(depth, 1, n))


def _ab_in_kernel(x_ref, sc_ref, sh_ref, wc_ref, wq_ref, wk_ref, wv_ref, wr_ref, wg_ref, gw_ref, gb_ref,
                  uc_ref, q_ref, k_ref, v_ref, r_ref, gl_ref):
    h = (x_ref[...] * (1.0 + sc_ref[0]) + sh_ref[0]).astype(BF16)
    uc_ref[...] = _dot(h, wc_ref[...])
    q_ref[...] = _dot(h, wq_ref[...])
    k_ref[...] = _dot(h, wk_ref[...])
    v_ref[...] = _dot(h, wv_ref[...])
    r_ref[...] = _dot(h, wr_ref[...])
    g_low = _dot(h, wg_ref[...])
    z = jnp.dot(g_low, gw_ref[...], precision=HIGHEST, preferred_element_type=F32) + gb_ref[...]
    gl_ref[...] = (jnp.minimum(z, 0.0) - jnp.log(1.0 + jnp.exp(-jnp.abs(z)))) * (1.0 / GLA_GATE_TAU)


def _ab_in(x2, sc, sh, w_in, gate_w, gate_b, tiles_per_b):
    t, d = x2.shape
    cc2 = d
    kw = d // 4
    vw = d // 2
    rank = gate_w.shape[0]
    splits = [cc2, cc2 + kw, cc2 + 2 * kw, cc2 + 2 * kw + vw, cc2 + 2 * kw + 2 * vw]
    wb = w_in.astype(BF16)
    ws = [wb[:, :splits[0]], wb[:, splits[0]:splits[1]], wb[:, splits[1]:splits[2]],
          wb[:, splits[2]:splits[3]], wb[:, splits[3]:splits[4]], wb[:, splits[4]:]]
    tm = TOKEN_TILE
    full = lambda a: pl.BlockSpec(a.shape, lambda i: (0,) * a.ndim)
    row = lambda n: pl.BlockSpec((tm, n), lambda i: (i, 0))
    mod = pl.BlockSpec((1, 1, d), lambda i: (i // tiles_per_b, 0, 0))
    gb2 = gate_b.reshape(1, kw)
    widths = [cc2, kw, kw, vw, vw, kw]
    return pl.pallas_call(
        _ab_in_kernel,
        out_shape=[jax.ShapeDtypeStruct((t, n), F32) for n in widths],
        grid=(t // tm,),
        in_specs=[row(d), mod, mod] + [full(w) for w in ws] + [full(gate_w), full(gb2)],
        out_specs=[row(n) for n in widths],
        compiler_params=_cparams(("parallel",)),
        name="ab_in",
    )(x2, sc, sh, *ws, gate_w, gb2)


def _conv_kernel(u_ref, halo_ref, cw_ref, cb_ref, lg_ref, lb_ref, o_ref, hp_ref, *, ts, cc):
    j = pl.program_id(1)

    def glu(u):
        return u[:, :cc] * _sigmoid(u[:, cc:])

    hp_ref[0, 0:CONV_HALO, :] = jnp.where(j > 0, glu(halo_ref[0]), 0.0)
    hp_ref[0, CONV_HALO:CONV_HALO + ts, :] = glu(u_ref[0])
    nrow = CONV_HALO + ts
    for b in range(1, SUBLANES):
        hp_ref[b, 0:nrow - SUBLANES, :] = hp_ref[0, b:nrow - SUBLANES + b, :]
    shift = CONV_HALO - (CONV_WIDTH - 1)
    for rb in range(ts // CONV_ROWS):
        r0 = rb * CONV_ROWS
        acc = jnp.zeros((CONV_ROWS, cc), F32)
        for tap in range(CONV_WIDTH):
            lo = r0 + shift + tap
            base = lo // SUBLANES * SUBLANES
            acc = acc + cw_ref[tap:tap + 1, :] * hp_ref[lo - base, base:base + CONV_ROWS, :]
        y = _layer_norm(acc + cb_ref[...], lg_ref[...], lb_ref[...])
        o_ref[0, r0:r0 + CONV_ROWS, :] = (y * _sigmoid(y)).astype(o_ref.dtype)


def _conv(u3, conv_w, conv_b, ln_g, ln_b):
    bsz, s, cc2 = u3.shape
    cc = cc2 // 2
    ts = TOKEN_TILE
    hb = ts // CONV_HALO
    vec = lambda a: pl.BlockSpec((1, cc), lambda b, j: (0, 0))
    return pl.pallas_call(
        functools.partial(_conv_kernel, ts=ts, cc=cc),
        out_shape=jax.ShapeDtypeStruct((bsz, s, cc), BF16),
        grid=(bsz, s // ts),
        in_specs=[pl.BlockSpec((1, ts, cc2), lambda b, j: (b, j, 0)),
                  pl.BlockSpec((1, CONV_HALO, cc2), lambda b, j: (b, jnp.maximum(j * hb - 1, 0), 0)),
                  pl.BlockSpec((CONV_WIDTH, cc), lambda b, j: (0, 0)),
                  vec(conv_b), vec(ln_g), vec(ln_b)],
        out_specs=pl.BlockSpec((1, ts, cc), lambda b, j: (b, j, 0)),
        scratch_shapes=[pltpu.VMEM((SUBLANES, CONV_HALO + ts, cc), F32)],
        compiler_params=_cparams(("parallel", "parallel")),
        name="conv",
    )(u3, u3, conv_w, conv_b.reshape(1, cc), ln_g.reshape(1, cc), ln_b.reshape(1, cc))


def _gla_kernel(q_ref, k_ref, v_ref, gl_ref, r_ref, ng_ref, o_ref, st_ref, *, nh, dk, dv, gc, nchunks):
    @pl.when(pl.program_id(1) == 0)
    def _():
        st_ref[...] = jnp.zeros_like(st_ref)

    row = lax.broadcasted_iota(I32, (gc, gc), 0)
    col = lax.broadcasted_iota(I32, (gc, gc), 1)
    causal = col <= row
    tri = jnp.where(causal, 1.0, 0.0).astype(F32)
    scale = dk ** -0.5

    def chunk(c, carry):
        r0 = pl.multiple_of(c * gc, gc)
        rows = pl.ds(r0, gc)
        g = gl_ref[0, rows, :]
        b = jnp.dot(tri, g, precision=HIGHEST, preferred_element_type=F32)
        b_last = b[gc - 1:gc, :]
        mid = 0.5 * b_last
        q = q_ref[0, rows, :] * scale
        k = k_ref[0, rows, :]
        v = v_ref[0, rows, :].astype(BF16)
        q_in = (q * jnp.exp(b - mid)).astype(BF16)
        k_in = (k * jnp.exp(mid - b)).astype(BF16)
        q_st = (q * jnp.exp(b)).astype(BF16)
        k_st = (k * jnp.exp(b_last - b)).astype(BF16)
        decay = jnp.exp(b_last)
        r = r_ref[0, rows, :]
        gate = r * _sigmoid(r)
        for h in range(nh):
            ks = slice(h * dk, (h + 1) * dk)
            vs = slice(h * dv, (h + 1) * dv)
            att = jnp.where(causal, _dot_nt(q_in[:, ks], k_in[:, ks]), 0.0).astype(BF16)
            st = st_ref[h]
            o = _dot(att, v[:, vs]) + _dot_nt(q_st[:, ks], st.astype(BF16))
            st_ref[h] = st * decay[:, ks] + _dot_tn(v[:, vs], k_st[:, ks])
            o = o * lax.rsqrt(jnp.mean(o * o, axis=-1, keepdims=True) + RMS_EPS) * ng_ref[:, vs]
            o_ref[0, rows, vs] = (o * gate[:, vs]).astype(o_ref.dtype)
        return carry

    lax.fori_loop(0, nchunks, chunk, 0)


def _gla(q3, k3, v3, gl3, r3, norm_g):
    bsz, s, kw = q3.shape
    vw = v3.shape[-1]
    nh = GLA_HEADS
    dk, dv = kw // nh, vw // nh
    cb = GLA_BLOCK
    gc = GLA_CHUNK
    blk = lambda n: pl.BlockSpec((1, cb, n), lambda b, j: (b, j, 0))
    return pl.pallas_call(
        functools.partial(_gla_kernel, nh=nh, dk=dk, dv=dv, gc=gc, nchunks=cb // gc),
        out_shape=jax.ShapeDtypeStruct((bsz, s, vw), BF16),
        grid=(bsz, s // cb),
        in_specs=[blk(kw), blk(kw), blk(vw), blk(kw), blk(vw), pl.BlockSpec((1, vw), lambda b, j: (0, 0))],
        out_specs=blk(vw),
        scratch_shapes=[pltpu.VMEM((nh, dv, dk), F32)],
        compiler_params=_cparams(("parallel", "arbitrary")),
        name="gla",
    )(q3, k3, v3, gl3, r3, norm_g.reshape(1, vw))


def _proj_ln_kernel(*refs, n_in, alpha):
    a_refs, w_refs = refs[:n_in], refs[n_in:2 * n_in]
    x_ref, gate_ref, lg_ref, lb_ref, o_ref = refs[2 * n_in:]
    mix = _dot(a_refs[0][...], w_refs[0][...])
    for a_ref, w_ref in zip(a_refs[1:], w_refs[1:]):
        mix = mix + _dot(a_ref[...], w_ref[...])
    y = alpha * x_ref[...] + (1.0 + gate_ref[0]) * mix
    o_ref[...] = _layer_norm(y, lg_ref[...], lb_ref[...])


def _proj_ln(acts, weights, x2, gate, ln_g, ln_b, tiles_per_b, alpha):
    t, d = x2.shape
    tm = TOKEN_TILE
    n_in = len(acts)
    row = lambda n: pl.BlockSpec((tm, n), lambda i: (i, 0))
    full = lambda a: pl.BlockSpec(a.shape, lambda i: (0,) * a.ndim)
    vec = pl.BlockSpec((1, d), lambda i: (0, 0))
    return pl.pallas_call(
        functools.partial(_proj_ln_kernel, n_in=n_in, alpha=alpha),
        out_shape=jax.ShapeDtypeStruct((t, d), F32),
        grid=(t // tm,),
        in_specs=[row(a.shape[1]) for a in acts] + [full(w) for w in weights]
                 + [row(d), pl.BlockSpec((1, 1, d), lambda i: (i // tiles_per_b, 0, 0)), vec, vec],
        out_specs=row(d),
        compiler_params=_cparams(("parallel",)),
        name="proj_ln",
    )(*acts, *weights, x2, gate, ln_g.reshape(1, d), ln_b.reshape(1, d))


def _mla_in_kernel(x_ref, sc_ref, sh_ref, pos_ref, invf_ref, sign_ref, win_ref, gq_ref, gkv_ref,
                   wqa_ref, wqb_ref, wk_ref, wvt_ref, q_ref, k_ref, vt_ref, *, nh, q_lora, kv_lora, scale, tk):
    h = (x_ref[...] * (1.0 + sc_ref[0]) + sh_ref[0]).astype(BF16)
    u = _dot(h, win_ref[...])
    cq = u[:, :q_lora]
    ckv = u[:, q_lora:q_lora + kv_lora]
    kr = u[:, q_lora + kv_lora:q_lora + kv_lora + LANES]
    kr_sw = u[:, q_lora + kv_lora + LANES:]
    cqn = (cq * lax.rsqrt(jnp.mean(cq * cq, axis=-1, keepdims=True) + RMS_EPS) * gq_ref[...]).astype(BF16)
    kvn = (ckv * lax.rsqrt(jnp.mean(ckv * ckv, axis=-1, keepdims=True) + RMS_EPS) * gkv_ref[...]).astype(BF16)
    ang = pos_ref[...].astype(F32) * invf_ref[...]
    cos = jnp.cos(ang)
    sin = jnp.sin(ang) * sign_ref[...]
    kr_rot = (kr * cos + kr_sw * sin).astype(BF16)
    qa = _dot(cqn, wqa_ref[...])
    qb = _dot(cqn, wqb_ref[...])
    kv = _dot(kvn, wk_ref[...])
    hw = 2 * LANES
    for hd in range(nh):
        q_ref[:, hd * hw:hd * hw + LANES] = (qa[:, hd * hw:hd * hw + LANES] * scale).astype(BF16)
        rope = qa[:, hd * hw + LANES:(hd + 1) * hw] * cos + qb[:, hd * LANES:(hd + 1) * LANES] * sin
        q_ref[:, hd * hw + LANES:(hd + 1) * hw] = (rope * scale).astype(BF16)
        k_ref[:, hd * hw:hd * hw + LANES] = kv[:, hd * LANES:(hd + 1) * LANES].astype(BF16)
        k_ref[:, hd * hw + LANES:(hd + 1) * hw] = kr_rot
    vt = _dot_nt(wvt_ref[...], kvn).astype(BF16)
    for c in range(vt.shape[1] // tk):
        vt_ref[0, c] = vt[:, c * tk:(c + 1) * tk]


def _mla_in(x2, sc, sh, pos2, w_in, gq, gkv, w_uq, w_ukv, tiles_per_b):
    t, d = x2.shape
    nh = MLA_HEADS
    q_lora, kv_lora = gq.shape[0], gkv.shape[0]
    half = MLA_ROPE // 2
    pad = LANES - MLA_ROPE
    kr_w = w_in[:, q_lora + kv_lora:]
    kr_sw = jnp.concatenate([kr_w[:, half:], kr_w[:, :half]], axis=1)
    zpad = jnp.zeros((d, pad), w_in.dtype)
    win_ext = jnp.concatenate([w_in[:, :q_lora + kv_lora], kr_w, zpad, kr_sw, zpad], axis=1).astype(BF16)
    wq = w_uq.reshape(q_lora, nh, MLA_NOPE + MLA_ROPE)
    q_nope, q_rope = wq[:, :, :MLA_NOPE], wq[:, :, MLA_NOPE:]
    q_rope_sw = jnp.concatenate([q_rope[:, :, half:], q_rope[:, :, :half]], axis=2)
    zq = jnp.zeros((q_lora, nh, pad), w_uq.dtype)
    wqa = jnp.concatenate([q_nope, q_rope, zq], axis=2).reshape(q_lora, nh * 2 * LANES).astype(BF16)
    wqb = jnp.concatenate([q_rope_sw, zq], axis=2).reshape(q_lora, nh * LANES).astype(BF16)
    wkv = w_ukv.reshape(kv_lora, nh, MLA_NOPE + MLA_V)
    wk = wkv[:, :, :MLA_NOPE].reshape(kv_lora, nh * MLA_NOPE).astype(BF16)
    wvt = wkv[:, :, MLA_NOPE:].reshape(kv_lora, nh * MLA_V).T.astype(BF16)
    inv_freq = 1.0 / (ROPE_THETA ** (jnp.arange(0, MLA_ROPE, 2, dtype=F32) / MLA_ROPE))
    invf = jnp.concatenate([inv_freq, inv_freq, jnp.zeros((pad,), F32)]).reshape(1, LANES)
    sign = jnp.concatenate([-jnp.ones((half,), F32), jnp.ones((half,), F32), jnp.zeros((pad,), F32)]).reshape(1, LANES)
    tm = TOKEN_TILE
    full = lambda a: pl.BlockSpec(a.shape, lambda i: (0,) * a.ndim)
    row = lambda n: pl.BlockSpec((tm, n), lambda i: (i, 0))
    mod = pl.BlockSpec((1, 1, d), lambda i: (i // tiles_per_b, 0, 0))
    gq2, gkv2 = gq.reshape(1, q_lora), gkv.reshape(1, kv_lora)
    scale = (MLA_NOPE + MLA_ROPE) ** -0.5
    tk = ATTN_TK
    kt_per_tile = tm // tk
    s = tiles_per_b * tm
    return pl.pallas_call(
        functools.partial(_mla_in_kernel, nh=nh, q_lora=q_lora, kv_lora=kv_lora, scale=scale, tk=tk),
        out_shape=[jax.ShapeDtypeStruct((t, nh * 2 * LANES), BF16), jax.ShapeDtypeStruct((t, nh * 2 * LANES), BF16),
                   jax.ShapeDtypeStruct((t // s, s // tk, nh * MLA_V, tk), BF16)],
        grid=(t // tm,),
        in_specs=[row(d), mod, mod, row(1), full(invf), full(sign), full(win_ext), full(gq2), full(gkv2),
                  full(wqa), full(wqb), full(wk), full(wvt)],
        out_specs=[row(nh * 2 * LANES), row(nh * 2 * LANES),
                   pl.BlockSpec((1, kt_per_tile, nh * MLA_V, tk),
                                lambda i: (i // tiles_per_b, i % tiles_per_b, 0, 0))],
        compiler_params=_cparams(("parallel",)),
        name="mla_in",
    )(x2, sc, sh, pos2, invf, sign, win_ext, gq2, gkv2, wqa, wqb, wk, wvt)


def _attn_kernel(q_ref, k_ref, vt_ref, o_ref, acc_ref, *, s, tq, tk, hb):
    kpq = tq // tk
    hw = 2 * LANES
    ones = jnp.ones((8, tk), BF16)

    def q_block(qi, carry):
        qrows = pl.ds(pl.multiple_of(qi * tq, tq), tq)
        acc_ref[...] = jnp.zeros_like(acc_ref)

        def tile(j, stats, masked):
            krows = pl.ds(pl.multiple_of(j * tk, tk), tk)
            sts = [_dot_nt(k_ref[krows, h * hw:(h + 1) * hw], q_ref[qrows, h * hw:(h + 1) * hw])
                   for h in range(hb)]
            ps, alphas, out = [], [], []
            for h in range(hb):
                m, st = stats[2 * h], sts[h]
                if masked:
                    key = j * tk + lax.broadcasted_iota(I32, (tk, tq), 0)
                    qry = qi * tq + lax.broadcasted_iota(I32, (tk, tq), 1)
                    st = jnp.where(key <= qry, st, -jnp.inf)
                m_new = jnp.maximum(m, jnp.max(st, axis=0, keepdims=True))
                ps.append(jnp.exp(st - m_new).astype(BF16))
                alphas.append(jnp.exp(m - m_new))
                out.append(m_new)
            for h in range(hb):
                acc_ref[h] = alphas[h] * acc_ref[h] + _dot(vt_ref[0, j, h * MLA_V:(h + 1) * MLA_V, :], ps[h])
                out.insert(2 * h + 1, alphas[h] * stats[2 * h + 1] + _dot(ones, ps[h])[0:1])
            return tuple(out)

        stats = (jnp.full((1, tq), -jnp.inf, F32), jnp.zeros((1, tq), F32)) * hb
        stats = lax.fori_loop(0, qi * kpq, lambda j, c: tile(j, c, False), stats)
        stats = lax.fori_loop(qi * kpq, (qi + 1) * kpq, lambda j, c: tile(j, c, True), stats)
        for h in range(hb):
            o_ref[qrows, h * MLA_V:(h + 1) * MLA_V] = (acc_ref[h] / stats[2 * h + 1]).T.astype(o_ref.dtype)
        return carry

    lax.fori_loop(0, s // tq, q_block, 0)


def _attn(q, k, vt, bsz, s):
    nh = MLA_HEADS
    tq, tk = ATTN_TQ, ATTN_TK
    hb = ATTN_HEADS_PER_STEP
    hw = 2 * LANES
    return pl.pallas_call(
        functools.partial(_attn_kernel, s=s, tq=tq, tk=tk, hb=hb),
        out_shape=jax.ShapeDtypeStruct((bsz * s, nh * MLA_V), BF16),
        grid=(bsz, nh // hb),
        in_specs=[pl.BlockSpec((s, hb * hw), lambda b, h: (b, h)),
                  pl.BlockSpec((s, hb * hw), lambda b, h: (b, h)),
                  pl.BlockSpec((1, s // tk, hb * MLA_V, tk), lambda b, h: (b, 0, h, 0))],
        out_specs=pl.BlockSpec((s, hb * MLA_V), lambda b, h: (b, h)),
        scratch_shapes=[pltpu.VMEM((hb, MLA_V, tq), F32)],
        compiler_params=_cparams(("parallel", "parallel")),
        name="attn",
    )(q, k, vt)


def _route_kernel(x_ref, sc_ref, sh_ref, w_ref, b_ref, lp_ref, wt_ref, tm_ref, offs_ref, te_ref, na_ref,
                  upper_ref, carry_ref, *, tm, ne, ng, row_tile, nt_pad):
    i = pl.program_id(0)
    epg = ne // ng

    @pl.when(i == 0)
    def _():
        r = lax.broadcasted_iota(I32, (tm, tm), 0)
        c = lax.broadcasted_iota(I32, (tm, tm), 1)
        upper_ref[...] = jnp.where(r < c, 1.0, 0.0).astype(BF16)
        carry_ref[...] = jnp.zeros_like(carry_ref)

    h = x_ref[...] * (1.0 + sc_ref[0]) + sh_ref[0]
    logits = jnp.dot(h, w_ref[...], precision=HIGHEST, preferred_element_type=F32) + b_ref[...]
    lt = logits.T
    lr = lt[0:ne]
    grow = lax.broadcasted_iota(I32, (8, tm), 0).astype(F32)
    lg = jnp.where(grow < ng, lt[ne:ne + 8], -jnp.inf)
    gmax = jnp.max(lg, axis=0, keepdims=True)
    g_idx = jnp.min(jnp.where(lg == gmax, grow, 1e9), axis=0, keepdims=True)
    g_w = 1.0 / jnp.sum(jnp.exp(lg - gmax), axis=0, keepdims=True)
    erow = lax.broadcasted_iota(I32, (ne, tm), 0).astype(F32)
    in_group = jnp.floor(erow * (1.0 / epg)) == g_idx
    sel = jnp.where(in_group, lr, -jnp.inf)
    v1 = jnp.max(sel, axis=0, keepdims=True)
    i1 = jnp.min(jnp.where(sel == v1, erow, 1e9), axis=0, keepdims=True)
    sel2 = jnp.where(erow == i1, -jnp.inf, sel)
    v2 = jnp.max(sel2, axis=0, keepdims=True)
    i2 = jnp.min(jnp.where(sel2 == v2, erow, 1e9), axis=0, keepdims=True)
    t = jnp.exp(v2 - v1)
    w1 = g_w / (1.0 + t)
    w2 = g_w * t / (1.0 + t)
    oh1 = erow == i1
    oh2 = erow == i2
    member = jnp.where(oh1 | oh2, 1.0, 0.0)
    lcnt = jnp.sum(member, axis=1, keepdims=True)
    lcnt = jnp.floor((lcnt + (SUBLANES - 1)) * (1.0 / SUBLANES)) * SUBLANES
    er_ = lax.broadcasted_iota(I32, (ne, ne), 0)
    ec_ = lax.broadcasted_iota(I32, (ne, ne), 1)
    lstart = jnp.dot(jnp.where(ec_ < er_, 1.0, 0.0).astype(F32), jnp.broadcast_to(lcnt, (ne, LANES)),
                     precision=HIGHEST, preferred_element_type=F32)[:, 0:1]
    lrank = _dot(member.astype(BF16), upper_ref[...]) + lstart
    p1 = jnp.sum(jnp.where(oh1, lrank, 0.0), axis=0, keepdims=True)
    p2 = jnp.sum(jnp.where(oh2, lrank, 0.0), axis=0, keepdims=True)

    orow = lax.broadcasted_iota(I32, (8, tm), 0)
    lp_ref[...] = jnp.where(orow == 0, p1, jnp.where(orow == 1, p2, 0.0)).astype(I32)
    wrow = lax.broadcasted_iota(I32, (LANES, tm), 0)
    wt_ref[...] = jnp.where(wrow == 0, w1, jnp.where(wrow == 1, w2,
                                                     jnp.where(wrow == 2, p1, jnp.where(wrow == 3, p2, 0.0)))).T
    mr = lax.broadcasted_iota(I32, (ne, LANES), 0)
    mc = lax.broadcasted_iota(I32, (ne, LANES), 1)
    to_row = lambda col, lane0: jnp.sum(jnp.where(mr + lane0 == mc, col, 0.0), axis=0, keepdims=True)
    total = jnp.sum(lcnt, axis=0, keepdims=True)
    lane = lax.broadcasted_iota(I32, (1, LANES), 1)
    packed = (to_row(lstart, 0) + to_row(lcnt, ne) + to_row(carry_ref[...], 2 * ne)
              + jnp.where(lane == 3 * ne, total, 0.0))
    trow = lax.broadcasted_iota(I32, (8, LANES), 0)
    tm_ref[...] = jnp.where(trow == 0, packed, 0.0).astype(I32)
    carry_ref[...] = carry_ref[...] + lcnt

    @pl.when(i == pl.num_programs(0) - 1)
    def _():
        cnt = carry_ref[...]
        ntl = jnp.floor((cnt + (row_tile - 1)) * (1.0 / row_tile))
        incl = jnp.where(ec_ <= er_, 1.0, 0.0).astype(F32)
        ends = jnp.dot(incl, jnp.broadcast_to(ntl, (ne, LANES)), precision=HIGHEST,
                       preferred_element_type=F32)
        starts = ends - ntl
        offs_ref[...] = jnp.concatenate([starts * row_tile, ends * row_tile, jnp.broadcast_to(ntl, (ne, LANES)),
                                         jnp.zeros((8, LANES), F32)], axis=0).astype(I32)
        tile = lax.broadcasted_iota(I32, (ne, nt_pad), 1).astype(F32)
        te = jnp.sum(jnp.where(ends[:, 0:1] <= tile, 1.0, 0.0), axis=0, keepdims=True)
        te_ref[...] = jnp.broadcast_to(jnp.minimum(te, ne - 1.0), (8, nt_pad)).astype(I32)
        na_ref[...] = jnp.broadcast_to(ends[ne - 1:ne, :], (8, LANES)).astype(I32)


def _moe_tiles(t):
    rows = 2 * t + (SUBLANES - 1) * MOE_EXPERTS * (t // TOKEN_TILE)
    nt_max = -(-rows // MOE_ROW_TILE) + MOE_EXPERTS
    nt_pad = -(-nt_max // LANES) * LANES
    return nt_max, nt_pad


def _sorted_rows(tm):
    return -(-(2 * tm + (SUBLANES - 1) * MOE_EXPERTS) // LANES) * LANES


def _route(x2, sc, sh, w_group, b_group, w_router, b_router, tiles_per_b):
    t, d = x2.shape
    ne, ng = MOE_EXPERTS, MOE_GROUPS
    tm = TOKEN_TILE
    _, nt_pad = _moe_tiles(t)
    wcat = jnp.concatenate([w_router, w_group, jnp.zeros((d, LANES - ne - ng), F32)], axis=1)
    bcat = jnp.concatenate([b_router, b_group, jnp.zeros((LANES - ne - ng,), F32)]).reshape(1, LANES)
    const = lambda shp: pl.BlockSpec(shp, lambda i: (0,) * len(shp))
    return pl.pallas_call(
        functools.partial(_route_kernel, tm=tm, ne=ne, ng=ng, row_tile=MOE_ROW_TILE, nt_pad=nt_pad),
        out_shape=[jax.ShapeDtypeStruct((8, t), I32), jax.ShapeDtypeStruct((t, LANES), F32),
                   jax.ShapeDtypeStruct((8 * (t // tm), LANES), I32),
                   jax.ShapeDtypeStruct((3 * ne + 8, LANES), I32), jax.ShapeDtypeStruct((8, nt_pad), I32),
                   jax.ShapeDtypeStruct((8, LANES), I32)],
        grid=(t // tm,),
        in_specs=[pl.BlockSpec((tm, d), lambda i: (i, 0)),
                  pl.BlockSpec((1, 1, d), lambda i: (i // tiles_per_b, 0, 0)),
                  pl.BlockSpec((1, 1, d), lambda i: (i // tiles_per_b, 0, 0)),
                  const((d, LANES)), const((1, LANES))],
        out_specs=[pl.BlockSpec((8, tm), lambda i: (0, i)), pl.BlockSpec((tm, LANES), lambda i: (i, 0)),
                   pl.BlockSpec((8, LANES), lambda i: (i, 0)),
                   const((3 * ne + 8, LANES)), const((8, nt_pad)), const((8, LANES))],
        scratch_shapes=[pltpu.VMEM((tm, tm), BF16), pltpu.VMEM((ne, 1), F32)],
        compiler_params=_cparams(("arbitrary",)),
        name="moe_route",
    )(x2, sc, sh, wcat, bcat)


def _start_runs(tm_ref, tile, ne, copy):
    def expert_run(e, carry):
        lstart = tm_ref[tile, e]
        n = tm_ref[tile, ne + e]
        before = tm_ref[tile, 2 * ne + e]
        nfull = n // RUN_PIECE

        def piece(c, carry2):
            copy(e, pl.multiple_of(lstart + c * RUN_PIECE, SUBLANES),
                 pl.multiple_of(before + c * RUN_PIECE, SUBLANES), RUN_PIECE).start()
            return carry2

        lax.fori_loop(0, nfull, piece, 0)
        off = nfull * RUN_PIECE
        p = RUN_PIECE // 2
        while p >= SUBLANES:
            @pl.when((n & p) != 0)
            def _():
                copy(e, pl.multiple_of(lstart + off, SUBLANES), pl.multiple_of(before + off, SUBLANES), p).start()
            off = off + (n & p)
            p //= 2
        return carry

    lax.fori_loop(0, ne, expert_run, 0)


def _wait_runs(tm_ref, tile, ne, piece_copy):
    def wait8(c, carry):
        piece_copy.wait()
        return carry
    lax.fori_loop(0, tm_ref[tile, 3 * ne] // SUBLANES, wait8, 0)


def _dispatch_kernel(offs_ref, gend_ref, ntl_ref, tm_ref, x_ref, sc_ref, sh_ref, lp_ref, xs_ref,
                     h_ref, z_ref, sem_z, sem_r, *, tm, ne, row_tile, nt_max, ntile, srows):
    i = pl.program_id(0)
    tile = i - 1

    @pl.when(i == 0)
    def _():
        z_ref[...] = jnp.zeros_like(z_ref)
        for e in range(ne):
            @pl.when(ntl_ref[e] > 0)
            def _():
                start = pl.multiple_of(gend_ref[e] - row_tile, row_tile)
                cp = pltpu.make_async_copy(z_ref, xs_ref.at[pl.ds(start, row_tile), :], sem_z)
                cp.start()
                cp.wait()
        for back in range(1, nt_max - (2 * tm * ntile) // row_tile + 1):
            @pl.when(nt_max - back >= gend_ref[ne - 1] // row_tile)
            def _():
                cp = pltpu.make_async_copy(z_ref, xs_ref.at[pl.ds((nt_max - back) * row_tile, row_tile), :], sem_z)
                cp.start()
                cp.wait()

    def waiter(t):
        slot = t % 2
        return pltpu.make_async_copy(h_ref.at[slot, pl.ds(0, SUBLANES), :], xs_ref.at[pl.ds(0, SUBLANES), :],
                                     sem_r.at[slot])

    @pl.when((tile >= 2) & (tile <= ntile))
    def _():
        _wait_runs(tm_ref, tile - 2, ne, waiter(tile - 2))

    @pl.when(tile == ntile)
    def _():
        _wait_runs(tm_ref, tile - 1, ne, waiter(tile - 1))

    @pl.when((tile >= 0) & (tile < ntile))
    def _():
        slot = tile % 2
        h = (x_ref[...] * (1.0 + sc_ref[0]) + sh_ref[0]).astype(BF16)
        row = lax.broadcasted_iota(I32, (srows, tm), 0)
        lp = lp_ref[...]
        onehot = jnp.where((row == lp[0:1, :]) | (row == lp[1:2, :]), 1.0, 0.0).astype(BF16)
        h_ref[slot] = _dot(onehot, h)

        def copy(e, local_row, rows_before, rows):
            dst = pl.multiple_of(offs_ref[e] + rows_before, SUBLANES)
            return pltpu.make_async_copy(h_ref.at[slot, pl.ds(local_row, rows), :], xs_ref.at[pl.ds(dst, rows), :],
                                         sem_r.at[slot])

        _start_runs(tm_ref, tile, ne, copy)


def _dispatch(x2, sc, sh, lp, tmeta, offs, gend, ntl, tiles_per_b):
    t, d = x2.shape
    tm = TOKEN_TILE
    nt_max, _ = _moe_tiles(t)
    ntile = t // tm
    srows = _sorted_rows(tm)
    cur = lambda i: jnp.clip(i - 1, 0, ntile - 1)
    grid_spec = pltpu.PrefetchScalarGridSpec(
        num_scalar_prefetch=4,
        grid=(ntile + 2,),
        in_specs=[pl.BlockSpec((tm, d), lambda i, *_: (cur(i), 0)),
                  pl.BlockSpec((1, 1, d), lambda i, *_: (cur(i) // tiles_per_b, 0, 0)),
                  pl.BlockSpec((1, 1, d), lambda i, *_: (cur(i) // tiles_per_b, 0, 0)),
                  pl.BlockSpec((8, tm), lambda i, *_: (0, cur(i)))],
        out_specs=pl.BlockSpec(memory_space=pl.ANY),
        scratch_shapes=[pltpu.VMEM((2, srows, d), F32), pltpu.VMEM((MOE_ROW_TILE, d), F32),
                        pltpu.SemaphoreType.DMA, pltpu.SemaphoreType.DMA((2,))])
    return pl.pallas_call(
        functools.partial(_dispatch_kernel, tm=tm, ne=MOE_EXPERTS, row_tile=MOE_ROW_TILE, nt_max=nt_max,
                          ntile=ntile, srows=srows),
        out_shape=jax.ShapeDtypeStruct((nt_max * MOE_ROW_TILE, d), F32),
        grid_spec=grid_spec,
        compiler_params=_cparams(("arbitrary",)),
        name="moe_dispatch",
    )(offs, gend, ntl, tmeta, x2, sc, sh, lp)


def _gmm_kernel(te_ref, na_ref, xs_ref, wg_ref, wu_ref, wd_ref, ys_ref, wgu_buf, wd_buf, *, ff, tr):
    j = pl.program_id(0)

    @pl.when(j < na_ref[0])
    def _():
        changed = (j == 0) | (te_ref[j] != te_ref[jnp.maximum(j - 1, 0)])

        @pl.when(changed)
        def _():
            wgu_buf[:, :ff] = wg_ref[0, 0].astype(BF16)
            wgu_buf[:, ff:] = wu_ref[0, 0].astype(BF16)
            wd_buf[...] = wd_ref[0, 0].astype(BF16)

        halves = [slice(c * (tr // 2), (c + 1) * (tr // 2)) for c in range(2)]
        gus = [_dot(xs_ref[rows, :].astype(BF16), wgu_buf[...]) for rows in halves]
        hids = [(gu[:, :ff] * _sigmoid(gu[:, :ff]) * gu[:, ff:]).astype(BF16) for gu in gus]
        for rows, hid in zip(halves, hids):
            ys_ref[rows, :] = _dot(hid, wd_buf[...])

    @pl.when(j >= na_ref[0])
    def _():
        ys_ref[...] = jnp.zeros_like(ys_ref)


def _gmm(xs, te, na, w_gate, w_up, w_down, layer):
    d, ff = w_gate.shape[-2:]
    tr = MOE_ROW_TILE
    ns = xs.shape[0]
    act = lambda j, te_ref, na_ref: jnp.minimum(j, na_ref[0] - 1)
    grid_spec = pltpu.PrefetchScalarGridSpec(
        num_scalar_prefetch=2,
        grid=(ns // tr,),
        in_specs=[pl.BlockSpec((tr, d), lambda j, te_ref, na_ref: (act(j, te_ref, na_ref), 0)),
                  pl.BlockSpec((1, 1, d, ff), lambda j, te_ref, na_ref: (layer, te_ref[act(j, te_ref, na_ref)], 0, 0)),
                  pl.BlockSpec((1, 1, d, ff), lambda j, te_ref, na_ref: (layer, te_ref[act(j, te_ref, na_ref)], 0, 0)),
                  pl.BlockSpec((1, 1, ff, d), lambda j, te_ref, na_ref: (layer, te_ref[act(j, te_ref, na_ref)], 0, 0))],
        out_specs=pl.BlockSpec((tr, d), lambda j, te_ref, na_ref: (j, 0)),
        scratch_shapes=[pltpu.VMEM((d, 2 * ff), BF16), pltpu.VMEM((ff, d), BF16)])
    return pl.pallas_call(
        functools.partial(_gmm_kernel, ff=ff, tr=tr),
        out_shape=jax.ShapeDtypeStruct((ns, d), F32),
        grid_spec=grid_spec,
        compiler_params=_cparams(("arbitrary",)),
        name="moe_gmm",
    )(te, na, xs, w_gate, w_up, w_down)


def _combine_kernel(offs_ref, tm_ref, wt_ref, x_ref, gate_ref, lg_ref, lb_ref, ys_ref, o_ref,
                    buf_ref, sem_r, *, tm, ne, ntile, srows, alpha):
    i = pl.program_id(0)

    @pl.when(i == 0)
    def _():
        buf_ref[...] = jnp.zeros_like(buf_ref)

    @pl.when(i < ntile)
    def _():
        slot = i % 2

        def copy(e, local_row, rows_before, rows):
            src = pl.multiple_of(offs_ref[e] + rows_before, SUBLANES)
            return pltpu.make_async_copy(ys_ref.at[pl.ds(src, rows), :], buf_ref.at[slot, pl.ds(local_row, rows), :],
                                         sem_r.at[slot])

        _start_runs(tm_ref, i, ne, copy)

    @pl.when(i > 0)
    def _():
        slot = (i - 1) % 2
        _wait_runs(tm_ref, i - 1, ne,
                   pltpu.make_async_copy(ys_ref.at[pl.ds(0, SUBLANES), :], buf_ref.at[slot, pl.ds(0, SUBLANES), :],
                                         sem_r.at[slot]))
        w = wt_ref[...]
        rows = buf_ref[slot].astype(BF16)
        col = lax.broadcasted_iota(I32, (tm, srows), 1).astype(F32)
        pick = lambda k: _dot(jnp.where(col == w[:, 2 + k:3 + k], 1.0, 0.0).astype(BF16), rows)
        ffn = w[:, 0:1] * pick(0) + w[:, 1:2] * pick(1)
        y = alpha * x_ref[...] + (1.0 + gate_ref[0]) * ffn
        o_ref[...] = _layer_norm(y, lg_ref[...], lb_ref[...])


def _combine(ys, tmeta, wt, x2, gate, ln_g, ln_b, offs, tiles_per_b, alpha):
    t, d = x2.shape
    tm = TOKEN_TILE
    ntile = t // tm
    srows = _sorted_rows(tm)
    prev = lambda i: jnp.maximum(i - 1, 0)
    vec = pl.BlockSpec((1, d), lambda i, *_: (0, 0))
    grid_spec = pltpu.PrefetchScalarGridSpec(
        num_scalar_prefetch=2,
        grid=(ntile + 1,),
        in_specs=[pl.BlockSpec((tm, LANES), lambda i, *_: (prev(i), 0)),
                  pl.BlockSpec((tm, d), lambda i, *_: (prev(i), 0)),
                  pl.BlockSpec((1, 1, d), lambda i, *_: (prev(i) // tiles_per_b, 0, 0)),
                  vec, vec,
                  pl.BlockSpec(memory_space=pl.ANY)],
        out_specs=pl.BlockSpec((tm, d), lambda i, *_: (prev(i), 0)),
        scratch_shapes=[pltpu.VMEM((2, srows, d), F32), pltpu.SemaphoreType.DMA((2,))])
    return pl.pallas_call(
        functools.partial(_combine_kernel, tm=tm, ne=MOE_EXPERTS, ntile=ntile, srows=srows, alpha=alpha),
        out_shape=jax.ShapeDtypeStruct((t, d), F32),
        grid_spec=grid_spec,
        compiler_params=_cparams(("arbitrary",)),
        name="moe_combine",
    )(offs, tmeta, wt, x2, gate, ln_g.reshape(1, d), ln_b.reshape(1, d), ys)


def _moe_block(x2, sc, sh, gate, ln_g, ln_b, w_group, b_group, w_router, b_router, w_gate, w_up, w_down,
               layer, tiles_per_b, alpha):
    ne = MOE_EXPERTS
    nt_max, _ = _moe_tiles(x2.shape[0])
    lp, wt, tmeta, meta, te, na = _route(x2, sc, sh, w_group, b_group, w_router, b_router, tiles_per_b)
    offs, gend, ntl = meta[:ne, 0], meta[ne:2 * ne, 0], meta[2 * ne:3 * ne, 0]
    tmeta = tmeta.reshape(-1, SUBLANES, LANES)[:, 0, :]
    xs = _dispatch(x2, sc, sh, lp, tmeta, offs, gend, ntl, tiles_per_b)
    ys = _gmm(xs, te[0, :nt_max], na[0, :1], w_gate, w_up, w_down, layer)
    return _combine(ys, tmeta, wt, x2, gate, ln_g, ln_b, offs, tiles_per_b, alpha)


def kernel(x, c, positions, ada_w, ada_b, ln_mix_g, ln_mix_b, ln_ffn_g, ln_ffn_b, ab_w_in, conv_w, conv_b, conv_ln_g, conv_ln_b, gla_gate_w, gla_gate_b, gla_norm_g, ab_w_out, mla_w_in, mla_q_norm_g, mla_kv_norm_g, mla_w_uq, mla_w_ukv, mla_w_out, moe_w_group, moe_b_group, moe_w_router, moe_b_router, moe_w_gate, moe_w_up, moe_w_down):
    bsz, s, d = x.shape
    depth = ada_w.shape[0]
    t = bsz * s
    tiles_per_b = s // TOKEN_TILE
    alpha = (2 * depth) ** 0.25
    mod = _ada(c, ada_w, ada_b).reshape(depth, bsz, 6, 1, d)
    x2 = x.reshape(t, d)
    for layer in range(depth):
        sh_m, sc_m, g_m, sh_f, sc_f, g_f = (mod[layer, :, n] for n in range(6))
        i = layer // 2
        if layer % 2 == 0:
            uc, q, k, v, r, gl = _ab_in(x2, sc_m, sh_m, ab_w_in[i], gla_gate_w[i], gla_gate_b[i], tiles_per_b)
            y_a = _conv(uc.reshape(bsz, s, -1), conv_w[i], conv_b[i], conv_ln_g[i], conv_ln_b[i])
            b3 = lambda a: a.reshape(bsz, s, -1)
            y_b = _gla(b3(q), b3(k), b3(v), b3(gl), b3(r), gla_norm_g[i])
            w_out = ab_w_out[i].astype(BF16)
            cc = y_a.shape[-1]
            acts = [y_a.reshape(t, cc), y_b.reshape(t, -1)]
            weights = [w_out[:cc], w_out[cc:]]
        else:
            qc, kc, vv = _mla_in(x2, sc_m, sh_m, positions.reshape(t, 1), mla_w_in[i], mla_q_norm_g[i],
                                 mla_kv_norm_g[i], mla_w_uq[i], mla_w_ukv[i], tiles_per_b)
            acts = [_attn(qc, kc, vv, bsz, s)]
            weights = [mla_w_out[i].astype(BF16)]
        x2 = _proj_ln(acts, weights, x2, g_m, ln_mix_g[layer], ln_mix_b[layer], tiles_per_b, alpha)
        x2 = _moe_block(x2, sc_f, sh_f, g_f, ln_ffn_g[layer], ln_ffn_b[layer], moe_w_group[layer], moe_b_group[layer],
                        moe_w_router[layer], moe_b_router[layer], moe_w_gate, moe_w_up, moe_w_down,
                        layer, tiles_per_b, alpha)
    return x2.reshape(bsz, s, d)
```

```python
import functools

import jax
import jax.numpy as jnp
from jax import lax
from jax.experimental import pallas as pl
from jax.experimental.pallas import tpu as pltpu

F32 = jnp.float32
BF16 = jnp.bfloat16
I32 = jnp.int32
U32 = jnp.uint32
HIGHEST = lax.Precision.HIGHEST

LN_EPS = 1e-5
RMS_EPS = 1e-6
CONV_WIDTH = 31
GLA_HEADS = 4
GLA_GATE_TAU = 16.0
MLA_HEADS = 8
MLA_NOPE = 128
MLA_ROPE = 64
MLA_V = 128
ROPE_THETA = 10000.0
MOE_GROUPS = 4
MOE_EXPERTS_PER_GROUP = 8
MOE_EXPERTS = MOE_GROUPS * MOE_EXPERTS_PER_GROUP

LANES = 128
SUBLANES = 8
TOKEN_TILE = 512
GLA_CHUNK = 128
GLA_BLOCK = 512
CONV_ROWS = 32
CONV_HALO = 32
ATTN_TQ = 512
ATTN_TK = 256
ATTN_HEADS_PER_STEP = 4
MOE_ROW_TILE = 256
RUN_PIECE = 32
VMEM_LIMIT = 48 * 1024 * 1024


def _cparams(sem):
    return pltpu.CompilerParams(dimension_semantics=sem, vmem_limit_bytes=VMEM_LIMIT)


def _sigmoid(x):
    return 1.0 / (1.0 + jnp.exp(-x))


def _dot(a, b):
    return jnp.dot(a, b, preferred_element_type=F32)


def _dot_nt(a, b):
    return lax.dot_general(a, b, (((1,), (1,)), ((), ())), preferred_element_type=F32)


def _dot_tn(a, b):
    return lax.dot_general(a, b, (((0,), (0,)), ((), ())), preferred_element_type=F32)


def _layer_norm(y, g, b):
    mu = jnp.mean(y, axis=-1, keepdims=True)
    d = y - mu
    var = jnp.mean(d * d, axis=-1, keepdims=True)
    return d * lax.rsqrt(var + LN_EPS) * g + b


def _ada_kernel(c_ref, w_ref, b_ref, o_ref):
    c = c_ref[...]
    o_ref[0] = jnp.dot(c * _sigmoid(c), w_ref[0], precision=HIGHEST, preferred_element_type=F32) + b_ref[0]


def _ada(c, ada_w, ada_b):
    depth, d, n = ada_w.shape
    bsz = c.shape[0]
    tn = 1536
    return pl.pallas_call(
        _ada_kernel,
        out_shape=jax.ShapeDtypeStruct((depth, bsz, n), F32),
        grid=(depth, n // tn),
        in_specs=[pl.BlockSpec((bsz, d), lambda l, j: (0, 0)),
                  pl.BlockSpec((1, d, tn), lambda l, j: (l, 0, j)),
                  pl.BlockSpec((1, 1, tn), lambda l, j: (l, 0, j))],
        out_specs=pl.BlockSpec((1, bsz, tn), lambda l, j: (l, 0, j)),
        compiler_params=_cparams(("parallel", "parallel")),
        name="ada",
    )(c, ada_w, ada_b.reshape(depth, 1, n))


def _ab_in_kernel(x_ref, sc_ref, sh_ref, wc_ref, wq_ref, wk_ref, wv_ref, wr_ref, wg_ref, gw_ref, gb_ref,
                  uc_ref, q_ref, k_ref, v_ref, r_ref, gl_ref):
    h = (x_ref[...] * (1.0 + sc_ref[0]) + sh_ref[0]).astype(BF16)
    uc_ref[...] = _dot(h, wc_ref[...])
    q_ref[...] = _dot(h, wq_ref[...])
    k_ref[...] = _dot(h, wk_ref[...])
    v_ref[...] = _dot(h, wv_ref[...]).astype(v_ref.dtype)
    r_ref[...] = _dot(h, wr_ref[...])
    g_low = _dot(h, wg_ref[...])
    z = jnp.dot(g_low, gw_ref[...], precision=HIGHEST, preferred_element_type=F32) + gb_ref[...]
    gl_ref[...] = (jnp.minimum(z, 0.0) - jnp.log(1.0 + jnp.exp(-jnp.abs(z)))) * (1.0 / GLA_GATE_TAU)


def _ab_in(x2, sc, sh, w_in, gate_w, gate_b, tiles_per_b):
    t, d = x2.shape
    cc2 = d
    kw = d // 4
    vw = d // 2
    rank = gate_w.shape[0]
    splits = [cc2, cc2 + kw, cc2 + 2 * kw, cc2 + 2 * kw + vw, cc2 + 2 * kw + 2 * vw]
    wb = w_in.astype(BF16)
    ws = [wb[:, :splits[0]], wb[:, splits[0]:splits[1]], wb[:, splits[1]:splits[2]],
          wb[:, splits[2]:splits[3]], wb[:, splits[3]:splits[4]], wb[:, splits[4]:]]
    tm = TOKEN_TILE
    full = lambda a: pl.BlockSpec(a.shape, lambda i: (0,) * a.ndim)
    row = lambda n: pl.BlockSpec((tm, n), lambda i: (i, 0))
    mod = pl.BlockSpec((1, 1, d), lambda i: (i // tiles_per_b, 0, 0))
    gb2 = gate_b.reshape(1, kw)
    widths = [cc2, kw, kw, vw, vw, kw]
    return pl.pallas_call(
        _ab_in_kernel,
        out_shape=[jax.ShapeDtypeStruct((t, n), BF16 if idx == 3 else F32) for idx, n in enumerate(widths)],
        grid=(t // tm,),
        in_specs=[row(d), mod, mod] + [full(w) for w in ws] + [full(gate_w), full(gb2)],
        out_specs=[row(n) for n in widths],
        compiler_params=_cparams(("parallel",)),
        name="ab_in",
    )(x2, sc, sh, *ws, gate_w, gb2)


def _conv_kernel(u_ref, halo_ref, cw_ref, cb_ref, lg_ref, lb_ref, o_ref, hp_ref, *, ts, cc):
    j = pl.program_id(1)

    def glu(u):
        return u[:, :cc] * _sigmoid(u[:, cc:])

    hp_ref[0, 0:CONV_HALO, :] = jnp.where(j > 0, glu(halo_ref[0]), 0.0)
    hp_ref[0, CONV_HALO:CONV_HALO + ts, :] = glu(u_ref[0])
    nrow = CONV_HALO + ts
    for b in range(1, SUBLANES):
        hp_ref[b, 0:nrow - SUBLANES, :] = hp_ref[0, b:nrow - SUBLANES + b, :]
    shift = CONV_HALO - (CONV_WIDTH - 1)
    for rb in range(ts // CONV_ROWS):
        r0 = rb * CONV_ROWS
        acc = jnp.zeros((CONV_ROWS, cc), F32)
        for tap in range(CONV_WIDTH):
            lo = r0 + shift + tap
            base = lo // SUBLANES * SUBLANES
            acc = acc + cw_ref[tap:tap + 1, :] * hp_ref[lo - base, base:base + CONV_ROWS, :]
        y = _layer_norm(acc + cb_ref[...], lg_ref[...], lb_ref[...])
        o_ref[0, r0:r0 + CONV_ROWS, :] = (y * _sigmoid(y)).astype(o_ref.dtype)


def _conv(u3, conv_w, conv_b, ln_g, ln_b):
    bsz, s, cc2 = u3.shape
    cc = cc2 // 2
    ts = TOKEN_TILE
    hb = ts // CONV_HALO
    vec = lambda a: pl.BlockSpec((1, cc), lambda b, j: (0, 0))
    return pl.pallas_call(
        functools.partial(_conv_kernel, ts=ts, cc=cc),
        out_shape=jax.ShapeDtypeStruct((bsz, s, cc), BF16),
        grid=(bsz, s // ts),
        in_specs=[pl.BlockSpec((1, ts, cc2), lambda b, j: (b, j, 0)),
                  pl.BlockSpec((1, CONV_HALO, cc2), lambda b, j: (b, jnp.maximum(j * hb - 1, 0), 0)),
                  pl.BlockSpec((CONV_WIDTH, cc), lambda b, j: (0, 0)),
                  vec(conv_b), vec(ln_g), vec(ln_b)],
        out_specs=pl.BlockSpec((1, ts, cc), lambda b, j: (b, j, 0)),
        scratch_shapes=[pltpu.VMEM((SUBLANES, CONV_HALO + ts, cc), F32)],
        compiler_params=_cparams(("parallel", "parallel")),
        name="conv",
    )(u3, u3, conv_w, conv_b.reshape(1, cc), ln_g.reshape(1, cc), ln_b.reshape(1, cc))


def _gla_kernel(q_ref, k_ref, v_ref, gl_ref, r_ref, ng_ref, o_ref, st_ref, *, nh, dk, dv, gc, nchunks):
    @pl.when(pl.program_id(1) == 0)
    def _():
        st_ref[...] = jnp.zeros_like(st_ref)

    row = lax.broadcasted_iota(I32, (gc, gc), 0)
    col = lax.broadcasted_iota(I32, (gc, gc), 1)
    causal = col <= row
    tri = jnp.where(causal, 1.0, 0.0).astype(F32)
    scale = dk ** -0.5

    def chunk(c, carry):
        r0 = pl.multiple_of(c * gc, gc)
        rows = pl.ds(r0, gc)
        g = gl_ref[0, rows, :]
        b = jnp.dot(tri, g, precision=HIGHEST, preferred_element_type=F32)
        b_last = b[gc - 1:gc, :]
        mid = 0.5 * b_last
        q = q_ref[0, rows, :] * scale
        k = k_ref[0, rows, :]
        v = v_ref[0, rows, :].astype(BF16)
        q_in = (q * jnp.exp(b - mid)).astype(BF16)
        k_in = (k * jnp.exp(mid - b)).astype(BF16)
        q_st = (q * jnp.exp(b)).astype(BF16)
        k_st = (k * jnp.exp(b_last - b)).astype(BF16)
        decay = jnp.exp(b_last)
        r = r_ref[0, rows, :]
        gate = r * _sigmoid(r)
        ks = [slice(h * dk, (h + 1) * dk) for h in range(nh)]
        vs = [slice(h * dv, (h + 1) * dv) for h in range(nh)]
        sts = [st_ref[h] for h in range(nh)]
        scores = [_dot_nt(q_in[:, ks[h]], k_in[:, ks[h]]) for h in range(nh)]
        inter = [_dot_nt(q_st[:, ks[h]], sts[h].astype(BF16)) for h in range(nh)]
        update = [_dot_tn(v[:, vs[h]], k_st[:, ks[h]]) for h in range(nh)]
        atts = [jnp.where(causal, sc, 0.0).astype(BF16) for sc in scores]
        outs = [_dot(atts[h], v[:, vs[h]]) + inter[h] for h in range(nh)]
        for h in range(nh):
            st_ref[h] = sts[h] * decay[:, ks[h]] + update[h]
            o = outs[h]
            o = o * lax.rsqrt(jnp.mean(o * o, axis=-1, keepdims=True) + RMS_EPS) * ng_ref[:, vs[h]]
            o_ref[0, rows, vs[h]] = (o * gate[:, vs[h]]).astype(o_ref.dtype)
        return carry

    lax.fori_loop(0, nchunks, chunk, 0)


def _gla(q3, k3, v3, gl3, r3, norm_g):
    bsz, s, kw = q3.shape
    vw = v3.shape[-1]
    nh = GLA_HEADS
    dk, dv = kw // nh, vw // nh
    cb = GLA_BLOCK
    gc = GLA_CHUNK
    blk = lambda n: pl.BlockSpec((1, cb, n), lambda b, j: (b, j, 0))
    return pl.pallas_call(
        functools.partial(_gla_kernel, nh=nh, dk=dk, dv=dv, gc=gc, nchunks=cb // gc),
        out_shape=jax.ShapeDtypeStruct((bsz, s, vw), BF16),
        grid=(bsz, s // cb),
        in_specs=[blk(kw), blk(kw), blk(vw), blk(kw), blk(vw), pl.BlockSpec((1, vw), lambda b, j: (0, 0))],
        out_specs=blk(vw),
        scratch_shapes=[pltpu.VMEM((nh, dv, dk), F32)],
        compiler_params=_cparams(("parallel", "arbitrary")),
        name="gla",
    )(q3, k3, v3, gl3, r3, norm_g.reshape(1, vw))


def _proj_ln_kernel(*refs, n_in, alpha):
    a_refs, w_refs = refs[:n_in], refs[n_in:2 * n_in]
    x_ref, gate_ref, lg_ref, lb_ref, o_ref = refs[2 * n_in:]
    mix = _dot(a_refs[0][...], w_refs[0][...])
    for a_ref, w_ref in zip(a_refs[1:], w_refs[1:]):
        mix = mix + _dot(a_ref[...], w_ref[...])
    y = alpha * x_ref[...] + (1.0 + gate_ref[0]) * mix
    o_ref[...] = _layer_norm(y, lg_ref[...], lb_ref[...])


def _proj_ln(acts, weights, x2, gate, ln_g, ln_b, tiles_per_b, alpha):
    t, d = x2.shape
    tm = TOKEN_TILE
    n_in = len(acts)
    row = lambda n: pl.BlockSpec((tm, n), lambda i: (i, 0))
    full = lambda a: pl.BlockSpec(a.shape, lambda i: (0,) * a.ndim)
    vec = pl.BlockSpec((1, d), lambda i: (0, 0))
    return pl.pallas_call(
        functools.partial(_proj_ln_kernel, n_in=n_in, alpha=alpha),
        out_shape=jax.ShapeDtypeStruct((t, d), F32),
        grid=(t // tm,),
        in_specs=[row(a.shape[1]) for a in acts] + [full(w) for w in weights]
                 + [row(d), pl.BlockSpec((1, 1, d), lambda i: (i // tiles_per_b, 0, 0)), vec, vec],
        out_specs=row(d),
        compiler_params=_cparams(("parallel",)),
        name="proj_ln",
    )(*acts, *weights, x2, gate, ln_g.reshape(1, d), ln_b.reshape(1, d))


def _mla_in_kernel(x_ref, sc_ref, sh_ref, pos_ref, invf_ref, phase_ref, sign_ref, win_ref, gq_ref, gkv_ref,
                   wqa_ref, wqb_ref, wk_ref, wvt_ref, q_ref, k_ref, vt_ref, *, nh, q_lora, kv_lora, scale, tk):
    h = (x_ref[...] * (1.0 + sc_ref[0]) + sh_ref[0]).astype(BF16)
    u = _dot(h, win_ref[...])
    cq = u[:, :q_lora]
    ckv = u[:, q_lora:q_lora + kv_lora]
    kr = u[:, q_lora + kv_lora:q_lora + kv_lora + LANES]
    kr_sw = u[:, q_lora + kv_lora + LANES:]
    cqn = (cq * lax.rsqrt(jnp.mean(cq * cq, axis=-1, keepdims=True) + RMS_EPS) * gq_ref[...]).astype(BF16)
    kvn = (ckv * lax.rsqrt(jnp.mean(ckv * ckv, axis=-1, keepdims=True) + RMS_EPS) * gkv_ref[...]).astype(BF16)
    cos = jnp.cos(pos_ref[...].astype(F32) * invf_ref[...] - phase_ref[...])
    sin = pltpu.roll(cos, LANES // 2, axis=1) * sign_ref[...]
    kr_rot = (kr * cos + kr_sw * sin).astype(BF16)
    qa = _dot(cqn, wqa_ref[...])
    qb = _dot(cqn, wqb_ref[...])
    kv = _dot(kvn, wk_ref[...])
    hw = 2 * LANES
    for hd in range(nh):
        q_ref[:, hd * hw:hd * hw + LANES] = (qa[:, hd * hw:hd * hw + LANES] * scale).astype(BF16)
        rope = qa[:, hd * hw + LANES:(hd + 1) * hw] * cos + qb[:, hd * LANES:(hd + 1) * LANES] * sin
        q_ref[:, hd * hw + LANES:(hd + 1) * hw] = (rope * scale).astype(BF16)
        k_ref[:, hd * hw:hd * hw + LANES] = kv[:, hd * LANES:(hd + 1) * LANES].astype(BF16)
        k_ref[:, hd * hw + LANES:(hd + 1) * hw] = kr_rot
    vt = _dot_nt(wvt_ref[...], kvn).astype(BF16)
    for c in range(vt.shape[1] // tk):
        vt_ref[0, c] = vt[:, c * tk:(c + 1) * tk]


def _mla_in(x2, sc, sh, pos2, w_in, gq, gkv, w_uq, w_ukv, tiles_per_b):
    t, d = x2.shape
    nh = MLA_HEADS
    q_lora, kv_lora = gq.shape[0], gkv.shape[0]
    half = MLA_ROPE // 2
    pad = LANES - MLA_ROPE
    kr_w = w_in[:, q_lora + kv_lora:]
    kr_sw = jnp.concatenate([kr_w[:, half:], kr_w[:, :half]], axis=1)
    zpad = jnp.zeros((d, pad), w_in.dtype)
    win_ext = jnp.concatenate([w_in[:, :q_lora + kv_lora], kr_w, zpad, kr_sw, zpad], axis=1).astype(BF16)
    wq = w_uq.reshape(q_lora, nh, MLA_NOPE + MLA_ROPE)
    q_nope, q_rope = wq[:, :, :MLA_NOPE], wq[:, :, MLA_NOPE:]
    q_rope_sw = jnp.concatenate([q_rope[:, :, half:], q_rope[:, :, :half]], axis=2)
    zq = jnp.zeros((q_lora, nh, pad), w_uq.dtype)
    wqa = jnp.concatenate([q_nope, q_rope, zq], axis=2).reshape(q_lora, nh * 2 * LANES).astype(BF16)
    wqb = jnp.concatenate([q_rope_sw, zq], axis=2).reshape(q_lora, nh * LANES).astype(BF16)
    wkv = w_ukv.reshape(kv_lora, nh, MLA_NOPE + MLA_V)
    wk = wkv[:, :, :MLA_NOPE].reshape(kv_lora, nh * MLA_NOPE).astype(BF16)
    wvt = wkv[:, :, MLA_NOPE:].reshape(kv_lora, nh * MLA_V).T.astype(BF16)
    inv_freq = 1.0 / (ROPE_THETA ** (jnp.arange(0, MLA_ROPE, 2, dtype=F32) / MLA_ROPE))
    invf = jnp.concatenate([inv_freq] * (LANES // half)).reshape(1, LANES)
    phase = jnp.concatenate([jnp.zeros((MLA_ROPE,), F32), jnp.full((pad,), jnp.pi / 2, F32)]).reshape(1, LANES)
    sign = jnp.concatenate([-jnp.ones((half,), F32), jnp.ones((half,), F32), jnp.zeros((pad,), F32)]).reshape(1, LANES)
    tm = TOKEN_TILE
    full = lambda a: pl.BlockSpec(a.shape, lambda i: (0,) * a.ndim)
    row = lambda n: pl.BlockSpec((tm, n), lambda i: (i, 0))
    mod = pl.BlockSpec((1, 1, d), lambda i: (i // tiles_per_b, 0, 0))
    gq2, gkv2 = gq.reshape(1, q_lora), gkv.reshape(1, kv_lora)
    scale = (MLA_NOPE + MLA_ROPE) ** -0.5
    tk = ATTN_TK
    kt_per_tile = tm // tk
    s = tiles_per_b * tm
    return pl.pallas_call(
        functools.partial(_mla_in_kernel, nh=nh, q_lora=q_lora, kv_lora=kv_lora, scale=scale, tk=tk),
        out_shape=[jax.ShapeDtypeStruct((t, nh * 2 * LANES), BF16), jax.ShapeDtypeStruct((t, nh * 2 * LANES), BF16),
                   jax.ShapeDtypeStruct((t // s, s // tk, nh * MLA_V, tk), BF16)],
        grid=(t // tm,),
        in_specs=[row(d), mod, mod, row(1), full(invf), full(phase), full(sign), full(win_ext), full(gq2), full(gkv2),
                  full(wqa), full(wqb), full(wk), full(wvt)],
        out_specs=[row(nh * 2 * LANES), row(nh * 2 * LANES),
                   pl.BlockSpec((1, kt_per_tile, nh * MLA_V, tk),
                                lambda i: (i // tiles_per_b, i % tiles_per_b, 0, 0))],
        compiler_params=_cparams(("parallel",)),
        name="mla_in",
    )(x2, sc, sh, pos2, invf, phase, sign, win_ext, gq2, gkv2, wqa, wqb, wk, wvt)


def _attn_kernel(q_ref, k_ref, vt_ref, o_ref, acc_ref, *, s, tq, tk, hb):
    kpq = tq // tk
    hw = 2 * LANES
    ones = jnp.ones((8, tk), BF16)

    def q_block(qi, carry):
        qrows = pl.ds(pl.multiple_of(qi * tq, tq), tq)
        acc_ref[...] = jnp.zeros_like(acc_ref)

        def tile(j, stats, masked):
            krows = pl.ds(pl.multiple_of(j * tk, tk), tk)
            sts = [_dot_nt(k_ref[krows, h * hw:(h + 1) * hw], q_ref[qrows, h * hw:(h + 1) * hw])
                   for h in range(hb)]
            ps, alphas, out = [], [], []
            for h in range(hb):
                m, st = stats[2 * h], sts[h]
                if masked:
                    key = j * tk + lax.broadcasted_iota(I32, (tk, tq), 0)
                    qry = qi * tq + lax.broadcasted_iota(I32, (tk, tq), 1)
                    st = jnp.where(key <= qry, st, -jnp.inf)
                m_new = jnp.maximum(m, jnp.max(st, axis=0, keepdims=True))
                ps.append(jnp.exp(st - m_new).astype(BF16))
                alphas.append(jnp.exp(m - m_new))
                out.append(m_new)
            for h in range(hb):
                acc_ref[h] = alphas[h] * acc_ref[h] + _dot(vt_ref[0, j, h * MLA_V:(h + 1) * MLA_V, :], ps[h])
                out.insert(2 * h + 1, alphas[h] * stats[2 * h + 1] + _dot(ones, ps[h])[0:1])
            return tuple(out)

        stats = (jnp.full((1, tq), -jnp.inf, F32), jnp.zeros((1, tq), F32)) * hb
        stats = lax.fori_loop(0, qi * kpq, lambda j, c: tile(j, c, False), stats)
        stats = lax.fori_loop(qi * kpq, (qi + 1) * kpq, lambda j, c: tile(j, c, True), stats)
        for h in range(hb):
            o_ref[qrows, h * MLA_V:(h + 1) * MLA_V] = (acc_ref[h] / stats[2 * h + 1]).T.astype(o_ref.dtype)
        return carry

    lax.fori_loop(0, s // tq, q_block, 0)


def _attn(q, k, vt, bsz, s):
    nh = MLA_HEADS
    tq, tk = ATTN_TQ, ATTN_TK
    hb = ATTN_HEADS_PER_STEP
    hw = 2 * LANES
    return pl.pallas_call(
        functools.partial(_attn_kernel, s=s, tq=tq, tk=tk, hb=hb),
        out_shape=jax.ShapeDtypeStruct((bsz * s, nh * MLA_V), BF16),
        grid=(bsz, nh // hb),
        in_specs=[pl.BlockSpec((s, hb * hw), lambda b, h: (b, h)),
                  pl.BlockSpec((s, hb * hw), lambda b, h: (b, h)),
                  pl.BlockSpec((1, s // tk, hb * MLA_V, tk), lambda b, h: (b, 0, h, 0))],
        out_specs=pl.BlockSpec((s, hb * MLA_V), lambda b, h: (b, h)),
        scratch_shapes=[pltpu.VMEM((hb, MLA_V, tq), F32)],
        compiler_params=_cparams(("parallel", "parallel")),
        name="attn",
    )(q, k, vt)


def _route_kernel(x_ref, sc_ref, sh_ref, w_ref, b_ref, lp_ref, wt_ref, tm_ref, offs_ref, te_ref, na_ref,
                  upper_ref, carry_ref, *, tm, ne, ng, row_tile, nt_pad):
    i = pl.program_id(0)
    epg = ne // ng

    @pl.when(i == 0)
    def _():
        r = lax.broadcasted_iota(I32, (tm, tm), 0)
        c = lax.broadcasted_iota(I32, (tm, tm), 1)
        upper_ref[...] = jnp.where(r < c, 1.0, 0.0).astype(BF16)
        carry_ref[...] = jnp.zeros_like(carry_ref)

    h = x_ref[...] * (1.0 + sc_ref[0]) + sh_ref[0]
    h_hi = h.astype(BF16)
    h_lo = (h - h_hi.astype(F32)).astype(BF16)
    w = w_ref[...]
    w_hi = w.astype(BF16)
    w_lo = (w - w_hi.astype(F32)).astype(BF16)
    logits = _dot(h_hi, w_hi) + (_dot(h_hi, w_lo) + _dot(h_lo, w_hi)) + b_ref[...]
    lt = logits.T
    lr = lt[0:ne]
    grow = lax.broadcasted_iota(I32, (8, tm), 0).astype(F32)
    lg = jnp.where(grow < ng, lt[ne:ne + 8], -jnp.inf)
    gmax = jnp.max(lg, axis=0, keepdims=True)
    g_idx = jnp.min(jnp.where(lg == gmax, grow, 1e9), axis=0, keepdims=True)
    g_w = 1.0 / jnp.sum(jnp.exp(lg - gmax), axis=0, keepdims=True)
    erow = lax.broadcasted_iota(I32, (ne, tm), 0).astype(F32)
    in_group = jnp.floor(erow * (1.0 / epg)) == g_idx
    sel = jnp.where(in_group, lr, -jnp.inf)
    v1 = jnp.max(sel, axis=0, keepdims=True)
    i1 = jnp.min(jnp.where(sel == v1, erow, 1e9), axis=0, keepdims=True)
    sel2 = jnp.where(erow == i1, -jnp.inf, sel)
    v2 = jnp.max(sel2, axis=0, keepdims=True)
    i2 = jnp.min(jnp.where(sel2 == v2, erow, 1e9), axis=0, keepdims=True)
    t = jnp.exp(v2 - v1)
    w1 = g_w / (1.0 + t)
    w2 = g_w * t / (1.0 + t)
    oh1 = erow == i1
    oh2 = erow == i2
    member = jnp.where(oh1 | oh2, 1.0, 0.0)
    lcnt = jnp.sum(member, axis=1, keepdims=True)
    lcnt = jnp.floor((lcnt + (SUBLANES - 1)) * (1.0 / SUBLANES)) * SUBLANES
    er_ = lax.broadcasted_iota(I32, (ne, ne), 0)
    ec_ = lax.broadcasted_iota(I32, (ne, ne), 1)
    lstart = jnp.dot(jnp.where(ec_ < er_, 1.0, 0.0).astype(F32), jnp.broadcast_to(lcnt, (ne, LANES)),
                     precision=HIGHEST, preferred_element_type=F32)[:, 0:1]
    lrank = _dot(member.astype(BF16), upper_ref[...]) + lstart
    p1 = jnp.sum(jnp.where(oh1, lrank, 0.0), axis=0, keepdims=True)
    p2 = jnp.sum(jnp.where(oh2, lrank, 0.0), axis=0, keepdims=True)

    orow = lax.broadcasted_iota(I32, (8, tm), 0)
    lp_ref[...] = jnp.where(orow == 0, p1, jnp.where(orow == 1, p2, 0.0)).astype(I32)
    wrow = lax.broadcasted_iota(I32, (LANES, tm), 0)
    wt_ref[...] = jnp.where(wrow == 0, w1, jnp.where(wrow == 1, w2,
                                                     jnp.where(wrow == 2, p1, jnp.where(wrow == 3, p2, 0.0)))).T
    mr = lax.broadcasted_iota(I32, (ne, LANES), 0)
    mc = lax.broadcasted_iota(I32, (ne, LANES), 1)
    to_row = lambda col, lane0: jnp.sum(jnp.where(mr + lane0 == mc, col, 0.0), axis=0, keepdims=True)
    total = jnp.sum(lcnt, axis=0, keepdims=True)
    lane = lax.broadcasted_iota(I32, (1, LANES), 1)
    packed = (to_row(lstart, 0) + to_row(lcnt, ne) + to_row(carry_ref[...], 2 * ne)
              + jnp.where(lane == 3 * ne, total, 0.0))
    trow = lax.broadcasted_iota(I32, (8, LANES), 0)
    tm_ref[...] = jnp.where(trow == 0, packed, 0.0).astype(I32)
    carry_ref[...] = carry_ref[...] + lcnt

    @pl.when(i == pl.num_programs(0) - 1)
    def _():
        cnt = carry_ref[...]
        ntl = jnp.floor((cnt + (row_tile - 1)) * (1.0 / row_tile))
        incl = jnp.where(ec_ <= er_, 1.0, 0.0).astype(F32)
        ends = jnp.dot(incl, jnp.broadcast_to(ntl, (ne, LANES)), precision=HIGHEST,
                       preferred_element_type=F32)
        starts = ends - ntl
        offs_ref[...] = jnp.concatenate([starts * row_tile, ends * row_tile, jnp.broadcast_to(ntl, (ne, LANES)),
                                         jnp.zeros((8, LANES), F32)], axis=0).astype(I32)
        tile = lax.broadcasted_iota(I32, (ne, nt_pad), 1).astype(F32)
        te = jnp.sum(jnp.where(ends[:, 0:1] <= tile, 1.0, 0.0), axis=0, keepdims=True)
        te_ref[...] = jnp.broadcast_to(jnp.minimum(te, ne - 1.0), (8, nt_pad)).astype(I32)
        na_ref[...] = jnp.broadcast_to(ends[ne - 1:ne, :], (8, LANES)).astype(I32)


def _moe_tiles(t):
    rows = 2 * t + (SUBLANES - 1) * MOE_EXPERTS * (t // TOKEN_TILE)
    nt_max = -(-rows // MOE_ROW_TILE) + MOE_EXPERTS
    nt_pad = -(-nt_max // LANES) * LANES
    return nt_max, nt_pad


def _sorted_rows(tm):
    return -(-(2 * tm + (SUBLANES - 1) * MOE_EXPERTS) // LANES) * LANES


def _route(x2, sc, sh, w_group, b_group, w_router, b_router, tiles_per_b):
    t, d = x2.shape
    ne, ng = MOE_EXPERTS, MOE_GROUPS
    tm = TOKEN_TILE
    _, nt_pad = _moe_tiles(t)
    wcat = jnp.concatenate([w_router, w_group, jnp.zeros((d, LANES - ne - ng), F32)], axis=1)
    bcat = jnp.concatenate([b_router, b_group, jnp.zeros((LANES - ne - ng,), F32)]).reshape(1, LANES)
    const = lambda shp: pl.BlockSpec(shp, lambda i: (0,) * len(shp))
    return pl.pallas_call(
        functools.partial(_route_kernel, tm=tm, ne=ne, ng=ng, row_tile=MOE_ROW_TILE, nt_pad=nt_pad),
        out_shape=[jax.ShapeDtypeStruct((8, t), I32), jax.ShapeDtypeStruct((t, LANES), F32),
                   jax.ShapeDtypeStruct((8 * (t // tm), LANES), I32),
                   jax.ShapeDtypeStruct((3 * ne + 8, LANES), I32), jax.ShapeDtypeStruct((8, nt_pad), I32),
                   jax.ShapeDtypeStruct((8, LANES), I32)],
        grid=(t // tm,),
        in_specs=[pl.BlockSpec((tm, d), lambda i: (i, 0)),
                  pl.BlockSpec((1, 1, d), lambda i: (i // tiles_per_b, 0, 0)),
                  pl.BlockSpec((1, 1, d), lambda i: (i // tiles_per_b, 0, 0)),
                  const((d, LANES)), const((1, LANES))],
        out_specs=[pl.BlockSpec((8, tm), lambda i: (0, i)), pl.BlockSpec((tm, LANES), lambda i: (i, 0)),
                   pl.BlockSpec((8, LANES), lambda i: (i, 0)),
                   const((3 * ne + 8, LANES)), const((8, nt_pad)), const((8, LANES))],
        scratch_shapes=[pltpu.VMEM((tm, tm), BF16), pltpu.VMEM((ne, 1), F32)],
        compiler_params=_cparams(("arbitrary",)),
        name="moe_route",
    )(x2, sc, sh, wcat, bcat)


def _pack_bf16_pairs(v):
    half = v.shape[1] // 2
    bits = lambda a: lax.bitcast_convert_type(a.astype(BF16).astype(F32), U32)
    return (bits(v[:, :half]) >> 16) | (bits(v[:, half:]) & jnp.uint32(0xFFFF0000))


def _unpack_bf16_pairs(u):
    lo = lax.bitcast_convert_type(u << 16, F32).astype(BF16)
    hi = lax.bitcast_convert_type(u & jnp.uint32(0xFFFF0000), F32).astype(BF16)
    return lo, hi


def _start_runs(tm_ref, tile, ne, copy):
    def expert_run(e, carry):
        lstart = tm_ref[tile, e]
        n = tm_ref[tile, ne + e]
        before = tm_ref[tile, 2 * ne + e]
        nfull = n // RUN_PIECE

        def piece(c, carry2):
            copy(e, pl.multiple_of(lstart + c * RUN_PIECE, SUBLANES),
                 pl.multiple_of(before + c * RUN_PIECE, SUBLANES), RUN_PIECE).start()
            return carry2

        lax.fori_loop(0, nfull, piece, 0)
        off = nfull * RUN_PIECE
        p = RUN_PIECE // 2
        while p >= SUBLANES:
            @pl.when((n & p) != 0)
            def _():
                copy(e, pl.multiple_of(lstart + off, SUBLANES), pl.multiple_of(before + off, SUBLANES), p).start()
            off = off + (n & p)
            p //= 2
        return carry

    lax.fori_loop(0, ne, expert_run, 0)


def _wait_runs(tm_ref, tile, ne, piece_copy):
    def wait8(c, carry):
        piece_copy.wait()
        return carry
    lax.fori_loop(0, tm_ref[tile, 3 * ne] // SUBLANES, wait8, 0)


def _dispatch_kernel(offs_ref, gend_ref, ntl_ref, tm_ref, x_ref, sc_ref, sh_ref, lp_ref, xs_ref,
                     h_ref, z_ref, sem_z, sem_r, *, tm, ne, row_tile, nt_max, ntile, srows):
    i = pl.program_id(0)
    tile = i - 1

    @pl.when(i == 0)
    def _():
        z_ref[...] = jnp.zeros_like(z_ref)
        for e in range(ne):
            @pl.when(ntl_ref[e] > 0)
            def _():
                start = pl.multiple_of(gend_ref[e] - row_tile, row_tile)
                cp = pltpu.make_async_copy(z_ref, xs_ref.at[pl.ds(start, row_tile), :], sem_z)
                cp.start()
                cp.wait()
        for back in range(1, nt_max - (2 * tm * ntile) // row_tile + 1):
            @pl.when(nt_max - back >= gend_ref[ne - 1] // row_tile)
            def _():
                cp = pltpu.make_async_copy(z_ref, xs_ref.at[pl.ds((nt_max - back) * row_tile, row_tile), :], sem_z)
                cp.start()
                cp.wait()

    def waiter(t):
        slot = t % 2
        return pltpu.make_async_copy(h_ref.at[slot, pl.ds(0, SUBLANES), :], xs_ref.at[pl.ds(0, SUBLANES), :],
                                     sem_r.at[slot])

    @pl.when((tile >= 2) & (tile <= ntile))
    def _():
        _wait_runs(tm_ref, tile - 2, ne, waiter(tile - 2))

    @pl.when(tile == ntile)
    def _():
        _wait_runs(tm_ref, tile - 1, ne, waiter(tile - 1))

    @pl.when((tile >= 0) & (tile < ntile))
    def _():
        slot = tile % 2
        h = (x_ref[...] * (1.0 + sc_ref[0]) + sh_ref[0]).astype(BF16)
        row = lax.broadcasted_iota(I32, (srows, tm), 0)
        lp = lp_ref[...]
        onehot = jnp.where(row == lp[0:1, :], 1.0, jnp.where(row == lp[1:2, :], 1.0, 0.0)).astype(BF16)
        h_ref[slot] = _pack_bf16_pairs(_dot(onehot, h))

        def copy(e, local_row, rows_before, rows):
            dst = pl.multiple_of(offs_ref[e] + rows_before, SUBLANES)
            return pltpu.make_async_copy(h_ref.at[slot, pl.ds(local_row, rows), :], xs_ref.at[pl.ds(dst, rows), :],
                                         sem_r.at[slot])

        _start_runs(tm_ref, tile, ne, copy)


def _dispatch(x2, sc, sh, lp, tmeta, offs, gend, ntl, tiles_per_b):
    t, d = x2.shape
    tm = TOKEN_TILE
    nt_max, _ = _moe_tiles(t)
    ntile = t // tm
    srows = _sorted_rows(tm)
    cur = lambda i: jnp.clip(i - 1, 0, ntile - 1)
    grid_spec = pltpu.PrefetchScalarGridSpec(
        num_scalar_prefetch=4,
        grid=(ntile + 2,),
        in_specs=[pl.BlockSpec((tm, d), lambda i, *_: (cur(i), 0)),
                  pl.BlockSpec((1, 1, d), lambda i, *_: (cur(i) // tiles_per_b, 0, 0)),
                  pl.BlockSpec((1, 1, d), lambda i, *_: (cur(i) // tiles_per_b, 0, 0)),
                  pl.BlockSpec((8, tm), lambda i, *_: (0, cur(i)))],
        out_specs=pl.BlockSpec(memory_space=pl.ANY),
        scratch_shapes=[pltpu.VMEM((2, srows, d // 2), U32), pltpu.VMEM((MOE_ROW_TILE, d // 2), U32),
                        pltpu.SemaphoreType.DMA, pltpu.SemaphoreType.DMA((2,))])
    return pl.pallas_call(
        functools.partial(_dispatch_kernel, tm=tm, ne=MOE_EXPERTS, row_tile=MOE_ROW_TILE, nt_max=nt_max,
                          ntile=ntile, srows=srows),
        out_shape=jax.ShapeDtypeStruct((nt_max * MOE_ROW_TILE, d // 2), U32),
        grid_spec=grid_spec,
        compiler_params=_cparams(("arbitrary",)),
        name="moe_dispatch",
    )(offs, gend, ntl, tmeta, x2, sc, sh, lp)


def _gmm_kernel(te_ref, na_ref, xs_ref, wg_ref, wu_ref, wd_ref, ys_ref, wgu_buf, wd_buf, *, ff, tr):
    j = pl.program_id(0)

    @pl.when(j < na_ref[0])
    def _():
        changed = (j == 0) | (te_ref[j] != te_ref[jnp.maximum(j - 1, 0)])

        @pl.when(changed)
        def _():
            wgu_buf[:, :ff] = wg_ref[0, 0].astype(BF16)
            wgu_buf[:, ff:] = wu_ref[0, 0].astype(BF16)
            wd_buf[...] = wd_ref[0, 0].astype(BF16)

        halves = [slice(c * (tr // 2), (c + 1) * (tr // 2)) for c in range(2)]
        half_d = wgu_buf.shape[0] // 2
        gus = []
        for rows in halves:
            lo, hi = _unpack_bf16_pairs(xs_ref[rows, :])
            gus.append(_dot(lo, wgu_buf[:half_d, :]) + _dot(hi, wgu_buf[half_d:, :]))
        hids = [(gu[:, :ff] * _sigmoid(gu[:, :ff]) * gu[:, ff:]).astype(BF16) for gu in gus]
        for rows, hid in zip(halves, hids):
            ys_ref[rows, :] = _pack_bf16_pairs(_dot(hid, wd_buf[...]))

    @pl.when(j >= na_ref[0])
    def _():
        ys_ref[...] = jnp.zeros_like(ys_ref)


def _gmm(xs, te, na, w_gate, w_up, w_down, layer):
    d, ff = w_gate.shape[-2:]
    tr = MOE_ROW_TILE
    ns = xs.shape[0]
    act = lambda j, te_ref, na_ref: jnp.minimum(j, na_ref[0] - 1)
    grid_spec = pltpu.PrefetchScalarGridSpec(
        num_scalar_prefetch=2,
        grid=(ns // tr,),
        in_specs=[pl.BlockSpec((tr, d // 2), lambda j, te_ref, na_ref: (act(j, te_ref, na_ref), 0)),
                  pl.BlockSpec((1, 1, d, ff), lambda j, te_ref, na_ref: (layer, te_ref[act(j, te_ref, na_ref)], 0, 0)),
                  pl.BlockSpec((1, 1, d, ff), lambda j, te_ref, na_ref: (layer, te_ref[act(j, te_ref, na_ref)], 0, 0)),
                  pl.BlockSpec((1, 1, ff, d), lambda j, te_ref, na_ref: (layer, te_ref[act(j, te_ref, na_ref)], 0, 0))],
        out_specs=pl.BlockSpec((tr, d // 2), lambda j, te_ref, na_ref: (j, 0)),
        scratch_shapes=[pltpu.VMEM((d, 2 * ff), BF16), pltpu.VMEM((ff, d), BF16)])
    return pl.pallas_call(
        functools.partial(_gmm_kernel, ff=ff, tr=tr),
        out_shape=jax.ShapeDtypeStruct((ns, d // 2), U32),
        grid_spec=grid_spec,
        compiler_params=_cparams(("arbitrary",)),
        name="moe_gmm",
    )(te, na, xs, w_gate, w_up, w_down)


def _combine_kernel(offs_ref, tm_ref, wt_ref, x_ref, gate_ref, lg_ref, lb_ref, ys_ref, o_ref,
                    buf_ref, sem_r, *, tm, ne, ntile, srows, alpha):
    i = pl.program_id(0)

    @pl.when(i == 0)
    def _():
        buf_ref[...] = jnp.zeros_like(buf_ref)

    @pl.when(i < ntile)
    def _():
        slot = i % 2

        def copy(e, local_row, rows_before, rows):
            src = pl.multiple_of(offs_ref[e] + rows_before, SUBLANES)
            return pltpu.make_async_copy(ys_ref.at[pl.ds(src, rows), :], buf_ref.at[slot, pl.ds(local_row, rows), :],
                                         sem_r.at[slot])

        _start_runs(tm_ref, i, ne, copy)

    @pl.when(i > 0)
    def _():
        slot = (i - 1) % 2
        _wait_runs(tm_ref, i - 1, ne,
                   pltpu.make_async_copy(ys_ref.at[pl.ds(0, SUBLANES), :], buf_ref.at[slot, pl.ds(0, SUBLANES), :],
                                         sem_r.at[slot]))
        w = wt_ref[...]
        lo, hi = _unpack_bf16_pairs(buf_ref[slot])
        col = lax.broadcasted_iota(I32, (tm, srows), 1).astype(F32)

        def pick(k):
            onehot = jnp.where(col == w[:, 2 + k:3 + k], 1.0, 0.0).astype(BF16)
            return jnp.concatenate([_dot(onehot, lo), _dot(onehot, hi)], axis=1)

        ffn = w[:, 0:1] * pick(0) + w[:, 1:2] * pick(1)
        y = alpha * x_ref[...] + (1.0 + gate_ref[0]) * ffn
        o_ref[...] = _layer_norm(y, lg_ref[...], lb_ref[...])


def _combine(ys, tmeta, wt, x2, gate, ln_g, ln_b, offs, tiles_per_b, alpha):
    t, d = x2.shape
    tm = TOKEN_TILE
    ntile = t // tm
    srows = _sorted_rows(tm)
    prev = lambda i: jnp.maximum(i - 1, 0)
    vec = pl.BlockSpec((1, d), lambda i, *_: (0, 0))
    grid_spec = pltpu.PrefetchScalarGridSpec(
        num_scalar_prefetch=2,
        grid=(ntile + 1,),
        in_specs=[pl.BlockSpec((tm, LANES), lambda i, *_: (prev(i), 0)),
                  pl.BlockSpec((tm, d), lambda i, *_: (prev(i), 0)),
                  pl.BlockSpec((1, 1, d), lambda i, *_: (prev(i) // tiles_per_b, 0, 0)),
                  vec, vec,
                  pl.BlockSpec(memory_space=pl.ANY)],
        out_specs=pl.BlockSpec((tm, d), lambda i, *_: (prev(i), 0)),
        scratch_shapes=[pltpu.VMEM((2, srows, d // 2), U32), pltpu.SemaphoreType.DMA((2,))])
    return pl.pallas_call(
        functools.partial(_combine_kernel, tm=tm, ne=MOE_EXPERTS, ntile=ntile, srows=srows, alpha=alpha),
        out_shape=jax.ShapeDtypeStruct((t, d), F32),
        grid_spec=grid_spec,
        compiler_params=_cparams(("arbitrary",)),
        name="moe_combine",
    )(offs, tmeta, wt, x2, gate, ln_g.reshape(1, d), ln_b.reshape(1, d), ys)


def _moe_block(x2, sc, sh, gate, ln_g, ln_b, w_group, b_group, w_router, b_router, w_gate, w_up, w_down,
               layer, tiles_per_b, alpha):
    ne = MOE_EXPERTS
    nt_max, _ = _moe_tiles(x2.shape[0])
    lp, wt, tmeta, meta, te, na = _route(x2, sc, sh, w_group, b_group, w_router, b_router, tiles_per_b)
    offs, gend, ntl = meta[:ne, 0], meta[ne:2 * ne, 0], meta[2 * ne:3 * ne, 0]
    tmeta = tmeta.reshape(-1, SUBLANES, LANES)[:, 0, :]
    xs = _dispatch(x2, sc, sh, lp, tmeta, offs, gend, ntl, tiles_per_b)
    ys = _gmm(xs, te[0, :nt_max], na[0, :1], w_gate, w_up, w_down, layer)
    return _combine(ys, tmeta, wt, x2, gate, ln_g, ln_b, offs, tiles_per_b, alpha)


def kernel(x, c, positions, ada_w, ada_b, ln_mix_g, ln_mix_b, ln_ffn_g, ln_ffn_b, ab_w_in, conv_w, conv_b, conv_ln_g, conv_ln_b, gla_gate_w, gla_gate_b, gla_norm_g, ab_w_out, mla_w_in, mla_q_norm_g, mla_kv_norm_g, mla_w_uq, mla_w_ukv, mla_w_out, moe_w_group, moe_b_group, moe_w_router, moe_b_router, moe_w_gate, moe_w_up, moe_w_down):
    bsz, s, d = x.shape
    depth = ada_w.shape[0]
    t = bsz * s
    tiles_per_b = s // TOKEN_TILE
    alpha = (2 * depth) ** 0.25
    mod = _ada(c, ada_w, ada_b).reshape(depth, bsz, 6, 1, d)
    x2 = x.reshape(t, d)
    for layer in range(depth):
        sh_m, sc_m, g_m, sh_f, sc_f, g_f = (mod[layer, :, n] for n in range(6))
        i = layer // 2
        if layer % 2 == 0:
            uc, q, k, v, r, gl = _ab_in(x2, sc_m, sh_m, ab_w_in[i], gla_gate_w[i], gla_gate_b[i], tiles_per_b)
            y_a = _conv(uc.reshape(bsz, s, -1), conv_w[i], conv_b[i], conv_ln_g[i], conv_ln_b[i])
            b3 = lambda a: a.reshape(bsz, s, -1)
            y_b = _gla(b3(q), b3(k), b3(v), b3(gl), b3(r), gla_norm_g[i])
            w_out = ab_w_out[i].astype(BF16)
            cc = y_a.shape[-1]
            acts = [y_a.reshape(t, cc), y_b.reshape(t, -1)]
            weights = [w_out[:cc], w_out[cc:]]
        else:
            qc, kc, vv = _mla_in(x2, sc_m, sh_m, positions.reshape(t, 1), mla_w_in[i], mla_q_norm_g[i],
                                 mla_kv_norm_g[i], mla_w_uq[i], mla_w_ukv[i], tiles_per_b)
            acts = [_attn(qc, kc, vv, bsz, s)]
            weights = [mla_w_out[i].astype(BF16)]
        x2 = _proj_ln(acts, weights, x2, g_m, ln_mix_g[layer], ln_mix_b[layer], tiles_per_b, alpha)
        x2 = _moe_block(x2, sc_f, sh_f, g_f, ln_ffn_g[layer], ln_ffn_b[layer], moe_w_group[layer], moe_b_group[layer],
                        moe_w_router[layer], moe_b_router[layer], moe_w_gate, moe_w_up, moe_w_down,
                        layer, tiles_per_b, alpha)
    return x2.reshape(bsz, s, d)
```

```python
import functools

import jax
import jax.numpy as jnp
from jax import lax
from jax.experimental import pallas as pl
from jax.experimental.pallas import tpu as pltpu

F32 = jnp.float32
BF16 = jnp.bfloat16
I32 = jnp.int32
U32 = jnp.uint32
HIGHEST = lax.Precision.HIGHEST

LN_EPS = 1e-5
RMS_EPS = 1e-6
CONV_WIDTH = 31
GLA_HEADS = 4
GLA_GATE_TAU = 16.0
MLA_HEADS = 8
MLA_NOPE = 128
MLA_ROPE = 64
MLA_V = 128
ROPE_THETA = 10000.0
MOE_GROUPS = 4
MOE_EXPERTS_PER_GROUP = 8
MOE_EXPERTS = MOE_GROUPS * MOE_EXPERTS_PER_GROUP

LANES = 128
SUBLANES = 8
TOKEN_TILE = 512
GLA_CHUNK = 128
GLA_BLOCK = 512
CONV_ROWS = 32
CONV_HALO = 32
ATTN_TQ = 512
ATTN_TK = 256
ATTN_HEADS_PER_STEP = 4
MOE_ROW_TILE = 256
RUN_PIECE = 64
VMEM_LIMIT = 48 * 1024 * 1024


def _cparams(sem):
    return pltpu.CompilerParams(dimension_semantics=sem, vmem_limit_bytes=VMEM_LIMIT)


def _sigmoid(x):
    return 1.0 / (1.0 + jnp.exp(-x))


def _dot(a, b):
    return jnp.dot(a, b, preferred_element_type=F32)


def _dot_nt(a, b):
    return lax.dot_general(a, b, (((1,), (1,)), ((), ())), preferred_element_type=F32)


def _dot_tn(a, b):
    return lax.dot_general(a, b, (((0,), (0,)), ((), ())), preferred_element_type=F32)


def _layer_norm(y, g, b):
    mu = jnp.mean(y, axis=-1, keepdims=True)
    d = y - mu
    var = jnp.mean(d * d, axis=-1, keepdims=True)
    return d * lax.rsqrt(var + LN_EPS) * g + b


def _ada_kernel(c_ref, w_ref, b_ref, o_ref):
    c = c_ref[...]
    o_ref[0] = jnp.dot(c * _sigmoid(c), w_ref[0], precision=HIGHEST, preferred_element_type=F32) + b_ref[0]


def _ada(c, ada_w, ada_b):
    depth, d, n = ada_w.shape
    bsz = c.shape[0]
    tn = 1536
    return pl.pallas_call(
        _ada_kernel,
        out_shape=jax.ShapeDtypeStruct((depth, bsz, n), F32),
        grid=(depth, n // tn),
        in_specs=[pl.BlockSpec((bsz, d), lambda l, j: (0, 0)),
                  pl.BlockSpec((1, d, tn), lambda l, j: (l, 0, j)),
                  pl.BlockSpec((1, 1, tn), lambda l, j: (l, 0, j))],
        out_specs=pl.BlockSpec((1, bsz, tn), lambda l, j: (l, 0, j)),
        compiler_params=_cparams(("parallel", "parallel")),
        name="ada",
    )(c, ada_w, ada_b.reshape(depth, 1, n))


def _ab_in_kernel(x_ref, sc_ref, sh_ref, wc_ref, wq_ref, wk_ref, wv_ref, wr_ref, wg_ref, gw_ref, gb_ref,
                  uc_ref, q_ref, k_ref, v_ref, r_ref, gl_ref):
    h = (x_ref[...] * (1.0 + sc_ref[0]) + sh_ref[0]).astype(BF16)
    uc_ref[...] = _dot(h, wc_ref[...])
    q_ref[...] = _dot(h, wq_ref[...])
    k_ref[...] = _dot(h, wk_ref[...])
    v_ref[...] = _dot(h, wv_ref[...]).astype(v_ref.dtype)
    r_ref[...] = _dot(h, wr_ref[...])
    g_low = _dot(h, wg_ref[...])
    z = jnp.dot(g_low, gw_ref[...], precision=HIGHEST, preferred_element_type=F32) + gb_ref[...]
    gl_ref[...] = (jnp.minimum(z, 0.0) - jnp.log(1.0 + jnp.exp(-jnp.abs(z)))) * (1.0 / GLA_GATE_TAU)


def _ab_in(x2, sc, sh, w_in, gate_w, gate_b, tiles_per_b):
    t, d = x2.shape
    cc2 = d
    kw = d // 4
    vw = d // 2
    rank = gate_w.shape[0]
    splits = [cc2, cc2 + kw, cc2 + 2 * kw, cc2 + 2 * kw + vw, cc2 + 2 * kw + 2 * vw]
    wb = w_in.astype(BF16)
    ws = [wb[:, :splits[0]], wb[:, splits[0]:splits[1]], wb[:, splits[1]:splits[2]],
          wb[:, splits[2]:splits[3]], wb[:, splits[3]:splits[4]], wb[:, splits[4]:]]
    tm = TOKEN_TILE
    full = lambda a: pl.BlockSpec(a.shape, lambda i: (0,) * a.ndim)
    row = lambda n: pl.BlockSpec((tm, n), lambda i: (i, 0))
    mod = pl.BlockSpec((1, 1, d), lambda i: (i // tiles_per_b, 0, 0))
    gb2 = gate_b.reshape(1, kw)
    widths = [cc2, kw, kw, vw, vw, kw]
    return pl.pallas_call(
        _ab_in_kernel,
        out_shape=[jax.ShapeDtypeStruct((t, n), BF16 if idx == 3 else F32) for idx, n in enumerate(widths)],
        grid=(t // tm,),
        in_specs=[row(d), mod, mod] + [full(w) for w in ws] + [full(gate_w), full(gb2)],
        out_specs=[row(n) for n in widths],
        compiler_params=_cparams(("parallel",)),
        name="ab_in",
    )(x2, sc, sh, *ws, gate_w, gb2)


def _conv_kernel(u_ref, halo_ref, cw_ref, cb_ref, lg_ref, lb_ref, o_ref, hp_ref, *, ts, cc):
    j = pl.program_id(1)

    def glu(u):
        return u[:, :cc] * _sigmoid(u[:, cc:])

    hp_ref[0, 0:CONV_HALO, :] = jnp.where(j > 0, glu(halo_ref[0]), 0.0)
    hp_ref[0, CONV_HALO:CONV_HALO + ts, :] = glu(u_ref[0])
    nrow = CONV_HALO + ts
    for b in range(1, SUBLANES):
        hp_ref[b, 0:nrow - SUBLANES, :] = hp_ref[0, b:nrow - SUBLANES + b, :]
    shift = CONV_HALO - (CONV_WIDTH - 1)
    for rb in range(ts // CONV_ROWS):
        r0 = rb * CONV_ROWS
        acc = jnp.zeros((CONV_ROWS, cc), F32)
        for tap in range(CONV_WIDTH):
            lo = r0 + shift + tap
            base = lo // SUBLANES * SUBLANES
            acc = acc + cw_ref[tap:tap + 1, :] * hp_ref[lo - base, base:base + CONV_ROWS, :]
        y = _layer_norm(acc + cb_ref[...], lg_ref[...], lb_ref[...])
        o_ref[0, r0:r0 + CONV_ROWS, :] = (y * _sigmoid(y)).astype(o_ref.dtype)


def _conv(u3, conv_w, conv_b, ln_g, ln_b):
    bsz, s, cc2 = u3.shape
    cc = cc2 // 2
    ts = TOKEN_TILE
    hb = ts // CONV_HALO
    vec = lambda a: pl.BlockSpec((1, cc), lambda b, j: (0, 0))
    return pl.pallas_call(
        functools.partial(_conv_kernel, ts=ts, cc=cc),
        out_shape=jax.ShapeDtypeStruct((bsz, s, cc), BF16),
        grid=(bsz, s // ts),
        in_specs=[pl.BlockSpec((1, ts, cc2), lambda b, j: (b, j, 0)),
                  pl.BlockSpec((1, CONV_HALO, cc2), lambda b, j: (b, jnp.maximum(j * hb - 1, 0), 0)),
                  pl.BlockSpec((CONV_WIDTH, cc), lambda b, j: (0, 0)),
                  vec(conv_b), vec(ln_g), vec(ln_b)],
        out_specs=pl.BlockSpec((1, ts, cc), lambda b, j: (b, j, 0)),
        scratch_shapes=[pltpu.VMEM((SUBLANES, CONV_HALO + ts, cc), F32)],
        compiler_params=_cparams(("parallel", "parallel")),
        name="conv",
    )(u3, u3, conv_w, conv_b.reshape(1, cc), ln_g.reshape(1, cc), ln_b.reshape(1, cc))


def _gla_kernel(q_ref, k_ref, v_ref, gl_ref, r_ref, ng_ref, o_ref, st_ref, *, nh, dk, dv, gc, nchunks):
    @pl.when(pl.program_id(1) == 0)
    def _():
        st_ref[...] = jnp.zeros_like(st_ref)

    row = lax.broadcasted_iota(I32, (gc, gc), 0)
    col = lax.broadcasted_iota(I32, (gc, gc), 1)
    causal = col <= row
    tri = jnp.where(causal, 1.0, 0.0).astype(F32)
    scale = dk ** -0.5

    def chunk(c, carry):
        r0 = pl.multiple_of(c * gc, gc)
        rows = pl.ds(r0, gc)
        g = gl_ref[0, rows, :]
        b = jnp.dot(tri, g, precision=HIGHEST, preferred_element_type=F32)
        b_last = b[gc - 1:gc, :]
        mid = 0.5 * b_last
        q = q_ref[0, rows, :] * scale
        k = k_ref[0, rows, :]
        v = v_ref[0, rows, :].astype(BF16)
        q_in = (q * jnp.exp(b - mid)).astype(BF16)
        k_in = (k * jnp.exp(mid - b)).astype(BF16)
        q_st = (q * jnp.exp(b)).astype(BF16)
        k_st = (k * jnp.exp(b_last - b)).astype(BF16)
        decay = jnp.exp(b_last)
        r = r_ref[0, rows, :]
        gate = r * _sigmoid(r)
        ks = [slice(h * dk, (h + 1) * dk) for h in range(nh)]
        vs = [slice(h * dv, (h + 1) * dv) for h in range(nh)]
        sts = [st_ref[h] for h in range(nh)]
        scores = [_dot_nt(q_in[:, ks[h]], k_in[:, ks[h]]) for h in range(nh)]
        inter = [_dot_nt(q_st[:, ks[h]], sts[h].astype(BF16)) for h in range(nh)]
        update = [_dot_tn(v[:, vs[h]], k_st[:, ks[h]]) for h in range(nh)]
        atts = [jnp.where(causal, sc, 0.0).astype(BF16) for sc in scores]
        outs = [_dot(atts[h], v[:, vs[h]]) + inter[h] for h in range(nh)]
        for h in range(nh):
            st_ref[h] = sts[h] * decay[:, ks[h]] + update[h]
            o = outs[h]
            o = o * lax.rsqrt(jnp.mean(o * o, axis=-1, keepdims=True) + RMS_EPS) * ng_ref[:, vs[h]]
            o_ref[0, rows, vs[h]] = (o * gate[:, vs[h]]).astype(o_ref.dtype)
        return carry

    lax.fori_loop(0, nchunks, chunk, 0)


def _gla(q3, k3, v3, gl3, r3, norm_g):
    bsz, s, kw = q3.shape
    vw = v3.shape[-1]
    nh = GLA_HEADS
    dk, dv = kw // nh, vw // nh
    cb = GLA_BLOCK
    gc = GLA_CHUNK
    blk = lambda n: pl.BlockSpec((1, cb, n), lambda b, j: (b, j, 0))
    return pl.pallas_call(
        functools.partial(_gla_kernel, nh=nh, dk=dk, dv=dv, gc=gc, nchunks=cb // gc),
        out_shape=jax.ShapeDtypeStruct((bsz, s, vw), BF16),
        grid=(bsz, s // cb),
        in_specs=[blk(kw), blk(kw), blk(vw), blk(kw), blk(vw), pl.BlockSpec((1, vw), lambda b, j: (0, 0))],
        out_specs=blk(vw),
        scratch_shapes=[pltpu.VMEM((nh, dv, dk), F32)],
        compiler_params=_cparams(("parallel", "arbitrary")),
        name="gla",
    )(q3, k3, v3, gl3, r3, norm_g.reshape(1, vw))


def _proj_ln_kernel(*refs, n_in, alpha):
    a_refs, w_refs = refs[:n_in], refs[n_in:2 * n_in]
    x_ref, gate_ref, lg_ref, lb_ref, o_ref = refs[2 * n_in:]
    mix = _dot(a_refs[0][...], w_refs[0][...])
    for a_ref, w_ref in zip(a_refs[1:], w_refs[1:]):
        mix = mix + _dot(a_ref[...], w_ref[...])
    y = alpha * x_ref[...] + (1.0 + gate_ref[0]) * mix
    o_ref[...] = _layer_norm(y, lg_ref[...], lb_ref[...])


def _proj_ln(acts, weights, x2, gate, ln_g, ln_b, tiles_per_b, alpha):
    t, d = x2.shape
    tm = TOKEN_TILE
    n_in = len(acts)
    row = lambda n: pl.BlockSpec((tm, n), lambda i: (i, 0))
    full = lambda a: pl.BlockSpec(a.shape, lambda i: (0,) * a.ndim)
    vec = pl.BlockSpec((1, d), lambda i: (0, 0))
    return pl.pallas_call(
        functools.partial(_proj_ln_kernel, n_in=n_in, alpha=alpha),
        out_shape=jax.ShapeDtypeStruct((t, d), F32),
        grid=(t // tm,),
        in_specs=[row(a.shape[1]) for a in acts] + [full(w) for w in weights]
                 + [row(d), pl.BlockSpec((1, 1, d), lambda i: (i // tiles_per_b, 0, 0)), vec, vec],
        out_specs=row(d),
        compiler_params=_cparams(("parallel",)),
        name="proj_ln",
    )(*acts, *weights, x2, gate, ln_g.reshape(1, d), ln_b.reshape(1, d))


def _mla_in_kernel(x_ref, sc_ref, sh_ref, pos_ref, invf_ref, sign_ref, win_ref, gq_ref, gkv_ref,
                   wqa_ref, wqb_ref, wk_ref, wvt_ref, q_ref, k_ref, vt_ref, *, nh, q_lora, kv_lora, scale, tk):
    h = (x_ref[...] * (1.0 + sc_ref[0]) + sh_ref[0]).astype(BF16)
    u = _dot(h, win_ref[...])
    cq = u[:, :q_lora]
    ckv = u[:, q_lora:q_lora + kv_lora]
    kr = u[:, q_lora + kv_lora:q_lora + kv_lora + LANES]
    kr_sw = u[:, q_lora + kv_lora + LANES:]
    cqn = (cq * lax.rsqrt(jnp.mean(cq * cq, axis=-1, keepdims=True) + RMS_EPS) * gq_ref[...]).astype(BF16)
    kvn = (ckv * lax.rsqrt(jnp.mean(ckv * ckv, axis=-1, keepdims=True) + RMS_EPS) * gkv_ref[...]).astype(BF16)
    ang = pos_ref[...].astype(F32) * invf_ref[...]
    cos = jnp.cos(ang)
    sin = jnp.sin(ang) * sign_ref[...]
    kr_rot = (kr * cos + kr_sw * sin).astype(BF16)
    qa = _dot(cqn, wqa_ref[...])
    qb = _dot(cqn, wqb_ref[...])
    kv = _dot(kvn, wk_ref[...])
    hw = 2 * LANES
    for hd in range(nh):
        q_ref[:, hd * hw:hd * hw + LANES] = (qa[:, hd * hw:hd * hw + LANES] * scale).astype(BF16)
        rope = qa[:, hd * hw + LANES:(hd + 1) * hw] * cos + qb[:, hd * LANES:(hd + 1) * LANES] * sin
        q_ref[:, hd * hw + LANES:(hd + 1) * hw] = (rope * scale).astype(BF16)
        k_ref[:, hd * hw:hd * hw + LANES] = kv[:, hd * LANES:(hd + 1) * LANES].astype(BF16)
        k_ref[:, hd * hw + LANES:(hd + 1) * hw] = kr_rot
    vt = _dot_nt(wvt_ref[...], kvn).astype(BF16)
    for c in range(vt.shape[1] // tk):
        vt_ref[0, c] = vt[:, c * tk:(c + 1) * tk]


def _mla_in(x2, sc, sh, pos2, w_in, gq, gkv, w_uq, w_ukv, tiles_per_b):
    t, d = x2.shape
    nh = MLA_HEADS
    q_lora, kv_lora = gq.shape[0], gkv.shape[0]
    half = MLA_ROPE // 2
    pad = LANES - MLA_ROPE
    kr_w = w_in[:, q_lora + kv_lora:]
    kr_sw = jnp.concatenate([kr_w[:, half:], kr_w[:, :half]], axis=1)
    zpad = jnp.zeros((d, pad), w_in.dtype)
    win_ext = jnp.concatenate([w_in[:, :q_lora + kv_lora], kr_w, zpad, kr_sw, zpad], axis=1).astype(BF16)
    wq = w_uq.reshape(q_lora, nh, MLA_NOPE + MLA_ROPE)
    q_nope, q_rope = wq[:, :, :MLA_NOPE], wq[:, :, MLA_NOPE:]
    q_rope_sw = jnp.concatenate([q_rope[:, :, half:], q_rope[:, :, :half]], axis=2)
    zq = jnp.zeros((q_lora, nh, pad), w_uq.dtype)
    wqa = jnp.concatenate([q_nope, q_rope, zq], axis=2).reshape(q_lora, nh * 2 * LANES).astype(BF16)
    wqb = jnp.concatenate([q_rope_sw, zq], axis=2).reshape(q_lora, nh * LANES).astype(BF16)
    wkv = w_ukv.reshape(kv_lora, nh, MLA_NOPE + MLA_V)
    wk = wkv[:, :, :MLA_NOPE].reshape(kv_lora, nh * MLA_NOPE).astype(BF16)
    wvt = wkv[:, :, MLA_NOPE:].reshape(kv_lora, nh * MLA_V).T.astype(BF16)
    inv_freq = 1.0 / (ROPE_THETA ** (jnp.arange(0, MLA_ROPE, 2, dtype=F32) / MLA_ROPE))
    invf = jnp.concatenate([inv_freq, inv_freq, jnp.zeros((pad,), F32)]).reshape(1, LANES)
    sign = jnp.concatenate([-jnp.ones((half,), F32), jnp.ones((half,), F32), jnp.zeros((pad,), F32)]).reshape(1, LANES)
    tm = TOKEN_TILE
    full = lambda a: pl.BlockSpec(a.shape, lambda i: (0,) * a.ndim)
    row = lambda n: pl.BlockSpec((tm, n), lambda i: (i, 0))
    mod = pl.BlockSpec((1, 1, d), lambda i: (i // tiles_per_b, 0, 0))
    gq2, gkv2 = gq.reshape(1, q_lora), gkv.reshape(1, kv_lora)
    scale = (MLA_NOPE + MLA_ROPE) ** -0.5
    tk = ATTN_TK
    kt_per_tile = tm // tk
    s = tiles_per_b * tm
    return pl.pallas_call(
        functools.partial(_mla_in_kernel, nh=nh, q_lora=q_lora, kv_lora=kv_lora, scale=scale, tk=tk),
        out_shape=[jax.ShapeDtypeStruct((t, nh * 2 * LANES), BF16), jax.ShapeDtypeStruct((t, nh * 2 * LANES), BF16),
                   jax.ShapeDtypeStruct((t // s, s // tk, nh * MLA_V, tk), BF16)],
        grid=(t // tm,),
        in_specs=[row(d), mod, mod, row(1), full(invf), full(sign), full(win_ext), full(gq2), full(gkv2),
                  full(wqa), full(wqb), full(wk), full(wvt)],
        out_specs=[row(nh * 2 * LANES), row(nh * 2 * LANES),
                   pl.BlockSpec((1, kt_per_tile, nh * MLA_V, tk),
                                lambda i: (i // tiles_per_b, i % tiles_per_b, 0, 0))],
        compiler_params=_cparams(("parallel",)),
        name="mla_in",
    )(x2, sc, sh, pos2, invf, sign, win_ext, gq2, gkv2, wqa, wqb, wk, wvt)


def _attn_kernel(q_ref, k_ref, vt_ref, o_ref, acc_ref, *, s, tq, tk, hb):
    kpq = tq // tk
    hw = 2 * LANES
    ones = jnp.ones((8, tk), BF16)

    def q_block(qi, carry):
        qrows = pl.ds(pl.multiple_of(qi * tq, tq), tq)
        acc_ref[...] = jnp.zeros_like(acc_ref)

        def tile(j, stats, masked):
            krows = pl.ds(pl.multiple_of(j * tk, tk), tk)
            sts = [_dot_nt(k_ref[krows, h * hw:(h + 1) * hw], q_ref[qrows, h * hw:(h + 1) * hw])
                   for h in range(hb)]
            ps, alphas, out = [], [], []
            for h in range(hb):
                m, st = stats[2 * h], sts[h]
                if masked:
                    key = j * tk + lax.broadcasted_iota(I32, (tk, tq), 0)
                    qry = qi * tq + lax.broadcasted_iota(I32, (tk, tq), 1)
                    st = jnp.where(key <= qry, st, -jnp.inf)
                m_new = jnp.maximum(m, jnp.max(st, axis=0, keepdims=True))
                ps.append(jnp.exp(st - m_new).astype(BF16))
                alphas.append(jnp.exp(m - m_new))
                out.append(m_new)
            for h in range(hb):
                acc_ref[h] = alphas[h] * acc_ref[h] + _dot(vt_ref[0, j, h * MLA_V:(h + 1) * MLA_V, :], ps[h])
                out.insert(2 * h + 1, alphas[h] * stats[2 * h + 1] + _dot(ones, ps[h])[0:1])
            return tuple(out)

        stats = (jnp.full((1, tq), -jnp.inf, F32), jnp.zeros((1, tq), F32)) * hb
        stats = lax.fori_loop(0, qi * kpq, lambda j, c: tile(j, c, False), stats)
        stats = lax.fori_loop(qi * kpq, (qi + 1) * kpq, lambda j, c: tile(j, c, True), stats)
        for h in range(hb):
            o_ref[qrows, h * MLA_V:(h + 1) * MLA_V] = (acc_ref[h] / stats[2 * h + 1]).T.astype(o_ref.dtype)
        return carry

    lax.fori_loop(0, s // tq, q_block, 0)


def _attn(q, k, vt, bsz, s):
    nh = MLA_HEADS
    tq, tk = ATTN_TQ, ATTN_TK
    hb = ATTN_HEADS_PER_STEP
    hw = 2 * LANES
    return pl.pallas_call(
        functools.partial(_attn_kernel, s=s, tq=tq, tk=tk, hb=hb),
        out_shape=jax.ShapeDtypeStruct((bsz * s, nh * MLA_V), BF16),
        grid=(bsz, nh // hb),
        in_specs=[pl.BlockSpec((s, hb * hw), lambda b, h: (b, h)),
                  pl.BlockSpec((s, hb * hw), lambda b, h: (b, h)),
                  pl.BlockSpec((1, s // tk, hb * MLA_V, tk), lambda b, h: (b, 0, h, 0))],
        out_specs=pl.BlockSpec((s, hb * MLA_V), lambda b, h: (b, h)),
        scratch_shapes=[pltpu.VMEM((hb, MLA_V, tq), F32)],
        compiler_params=_cparams(("parallel", "parallel")),
        name="attn",
    )(q, k, vt)


def _route_kernel(x_ref, sc_ref, sh_ref, w_ref, b_ref, lp_ref, wt_ref, tm_ref, offs_ref, te_ref, na_ref,
                  upper_ref, carry_ref, *, tm, ne, ng, row_tile, nt_pad):
    i = pl.program_id(0)
    epg = ne // ng

    @pl.when(i == 0)
    def _():
        r = lax.broadcasted_iota(I32, (tm, tm), 0)
        c = lax.broadcasted_iota(I32, (tm, tm), 1)
        upper_ref[...] = jnp.where(r < c, 1.0, 0.0).astype(BF16)
        carry_ref[...] = jnp.zeros_like(carry_ref)

    h = x_ref[...] * (1.0 + sc_ref[0]) + sh_ref[0]
    logits = jnp.dot(h, w_ref[...], precision=HIGHEST, preferred_element_type=F32) + b_ref[...]
    lt = logits.T
    lr = lt[0:ne]
    grow = lax.broadcasted_iota(I32, (8, tm), 0).astype(F32)
    lg = jnp.where(grow < ng, lt[ne:ne + 8], -jnp.inf)
    gmax = jnp.max(lg, axis=0, keepdims=True)
    g_idx = jnp.min(jnp.where(lg == gmax, grow, 1e9), axis=0, keepdims=True)
    g_w = 1.0 / jnp.sum(jnp.exp(lg - gmax), axis=0, keepdims=True)
    erow = lax.broadcasted_iota(I32, (ne, tm), 0).astype(F32)
    in_group = jnp.floor(erow * (1.0 / epg)) == g_idx
    sel = jnp.where(in_group, lr, -jnp.inf)
    v1 = jnp.max(sel, axis=0, keepdims=True)
    i1 = jnp.min(jnp.where(sel == v1, erow, 1e9), axis=0, keepdims=True)
    sel2 = jnp.where(erow == i1, -jnp.inf, sel)
    v2 = jnp.max(sel2, axis=0, keepdims=True)
    i2 = jnp.min(jnp.where(sel2 == v2, erow, 1e9), axis=0, keepdims=True)
    t = jnp.exp(v2 - v1)
    w1 = g_w / (1.0 + t)
    w2 = g_w * t / (1.0 + t)
    oh1 = erow == i1
    oh2 = erow == i2
    member = jnp.where(oh1 | oh2, 1.0, 0.0)
    lcnt = jnp.sum(member, axis=1, keepdims=True)
    lcnt = jnp.floor((lcnt + (SUBLANES - 1)) * (1.0 / SUBLANES)) * SUBLANES
    er_ = lax.broadcasted_iota(I32, (ne, ne), 0)
    ec_ = lax.broadcasted_iota(I32, (ne, ne), 1)
    lstart = jnp.dot(jnp.where(ec_ < er_, 1.0, 0.0).astype(F32), jnp.broadcast_to(lcnt, (ne, LANES)),
                     precision=HIGHEST, preferred_element_type=F32)[:, 0:1]
    lrank = _dot(member.astype(BF16), upper_ref[...]) + lstart
    p1 = jnp.sum(jnp.where(oh1, lrank, 0.0), axis=0, keepdims=True)
    p2 = jnp.sum(jnp.where(oh2, lrank, 0.0), axis=0, keepdims=True)

    orow = lax.broadcasted_iota(I32, (8, tm), 0)
    lp_ref[...] = jnp.where(orow == 0, p1, jnp.where(orow == 1, p2, 0.0)).astype(I32)
    wrow = lax.broadcasted_iota(I32, (LANES, tm), 0)
    wt_ref[...] = jnp.where(wrow == 0, w1, jnp.where(wrow == 1, w2,
                                                     jnp.where(wrow == 2, p1, jnp.where(wrow == 3, p2, 0.0)))).T
    mr = lax.broadcasted_iota(I32, (ne, LANES), 0)
    mc = lax.broadcasted_iota(I32, (ne, LANES), 1)
    to_row = lambda col, lane0: jnp.sum(jnp.where(mr + lane0 == mc, col, 0.0), axis=0, keepdims=True)
    total = jnp.sum(lcnt, axis=0, keepdims=True)
    lane = lax.broadcasted_iota(I32, (1, LANES), 1)
    packed = (to_row(lstart, 0) + to_row(lcnt, ne) + to_row(carry_ref[...], 2 * ne)
              + jnp.where(lane == 3 * ne, total, 0.0))
    trow = lax.broadcasted_iota(I32, (8, LANES), 0)
    tm_ref[...] = jnp.where(trow == 0, packed, 0.0).astype(I32)
    carry_ref[...] = carry_ref[...] + lcnt

    @pl.when(i == pl.num_programs(0) - 1)
    def _():
        cnt = carry_ref[...]
        ntl = jnp.floor((cnt + (row_tile - 1)) * (1.0 / row_tile))
        incl = jnp.where(ec_ <= er_, 1.0, 0.0).astype(F32)
        ends = jnp.dot(incl, jnp.broadcast_to(ntl, (ne, LANES)), precision=HIGHEST,
                       preferred_element_type=F32)
        starts = ends - ntl
        offs_ref[...] = jnp.concatenate([starts * row_tile, ends * row_tile, jnp.broadcast_to(ntl, (ne, LANES)),
                                         jnp.zeros((8, LANES), F32)], axis=0).astype(I32)
        tile = lax.broadcasted_iota(I32, (ne, nt_pad), 1).astype(F32)
        te = jnp.sum(jnp.where(ends[:, 0:1] <= tile, 1.0, 0.0), axis=0, keepdims=True)
        te_ref[...] = jnp.broadcast_to(jnp.minimum(te, ne - 1.0), (8, nt_pad)).astype(I32)
        na_ref[...] = jnp.broadcast_to(ends[ne - 1:ne, :], (8, LANES)).astype(I32)


def _moe_tiles(t):
    rows = 2 * t + (SUBLANES - 1) * MOE_EXPERTS * (t // TOKEN_TILE)
    nt_max = -(-rows // MOE_ROW_TILE) + MOE_EXPERTS
    nt_pad = -(-nt_max // LANES) * LANES
    return nt_max, nt_pad


def _sorted_rows(tm):
    return -(-(2 * tm + (SUBLANES - 1) * MOE_EXPERTS) // LANES) * LANES


def _route(x2, sc, sh, w_group, b_group, w_router, b_router, tiles_per_b):
    t, d = x2.shape
    ne, ng = MOE_EXPERTS, MOE_GROUPS
    tm = TOKEN_TILE
    _, nt_pad = _moe_tiles(t)
    wcat = jnp.concatenate([w_router, w_group, jnp.zeros((d, LANES - ne - ng), F32)], axis=1)
    bcat = jnp.concatenate([b_router, b_group, jnp.zeros((LANES - ne - ng,), F32)]).reshape(1, LANES)
    const = lambda shp: pl.BlockSpec(shp, lambda i: (0,) * len(shp))
    return pl.pallas_call(
        functools.partial(_route_kernel, tm=tm, ne=ne, ng=ng, row_tile=MOE_ROW_TILE, nt_pad=nt_pad),
        out_shape=[jax.ShapeDtypeStruct((8, t), I32), jax.ShapeDtypeStruct((t, LANES), F32),
                   jax.ShapeDtypeStruct((8 * (t // tm), LANES), I32),
                   jax.ShapeDtypeStruct((3 * ne + 8, LANES), I32), jax.ShapeDtypeStruct((8, nt_pad), I32),
                   jax.ShapeDtypeStruct((8, LANES), I32)],
        grid=(t // tm,),
        in_specs=[pl.BlockSpec((tm, d), lambda i: (i, 0)),
                  pl.BlockSpec((1, 1, d), lambda i: (i // tiles_per_b, 0, 0)),
                  pl.BlockSpec((1, 1, d), lambda i: (i // tiles_per_b, 0, 0)),
                  const((d, LANES)), const((1, LANES))],
        out_specs=[pl.BlockSpec((8, tm), lambda i: (0, i)), pl.BlockSpec((tm, LANES), lambda i: (i, 0)),
                   pl.BlockSpec((8, LANES), lambda i: (i, 0)),
                   const((3 * ne + 8, LANES)), const((8, nt_pad)), const((8, LANES))],
        scratch_shapes=[pltpu.VMEM((tm, tm), BF16), pltpu.VMEM((ne, 1), F32)],
        compiler_params=_cparams(("arbitrary",)),
        name="moe_route",
    )(x2, sc, sh, wcat, bcat)


def _pack_bf16_pairs(v):
    half = v.shape[1] // 2
    bits = lambda a: lax.bitcast_convert_type(a.astype(BF16).astype(F32), U32)
    return (bits(v[:, :half]) >> 16) | (bits(v[:, half:]) & jnp.uint32(0xFFFF0000))


def _unpack_bf16_pairs(u):
    lo = lax.bitcast_convert_type(u << 16, F32).astype(BF16)
    hi = lax.bitcast_convert_type(u & jnp.uint32(0xFFFF0000), F32).astype(BF16)
    return lo, hi


def _start_runs(tm_ref, tile, ne, copy):
    for e in range(ne):
        lstart = tm_ref[tile, e]
        n = tm_ref[tile, ne + e]
        before = tm_ref[tile, 2 * ne + e]
        _binary_pieces(n, TOKEN_TILE, lambda off, rows: copy(
            e, pl.multiple_of(lstart + off, SUBLANES), pl.multiple_of(before + off, SUBLANES), rows))


def _binary_pieces(n, n_max, copy):
    del n_max
    nbig = n >> (RUN_PIECE.bit_length() - 1)

    def big_piece(c, carry):
        copy(c * RUN_PIECE, RUN_PIECE).start()
        return carry

    lax.fori_loop(0, nbig, big_piece, 0)
    off = nbig * RUN_PIECE
    p = RUN_PIECE // 2
    while p >= SUBLANES:
        @pl.when((n & p) != 0)
        def _():
            copy(off, p).start()
        off = off + (n & p)
        p //= 2


def _dispatch_kernel(offs_ref, gend_ref, ntl_ref, tm_ref, x_ref, sc_ref, sh_ref, lp_ref, xs_ref,
                     h_ref, z_ref, sem_z, sem_r, *, tm, ne, row_tile, nt_max, ntile, srows):
    i = pl.program_id(0)
    tile = i - 1

    @pl.when(i == 0)
    def _():
        z_ref[...] = jnp.zeros_like(z_ref)
        for e in range(ne):
            @pl.when(ntl_ref[e] > 0)
            def _():
                start = pl.multiple_of(gend_ref[e] - row_tile, row_tile)
                cp = pltpu.make_async_copy(z_ref, xs_ref.at[pl.ds(start, row_tile), :], sem_z)
                cp.start()
                cp.wait()
        for back in range(1, nt_max - (2 * tm * ntile) // row_tile + 1):
            @pl.when(nt_max - back >= gend_ref[ne - 1] // row_tile)
            def _():
                cp = pltpu.make_async_copy(z_ref, xs_ref.at[pl.ds((nt_max - back) * row_tile, row_tile), :], sem_z)
                cp.start()
                cp.wait()
        for spill in range(2):
            cp = pltpu.make_async_copy(z_ref, xs_ref.at[pl.ds((nt_max + spill) * row_tile, row_tile), :], sem_z)
            cp.start()
            cp.wait()

    def wait_tile(t):
        slot = t % 2
        pltpu.make_async_copy(h_ref.at[slot], xs_ref.at[pl.ds(0, srows), :], sem_r.at[slot]).wait()

    @pl.when((tile >= 2) & (tile <= ntile))
    def _():
        wait_tile(tile - 2)

    @pl.when(tile == ntile)
    def _():
        wait_tile(tile - 1)

    @pl.when((tile >= 0) & (tile < ntile))
    def _():
        slot = tile % 2
        h = (x_ref[...] * (1.0 + sc_ref[0]) + sh_ref[0]).astype(BF16)
        row = lax.broadcasted_iota(I32, (srows, tm), 0)
        lp = lp_ref[...]
        onehot = jnp.where(row == lp[0:1, :], 1.0, jnp.where(row == lp[1:2, :], 1.0, 0.0)).astype(BF16)
        h_ref[slot] = _pack_bf16_pairs(_dot(onehot, h))

        def copy(e, local_row, rows_before, rows):
            dst = pl.multiple_of(offs_ref[e] + rows_before, SUBLANES)
            return pltpu.make_async_copy(h_ref.at[slot, pl.ds(local_row, rows), :], xs_ref.at[pl.ds(dst, rows), :],
                                         sem_r.at[slot])

        _start_runs(tm_ref, tile, ne, copy)
        used = tm_ref[tile, 3 * ne]
        _binary_pieces(srows - used, srows - 2 * tm, lambda off, rows: pltpu.make_async_copy(
            h_ref.at[slot, pl.ds(pl.multiple_of(used + off, SUBLANES), rows), :],
            xs_ref.at[pl.ds(pl.multiple_of((nt_max + slot) * row_tile + off, SUBLANES), rows), :], sem_r.at[slot]))


def _dispatch(x2, sc, sh, lp, tmeta, offs, gend, ntl, tiles_per_b):
    t, d = x2.shape
    tm = TOKEN_TILE
    nt_max, _ = _moe_tiles(t)
    ntile = t // tm
    srows = _sorted_rows(tm)
    cur = lambda i: jnp.clip(i - 1, 0, ntile - 1)
    grid_spec = pltpu.PrefetchScalarGridSpec(
        num_scalar_prefetch=4,
        grid=(ntile + 2,),
        in_specs=[pl.BlockSpec((tm, d), lambda i, *_: (cur(i), 0)),
                  pl.BlockSpec((1, 1, d), lambda i, *_: (cur(i) // tiles_per_b, 0, 0)),
                  pl.BlockSpec((1, 1, d), lambda i, *_: (cur(i) // tiles_per_b, 0, 0)),
                  pl.BlockSpec((8, tm), lambda i, *_: (0, cur(i)))],
        out_specs=pl.BlockSpec(memory_space=pl.ANY),
        scratch_shapes=[pltpu.VMEM((2, srows, d // 2), U32), pltpu.VMEM((MOE_ROW_TILE, d // 2), U32),
                        pltpu.SemaphoreType.DMA, pltpu.SemaphoreType.DMA((2,))])
    return pl.pallas_call(
        functools.partial(_dispatch_kernel, tm=tm, ne=MOE_EXPERTS, row_tile=MOE_ROW_TILE, nt_max=nt_max,
                          ntile=ntile, srows=srows),
        out_shape=jax.ShapeDtypeStruct(((nt_max + 2) * MOE_ROW_TILE, d // 2), U32),
        grid_spec=grid_spec,
        compiler_params=_cparams(("arbitrary",)),
        name="moe_dispatch",
    )(offs, gend, ntl, tmeta, x2, sc, sh, lp)


def _gmm_kernel(te_ref, na_ref, xs_ref, wg_ref, wu_ref, wd_ref, ys_ref, wgu_buf, wd_buf, *, ff, tr):
    j = pl.program_id(0)

    @pl.when(j < na_ref[0])
    def _():
        changed = (j == 0) | (te_ref[j] != te_ref[jnp.maximum(j - 1, 0)])

        @pl.when(changed)
        def _():
            wgu_buf[:, :ff] = wg_ref[0, 0].astype(BF16)
            wgu_buf[:, ff:] = wu_ref[0, 0].astype(BF16)
            wd_buf[...] = wd_ref[0, 0].astype(BF16)

        halves = [slice(c * (tr // 2), (c + 1) * (tr // 2)) for c in range(2)]
        half_d = wgu_buf.shape[0] // 2
        gus = []
        for rows in halves:
            lo, hi = _unpack_bf16_pairs(xs_ref[rows, :])
            gus.append(_dot(lo, wgu_buf[:half_d, :]) + _dot(hi, wgu_buf[half_d:, :]))
        hids = [(gu[:, :ff] * _sigmoid(gu[:, :ff]) * gu[:, ff:]).astype(BF16) for gu in gus]
        for rows, hid in zip(halves, hids):
            ys_ref[rows, :] = _pack_bf16_pairs(_dot(hid, wd_buf[...]))

    @pl.when(j >= na_ref[0])
    def _():
        ys_ref[...] = jnp.zeros_like(ys_ref)


def _gmm(xs, te, na, w_gate, w_up, w_down, layer):
    d, ff = w_gate.shape[-2:]
    tr = MOE_ROW_TILE
    ns = te.shape[0] * tr
    act = lambda j, te_ref, na_ref: jnp.minimum(j, na_ref[0] - 1)
    grid_spec = pltpu.PrefetchScalarGridSpec(
        num_scalar_prefetch=2,
        grid=(ns // tr,),
        in_specs=[pl.BlockSpec((tr, d // 2), lambda j, te_ref, na_ref: (act(j, te_ref, na_ref), 0)),
                  pl.BlockSpec((1, 1, d, ff), lambda j, te_ref, na_ref: (layer, te_ref[act(j, te_ref, na_ref)], 0, 0)),
                  pl.BlockSpec((1, 1, d, ff), lambda j, te_ref, na_ref: (layer, te_ref[act(j, te_ref, na_ref)], 0, 0)),
                  pl.BlockSpec((1, 1, ff, d), lambda j, te_ref, na_ref: (layer, te_ref[act(j, te_ref, na_ref)], 0, 0))],
        out_specs=pl.BlockSpec((tr, d // 2), lambda j, te_ref, na_ref: (j, 0)),
        scratch_shapes=[pltpu.VMEM((d, 2 * ff), BF16), pltpu.VMEM((ff, d), BF16)])
    return pl.pallas_call(
        functools.partial(_gmm_kernel, ff=ff, tr=tr),
        out_shape=jax.ShapeDtypeStruct((ns, d // 2), U32),
        grid_spec=grid_spec,
        compiler_params=_cparams(("arbitrary",)),
        name="moe_gmm",
    )(te, na, xs, w_gate, w_up, w_down)


def _combine_kernel(offs_ref, tm_ref, wt_ref, x_ref, gate_ref, lg_ref, lb_ref, ys_ref, o_ref,
                    buf_ref, sem_r, *, tm, ne, ntile, srows, alpha):
    i = pl.program_id(0)

    def fetch(tile):
        slot = tile % 2

        def copy(e, local_row, rows_before, rows):
            src = pl.multiple_of(offs_ref[e] + rows_before, SUBLANES)
            return pltpu.make_async_copy(ys_ref.at[pl.ds(src, rows), :], buf_ref.at[slot, pl.ds(local_row, rows), :],
                                         sem_r.at[slot])

        _start_runs(tm_ref, tile, ne, copy)
        used = tm_ref[tile, 3 * ne]
        _binary_pieces(srows - used, srows - 2 * tm, lambda off, rows: pltpu.make_async_copy(
            ys_ref.at[pl.ds(pl.multiple_of(off, SUBLANES), rows), :],
            buf_ref.at[slot, pl.ds(pl.multiple_of(used + off, SUBLANES), rows), :], sem_r.at[slot]))

    def wait(tile):
        slot = tile % 2
        pltpu.make_async_copy(ys_ref.at[pl.ds(0, srows), :], buf_ref.at[slot], sem_r.at[slot]).wait()

    def finish(tile):
        w = wt_ref[...]
        lo, hi = _unpack_bf16_pairs(buf_ref[tile % 2])
        col = lax.broadcasted_iota(I32, (tm, srows), 1).astype(F32)

        def pick(k):
            onehot = jnp.where(col == w[:, 2 + k:3 + k], 1.0, 0.0).astype(BF16)
            return jnp.concatenate([_dot(onehot, lo), _dot(onehot, hi)], axis=1)

        ffn = w[:, 0:1] * pick(0) + w[:, 1:2] * pick(1)
        y = alpha * x_ref[...] + (1.0 + gate_ref[0]) * ffn
        o_ref[...] = _layer_norm(y, lg_ref[...], lb_ref[...])

    @pl.when(i == 0)
    def _():
        fetch(i)

    @pl.when((i > 0) & (i < ntile))
    def _():
        wait(i - 1)
        fetch(i)
        finish(i - 1)

    @pl.when(i == ntile)
    def _():
        wait(i - 1)
        finish(i - 1)


def _combine(ys, tmeta, wt, x2, gate, ln_g, ln_b, offs, tiles_per_b, alpha):
    t, d = x2.shape
    tm = TOKEN_TILE
    ntile = t // tm
    srows = _sorted_rows(tm)
    prev = lambda i: jnp.maximum(i - 1, 0)
    vec = pl.BlockSpec((1, d), lambda i, *_: (0, 0))
    grid_spec = pltpu.PrefetchScalarGridSpec(
        num_scalar_prefetch=2,
        grid=(ntile + 1,),
        in_specs=[pl.BlockSpec((tm, LANES), lambda i, *_: (prev(i), 0)),
                  pl.BlockSpec((tm, d), lambda i, *_: (prev(i), 0)),
                  pl.BlockSpec((1, 1, d), lambda i, *_: (prev(i) // tiles_per_b, 0, 0)),
                  vec, vec,
                  pl.BlockSpec(memory_space=pl.ANY)],
        out_specs=pl.BlockSpec((tm, d), lambda i, *_: (prev(i), 0)),
        scratch_shapes=[pltpu.VMEM((2, srows, d // 2), U32), pltpu.SemaphoreType.DMA((2,))])
    return pl.pallas_call(
        functools.partial(_combine_kernel, tm=tm, ne=MOE_EXPERTS, ntile=ntile, srows=srows, alpha=alpha),
        out_shape=jax.ShapeDtypeStruct((t, d), F32),
        grid_spec=grid_spec,
        compiler_params=_cparams(("arbitrary",)),
        name="moe_combine",
    )(offs, tmeta, wt, x2, gate, ln_g.reshape(1, d), ln_b.reshape(1, d), ys)


def _moe_block(x2, sc, sh, gate, ln_g, ln_b, w_group, b_group, w_router, b_router, w_gate, w_up, w_down,
               layer, tiles_per_b, alpha):
    ne = MOE_EXPERTS
    nt_max, _ = _moe_tiles(x2.shape[0])
    lp, wt, tmeta, meta, te, na = _route(x2, sc, sh, w_group, b_group, w_router, b_router, tiles_per_b)
    offs, gend, ntl = meta[:ne, 0], meta[ne:2 * ne, 0], meta[2 * ne:3 * ne, 0]
    tmeta = tmeta.reshape(-1, SUBLANES, LANES)[:, 0, :]
    xs = _dispatch(x2, sc, sh, lp, tmeta, offs, gend, ntl, tiles_per_b)
    ys = _gmm(xs, te[0, :nt_max], na[0, :1], w_gate, w_up, w_down, layer)
    return _combine(ys, tmeta, wt, x2, gate, ln_g, ln_b, offs, tiles_per_b, alpha)


def kernel(x, c, positions, ada_w, ada_b, ln_mix_g, ln_mix_b, ln_ffn_g, ln_ffn_b, ab_w_in, conv_w, conv_b, conv_ln_g, conv_ln_b, gla_gate_w, gla_gate_b, gla_norm_g, ab_w_out, mla_w_in, mla_q_norm_g, mla_kv_norm_g, mla_w_uq, mla_w_ukv, mla_w_out, moe_w_group, moe_b_group, moe_w_router, moe_b_router, moe_w_gate, moe_w_up, moe_w_down):
    bsz, s, d = x.shape
    depth = ada_w.shape[0]
    t = bsz * s
    tiles_per_b = s // TOKEN_TILE
    alpha = (2 * depth) ** 0.25
    mod = _ada(c, ada_w, ada_b).reshape(depth, bsz, 6, 1, d)
    x2 = x.reshape(t, d)
    for layer in range(depth):
        sh_m, sc_m, g_m, sh_f, sc_f, g_f = (mod[layer, :, n] for n in range(6))
        i = layer // 2
        if layer % 2 == 0:
            uc, q, k, v, r, gl = _ab_in(x2, sc_m, sh_m, ab_w_in[i], gla_gate_w[i], gla_gate_b[i], tiles_per_b)
            y_a = _conv(uc.reshape(bsz, s, -1), conv_w[i], conv_b[i], conv_ln_g[i], conv_ln_b[i])
            b3 = lambda a: a.reshape(bsz, s, -1)
            y_b = _gla(b3(q), b3(k), b3(v), b3(gl), b3(r), gla_norm_g[i])
            w_out = ab_w_out[i].astype(BF16)
            cc = y_a.shape[-1]
            acts = [y_a.reshape(t, cc), y_b.reshape(t, -1)]
            weights = [w_out[:cc], w_out[cc:]]
        else:
            qc, kc, vv = _mla_in(x2, sc_m, sh_m, positions.reshape(t, 1), mla_w_in[i], mla_q_norm_g[i],
                                 mla_kv_norm_g[i], mla_w_uq[i], mla_w_ukv[i], tiles_per_b)
            acts = [_attn(qc, kc, vv, bsz, s)]
            weights = [mla_w_out[i].astype(BF16)]
        x2 = _proj_ln(acts, weights, x2, g_m, ln_mix_g[layer], ln_mix_b[layer], tiles_per_b, alpha)
        x2 = _moe_block(x2, sc_f, sh_f, g_f, ln_ffn_g[layer], ln_ffn_b[layer], moe_w_group[layer], moe_b_group[layer],
                        moe_w_router[layer], moe_b_router[layer], moe_w_gate, moe_w_up, moe_w_down,
                        layer, tiles_per_b, alpha)
    return x2.reshape(bsz, s, d)
```

```python
import functools

import jax
import jax.numpy as jnp
from jax import lax
from jax.experimental import pallas as pl
from jax.experimental.pallas import tpu as pltpu

F32 = jnp.float32
BF16 = jnp.bfloat16
I32 = jnp.int32
HIGHEST = lax.Precision.HIGHEST

LN_EPS = 1e-5
RMS_EPS = 1e-6
CONV_WIDTH = 31
GLA_HEADS = 4
GLA_GATE_TAU = 16.0
MLA_HEADS = 8
MLA_NOPE = 128
MLA_ROPE = 64
MLA_V = 128
ROPE_THETA = 10000.0
MOE_GROUPS = 4
MOE_EXPERTS_PER_GROUP = 8
MOE_EXPERTS = MOE_GROUPS * MOE_EXPERTS_PER_GROUP

LANES = 128
SUBLANES = 8
TOKEN_TILE = 512
GLA_CHUNK = 128
GLA_BLOCK = 512
GLA_BATCH_PER_STEP = 2
CONV_ROWS = 32
CONV_HALO = 32
ATTN_TQ = 512
ATTN_TK = 256
ATTN_HEADS_PER_STEP = 4
MOE_ROW_TILE = 512
RUN_PIECE = 64
VMEM_LIMIT = 48 * 1024 * 1024


def _cparams(sem):
    return pltpu.CompilerParams(dimension_semantics=sem, vmem_limit_bytes=VMEM_LIMIT)


def _sigmoid(x):
    return 1.0 / (1.0 + jnp.exp(-x))


def _dot(a, b):
    return jnp.dot(a, b, preferred_element_type=F32)


def _dot_nt(a, b):
    return lax.dot_general(a, b, (((1,), (1,)), ((), ())), preferred_element_type=F32)


def _dot_tn(a, b):
    return lax.dot_general(a, b, (((0,), (0,)), ((), ())), preferred_element_type=F32)


def _layer_norm(y, g, b):
    mu = jnp.mean(y, axis=-1, keepdims=True)
    d = y - mu
    var = jnp.mean(d * d, axis=-1, keepdims=True)
    return d * lax.rsqrt(var + LN_EPS) * g + b


def _ada_kernel(c_ref, w_ref, b_ref, o_ref):
    c = c_ref[...]
    o_ref[0] = jnp.dot(c * _sigmoid(c), w_ref[0], precision=HIGHEST, preferred_element_type=F32) + b_ref[0]


def _ada(c, ada_w, ada_b):
    depth, d, n = ada_w.shape
    bsz = c.shape[0]
    tn = 1536
    return pl.pallas_call(
        _ada_kernel,
        out_shape=jax.ShapeDtypeStruct((depth, bsz, n), F32),
        grid=(depth, n // tn),
        in_specs=[pl.BlockSpec((bsz, d), lambda l, j: (0, 0)),
                  pl.BlockSpec((1, d, tn), lambda l, j: (l, 0, j)),
                  pl.BlockSpec((1, 1, tn), lambda l, j: (l, 0, j))],
        out_specs=pl.BlockSpec((1, bsz, tn), lambda l, j: (l, 0, j)),
        compiler_params=_cparams(("parallel", "parallel")),
        name="ada",
    )(c, ada_w, ada_b.reshape(depth, 1, n))


def _ab_in_kernel(x_ref, sc_ref, sh_ref, wc_ref, wq_ref, wk_ref, wv_ref, wr_ref, wg_ref, gw_ref, gb_ref,
                  uc_ref, q_ref, k_ref, v_ref, r_ref, gl_ref):
    h = (x_ref[...] * (1.0 + sc_ref[0]) + sh_ref[0]).astype(BF16)
    uc_ref[...] = _dot(h, wc_ref[...])
    q_ref[...] = _dot(h, wq_ref[...])
    k_ref[...] = _dot(h, wk_ref[...])
    v_ref[...] = _dot(h, wv_ref[...]).astype(v_ref.dtype)
    r_ref[...] = _dot(h, wr_ref[...])
    g_low = _dot(h, wg_ref[...])
    z = jnp.dot(g_low, gw_ref[...], precision=HIGHEST, preferred_element_type=F32) + gb_ref[...]
    gl_ref[...] = (jnp.minimum(z, 0.0) - jnp.log(1.0 + jnp.exp(-jnp.abs(z)))) * (1.0 / GLA_GATE_TAU)


def _ab_in(x2, sc, sh, w_in, gate_w, gate_b, tiles_per_b):
    t, d = x2.shape
    cc2 = d
    kw = d // 4
    vw = d // 2
    rank = gate_w.shape[0]
    splits = [cc2, cc2 + kw, cc2 + 2 * kw, cc2 + 2 * kw + vw, cc2 + 2 * kw + 2 * vw]
    wb = w_in.astype(BF16)
    ws = [wb[:, :splits[0]], wb[:, splits[0]:splits[1]], wb[:, splits[1]:splits[2]],
          wb[:, splits[2]:splits[3]], wb[:, splits[3]:splits[4]], wb[:, splits[4]:]]
    tm = TOKEN_TILE
    full = lambda a: pl.BlockSpec(a.shape, lambda i: (0,) * a.ndim)
    row = lambda n: pl.BlockSpec((tm, n), lambda i: (i, 0))
    mod = pl.BlockSpec((1, 1, d), lambda i: (i // tiles_per_b, 0, 0))
    gb2 = gate_b.reshape(1, kw)
    widths = [cc2, kw, kw, vw, vw, kw]
    return pl.pallas_call(
        _ab_in_kernel,
        out_shape=[jax.ShapeDtypeStruct((t, n), BF16 if idx == 3 else F32) for idx, n in enumerate(widths)],
        grid=(t // tm,),
        in_specs=[row(d), mod, mod] + [full(w) for w in ws] + [full(gate_w), full(gb2)],
        out_specs=[row(n) for n in widths],
        compiler_params=_cparams(("parallel",)),
        name="ab_in",
    )(x2, sc, sh, *ws, gate_w, gb2)


def _conv_kernel(u_ref, halo_ref, cw_ref, cb_ref, lg_ref, lb_ref, o_ref, hp_ref, *, ts, cc):
    j = pl.program_id(1)

    def glu(u):
        return u[:, :cc] * _sigmoid(u[:, cc:])

    hp_ref[0, 0:CONV_HALO, :] = jnp.where(j > 0, glu(halo_ref[0]), 0.0)
    hp_ref[0, CONV_HALO:CONV_HALO + ts, :] = glu(u_ref[0])
    nrow = CONV_HALO + ts
    for b in range(1, SUBLANES):
        hp_ref[b, 0:nrow - SUBLANES, :] = hp_ref[0, b:nrow - SUBLANES + b, :]
    shift = CONV_HALO - (CONV_WIDTH - 1)
    for rb in range(ts // CONV_ROWS):
        r0 = rb * CONV_ROWS
        acc = jnp.zeros((CONV_ROWS, cc), F32)
        for tap in range(CONV_WIDTH):
            lo = r0 + shift + tap
            base = lo // SUBLANES * SUBLANES
            acc = acc + cw_ref[tap:tap + 1, :] * hp_ref[lo - base, base:base + CONV_ROWS, :]
        y = _layer_norm(acc + cb_ref[...], lg_ref[...], lb_ref[...])
        o_ref[0, r0:r0 + CONV_ROWS, :] = (y * _sigmoid(y)).astype(o_ref.dtype)


def _conv(u3, conv_w, conv_b, ln_g, ln_b):
    bsz, s, cc2 = u3.shape
    cc = cc2 // 2
    ts = TOKEN_TILE
    hb = ts // CONV_HALO
    vec = lambda a: pl.BlockSpec((1, cc), lambda b, j: (0, 0))
    return pl.pallas_call(
        functools.partial(_conv_kernel, ts=ts, cc=cc),
        out_shape=jax.ShapeDtypeStruct((bsz, s, cc), BF16),
        grid=(bsz, s // ts),
        in_specs=[pl.BlockSpec((1, ts, cc2), lambda b, j: (b, j, 0)),
                  pl.BlockSpec((1, CONV_HALO, cc2), lambda b, j: (b, jnp.maximum(j * hb - 1, 0), 0)),
                  pl.BlockSpec((CONV_WIDTH, cc), lambda b, j: (0, 0)),
                  vec(conv_b), vec(ln_g), vec(ln_b)],
        out_specs=pl.BlockSpec((1, ts, cc), lambda b, j: (b, j, 0)),
        scratch_shapes=[pltpu.VMEM((SUBLANES, CONV_HALO + ts, cc), F32)],
        compiler_params=_cparams(("parallel", "parallel")),
        name="conv",
    )(u3, u3, conv_w, conv_b.reshape(1, cc), ln_g.reshape(1, cc), ln_b.reshape(1, cc))


def _gla_kernel(q_ref, k_ref, v_ref, gl_ref, r_ref, ng_ref, o_ref, st_ref, *, nb, nh, dk, dv, gc, nchunks):
    @pl.when(pl.program_id(1) == 0)
    def _():
        st_ref[...] = jnp.zeros_like(st_ref)

    row = lax.broadcasted_iota(I32, (gc, gc), 0)
    col = lax.broadcasted_iota(I32, (gc, gc), 1)
    causal = col <= row
    tri = jnp.where(causal, 1.0, 0.0).astype(F32)
    scale = dk ** -0.5

    ks = [slice(h * dk, (h + 1) * dk) for h in range(nh)]
    vs = [slice(h * dv, (h + 1) * dv) for h in range(nh)]
    streams = [(bb, h) for bb in range(nb) for h in range(nh)]

    def chunk(c, carry):
        r0 = pl.multiple_of(c * gc, gc)
        rows = pl.ds(r0, gc)
        bs = [jnp.dot(tri, gl_ref[bb, rows, :], precision=HIGHEST, preferred_element_type=F32)
              for bb in range(nb)]
        q_in, k_in, q_st, k_st, decay, v = [], [], [], [], [], []
        for bb in range(nb):
            b = bs[bb]
            b_last = b[gc - 1:gc, :]
            mid = 0.5 * b_last
            q = q_ref[bb, rows, :] * scale
            k = k_ref[bb, rows, :]
            v.append(v_ref[bb, rows, :].astype(BF16))
            q_in.append((q * jnp.exp(b - mid)).astype(BF16))
            k_in.append((k * jnp.exp(mid - b)).astype(BF16))
            q_st.append((q * jnp.exp(b)).astype(BF16))
            k_st.append((k * jnp.exp(b_last - b)).astype(BF16))
            decay.append(jnp.exp(b_last))
        sts = [st_ref[bb, h] for bb, h in streams]
        scores = [_dot_nt(q_in[bb][:, ks[h]], k_in[bb][:, ks[h]]) for bb, h in streams]
        inter = [_dot_nt(q_st[bb][:, ks[h]], st.astype(BF16)) for (bb, h), st in zip(streams, sts)]
        update = [_dot_tn(v[bb][:, vs[h]], k_st[bb][:, ks[h]]) for bb, h in streams]
        atts = [jnp.where(causal, sc, 0.0).astype(BF16) for sc in scores]
        outs = [_dot(att, v[bb][:, vs[h]]) + it for (bb, h), att, it in zip(streams, atts, inter)]
        for (bb, h), st, up, o in zip(streams, sts, update, outs):
            st_ref[bb, h] = st * decay[bb][:, ks[h]] + up
            r = r_ref[bb, rows, vs[h]]
            o = o * lax.rsqrt(jnp.mean(o * o, axis=-1, keepdims=True) + RMS_EPS) * ng_ref[:, vs[h]]
            o_ref[bb, rows, vs[h]] = (o * (r * _sigmoid(r))).astype(o_ref.dtype)
        return carry

    lax.fori_loop(0, nchunks, chunk, 0)


def _gla(q3, k3, v3, gl3, r3, norm_g):
    bsz, s, kw = q3.shape
    vw = v3.shape[-1]
    nh = GLA_HEADS
    dk, dv = kw // nh, vw // nh
    cb = GLA_BLOCK
    gc = GLA_CHUNK
    nb = GLA_BATCH_PER_STEP
    blk = lambda n: pl.BlockSpec((nb, cb, n), lambda b, j: (b, j, 0))
    return pl.pallas_call(
        functools.partial(_gla_kernel, nb=nb, nh=nh, dk=dk, dv=dv, gc=gc, nchunks=cb // gc),
        out_shape=jax.ShapeDtypeStruct((bsz, s, vw), BF16),
        grid=(bsz // nb, s // cb),
        in_specs=[blk(kw), blk(kw), blk(vw), blk(kw), blk(vw), pl.BlockSpec((1, vw), lambda b, j: (0, 0))],
        out_specs=blk(vw),
        scratch_shapes=[pltpu.VMEM((nb, nh, dv, dk), F32)],
        compiler_params=_cparams(("parallel", "arbitrary")),
        name="gla",
    )(q3, k3, v3, gl3, r3, norm_g.reshape(1, vw))


def _proj_ln_kernel(*refs, n_in, alpha):
    a_refs, w_refs = refs[:n_in], refs[n_in:2 * n_in]
    x_ref, gate_ref, lg_ref, lb_ref, o_ref = refs[2 * n_in:]
    mix = _dot(a_refs[0][...], w_refs[0][...])
    for a_ref, w_ref in zip(a_refs[1:], w_refs[1:]):
        mix = mix + _dot(a_ref[...], w_ref[...])
    y = alpha * x_ref[...] + (1.0 + gate_ref[0]) * mix
    o_ref[...] = _layer_norm(y, lg_ref[...], lb_ref[...])


def _proj_ln(acts, weights, x2, gate, ln_g, ln_b, tiles_per_b, alpha):
    t, d = x2.shape
    tm = TOKEN_TILE
    n_in = len(acts)
    row = lambda n: pl.BlockSpec((tm, n), lambda i: (i, 0))
    full = lambda a: pl.BlockSpec(a.shape, lambda i: (0,) * a.ndim)
    vec = pl.BlockSpec((1, d), lambda i: (0, 0))
    return pl.pallas_call(
        functools.partial(_proj_ln_kernel, n_in=n_in, alpha=alpha),
        out_shape=jax.ShapeDtypeStruct((t, d), F32),
        grid=(t // tm,),
        in_specs=[row(a.shape[1]) for a in acts] + [full(w) for w in weights]
                 + [row(d), pl.BlockSpec((1, 1, d), lambda i: (i // tiles_per_b, 0, 0)), vec, vec],
        out_specs=row(d),
        compiler_params=_cparams(("parallel",)),
        name="proj_ln",
    )(*acts, *weights, x2, gate, ln_g.reshape(1, d), ln_b.reshape(1, d))


def _mla_in_kernel(x_ref, sc_ref, sh_ref, pos_ref, invf_ref, sign_ref, win_ref, gq_ref, gkv_ref,
                   wqa_ref, wqb_ref, wk_ref, wvt_ref, q_ref, k_ref, vt_ref, *, nh, q_lora, kv_lora, scale, tk):
    h = (x_ref[...] * (1.0 + sc_ref[0]) + sh_ref[0]).astype(BF16)
    u = _dot(h, win_ref[...])
    cq = u[:, :q_lora]
    ckv = u[:, q_lora:q_lora + kv_lora]
    kr = u[:, q_lora + kv_lora:q_lora + kv_lora + LANES]
    kr_sw = u[:, q_lora + kv_lora + LANES:]
    cqn = (cq * lax.rsqrt(jnp.mean(cq * cq, axis=-1, keepdims=True) + RMS_EPS) * gq_ref[...]).astype(BF16)
    kvn = (ckv * lax.rsqrt(jnp.mean(ckv * ckv, axis=-1, keepdims=True) + RMS_EPS) * gkv_ref[...]).astype(BF16)
    ang = pos_ref[...].astype(F32) * invf_ref[...]
    cos = jnp.cos(ang)
    sin = jnp.sin(ang) * sign_ref[...]
    kr_rot = (kr * cos + kr_sw * sin).astype(BF16)
    qa = _dot(cqn, wqa_ref[...])
    qb = _dot(cqn, wqb_ref[...])
    kv = _dot(kvn, wk_ref[...])
    hw = 2 * LANES
    for hd in range(nh):
        q_ref[:, hd * hw:hd * hw + LANES] = (qa[:, hd * hw:hd * hw + LANES] * scale).astype(BF16)
        rope = qa[:, hd * hw + LANES:(hd + 1) * hw] * cos + qb[:, hd * LANES:(hd + 1) * LANES] * sin
        q_ref[:, hd * hw + LANES:(hd + 1) * hw] = (rope * scale).astype(BF16)
        k_ref[:, hd * hw:hd * hw + LANES] = kv[:, hd * LANES:(hd + 1) * LANES].astype(BF16)
        k_ref[:, hd * hw + LANES:(hd + 1) * hw] = kr_rot
    vt = _dot_nt(wvt_ref[...], kvn).astype(BF16)
    for c in range(vt.shape[1] // tk):
        vt_ref[0, c] = vt[:, c * tk:(c + 1) * tk]


def _mla_in(x2, sc, sh, pos2, w_in, gq, gkv, w_uq, w_ukv, tiles_per_b):
    t, d = x2.shape
    nh = MLA_HEADS
    q_lora, kv_lora = gq.shape[0], gkv.shape[0]
    half = MLA_ROPE // 2
    pad = LANES - MLA_ROPE
    kr_w = w_in[:, q_lora + kv_lora:]
    kr_sw = jnp.concatenate([kr_w[:, half:], kr_w[:, :half]], axis=1)
    zpad = jnp.zeros((d, pad), w_in.dtype)
    win_ext = jnp.concatenate([w_in[:, :q_lora + kv_lora], kr_w, zpad, kr_sw, zpad], axis=1).astype(BF16)
    wq = w_uq.reshape(q_lora, nh, MLA_NOPE + MLA_ROPE)
    q_nope, q_rope = wq[:, :, :MLA_NOPE], wq[:, :, MLA_NOPE:]
    q_rope_sw = jnp.concatenate([q_rope[:, :, half:], q_rope[:, :, :half]], axis=2)
    zq = jnp.zeros((q_lora, nh, pad), w_uq.dtype)
    wqa = jnp.concatenate([q_nope, q_rope, zq], axis=2).reshape(q_lora, nh * 2 * LANES).astype(BF16)
    wqb = jnp.concatenate([q_rope_sw, zq], axis=2).reshape(q_lora, nh * LANES).astype(BF16)
    wkv = w_ukv.reshape(kv_lora, nh, MLA_NOPE + MLA_V)
    wk = wkv[:, :, :MLA_NOPE].reshape(kv_lora, nh * MLA_NOPE).astype(BF16)
    wvt = wkv[:, :, MLA_NOPE:].reshape(kv_lora, nh * MLA_V).T.astype(BF16)
    inv_freq = 1.0 / (ROPE_THETA ** (jnp.arange(0, MLA_ROPE, 2, dtype=F32) / MLA_ROPE))
    invf = jnp.concatenate([inv_freq, inv_freq, jnp.zeros((pad,), F32)]).reshape(1, LANES)
    sign = jnp.concatenate([-jnp.ones((half,), F32), jnp.ones((half,), F32), jnp.zeros((pad,), F32)]).reshape(1, LANES)
    tm = TOKEN_TILE
    full = lambda a: pl.BlockSpec(a.shape, lambda i: (0,) * a.ndim)
    row = lambda n: pl.BlockSpec((tm, n), lambda i: (i, 0))
    mod = pl.BlockSpec((1, 1, d), lambda i: (i // tiles_per_b, 0, 0))
    gq2, gkv2 = gq.reshape(1, q_lora), gkv.reshape(1, kv_lora)
    scale = (MLA_NOPE + MLA_ROPE) ** -0.5
    tk = ATTN_TK
    kt_per_tile = tm // tk
    s = tiles_per_b * tm
    return pl.pallas_call(
        functools.partial(_mla_in_kernel, nh=nh, q_lora=q_lora, kv_lora=kv_lora, scale=scale, tk=tk),
        out_shape=[jax.ShapeDtypeStruct((t, nh * 2 * LANES), BF16), jax.ShapeDtypeStruct((t, nh * 2 * LANES), BF16),
                   jax.ShapeDtypeStruct((t // s, s // tk, nh * MLA_V, tk), BF16)],
        grid=(t // tm,),
        in_specs=[row(d), mod, mod, row(1), full(invf), full(sign), full(win_ext), full(gq2), full(gkv2),
                  full(wqa), full(wqb), full(wk), full(wvt)],
        out_specs=[row(nh * 2 * LANES), row(nh * 2 * LANES),
                   pl.BlockSpec((1, kt_per_tile, nh * MLA_V, tk),
                                lambda i: (i // tiles_per_b, i % tiles_per_b, 0, 0))],
        compiler_params=_cparams(("parallel",)),
        name="mla_in",
    )(x2, sc, sh, pos2, invf, sign, win_ext, gq2, gkv2, wqa, wqb, wk, wvt)


def _attn_kernel(q_ref, k_ref, vt_ref, o_ref, acc_ref, s0_ref, s1_ref, *, s, tq, tk, hb):
    kpq = tq // tk
    assert kpq == 2, "the pipeline below alternates two score buffers over pairs of key tiles"
    hw = 2 * LANES
    ones = jnp.ones((8, tk), BF16)

    def q_block(qi, carry):
        qrows = pl.ds(pl.multiple_of(qi * tq, tq), tq)
        acc_ref[...] = jnp.zeros_like(acc_ref)

        def scores(j, s_ref, c0=0, nc=tq):
            krows = pl.ds(pl.multiple_of(j * tk, tk), tk)
            cols = pl.ds(pl.multiple_of(qi * tq + c0, tk), nc)
            for h in range(hb):
                s_ref[h, :, c0:c0 + nc] = _dot_nt(k_ref[krows, h * hw:(h + 1) * hw], q_ref[cols, h * hw:(h + 1) * hw])

        def tile(j, s_ref, stats, masked, c0=0, nc=tq):
            def put(full, part):
                pieces = ([full[:, :c0]] if c0 else []) + [part] + ([full[:, c0 + nc:]] if c0 + nc < tq else [])
                return pieces[0] if len(pieces) == 1 else jnp.concatenate(pieces, axis=1)
            ps, alphas, out = [], [], []
            for h in range(hb):
                m, st = stats[2 * h][:, c0:c0 + nc], s_ref[h, :, c0:c0 + nc]
                if masked:
                    key = j * tk + lax.broadcasted_iota(I32, (tk, nc), 0)
                    qry = qi * tq + c0 + lax.broadcasted_iota(I32, (tk, nc), 1)
                    st = jnp.where(key <= qry, st, -jnp.inf)
                m_new = jnp.maximum(m, jnp.max(st, axis=0, keepdims=True))
                ps.append(jnp.exp(st - m_new).astype(BF16))
                alphas.append(jnp.exp(m - m_new))
                out.append(put(stats[2 * h], m_new))
            for h in range(hb):
                acc_ref[h, :, c0:c0 + nc] = (alphas[h] * acc_ref[h, :, c0:c0 + nc]
                                             + _dot(vt_ref[0, j, h * MLA_V:(h + 1) * MLA_V, :], ps[h]))
                l_new = alphas[h] * stats[2 * h + 1][:, c0:c0 + nc] + _dot(ones, ps[h])[0:1]
                out.insert(2 * h + 1, put(stats[2 * h + 1], l_new))
            return tuple(out)

        stats = (jnp.full((1, tq), -jnp.inf, F32), jnp.zeros((1, tq), F32)) * hb
        scores(0, s0_ref)

        def pair(jj, c):
            j = 2 * jj
            scores(j + 1, s1_ref)
            c = tile(j, s0_ref, c, False)
            scores(j + 2, s0_ref)
            return tile(j + 1, s1_ref, c, False)

        stats = lax.fori_loop(0, qi, pair, stats)
        scores(2 * qi + 1, s1_ref, tk, tk)
        stats = tile(2 * qi, s0_ref, stats, True, 0, tk)
        stats = tile(2 * qi, s0_ref, stats, False, tk, tk)
        stats = tile(2 * qi + 1, s1_ref, stats, True, tk, tk)
        for h in range(hb):
            o_ref[qrows, h * MLA_V:(h + 1) * MLA_V] = (acc_ref[h] / stats[2 * h + 1]).T.astype(o_ref.dtype)
        return carry

    lax.fori_loop(0, s // tq, q_block, 0)


def _attn(q, k, vt, bsz, s):
    nh = MLA_HEADS
    tq, tk = ATTN_TQ, ATTN_TK
    hb = ATTN_HEADS_PER_STEP
    hw = 2 * LANES
    return pl.pallas_call(
        functools.partial(_attn_kernel, s=s, tq=tq, tk=tk, hb=hb),
        out_shape=jax.ShapeDtypeStruct((bsz * s, nh * MLA_V), BF16),
        grid=(bsz, nh // hb),
        in_specs=[pl.BlockSpec((s, hb * hw), lambda b, h: (b, h)),
                  pl.BlockSpec((s, hb * hw), lambda b, h: (b, h)),
                  pl.BlockSpec((1, s // tk, hb * MLA_V, tk), lambda b, h: (b, 0, h, 0))],
        out_specs=pl.BlockSpec((s, hb * MLA_V), lambda b, h: (b, h)),
        scratch_shapes=[pltpu.VMEM((hb, MLA_V, tq), F32), pltpu.VMEM((hb, tk, tq), F32),
                        pltpu.VMEM((hb, tk, tq), F32)],
        compiler_params=_cparams(("parallel", "parallel")),
        name="attn",
    )(q, k, vt)


def _route_kernel(x_ref, sc_ref, sh_ref, w_ref, b_ref, lp_ref, wt_ref, tm_ref, offs_ref, te_ref, na_ref,
                  upper_ref, carry_ref, *, tm, ne, ng, row_tile, nt_pad):
    i = pl.program_id(0)
    epg = ne // ng

    @pl.when(i == 0)
    def _():
        r = lax.broadcasted_iota(I32, (tm, tm), 0)
        c = lax.broadcasted_iota(I32, (tm, tm), 1)
        upper_ref[...] = jnp.where(r < c, 1.0, 0.0).astype(BF16)
        carry_ref[...] = jnp.zeros_like(carry_ref)

    h = x_ref[...] * (1.0 + sc_ref[0]) + sh_ref[0]
    logits = jnp.dot(h, w_ref[...], precision=HIGHEST, preferred_element_type=F32) + b_ref[...]
    lt = logits.T
    lr = lt[0:ne]
    grow = lax.broadcasted_iota(I32, (8, tm), 0).astype(F32)
    lg = jnp.where(grow < ng, lt[ne:ne + 8], -jnp.inf)
    gmax = jnp.max(lg, axis=0, keepdims=True)
    g_idx = jnp.min(jnp.where(lg == gmax, grow, 1e9), axis=0, keepdims=True)
    g_w = 1.0 / jnp.sum(jnp.exp(lg - gmax), axis=0, keepdims=True)
    erow = lax.broadcasted_iota(I32, (ne, tm), 0).astype(F32)
    in_group = jnp.floor(erow * (1.0 / epg)) == g_idx
    sel = jnp.where(in_group, lr, -jnp.inf)
    v1 = jnp.max(sel, axis=0, keepdims=True)
    i1 = jnp.min(jnp.where(sel == v1, erow, 1e9), axis=0, keepdims=True)
    sel2 = jnp.where(erow == i1, -jnp.inf, sel)
    v2 = jnp.max(sel2, axis=0, keepdims=True)
    i2 = jnp.min(jnp.where(sel2 == v2, erow, 1e9), axis=0, keepdims=True)
    t = jnp.exp(v2 - v1)
    w1 = g_w / (1.0 + t)
    w2 = g_w * t / (1.0 + t)
    oh1 = erow == i1
    oh2 = erow == i2
    member = jnp.where(oh1 | oh2, 1.0, 0.0)
    lcnt = jnp.sum(member, axis=1, keepdims=True)
    lcnt = jnp.floor((lcnt + (SUBLANES - 1)) * (1.0 / SUBLANES)) * SUBLANES
    er_ = lax.broadcasted_iota(I32, (ne, ne), 0)
    ec_ = lax.broadcasted_iota(I32, (ne, ne), 1)
    lstart = jnp.dot(jnp.where(ec_ < er_, 1.0, 0.0).astype(F32), jnp.broadcast_to(lcnt, (ne, LANES)),
                     precision=HIGHEST, preferred_element_type=F32)[:, 0:1]
    lrank = _dot(member.astype(BF16), upper_ref[...]) + lstart
    p1 = jnp.sum(jnp.where(oh1, lrank, 0.0), axis=0, keepdims=True)
    p2 = jnp.sum(jnp.where(oh2, lrank, 0.0), axis=0, keepdims=True)

    orow = lax.broadcasted_iota(I32, (8, tm), 0)
    lp_ref[...] = jnp.where(orow == 0, p1, jnp.where(orow == 1, p2, 0.0)).astype(I32)
    wrow = lax.broadcasted_iota(I32, (LANES, tm), 0)
    wt_ref[...] = jnp.where(wrow == 0, w1, jnp.where(wrow == 1, w2,
                                                     jnp.where(wrow == 2, p1, jnp.where(wrow == 3, p2, 0.0)))).T
    mr = lax.broadcasted_iota(I32, (ne, LANES), 0)
    mc = lax.broadcasted_iota(I32, (ne, LANES), 1)
    to_row = lambda col, lane0: jnp.sum(jnp.where(mr + lane0 == mc, col, 0.0), axis=0, keepdims=True)
    total = jnp.sum(lcnt, axis=0, keepdims=True)
    lane = lax.broadcasted_iota(I32, (1, LANES), 1)
    packed = (to_row(lstart, 0) + to_row(lcnt, ne) + to_row(carry_ref[...], 2 * ne)
              + jnp.where(lane == 3 * ne, total, 0.0))
    trow = lax.broadcasted_iota(I32, (8, LANES), 0)
    tm_ref[...] = jnp.where(trow == 0, packed, 0.0).astype(I32)
    carry_ref[...] = carry_ref[...] + lcnt

    @pl.when(i == pl.num_programs(0) - 1)
    def _():
        cnt = carry_ref[...]
        ntl = jnp.floor((cnt + (row_tile - 1)) * (1.0 / row_tile))
        incl = jnp.where(ec_ <= er_, 1.0, 0.0).astype(F32)
        ends = jnp.dot(incl, jnp.broadcast_to(ntl, (ne, LANES)), precision=HIGHEST,
                       preferred_element_type=F32)
        starts = ends - ntl
        offs_ref[...] = jnp.concatenate([starts * row_tile, ends * row_tile, jnp.broadcast_to(ntl, (ne, LANES)),
                                         jnp.zeros((8, LANES), F32)], axis=0).astype(I32)
        tile = lax.broadcasted_iota(I32, (ne, nt_pad), 1).astype(F32)
        te = jnp.sum(jnp.where(ends[:, 0:1] <= tile, 1.0, 0.0), axis=0, keepdims=True)
        te_ref[...] = jnp.broadcast_to(jnp.minimum(te, ne - 1.0), (8, nt_pad)).astype(I32)
        na_ref[...] = jnp.broadcast_to(ends[ne - 1:ne, :], (8, LANES)).astype(I32)


def _moe_tiles(t):
    rows = 2 * t + (SUBLANES - 1) * MOE_EXPERTS * (t // TOKEN_TILE)
    nt_max = -(-rows // MOE_ROW_TILE) + MOE_EXPERTS
    nt_pad = -(-nt_max // LANES) * LANES
    return nt_max, nt_pad


def _sorted_rows(tm):
    return -(-(2 * tm + (SUBLANES - 1) * MOE_EXPERTS) // LANES) * LANES


def _route(x2, sc, sh, w_group, b_group, w_router, b_router, tiles_per_b):
    t, d = x2.shape
    ne, ng = MOE_EXPERTS, MOE_GROUPS
    tm = TOKEN_TILE
    _, nt_pad = _moe_tiles(t)
    wcat = jnp.concatenate([w_router, w_group, jnp.zeros((d, LANES - ne - ng), F32)], axis=1)
    bcat = jnp.concatenate([b_router, b_group, jnp.zeros((LANES - ne - ng,), F32)]).reshape(1, LANES)
    const = lambda shp: pl.BlockSpec(shp, lambda i: (0,) * len(shp))
    return pl.pallas_call(
        functools.partial(_route_kernel, tm=tm, ne=ne, ng=ng, row_tile=MOE_ROW_TILE, nt_pad=nt_pad),
        out_shape=[jax.ShapeDtypeStruct((8, t), I32), jax.ShapeDtypeStruct((t, LANES), F32),
                   jax.ShapeDtypeStruct((8 * (t // tm), LANES), I32),
                   jax.ShapeDtypeStruct((3 * ne + 8, LANES), I32), jax.ShapeDtypeStruct((8, nt_pad), I32),
                   jax.ShapeDtypeStruct((8, LANES), I32)],
        grid=(t // tm,),
        in_specs=[pl.BlockSpec((tm, d), lambda i: (i, 0)),
                  pl.BlockSpec((1, 1, d), lambda i: (i // tiles_per_b, 0, 0)),
                  pl.BlockSpec((1, 1, d), lambda i: (i // tiles_per_b, 0, 0)),
                  const((d, LANES)), const((1, LANES))],
        out_specs=[pl.BlockSpec((8, tm), lambda i: (0, i)), pl.BlockSpec((tm, LANES), lambda i: (i, 0)),
                   pl.BlockSpec((8, LANES), lambda i: (i, 0)),
                   const((3 * ne + 8, LANES)), const((8, nt_pad)), const((8, LANES))],
        scratch_shapes=[pltpu.VMEM((tm, tm), BF16), pltpu.VMEM((ne, 1), F32)],
        compiler_params=_cparams(("arbitrary",)),
        name="moe_route",
    )(x2, sc, sh, wcat, bcat)


def _start_runs(tm_ref, tile, ne, copy):
    for e in range(ne):
        lstart = tm_ref[tile, e]
        n = tm_ref[tile, ne + e]
        before = tm_ref[tile, 2 * ne + e]
        _binary_pieces(n, TOKEN_TILE, lambda off, rows: copy(
            e, pl.multiple_of(lstart + off, SUBLANES), pl.multiple_of(before + off, SUBLANES), rows))


def _binary_pieces(n, n_max, copy):
    del n_max
    nbig = n >> (RUN_PIECE.bit_length() - 1)

    def big_piece(c, carry):
        copy(c * RUN_PIECE, RUN_PIECE).start()
        return carry

    lax.fori_loop(0, nbig, big_piece, 0)
    off = nbig * RUN_PIECE
    p = RUN_PIECE // 2
    while p >= SUBLANES:
        @pl.when((n & p) != 0)
        def _():
            copy(off, p).start()
        off = off + (n & p)
        p //= 2


def _dispatch_kernel(offs_ref, gend_ref, ntl_ref, tm_ref, x_ref, sc_ref, sh_ref, lp_ref, xs_ref,
                     h_ref, z_ref, sem_z, sem_r, *, tm, ne, row_tile, nt_max, ntile, srows):
    i = pl.program_id(0)
    tile = i - 1

    @pl.when(i == 0)
    def _():
        z_ref[...] = jnp.zeros_like(z_ref)
        for e in range(ne):
            @pl.when(ntl_ref[e] > 0)
            def _():
                start = pl.multiple_of(gend_ref[e] - row_tile, row_tile)
                cp = pltpu.make_async_copy(z_ref, xs_ref.at[pl.ds(start, row_tile), :], sem_z)
                cp.start()
                cp.wait()
        for back in range(1, nt_max - (2 * tm * ntile) // row_tile + 1):
            @pl.when(nt_max - back >= gend_ref[ne - 1] // row_tile)
            def _():
                cp = pltpu.make_async_copy(z_ref, xs_ref.at[pl.ds((nt_max - back) * row_tile, row_tile), :], sem_z)
                cp.start()
                cp.wait()
        for spill in range(2):
            cp = pltpu.make_async_copy(z_ref, xs_ref.at[pl.ds((nt_max + spill) * row_tile, row_tile), :], sem_z)
            cp.start()
            cp.wait()

    def wait_tile(t):
        slot = t % 2
        pltpu.make_async_copy(h_ref.at[slot], xs_ref.at[pl.ds(0, srows), :], sem_r.at[slot]).wait()

    @pl.when((tile >= 2) & (tile <= ntile))
    def _():
        wait_tile(tile - 2)

    @pl.when(tile == ntile)
    def _():
        wait_tile(tile - 1)

    @pl.when((tile >= 0) & (tile < ntile))
    def _():
        slot = tile % 2
        h = (x_ref[...] * (1.0 + sc_ref[0]) + sh_ref[0]).astype(BF16)
        row = lax.broadcasted_iota(I32, (srows, tm), 0)
        lp = lp_ref[...]
        onehot = jnp.where(row == lp[0:1, :], 1.0, jnp.where(row == lp[1:2, :], 1.0, 0.0)).astype(BF16)
        h_ref[slot] = _dot(onehot, h)

        def copy(e, local_row, rows_before, rows):
            dst = pl.multiple_of(offs_ref[e] + rows_before, SUBLANES)
            return pltpu.make_async_copy(h_ref.at[slot, pl.ds(local_row, rows), :], xs_ref.at[pl.ds(dst, rows), :],
                                         sem_r.at[slot])

        _start_runs(tm_ref, tile, ne, copy)
        used = tm_ref[tile, 3 * ne]
        _binary_pieces(srows - used, srows - 2 * tm, lambda off, rows: pltpu.make_async_copy(
            h_ref.at[slot, pl.ds(pl.multiple_of(used + off, SUBLANES), rows), :],
            xs_ref.at[pl.ds(pl.multiple_of((nt_max + slot) * row_tile + off, SUBLANES), rows), :], sem_r.at[slot]))


def _dispatch(x2, sc, sh, lp, tmeta, offs, gend, ntl, tiles_per_b):
    t, d = x2.shape
    tm = TOKEN_TILE
    nt_max, _ = _moe_tiles(t)
    ntile = t // tm
    srows = _sorted_rows(tm)
    cur = lambda i: jnp.clip(i - 1, 0, ntile - 1)
    grid_spec = pltpu.PrefetchScalarGridSpec(
        num_scalar_prefetch=4,
        grid=(ntile + 2,),
        in_specs=[pl.BlockSpec((tm, d), lambda i, *_: (cur(i), 0)),
                  pl.BlockSpec((1, 1, d), lambda i, *_: (cur(i) // tiles_per_b, 0, 0)),
                  pl.BlockSpec((1, 1, d), lambda i, *_: (cur(i) // tiles_per_b, 0, 0)),
                  pl.BlockSpec((8, tm), lambda i, *_: (0, cur(i)))],
        out_specs=pl.BlockSpec(memory_space=pl.ANY),
        scratch_shapes=[pltpu.VMEM((2, srows, d), F32), pltpu.VMEM((MOE_ROW_TILE, d), F32),
                        pltpu.SemaphoreType.DMA, pltpu.SemaphoreType.DMA((2,))])
    return pl.pallas_call(
        functools.partial(_dispatch_kernel, tm=tm, ne=MOE_EXPERTS, row_tile=MOE_ROW_TILE, nt_max=nt_max,
                          ntile=ntile, srows=srows),
        out_shape=jax.ShapeDtypeStruct(((nt_max + 2) * MOE_ROW_TILE, d), F32),
        grid_spec=grid_spec,
        compiler_params=_cparams(("arbitrary",)),
        name="moe_dispatch",
    )(offs, gend, ntl, tmeta, x2, sc, sh, lp)


def _gmm_kernel(te_ref, na_ref, xs_ref, wg_ref, wu_ref, wd_ref, ys_ref, wgu_buf, wd_buf, *, ff, tr):
    j = pl.program_id(0)

    @pl.when(j < na_ref[0])
    def _():
        changed = (j == 0) | (te_ref[j] != te_ref[jnp.maximum(j - 1, 0)])

        @pl.when(changed)
        def _():
            wgu_buf[:, :ff] = wg_ref[0, 0].astype(BF16)
            wgu_buf[:, ff:] = wu_ref[0, 0].astype(BF16)
            wd_buf[...] = wd_ref[0, 0].astype(BF16)

        halves = [slice(c * (tr // 2), (c + 1) * (tr // 2)) for c in range(2)]
        gus = [_dot(xs_ref[rows, :].astype(BF16), wgu_buf[...]) for rows in halves]
        hids = [(gu[:, :ff] * _sigmoid(gu[:, :ff]) * gu[:, ff:]).astype(BF16) for gu in gus]
        for rows, hid in zip(halves, hids):
            ys_ref[rows, :] = _dot(hid, wd_buf[...])

    @pl.when(j >= na_ref[0])
    def _():
        ys_ref[...] = jnp.zeros_like(ys_ref)


def _gmm(xs, te, na, w_gate, w_up, w_down, layer):
    d, ff = w_gate.shape[-2:]
    tr = MOE_ROW_TILE
    ns = te.shape[0] * tr
    act = lambda j, te_ref, na_ref: jnp.minimum(j, na_ref[0] - 1)
    grid_spec = pltpu.PrefetchScalarGridSpec(
        num_scalar_prefetch=2,
        grid=(ns // tr,),
        in_specs=[pl.BlockSpec((tr, d), lambda j, te_ref, na_ref: (act(j, te_ref, na_ref), 0)),
                  pl.BlockSpec((1, 1, d, ff), lambda j, te_ref, na_ref: (layer, te_ref[act(j, te_ref, na_ref)], 0, 0)),
                  pl.BlockSpec((1, 1, d, ff), lambda j, te_ref, na_ref: (layer, te_ref[act(j, te_ref, na_ref)], 0, 0)),
                  pl.BlockSpec((1, 1, ff, d), lambda j, te_ref, na_ref: (layer, te_ref[act(j, te_ref, na_ref)], 0, 0))],
        out_specs=pl.BlockSpec((tr, d), lambda j, te_ref, na_ref: (j, 0)),
        scratch_shapes=[pltpu.VMEM((d, 2 * ff), BF16), pltpu.VMEM((ff, d), BF16)])
    return pl.pallas_call(
        functools.partial(_gmm_kernel, ff=ff, tr=tr),
        out_shape=jax.ShapeDtypeStruct((ns, d), F32),
        grid_spec=grid_spec,
        compiler_params=_cparams(("arbitrary",)),
        name="moe_gmm",
    )(te, na, xs, w_gate, w_up, w_down)


def _combine_kernel(offs_ref, tm_ref, wt_ref, x_ref, gate_ref, lg_ref, lb_ref, ys_ref, o_ref,
                    buf_ref, sem_r, *, tm, ne, ntile, srows, alpha):
    i = pl.program_id(0)

    def fetch(tile):
        slot = tile % 2

        def copy(e, local_row, rows_before, rows):
            src = pl.multiple_of(offs_ref[e] + rows_before, SUBLANES)
            return pltpu.make_async_copy(ys_ref.at[pl.ds(src, rows), :], buf_ref.at[slot, pl.ds(local_row, rows), :],
                                         sem_r.at[slot])

        _start_runs(tm_ref, tile, ne, copy)
        used = tm_ref[tile, 3 * ne]
        _binary_pieces(srows - used, srows - 2 * tm, lambda off, rows: pltpu.make_async_copy(
            ys_ref.at[pl.ds(pl.multiple_of(off, SUBLANES), rows), :],
            buf_ref.at[slot, pl.ds(pl.multiple_of(used + off, SUBLANES), rows), :], sem_r.at[slot]))

    def wait(tile):
        slot = tile % 2
        pltpu.make_async_copy(ys_ref.at[pl.ds(0, srows), :], buf_ref.at[slot], sem_r.at[slot]).wait()

    def finish(tile):
        w = wt_ref[...]
        rows = buf_ref[tile % 2].astype(BF16)
        col = lax.broadcasted_iota(I32, (tm, srows), 1).astype(F32)
        pick = lambda k: _dot(jnp.where(col == w[:, 2 + k:3 + k], 1.0, 0.0).astype(BF16), rows)

        ffn = w[:, 0:1] * pick(0) + w[:, 1:2] * pick(1)
        y = alpha * x_ref[...] + (1.0 + gate_ref[0]) * ffn
        o_ref[...] = _layer_norm(y, lg_ref[...], lb_ref[...])

    @pl.when(i == 0)
    def _():
        fetch(i)

    @pl.when((i > 0) & (i < ntile))
    def _():
        wait(i - 1)
        fetch(i)
        finish(i - 1)

    @pl.when(i == ntile)
    def _():
        wait(i - 1)
        finish(i - 1)


def _combine(ys, tmeta, wt, x2, gate, ln_g, ln_b, offs, tiles_per_b, alpha):
    t, d = x2.shape
    tm = TOKEN_TILE
    ntile = t // tm
    srows = _sorted_rows(tm)
    prev = lambda i: jnp.maximum(i - 1, 0)
    vec = pl.BlockSpec((1, d), lambda i, *_: (0, 0))
    grid_spec = pltpu.PrefetchScalarGridSpec(
        num_scalar_prefetch=2,
        grid=(ntile + 1,),
        in_specs=[pl.BlockSpec((tm, LANES), lambda i, *_: (prev(i), 0)),
                  pl.BlockSpec((tm, d), lambda i, *_: (prev(i), 0)),
                  pl.BlockSpec((1, 1, d), lambda i, *_: (prev(i) // tiles_per_b, 0, 0)),
                  vec, vec,
                  pl.BlockSpec(memory_space=pl.ANY)],
        out_specs=pl.BlockSpec((tm, d), lambda i, *_: (prev(i), 0)),
        scratch_shapes=[pltpu.VMEM((2, srows, d), F32), pltpu.SemaphoreType.DMA((2,))])
    return pl.pallas_call(
        functools.partial(_combine_kernel, tm=tm, ne=MOE_EXPERTS, ntile=ntile, srows=srows, alpha=alpha),
        out_shape=jax.ShapeDtypeStruct((t, d), F32),
        grid_spec=grid_spec,
        compiler_params=_cparams(("arbitrary",)),
        name="moe_combine",
    )(offs, tmeta, wt, x2, gate, ln_g.reshape(1, d), ln_b.reshape(1, d), ys)


def _moe_block(x2, sc, sh, gate, ln_g, ln_b, w_group, b_group, w_router, b_router, w_gate, w_up, w_down,
               layer, tiles_per_b, alpha):
    ne = MOE_EXPERTS
    nt_max, _ = _moe_tiles(x2.shape[0])
    lp, wt, tmeta, meta, te, na = _route(x2, sc, sh, w_group, b_group, w_router, b_router, tiles_per_b)
    offs, gend, ntl = meta[:ne, 0], meta[ne:2 * ne, 0], meta[2 * ne:3 * ne, 0]
    tmeta = tmeta.reshape(-1, SUBLANES, LANES)[:, 0, :]
    xs = _dispatch(x2, sc, sh, lp, tmeta, offs, gend, ntl, tiles_per_b)
    ys = _gmm(xs, te[0, :nt_max], na[0, :1], w_gate, w_up, w_down, layer)
    return _combine(ys, tmeta, wt, x2, gate, ln_g, ln_b, offs, tiles_per_b, alpha)


def kernel(x, c, positions, ada_w, ada_b, ln_mix_g, ln_mix_b, ln_ffn_g, ln_ffn_b, ab_w_in, conv_w, conv_b, conv_ln_g, conv_ln_b, gla_gate_w, gla_gate_b, gla_norm_g, ab_w_out, mla_w_in, mla_q_norm_g, mla_kv_norm_g, mla_w_uq, mla_w_ukv, mla_w_out, moe_w_group, moe_b_group, moe_w_router, moe_b_router, moe_w_gate, moe_w_up, moe_w_down):
    bsz, s, d = x.shape
    depth = ada_w.shape[0]
    t = bsz * s
    tiles_per_b = s // TOKEN_TILE
    alpha = (2 * depth) ** 0.25
    mod = _ada(c, ada_w, ada_b).reshape(depth, bsz, 6, 1, d)
    x2 = x.reshape(t, d)
    for layer in range(depth):
        sh_m, sc_m, g_m, sh_f, sc_f, g_f = (mod[layer, :, n] for n in range(6))
        i = layer // 2
        if layer % 2 == 0:
            uc, q, k, v, r, gl = _ab_in(x2, sc_m, sh_m, ab_w_in[i], gla_gate_w[i], gla_gate_b[i], tiles_per_b)
            y_a = _conv(uc.reshape(bsz, s, -1), conv_w[i], conv_b[i], conv_ln_g[i], conv_ln_b[i])
            b3 = lambda a: a.reshape(bsz, s, -1)
            y_b = _gla(b3(q), b3(k), b3(v), b3(gl), b3(r), gla_norm_g[i])
            w_out = ab_w_out[i].astype(BF16)
            cc = y_a.shape[-1]
            acts = [y_a.reshape(t, cc), y_b.reshape(t, -1)]
            weights = [w_out[:cc], w_out[cc:]]
        else:
            qc, kc, vv = _mla_in(x2, sc_m, sh_m, positions.reshape(t, 1), mla_w_in[i], mla_q_norm_g[i],
                                 mla_kv_norm_g[i], mla_w_uq[i], mla_w_ukv[i], tiles_per_b)
            acts = [_attn(qc, kc, vv, bsz, s)]
            weights = [mla_w_out[i].astype(BF16)]
        x2 = _proj_ln(acts, weights, x2, g_m, ln_mix_g[layer], ln_mix_b[layer], tiles_per_b, alpha)
        x2 = _moe_block(x2, sc_f, sh_f, g_f, ln_ffn_g[layer], ln_ffn_b[layer], moe_w_group[layer], moe_b_group[layer],
                        moe_w_router[layer], moe_b_router[layer], moe_w_gate, moe_w_up, moe_w_down,
                        layer, tiles_per_b, alpha)
    return x2.reshape(bsz, s, d)
```

```python
import functools

import jax
import jax.numpy as jnp
from jax import lax
from jax.experimental import pallas as pl
from jax.experimental.pallas import tpu as pltpu

F32 = jnp.float32
BF16 = jnp.bfloat16
I32 = jnp.int32
HIGHEST = lax.Precision.HIGHEST

LN_EPS = 1e-5
RMS_EPS = 1e-6
CONV_WIDTH = 31
GLA_HEADS = 4
GLA_GATE_TAU = 16.0
MLA_HEADS = 8
MLA_NOPE = 128
MLA_ROPE = 64
MLA_V = 128
ROPE_THETA = 10000.0
MOE_GROUPS = 4
MOE_EXPERTS_PER_GROUP = 8
MOE_EXPERTS = MOE_GROUPS * MOE_EXPERTS_PER_GROUP

LANES = 128
SUBLANES = 8
TOKEN_TILE = 512
GLA_CHUNK = 128
GLA_BLOCK = 512
GLA_BATCH_PER_STEP = 2
CONV_ROWS = 32
CONV_HALO = 32
ATTN_TQ = 512
ATTN_TK = 256
ATTN_HEADS_PER_STEP = 4
MOE_ROW_TILE = 512
RUN_PIECE = 64
VMEM_LIMIT = 48 * 1024 * 1024


def _cparams(sem):
    return pltpu.CompilerParams(dimension_semantics=sem, vmem_limit_bytes=VMEM_LIMIT)


def _sigmoid(x):
    return 1.0 / (1.0 + jnp.exp(-x))


def _dot(a, b):
    return jnp.dot(a, b, preferred_element_type=F32)


def _dot_nt(a, b):
    return lax.dot_general(a, b, (((1,), (1,)), ((), ())), preferred_element_type=F32)


def _dot_tn(a, b):
    return lax.dot_general(a, b, (((0,), (0,)), ((), ())), preferred_element_type=F32)


def _dot_01_f32(a01, x):
    hi = x.astype(BF16)
    r1 = x - hi.astype(F32)
    mid = r1.astype(BF16)
    lo = (r1 - mid.astype(F32)).astype(BF16)
    return _dot(a01, hi) + (_dot(a01, mid) + _dot(a01, lo))


def _layer_norm(y, g, b):
    mu = jnp.mean(y, axis=-1, keepdims=True)
    d = y - mu
    var = jnp.mean(d * d, axis=-1, keepdims=True)
    return d * lax.rsqrt(var + LN_EPS) * g + b


def _ada_kernel(c_ref, w_ref, b_ref, o_ref):
    c = c_ref[...]
    o_ref[0] = jnp.dot(c * _sigmoid(c), w_ref[0], precision=HIGHEST, preferred_element_type=F32) + b_ref[0]


def _ada(c, ada_w, ada_b):
    depth, d, n = ada_w.shape
    bsz = c.shape[0]
    tn = 1536
    return pl.pallas_call(
        _ada_kernel,
        out_shape=jax.ShapeDtypeStruct((depth, bsz, n), F32),
        grid=(depth, n // tn),
        in_specs=[pl.BlockSpec((bsz, d), lambda l, j: (0, 0)),
                  pl.BlockSpec((1, d, tn), lambda l, j: (l, 0, j)),
                  pl.BlockSpec((1, 1, tn), lambda l, j: (l, 0, j))],
        out_specs=pl.BlockSpec((1, bsz, tn), lambda l, j: (l, 0, j)),
        compiler_params=_cparams(("parallel", "parallel")),
        name="ada",
    )(c, ada_w, ada_b.reshape(depth, 1, n))


def _ab_in_kernel(x_ref, sc_ref, sh_ref, wc_ref, wq_ref, wk_ref, wv_ref, wr_ref, wg_ref, gw_ref, gb_ref,
                  uc_ref, q_ref, k_ref, v_ref, r_ref, gl_ref):
    h = (x_ref[...] * (1.0 + sc_ref[0]) + sh_ref[0]).astype(BF16)
    uc_ref[...] = _dot(h, wc_ref[...])
    q_ref[...] = _dot(h, wq_ref[...])
    k_ref[...] = _dot(h, wk_ref[...])
    v_ref[...] = _dot(h, wv_ref[...]).astype(v_ref.dtype)
    r_ref[...] = _dot(h, wr_ref[...])
    g_low = _dot(h, wg_ref[...])
    z = jnp.dot(g_low, gw_ref[...], precision=HIGHEST, preferred_element_type=F32) + gb_ref[...]
    gl_ref[...] = (jnp.minimum(z, 0.0) - jnp.log(1.0 + jnp.exp(-jnp.abs(z)))) * (1.0 / GLA_GATE_TAU)


def _ab_in(x2, sc, sh, w_in, gate_w, gate_b, tiles_per_b):
    t, d = x2.shape
    cc2 = d
    kw = d // 4
    vw = d // 2
    rank = gate_w.shape[0]
    splits = [cc2, cc2 + kw, cc2 + 2 * kw, cc2 + 2 * kw + vw, cc2 + 2 * kw + 2 * vw]
    wb = w_in.astype(BF16)
    ws = [wb[:, :splits[0]], wb[:, splits[0]:splits[1]], wb[:, splits[1]:splits[2]],
          wb[:, splits[2]:splits[3]], wb[:, splits[3]:splits[4]], wb[:, splits[4]:]]
    tm = TOKEN_TILE
    full = lambda a: pl.BlockSpec(a.shape, lambda i: (0,) * a.ndim)
    row = lambda n: pl.BlockSpec((tm, n), lambda i: (i, 0))
    mod = pl.BlockSpec((1, 1, d), lambda i: (i // tiles_per_b, 0, 0))
    gb2 = gate_b.reshape(1, kw)
    widths = [cc2, kw, kw, vw, vw, kw]
    return pl.pallas_call(
        _ab_in_kernel,
        out_shape=[jax.ShapeDtypeStruct((t, n), BF16 if idx == 3 else F32) for idx, n in enumerate(widths)],
        grid=(t // tm,),
        in_specs=[row(d), mod, mod] + [full(w) for w in ws] + [full(gate_w), full(gb2)],
        out_specs=[row(n) for n in widths],
        compiler_params=_cparams(("parallel",)),
        name="ab_in",
    )(x2, sc, sh, *ws, gate_w, gb2)


def _conv_kernel(u_ref, halo_ref, cw_ref, cb_ref, lg_ref, lb_ref, o_ref, hp_ref, *, ts, cc):
    j = pl.program_id(1)

    def glu(u):
        return u[:, :cc] * _sigmoid(u[:, cc:])

    hp_ref[0, 0:CONV_HALO, :] = jnp.where(j > 0, glu(halo_ref[0]), 0.0)
    hp_ref[0, CONV_HALO:CONV_HALO + ts, :] = glu(u_ref[0])
    nrow = CONV_HALO + ts
    for b in range(1, SUBLANES):
        hp_ref[b, 0:nrow - SUBLANES, :] = hp_ref[0, b:nrow - SUBLANES + b, :]
    shift = CONV_HALO - (CONV_WIDTH - 1)
    for rb in range(ts // CONV_ROWS):
        r0 = rb * CONV_ROWS
        acc = jnp.zeros((CONV_ROWS, cc), F32)
        for tap in range(CONV_WIDTH):
            lo = r0 + shift + tap
            base = lo // SUBLANES * SUBLANES
            acc = acc + cw_ref[tap:tap + 1, :] * hp_ref[lo - base, base:base + CONV_ROWS, :]
        y = _layer_norm(acc + cb_ref[...], lg_ref[...], lb_ref[...])
        o_ref[0, r0:r0 + CONV_ROWS, :] = (y * _sigmoid(y)).astype(o_ref.dtype)


def _conv(u3, conv_w, conv_b, ln_g, ln_b):
    bsz, s, cc2 = u3.shape
    cc = cc2 // 2
    ts = TOKEN_TILE
    hb = ts // CONV_HALO
    vec = lambda a: pl.BlockSpec((1, cc), lambda b, j: (0, 0))
    return pl.pallas_call(
        functools.partial(_conv_kernel, ts=ts, cc=cc),
        out_shape=jax.ShapeDtypeStruct((bsz, s, cc), BF16),
        grid=(bsz, s // ts),
        in_specs=[pl.BlockSpec((1, ts, cc2), lambda b, j: (b, j, 0)),
                  pl.BlockSpec((1, CONV_HALO, cc2), lambda b, j: (b, jnp.maximum(j * hb - 1, 0), 0)),
                  pl.BlockSpec((CONV_WIDTH, cc), lambda b, j: (0, 0)),
                  vec(conv_b), vec(ln_g), vec(ln_b)],
        out_specs=pl.BlockSpec((1, ts, cc), lambda b, j: (b, j, 0)),
        scratch_shapes=[pltpu.VMEM((SUBLANES, CONV_HALO + ts, cc), F32)],
        compiler_params=_cparams(("parallel", "parallel")),
        name="conv",
    )(u3, u3, conv_w, conv_b.reshape(1, cc), ln_g.reshape(1, cc), ln_b.reshape(1, cc))


def _gla_kernel(q_ref, k_ref, v_ref, gl_ref, r_ref, ng_ref, o_ref, st_ref, *, nb, nh, dk, dv, gc, nchunks):
    @pl.when(pl.program_id(1) == 0)
    def _():
        st_ref[...] = jnp.zeros_like(st_ref)

    row = lax.broadcasted_iota(I32, (gc, gc), 0)
    col = lax.broadcasted_iota(I32, (gc, gc), 1)
    causal = col <= row
    tri = jnp.where(causal, 1.0, 0.0).astype(BF16)
    scale = dk ** -0.5

    ks = [slice(h * dk, (h + 1) * dk) for h in range(nh)]
    vs = [slice(h * dv, (h + 1) * dv) for h in range(nh)]
    streams = [(bb, h) for bb in range(nb) for h in range(nh)]

    def chunk(c, carry):
        r0 = pl.multiple_of(c * gc, gc)
        rows = pl.ds(r0, gc)
        bs = [_dot_01_f32(tri, gl_ref[bb, rows, :]) for bb in range(nb)]
        q_in, k_in, q_st, k_st, decay, v = [], [], [], [], [], []
        for bb in range(nb):
            b = bs[bb]
            b_last = b[gc - 1:gc, :]
            mid = 0.5 * b_last
            q = q_ref[bb, rows, :] * scale
            k = k_ref[bb, rows, :]
            v.append(v_ref[bb, rows, :].astype(BF16))
            q_in.append((q * jnp.exp(b - mid)).astype(BF16))
            k_in.append((k * jnp.exp(mid - b)).astype(BF16))
            q_st.append((q * jnp.exp(b)).astype(BF16))
            k_st.append((k * jnp.exp(b_last - b)).astype(BF16))
            decay.append(jnp.exp(b_last))
        sts = [st_ref[bb, h] for bb, h in streams]
        scores = [_dot_nt(q_in[bb][:, ks[h]], k_in[bb][:, ks[h]]) for bb, h in streams]
        inter = [_dot_nt(q_st[bb][:, ks[h]], st.astype(BF16)) for (bb, h), st in zip(streams, sts)]
        update = [_dot_tn(v[bb][:, vs[h]], k_st[bb][:, ks[h]]) for bb, h in streams]
        atts = [jnp.where(causal, sc, 0.0).astype(BF16) for sc in scores]
        outs = [_dot(att, v[bb][:, vs[h]]) + it for (bb, h), att, it in zip(streams, atts, inter)]
        for (bb, h), st, up, o in zip(streams, sts, update, outs):
            st_ref[bb, h] = st * decay[bb][:, ks[h]] + up
            r = r_ref[bb, rows, vs[h]]
            o = o * lax.rsqrt(jnp.mean(o * o, axis=-1, keepdims=True) + RMS_EPS) * ng_ref[:, vs[h]]
            o_ref[bb, rows, vs[h]] = (o * (r * _sigmoid(r))).astype(o_ref.dtype)
        return carry

    lax.fori_loop(0, nchunks, chunk, 0)


def _gla(q3, k3, v3, gl3, r3, norm_g):
    bsz, s, kw = q3.shape
    vw = v3.shape[-1]
    nh = GLA_HEADS
    dk, dv = kw // nh, vw // nh
    cb = GLA_BLOCK
    gc = GLA_CHUNK
    nb = GLA_BATCH_PER_STEP
    blk = lambda n: pl.BlockSpec((nb, cb, n), lambda b, j: (b, j, 0))
    return pl.pallas_call(
        functools.partial(_gla_kernel, nb=nb, nh=nh, dk=dk, dv=dv, gc=gc, nchunks=cb // gc),
        out_shape=jax.ShapeDtypeStruct((bsz, s, vw), BF16),
        grid=(bsz // nb, s // cb),
        in_specs=[blk(kw), blk(kw), blk(vw), blk(kw), blk(vw), pl.BlockSpec((1, vw), lambda b, j: (0, 0))],
        out_specs=blk(vw),
        scratch_shapes=[pltpu.VMEM((nb, nh, dv, dk), F32)],
        compiler_params=_cparams(("parallel", "arbitrary")),
        name="gla",
    )(q3, k3, v3, gl3, r3, norm_g.reshape(1, vw))


def _proj_ln_kernel(*refs, n_in, alpha):
    a_refs, w_refs = refs[:n_in], refs[n_in:2 * n_in]
    x_ref, gate_ref, lg_ref, lb_ref, o_ref = refs[2 * n_in:]
    mix = _dot(a_refs[0][...], w_refs[0][...])
    for a_ref, w_ref in zip(a_refs[1:], w_refs[1:]):
        mix = mix + _dot(a_ref[...], w_ref[...])
    y = alpha * x_ref[...] + (1.0 + gate_ref[0]) * mix
    o_ref[...] = _layer_norm(y, lg_ref[...], lb_ref[...])


def _proj_ln(acts, weights, x2, gate, ln_g, ln_b, tiles_per_b, alpha):
    t, d = x2.shape
    tm = TOKEN_TILE
    n_in = len(acts)
    row = lambda n: pl.BlockSpec((tm, n), lambda i: (i, 0))
    full = lambda a: pl.BlockSpec(a.shape, lambda i: (0,) * a.ndim)
    vec = pl.BlockSpec((1, d), lambda i: (0, 0))
    return pl.pallas_call(
        functools.partial(_proj_ln_kernel, n_in=n_in, alpha=alpha),
        out_shape=jax.ShapeDtypeStruct((t, d), F32),
        grid=(t // tm,),
        in_specs=[row(a.shape[1]) for a in acts] + [full(w) for w in weights]
                 + [row(d), pl.BlockSpec((1, 1, d), lambda i: (i // tiles_per_b, 0, 0)), vec, vec],
        out_specs=row(d),
        compiler_params=_cparams(("parallel",)),
        name="proj_ln",
    )(*acts, *weights, x2, gate, ln_g.reshape(1, d), ln_b.reshape(1, d))


def _mla_in_kernel(x_ref, sc_ref, sh_ref, pos_ref, invf_ref, sign_ref, win_ref, gq_ref, gkv_ref,
                   wqa_ref, wqb_ref, wk_ref, wvt_ref, q_ref, k_ref, vt_ref, *, nh, q_lora, kv_lora, scale, tk):
    h = (x_ref[...] * (1.0 + sc_ref[0]) + sh_ref[0]).astype(BF16)
    u = _dot(h, win_ref[...])
    cq = u[:, :q_lora]
    ckv = u[:, q_lora:q_lora + kv_lora]
    kr = u[:, q_lora + kv_lora:q_lora + kv_lora + LANES]
    kr_sw = u[:, q_lora + kv_lora + LANES:]
    cqn = (cq * lax.rsqrt(jnp.mean(cq * cq, axis=-1, keepdims=True) + RMS_EPS) * gq_ref[...]).astype(BF16)
    kvn = (ckv * lax.rsqrt(jnp.mean(ckv * ckv, axis=-1, keepdims=True) + RMS_EPS) * gkv_ref[...]).astype(BF16)
    ang = pos_ref[...].astype(F32) * invf_ref[...]
    cos = jnp.cos(ang)
    sin = jnp.sin(ang) * sign_ref[...]
    kr_rot = (kr * cos + kr_sw * sin).astype(BF16)
    qa = _dot(cqn, wqa_ref[...])
    qb = _dot(cqn, wqb_ref[...])
    kv = _dot(kvn, wk_ref[...])
    hw = 2 * LANES
    for hd in range(nh):
        q_ref[:, hd * hw:hd * hw + LANES] = (qa[:, hd * hw:hd * hw + LANES] * scale).astype(BF16)
        rope = qa[:, hd * hw + LANES:(hd + 1) * hw] * cos + qb[:, hd * LANES:(hd + 1) * LANES] * sin
        q_ref[:, hd * hw + LANES:(hd + 1) * hw] = (rope * scale).astype(BF16)
        k_ref[:, hd * hw:hd * hw + LANES] = kv[:, hd * LANES:(hd + 1) * LANES].astype(BF16)
        k_ref[:, hd * hw + LANES:(hd + 1) * hw] = kr_rot
    vt = _dot_nt(wvt_ref[...], kvn).astype(BF16)
    for c in range(vt.shape[1] // tk):
        vt_ref[0, c] = vt[:, c * tk:(c + 1) * tk]


def _mla_in(x2, sc, sh, pos2, w_in, gq, gkv, w_uq, w_ukv, tiles_per_b):
    t, d = x2.shape
    nh = MLA_HEADS
    q_lora, kv_lora = gq.shape[0], gkv.shape[0]
    half = MLA_ROPE // 2
    pad = LANES - MLA_ROPE
    kr_w = w_in[:, q_lora + kv_lora:]
    kr_sw = jnp.concatenate([kr_w[:, half:], kr_w[:, :half]], axis=1)
    zpad = jnp.zeros((d, pad), w_in.dtype)
    win_ext = jnp.concatenate([w_in[:, :q_lora + kv_lora], kr_w, zpad, kr_sw, zpad], axis=1).astype(BF16)
    wq = w_uq.reshape(q_lora, nh, MLA_NOPE + MLA_ROPE)
    q_nope, q_rope = wq[:, :, :MLA_NOPE], wq[:, :, MLA_NOPE:]
    q_rope_sw = jnp.concatenate([q_rope[:, :, half:], q_rope[:, :, :half]], axis=2)
    zq = jnp.zeros((q_lora, nh, pad), w_uq.dtype)
    wqa = jnp.concatenate([q_nope, q_rope, zq], axis=2).reshape(q_lora, nh * 2 * LANES).astype(BF16)
    wqb = jnp.concatenate([q_rope_sw, zq], axis=2).reshape(q_lora, nh * LANES).astype(BF16)
    wkv = w_ukv.reshape(kv_lora, nh, MLA_NOPE + MLA_V)
    wk = wkv[:, :, :MLA_NOPE].reshape(kv_lora, nh * MLA_NOPE).astype(BF16)
    wvt = wkv[:, :, MLA_NOPE:].reshape(kv_lora, nh * MLA_V).T.astype(BF16)
    inv_freq = 1.0 / (ROPE_THETA ** (jnp.arange(0, MLA_ROPE, 2, dtype=F32) / MLA_ROPE))
    invf = jnp.concatenate([inv_freq, inv_freq, jnp.zeros((pad,), F32)]).reshape(1, LANES)
    sign = jnp.concatenate([-jnp.ones((half,), F32), jnp.ones((half,), F32), jnp.zeros((pad,), F32)]).reshape(1, LANES)
    tm = TOKEN_TILE
    full = lambda a: pl.BlockSpec(a.shape, lambda i: (0,) * a.ndim)
    row = lambda n: pl.BlockSpec((tm, n), lambda i: (i, 0))
    mod = pl.BlockSpec((1, 1, d), lambda i: (i // tiles_per_b, 0, 0))
    gq2, gkv2 = gq.reshape(1, q_lora), gkv.reshape(1, kv_lora)
    scale = (MLA_NOPE + MLA_ROPE) ** -0.5
    tk = ATTN_TK
    kt_per_tile = tm // tk
    s = tiles_per_b * tm
    return pl.pallas_call(
        functools.partial(_mla_in_kernel, nh=nh, q_lora=q_lora, kv_lora=kv_lora, scale=scale, tk=tk),
        out_shape=[jax.ShapeDtypeStruct((t, nh * 2 * LANES), BF16), jax.ShapeDtypeStruct((t, nh * 2 * LANES), BF16),
                   jax.ShapeDtypeStruct((t // s, s // tk, nh * MLA_V, tk), BF16)],
        grid=(t // tm,),
        in_specs=[row(d), mod, mod, row(1), full(invf), full(sign), full(win_ext), full(gq2), full(gkv2),
                  full(wqa), full(wqb), full(wk), full(wvt)],
        out_specs=[row(nh * 2 * LANES), row(nh * 2 * LANES),
                   pl.BlockSpec((1, kt_per_tile, nh * MLA_V, tk),
                                lambda i: (i // tiles_per_b, i % tiles_per_b, 0, 0))],
        compiler_params=_cparams(("parallel",)),
        name="mla_in",
    )(x2, sc, sh, pos2, invf, sign, win_ext, gq2, gkv2, wqa, wqb, wk, wvt)


def _attn_kernel(q_ref, k_ref, vt_ref, o_ref, acc_ref, s0_ref, s1_ref, *, s, tq, tk, hb):
    kpq = tq // tk
    assert kpq == 2, "the pipeline below alternates two score buffers over pairs of key tiles"
    hw = 2 * LANES
    ones = jnp.ones((8, tk), BF16)

    def q_block(qi, carry):
        qrows = pl.ds(pl.multiple_of(qi * tq, tq), tq)
        acc_ref[...] = jnp.zeros_like(acc_ref)

        def scores(j, s_ref, c0=0, nc=tq):
            krows = pl.ds(pl.multiple_of(j * tk, tk), tk)
            cols = pl.ds(pl.multiple_of(qi * tq + c0, tk), nc)
            for h in range(hb):
                s_ref[h, :, c0:c0 + nc] = _dot_nt(k_ref[krows, h * hw:(h + 1) * hw], q_ref[cols, h * hw:(h + 1) * hw])

        def tile(j, s_ref, stats, masked, c0=0, nc=tq):
            def put(full, part):
                pieces = ([full[:, :c0]] if c0 else []) + [part] + ([full[:, c0 + nc:]] if c0 + nc < tq else [])
                return pieces[0] if len(pieces) == 1 else jnp.concatenate(pieces, axis=1)
            ps, alphas, out = [], [], []
            for h in range(hb):
                m, st = stats[2 * h][:, c0:c0 + nc], s_ref[h, :, c0:c0 + nc]
                if masked:
                    key = j * tk + lax.broadcasted_iota(I32, (tk, nc), 0)
                    qry = qi * tq + c0 + lax.broadcasted_iota(I32, (tk, nc), 1)
                    st = jnp.where(key <= qry, st, -jnp.inf)
                m_new = jnp.maximum(m, jnp.max(st, axis=0, keepdims=True))
                ps.append(jnp.exp(st - m_new).astype(BF16))
                alphas.append(jnp.exp(m - m_new))
                out.append(put(stats[2 * h], m_new))
            for h in range(hb):
                acc_ref[h, :, c0:c0 + nc] = (alphas[h] * acc_ref[h, :, c0:c0 + nc]
                                             + _dot(vt_ref[0, j, h * MLA_V:(h + 1) * MLA_V, :], ps[h]))
                l_new = alphas[h] * stats[2 * h + 1][:, c0:c0 + nc] + _dot(ones, ps[h])[0:1]
                out.insert(2 * h + 1, put(stats[2 * h + 1], l_new))
            return tuple(out)

        stats = (jnp.full((1, tq), -jnp.inf, F32), jnp.zeros((1, tq), F32)) * hb
        scores(0, s0_ref)

        def pair(jj, c):
            j = 2 * jj
            scores(j + 1, s1_ref)
            c = tile(j, s0_ref, c, False)
            scores(j + 2, s0_ref)
            return tile(j + 1, s1_ref, c, False)

        stats = lax.fori_loop(0, qi, pair, stats)
        scores(2 * qi + 1, s1_ref, tk, tk)
        stats = tile(2 * qi, s0_ref, stats, True, 0, tk)
        stats = tile(2 * qi, s0_ref, stats, False, tk, tk)
        stats = tile(2 * qi + 1, s1_ref, stats, True, tk, tk)
        for h in range(hb):
            o_ref[qrows, h * MLA_V:(h + 1) * MLA_V] = (acc_ref[h] / stats[2 * h + 1]).T.astype(o_ref.dtype)
        return carry

    lax.fori_loop(0, s // tq, q_block, 0)


def _attn(q, k, vt, bsz, s):
    nh = MLA_HEADS
    tq, tk = ATTN_TQ, ATTN_TK
    hb = ATTN_HEADS_PER_STEP
    hw = 2 * LANES
    return pl.pallas_call(
        functools.partial(_attn_kernel, s=s, tq=tq, tk=tk, hb=hb),
        out_shape=jax.ShapeDtypeStruct((bsz * s, nh * MLA_V), BF16),
        grid=(bsz, nh // hb),
        in_specs=[pl.BlockSpec((s, hb * hw), lambda b, h: (b, h)),
                  pl.BlockSpec((s, hb * hw), lambda b, h: (b, h)),
                  pl.BlockSpec((1, s // tk, hb * MLA_V, tk), lambda b, h: (b, 0, h, 0))],
        out_specs=pl.BlockSpec((s, hb * MLA_V), lambda b, h: (b, h)),
        scratch_shapes=[pltpu.VMEM((hb, MLA_V, tq), F32), pltpu.VMEM((hb, tk, tq), F32),
                        pltpu.VMEM((hb, tk, tq), F32)],
        compiler_params=_cparams(("parallel", "parallel")),
        name="attn",
    )(q, k, vt)


def _route_kernel(x_ref, sc_ref, sh_ref, w_ref, b_ref, lp_ref, wt_ref, tm_ref, offs_ref, te_ref, na_ref,
                  upper_ref, carry_ref, *, tm, ne, ng, row_tile, nt_pad):
    i = pl.program_id(0)
    epg = ne // ng

    @pl.when(i == 0)
    def _():
        r = lax.broadcasted_iota(I32, (tm, tm), 0)
        c = lax.broadcasted_iota(I32, (tm, tm), 1)
        upper_ref[...] = jnp.where(r < c, 1.0, 0.0).astype(BF16)
        carry_ref[...] = jnp.zeros_like(carry_ref)

    h = x_ref[...] * (1.0 + sc_ref[0]) + sh_ref[0]
    logits = jnp.dot(h, w_ref[...], precision=HIGHEST, preferred_element_type=F32) + b_ref[...]
    lt = logits.T
    lr = lt[0:ne]
    grow = lax.broadcasted_iota(I32, (8, tm), 0).astype(F32)
    lg = jnp.where(grow < ng, lt[ne:ne + 8], -jnp.inf)
    gmax = jnp.max(lg, axis=0, keepdims=True)
    g_idx = jnp.min(jnp.where(lg == gmax, grow, 1e9), axis=0, keepdims=True)
    g_w = 1.0 / jnp.sum(jnp.exp(lg - gmax), axis=0, keepdims=True)
    erow = lax.broadcasted_iota(I32, (ne, tm), 0).astype(F32)
    in_group = jnp.floor(erow * (1.0 / epg)) == g_idx
    sel = jnp.where(in_group, lr, -jnp.inf)
    v1 = jnp.max(sel, axis=0, keepdims=True)
    i1 = jnp.min(jnp.where(sel == v1, erow, 1e9), axis=0, keepdims=True)
    sel2 = jnp.where(erow == i1, -jnp.inf, sel)
    v2 = jnp.max(sel2, axis=0, keepdims=True)
    i2 = jnp.min(jnp.where(sel2 == v2, erow, 1e9), axis=0, keepdims=True)
    t = jnp.exp(v2 - v1)
    w1 = g_w / (1.0 + t)
    w2 = g_w * t / (1.0 + t)
    oh1 = erow == i1
    oh2 = erow == i2
    member = jnp.where(oh1 | oh2, 1.0, 0.0)
    lcnt = jnp.sum(member, axis=1, keepdims=True)
    lcnt = jnp.floor((lcnt + (SUBLANES - 1)) * (1.0 / SUBLANES)) * SUBLANES
    er_ = lax.broadcasted_iota(I32, (ne, ne), 0)
    ec_ = lax.broadcasted_iota(I32, (ne, ne), 1)
    lstart = jnp.dot(jnp.where(ec_ < er_, 1.0, 0.0).astype(F32), jnp.broadcast_to(lcnt, (ne, LANES)),
                     precision=HIGHEST, preferred_element_type=F32)[:, 0:1]
    lrank = _dot(member.astype(BF16), upper_ref[...]) + lstart
    p1 = jnp.sum(jnp.where(oh1, lrank, 0.0), axis=0, keepdims=True)
    p2 = jnp.sum(jnp.where(oh2, lrank, 0.0), axis=0, keepdims=True)

    orow = lax.broadcasted_iota(I32, (8, tm), 0)
    lp_ref[...] = jnp.where(orow == 0, p1, jnp.where(orow == 1, p2, 0.0)).astype(I32)
    wrow = lax.broadcasted_iota(I32, (LANES, tm), 0)
    wt_ref[...] = jnp.where(wrow == 0, w1, jnp.where(wrow == 1, w2,
                                                     jnp.where(wrow == 2, p1, jnp.where(wrow == 3, p2, 0.0)))).T
    mr = lax.broadcasted_iota(I32, (ne, LANES), 0)
    mc = lax.broadcasted_iota(I32, (ne, LANES), 1)
    to_row = lambda col, lane0: jnp.sum(jnp.where(mr + lane0 == mc, col, 0.0), axis=0, keepdims=True)
    total = jnp.sum(lcnt, axis=0, keepdims=True)
    lane = lax.broadcasted_iota(I32, (1, LANES), 1)
    packed = (to_row(lstart, 0) + to_row(lcnt, ne) + to_row(carry_ref[...], 2 * ne)
              + jnp.where(lane == 3 * ne, total, 0.0))
    trow = lax.broadcasted_iota(I32, (8, LANES), 0)
    tm_ref[...] = jnp.where(trow == 0, packed, 0.0).astype(I32)
    carry_ref[...] = carry_ref[...] + lcnt

    @pl.when(i == pl.num_programs(0) - 1)
    def _():
        cnt = carry_ref[...]
        ntl = jnp.floor((cnt + (row_tile - 1)) * (1.0 / row_tile))
        incl = jnp.where(ec_ <= er_, 1.0, 0.0).astype(F32)
        ends = jnp.dot(incl, jnp.broadcast_to(ntl, (ne, LANES)), precision=HIGHEST,
                       preferred_element_type=F32)
        starts = ends - ntl
        offs_ref[...] = jnp.concatenate([starts * row_tile, ends * row_tile, jnp.broadcast_to(ntl, (ne, LANES)),
                                         jnp.zeros((8, LANES), F32)], axis=0).astype(I32)
        tile = lax.broadcasted_iota(I32, (ne, nt_pad), 1).astype(F32)
        te = jnp.sum(jnp.where(ends[:, 0:1] <= tile, 1.0, 0.0), axis=0, keepdims=True)
        te_ref[...] = jnp.broadcast_to(jnp.minimum(te, ne - 1.0), (8, nt_pad)).astype(I32)
        na_ref[...] = jnp.broadcast_to(ends[ne - 1:ne, :], (8, LANES)).astype(I32)


def _moe_tiles(t):
    rows = 2 * t + (SUBLANES - 1) * MOE_EXPERTS * (t // TOKEN_TILE)
    nt_max = -(-rows // MOE_ROW_TILE) + MOE_EXPERTS
    nt_pad = -(-nt_max // LANES) * LANES
    return nt_max, nt_pad


def _sorted_rows(tm):
    return -(-(2 * tm + (SUBLANES - 1) * MOE_EXPERTS) // LANES) * LANES


def _route(x2, sc, sh, w_group, b_group, w_router, b_router, tiles_per_b):
    t, d = x2.shape
    ne, ng = MOE_EXPERTS, MOE_GROUPS
    tm = TOKEN_TILE
    _, nt_pad = _moe_tiles(t)
    wcat = jnp.concatenate([w_router, w_group, jnp.zeros((d, LANES - ne - ng), F32)], axis=1)
    bcat = jnp.concatenate([b_router, b_group, jnp.zeros((LANES - ne - ng,), F32)]).reshape(1, LANES)
    const = lambda shp: pl.BlockSpec(shp, lambda i: (0,) * len(shp))
    return pl.pallas_call(
        functools.partial(_route_kernel, tm=tm, ne=ne, ng=ng, row_tile=MOE_ROW_TILE, nt_pad=nt_pad),
        out_shape=[jax.ShapeDtypeStruct((8, t), I32), jax.ShapeDtypeStruct((t, LANES), F32),
                   jax.ShapeDtypeStruct((8 * (t // tm), LANES), I32),
                   jax.ShapeDtypeStruct((3 * ne + 8, LANES), I32), jax.ShapeDtypeStruct((8, nt_pad), I32),
                   jax.ShapeDtypeStruct((8, LANES), I32)],
        grid=(t // tm,),
        in_specs=[pl.BlockSpec((tm, d), lambda i: (i, 0)),
                  pl.BlockSpec((1, 1, d), lambda i: (i // tiles_per_b, 0, 0)),
                  pl.BlockSpec((1, 1, d), lambda i: (i // tiles_per_b, 0, 0)),
                  const((d, LANES)), const((1, LANES))],
        out_specs=[pl.BlockSpec((8, tm), lambda i: (0, i)), pl.BlockSpec((tm, LANES), lambda i: (i, 0)),
                   pl.BlockSpec((8, LANES), lambda i: (i, 0)),
                   const((3 * ne + 8, LANES)), const((8, nt_pad)), const((8, LANES))],
        scratch_shapes=[pltpu.VMEM((tm, tm), BF16), pltpu.VMEM((ne, 1), F32)],
        compiler_params=_cparams(("arbitrary",)),
        name="moe_route",
    )(x2, sc, sh, wcat, bcat)


def _start_runs(tm_ref, tile, ne, copy):
    for e in range(ne):
        lstart = tm_ref[tile, e]
        n = tm_ref[tile, ne + e]
        before = tm_ref[tile, 2 * ne + e]
        _binary_pieces(n, TOKEN_TILE, lambda off, rows: copy(
            e, pl.multiple_of(lstart + off, SUBLANES), pl.multiple_of(before + off, SUBLANES), rows))


def _binary_pieces(n, n_max, copy):
    del n_max
    nbig = n >> (RUN_PIECE.bit_length() - 1)

    def big_piece(c, carry):
        copy(c * RUN_PIECE, RUN_PIECE).start()
        return carry

    lax.fori_loop(0, nbig, big_piece, 0)
    off = nbig * RUN_PIECE
    p = RUN_PIECE // 2
    while p >= SUBLANES:
        @pl.when((n & p) != 0)
        def _():
            copy(off, p).start()
        off = off + (n & p)
        p //= 2


def _dispatch_kernel(offs_ref, gend_ref, ntl_ref, tm_ref, x_ref, sc_ref, sh_ref, lp_ref, wt_ref, xs_ref,
                     h_ref, z_ref, sem_z, sem_r, *, tm, ne, row_tile, nt_max, ntile, srows):
    i = pl.program_id(0)
    tile = i - 1

    @pl.when(i == 0)
    def _():
        z_ref[...] = jnp.zeros_like(z_ref)
        for e in range(ne):
            @pl.when(ntl_ref[e] > 0)
            def _():
                start = pl.multiple_of(gend_ref[e] - row_tile, row_tile)
                cp = pltpu.make_async_copy(z_ref, xs_ref.at[pl.ds(start, row_tile), :], sem_z)
                cp.start()
                cp.wait()
        for back in range(1, nt_max - (2 * tm * ntile) // row_tile + 1):
            @pl.when(nt_max - back >= gend_ref[ne - 1] // row_tile)
            def _():
                cp = pltpu.make_async_copy(z_ref, xs_ref.at[pl.ds((nt_max - back) * row_tile, row_tile), :], sem_z)
                cp.start()
                cp.wait()
        for spill in range(2):
            cp = pltpu.make_async_copy(z_ref, xs_ref.at[pl.ds((nt_max + spill) * row_tile, row_tile), :], sem_z)
            cp.start()
            cp.wait()

    def wait_tile(t):
        slot = t % 2
        pltpu.make_async_copy(h_ref.at[slot], xs_ref.at[pl.ds(0, srows), :], sem_r.at[slot]).wait()

    @pl.when((tile >= 2) & (tile <= ntile))
    def _():
        wait_tile(tile - 2)

    @pl.when(tile == ntile)
    def _():
        wait_tile(tile - 1)

    @pl.when((tile >= 0) & (tile < ntile))
    def _():
        slot = tile % 2
        h = (x_ref[...] * (1.0 + sc_ref[0]) + sh_ref[0]).astype(BF16)
        row = lax.broadcasted_iota(I32, (srows, tm), 0)
        lp = lp_ref[...]
        oh1 = jnp.where(row == lp[0:1, :], 1.0, 0.0).astype(BF16)
        oh2 = jnp.where(row == lp[1:2, :], 1.0, 0.0).astype(BF16)
        d = h.shape[1]
        h_ref[slot, :, :d] = _dot(oh1 + oh2, h)
        lane = lax.broadcasted_iota(I32, (tm, LANES), 1)

        def weight_terms(w):
            hi = w.astype(BF16).astype(F32)
            mid = (w - hi).astype(BF16).astype(F32)
            lo = w - hi - mid
            return jnp.where(lane == 0, hi, jnp.where(lane == 1, mid, jnp.where(lane == 2, lo, 0.0))).astype(BF16)

        wt = wt_ref[...]
        h_ref[slot, :, d:] = _dot(oh1, weight_terms(wt[:, 0:1])) + _dot(oh2, weight_terms(wt[:, 1:2]))

        def copy(e, local_row, rows_before, rows):
            dst = pl.multiple_of(offs_ref[e] + rows_before, SUBLANES)
            return pltpu.make_async_copy(h_ref.at[slot, pl.ds(local_row, rows), :], xs_ref.at[pl.ds(dst, rows), :],
                                         sem_r.at[slot])

        _start_runs(tm_ref, tile, ne, copy)
        used = tm_ref[tile, 3 * ne]
        _binary_pieces(srows - used, srows - 2 * tm, lambda off, rows: pltpu.make_async_copy(
            h_ref.at[slot, pl.ds(pl.multiple_of(used + off, SUBLANES), rows), :],
            xs_ref.at[pl.ds(pl.multiple_of((nt_max + slot) * row_tile + off, SUBLANES), rows), :], sem_r.at[slot]))


def _dispatch(x2, sc, sh, lp, wt, tmeta, offs, gend, ntl, tiles_per_b):
    t, d = x2.shape
    tm = TOKEN_TILE
    nt_max, _ = _moe_tiles(t)
    ntile = t // tm
    srows = _sorted_rows(tm)
    cur = lambda i: jnp.clip(i - 1, 0, ntile - 1)
    grid_spec = pltpu.PrefetchScalarGridSpec(
        num_scalar_prefetch=4,
        grid=(ntile + 2,),
        in_specs=[pl.BlockSpec((tm, d), lambda i, *_: (cur(i), 0)),
                  pl.BlockSpec((1, 1, d), lambda i, *_: (cur(i) // tiles_per_b, 0, 0)),
                  pl.BlockSpec((1, 1, d), lambda i, *_: (cur(i) // tiles_per_b, 0, 0)),
                  pl.BlockSpec((8, tm), lambda i, *_: (0, cur(i))),
                  pl.BlockSpec((tm, LANES), lambda i, *_: (cur(i), 0))],
        out_specs=pl.BlockSpec(memory_space=pl.ANY),
        scratch_shapes=[pltpu.VMEM((2, srows, d + LANES), F32), pltpu.VMEM((MOE_ROW_TILE, d + LANES), F32),
                        pltpu.SemaphoreType.DMA, pltpu.SemaphoreType.DMA((2,))])
    return pl.pallas_call(
        functools.partial(_dispatch_kernel, tm=tm, ne=MOE_EXPERTS, row_tile=MOE_ROW_TILE, nt_max=nt_max,
                          ntile=ntile, srows=srows),
        out_shape=jax.ShapeDtypeStruct(((nt_max + 2) * MOE_ROW_TILE, d + LANES), F32),
        grid_spec=grid_spec,
        compiler_params=_cparams(("arbitrary",)),
        name="moe_dispatch",
    )(offs, gend, ntl, tmeta, x2, sc, sh, lp, wt)


def _gmm_kernel(te_ref, na_ref, xs_ref, wg_ref, wu_ref, wd_ref, ys_ref, wgu_buf, wd_buf, *, ff, tr):
    j = pl.program_id(0)

    @pl.when(j < na_ref[0])
    def _():
        changed = (j == 0) | (te_ref[j] != te_ref[jnp.maximum(j - 1, 0)])

        @pl.when(changed)
        def _():
            wgu_buf[:, :ff] = wg_ref[0, 0].astype(BF16)
            wgu_buf[:, ff:] = wu_ref[0, 0].astype(BF16)
            wd_buf[...] = wd_ref[0, 0].astype(BF16)

        halves = [slice(c * (tr // 2), (c + 1) * (tr // 2)) for c in range(2)]
        d = wgu_buf.shape[0]
        gus = [_dot(xs_ref[rows, :d].astype(BF16), wgu_buf[...]) for rows in halves]
        hids = [(gu[:, :ff] * _sigmoid(gu[:, :ff]) * gu[:, ff:]).astype(BF16) for gu in gus]
        for rows, hid in zip(halves, hids):
            w = xs_ref[rows, d:d + 1] + xs_ref[rows, d + 1:d + 2] + xs_ref[rows, d + 2:d + 3]
            ys_ref[rows, :] = w * _dot(hid, wd_buf[...])

    @pl.when(j >= na_ref[0])
    def _():
        ys_ref[...] = jnp.zeros_like(ys_ref)


def _gmm(xs, te, na, w_gate, w_up, w_down, layer):
    d, ff = w_gate.shape[-2:]
    tr = MOE_ROW_TILE
    ns = te.shape[0] * tr
    act = lambda j, te_ref, na_ref: jnp.minimum(j, na_ref[0] - 1)
    grid_spec = pltpu.PrefetchScalarGridSpec(
        num_scalar_prefetch=2,
        grid=(ns // tr,),
        in_specs=[pl.BlockSpec((tr, d + LANES), lambda j, te_ref, na_ref: (act(j, te_ref, na_ref), 0)),
                  pl.BlockSpec((1, 1, d, ff), lambda j, te_ref, na_ref: (layer, te_ref[act(j, te_ref, na_ref)], 0, 0)),
                  pl.BlockSpec((1, 1, d, ff), lambda j, te_ref, na_ref: (layer, te_ref[act(j, te_ref, na_ref)], 0, 0)),
                  pl.BlockSpec((1, 1, ff, d), lambda j, te_ref, na_ref: (layer, te_ref[act(j, te_ref, na_ref)], 0, 0))],
        out_specs=pl.BlockSpec((tr, d), lambda j, te_ref, na_ref: (j, 0)),
        scratch_shapes=[pltpu.VMEM((d, 2 * ff), BF16), pltpu.VMEM((ff, d), BF16)])
    return pl.pallas_call(
        functools.partial(_gmm_kernel, ff=ff, tr=tr),
        out_shape=jax.ShapeDtypeStruct((ns, d), F32),
        grid_spec=grid_spec,
        compiler_params=_cparams(("arbitrary",)),
        name="moe_gmm",
    )(te, na, xs, w_gate, w_up, w_down)


def _combine_kernel(offs_ref, tm_ref, wt_ref, x_ref, gate_ref, lg_ref, lb_ref, ys_ref, o_ref,
                    buf_ref, sem_r, *, tm, ne, ntile, srows, alpha):
    i = pl.program_id(0)

    def fetch(tile):
        slot = tile % 2

        def copy(e, local_row, rows_before, rows):
            src = pl.multiple_of(offs_ref[e] + rows_before, SUBLANES)
            return pltpu.make_async_copy(ys_ref.at[pl.ds(src, rows), :], buf_ref.at[slot, pl.ds(local_row, rows), :],
                                         sem_r.at[slot])

        _start_runs(tm_ref, tile, ne, copy)
        used = tm_ref[tile, 3 * ne]
        _binary_pieces(srows - used, srows - 2 * tm, lambda off, rows: pltpu.make_async_copy(
            ys_ref.at[pl.ds(pl.multiple_of(off, SUBLANES), rows), :],
            buf_ref.at[slot, pl.ds(pl.multiple_of(used + off, SUBLANES), rows), :], sem_r.at[slot]))

    def wait(tile):
        slot = tile % 2
        pltpu.make_async_copy(ys_ref.at[pl.ds(0, srows), :], buf_ref.at[slot], sem_r.at[slot]).wait()

    def finish(tile):
        w = wt_ref[...]
        rows = buf_ref[tile % 2].astype(BF16)
        col = lax.broadcasted_iota(I32, (tm, srows), 1).astype(F32)
        both = jnp.where(col == w[:, 2:3], 1.0, jnp.where(col == w[:, 3:4], 1.0, 0.0)).astype(BF16)
        ffn = _dot(both, rows)
        y = alpha * x_ref[...] + (1.0 + gate_ref[0]) * ffn
        o_ref[...] = _layer_norm(y, lg_ref[...], lb_ref[...])

    @pl.when(i == 0)
    def _():
        fetch(i)

    @pl.when((i > 0) & (i < ntile))
    def _():
        wait(i - 1)
        fetch(i)
        finish(i - 1)

    @pl.when(i == ntile)
    def _():
        wait(i - 1)
        finish(i - 1)


def _combine(ys, tmeta, wt, x2, gate, ln_g, ln_b, offs, tiles_per_b, alpha):
    t, d = x2.shape
    tm = TOKEN_TILE
    ntile = t // tm
    srows = _sorted_rows(tm)
    prev = lambda i: jnp.maximum(i - 1, 0)
    vec = pl.BlockSpec((1, d), lambda i, *_: (0, 0))
    grid_spec = pltpu.PrefetchScalarGridSpec(
        num_scalar_prefetch=2,
        grid=(ntile + 1,),
        in_specs=[pl.BlockSpec((tm, LANES), lambda i, *_: (prev(i), 0)),
                  pl.BlockSpec((tm, d), lambda i, *_: (prev(i), 0)),
                  pl.BlockSpec((1, 1, d), lambda i, *_: (prev(i) // tiles_per_b, 0, 0)),
                  vec, vec,
                  pl.BlockSpec(memory_space=pl.ANY)],
        out_specs=pl.BlockSpec((tm, d), lambda i, *_: (prev(i), 0)),
        scratch_shapes=[pltpu.VMEM((2, srows, d), F32), pltpu.SemaphoreType.DMA((2,))])
    return pl.pallas_call(
        functools.partial(_combine_kernel, tm=tm, ne=MOE_EXPERTS, ntile=ntile, srows=srows, alpha=alpha),
        out_shape=jax.ShapeDtypeStruct((t, d), F32),
        grid_spec=grid_spec,
        compiler_params=_cparams(("arbitrary",)),
        name="moe_combine",
    )(offs, tmeta, wt, x2, gate, ln_g.reshape(1, d), ln_b.reshape(1, d), ys)


def _moe_block(x2, sc, sh, gate, ln_g, ln_b, w_group, b_group, w_router, b_router, w_gate, w_up, w_down,
               layer, tiles_per_b, alpha):
    ne = MOE_EXPERTS
    nt_max, _ = _moe_tiles(x2.shape[0])
    lp, wt, tmeta, meta, te, na = _route(x2, sc, sh, w_group, b_group, w_router, b_router, tiles_per_b)
    offs, gend, ntl = meta[:ne, 0], meta[ne:2 * ne, 0], meta[2 * ne:3 * ne, 0]
    tmeta = tmeta.reshape(-1, SUBLANES, LANES)[:, 0, :]
    xs = _dispatch(x2, sc, sh, lp, wt, tmeta, offs, gend, ntl, tiles_per_b)
    ys = _gmm(xs, te[0, :nt_max], na[0, :1], w_gate, w_up, w_down, layer)
    return _combine(ys, tmeta, wt, x2, gate, ln_g, ln_b, offs, tiles_per_b, alpha)


def kernel(x, c, positions, ada_w, ada_b, ln_mix_g, ln_mix_b, ln_ffn_g, ln_ffn_b, ab_w_in, conv_w, conv_b, conv_ln_g, conv_ln_b, gla_gate_w, gla_gate_b, gla_norm_g, ab_w_out, mla_w_in, mla_q_norm_g, mla_kv_norm_g, mla_w_uq, mla_w_ukv, mla_w_out, moe_w_group, moe_b_group, moe_w_router, moe_b_router, moe_w_gate, moe_w_up, moe_w_down):
    bsz, s, d = x.shape
    depth = ada_w.shape[0]
    t = bsz * s
    tiles_per_b = s // TOKEN_TILE
    alpha = (2 * depth) ** 0.25
    mod = _ada(c, ada_w, ada_b).reshape(depth, bsz, 6, 1, d)
    x2 = x.reshape(t, d)
    for layer in range(depth):
        sh_m, sc_m, g_m, sh_f, sc_f, g_f = (mod[layer, :, n] for n in range(6))
        i = layer // 2
        if layer % 2 == 0:
            uc, q, k, v, r, gl = _ab_in(x2, sc_m, sh_m, ab_w_in[i], gla_gate_w[i], gla_gate_b[i], tiles_per_b)
            y_a = _conv(uc.reshape(bsz, s, -1), conv_w[i], conv_b[i], conv_ln_g[i], conv_ln_b[i])
            b3 = lambda a: a.reshape(bsz, s, -1)
            y_b = _gla(b3(q), b3(k), b3(v), b3(gl), b3(r), gla_norm_g[i])
            w_out = ab_w_out[i].astype(BF16)
            cc = y_a.shape[-1]
            acts = [y_a.reshape(t, cc), y_b.reshape(t, -1)]
            weights = [w_out[:cc], w_out[cc:]]
        else:
            qc, kc, vv = _mla_in(x2, sc_m, sh_m, positions.reshape(t, 1), mla_w_in[i], mla_q_norm_g[i],
                                 mla_kv_norm_g[i], mla_w_uq[i], mla_w_ukv[i], tiles_per_b)
            acts = [_attn(qc, kc, vv, bsz, s)]
            weights = [mla_w_out[i].astype(BF16)]
        x2 = _proj_ln(acts, weights, x2, g_m, ln_mix_g[layer], ln_mix_b[layer], tiles_per_b, alpha)
        x2 = _moe_block(x2, sc_f, sh_f, g_f, ln_ffn_g[layer], ln_ffn_b[layer], moe_w_group[layer], moe_b_group[layer],
                        moe_w_router[layer], moe_b_router[layer], moe_w_gate, moe_w_up, moe_w_down,
                        layer, tiles_per_b, alpha)
    return x2.reshape(bsz, s, d)
```

```python
import functools

import jax
import jax.numpy as jnp
from jax import lax
from jax.experimental import pallas as pl
from jax.experimental.pallas import tpu as pltpu

F32 = jnp.float32
BF16 = jnp.bfloat16
I32 = jnp.int32
HIGHEST = lax.Precision.HIGHEST

LN_EPS = 1e-5
RMS_EPS = 1e-6
CONV_WIDTH = 31
GLA_HEADS = 4
GLA_GATE_TAU = 16.0
MLA_HEADS = 8
MLA_NOPE = 128
MLA_ROPE = 64
MLA_V = 128
ROPE_THETA = 10000.0
MOE_GROUPS = 4
MOE_EXPERTS_PER_GROUP = 8
MOE_EXPERTS = MOE_GROUPS * MOE_EXPERTS_PER_GROUP

LANES = 128
SUBLANES = 8
TOKEN_TILE = 512
GLA_CHUNK = 128
GLA_BLOCK = 512
GLA_BATCH_PER_STEP = 2
CONV_ROWS = 32
CONV_HALO = 32
ATTN_TQ = 512
ATTN_TK = 256
ATTN_HEADS_PER_STEP = 4
MOE_ROW_TILE = 512
RUN_PIECE = 64
RUN_ALIGN = 16
VMEM_LIMIT = 48 * 1024 * 1024


def _cparams(sem):
    return pltpu.CompilerParams(dimension_semantics=sem, vmem_limit_bytes=VMEM_LIMIT)


def _sigmoid(x):
    return 1.0 / (1.0 + jnp.exp(-x))


def _dot(a, b):
    return jnp.dot(a, b, preferred_element_type=F32)


def _dot_nt(a, b):
    return lax.dot_general(a, b, (((1,), (1,)), ((), ())), preferred_element_type=F32)


def _dot_tn(a, b):
    return lax.dot_general(a, b, (((0,), (0,)), ((), ())), preferred_element_type=F32)


def _dot_01_f32(a01, x):
    hi = x.astype(BF16)
    r1 = x - hi.astype(F32)
    mid = r1.astype(BF16)
    lo = (r1 - mid.astype(F32)).astype(BF16)
    return _dot(a01, hi) + (_dot(a01, mid) + _dot(a01, lo))


def _layer_norm(y, g, b):
    mu = jnp.mean(y, axis=-1, keepdims=True)
    d = y - mu
    var = jnp.mean(d * d, axis=-1, keepdims=True)
    return d * lax.rsqrt(var + LN_EPS) * g + b


def _ada_kernel(c_ref, w_ref, b_ref, o_ref):
    c = c_ref[...]
    o_ref[0] = jnp.dot(c * _sigmoid(c), w_ref[0], precision=HIGHEST, preferred_element_type=F32) + b_ref[0]


def _ada(c, ada_w, ada_b):
    depth, d, n = ada_w.shape
    bsz = c.shape[0]
    tn = 1536
    return pl.pallas_call(
        _ada_kernel,
        out_shape=jax.ShapeDtypeStruct((depth, bsz, n), F32),
        grid=(depth, n // tn),
        in_specs=[pl.BlockSpec((bsz, d), lambda l, j: (0, 0)),
                  pl.BlockSpec((1, d, tn), lambda l, j: (l, 0, j)),
                  pl.BlockSpec((1, 1, tn), lambda l, j: (l, 0, j))],
        out_specs=pl.BlockSpec((1, bsz, tn), lambda l, j: (l, 0, j)),
        compiler_params=_cparams(("parallel", "parallel")),
        name="ada",
    )(c, ada_w, ada_b.reshape(depth, 1, n))


def _ab_in_kernel(x_ref, sc_ref, sh_ref, wc_ref, wq_ref, wk_ref, wv_ref, wr_ref, wg_ref, gw_ref, gb_ref,
                  uc_ref, q_ref, k_ref, v_ref, r_ref, gl_ref):
    h = (x_ref[...] * (1.0 + sc_ref[0]) + sh_ref[0]).astype(BF16)
    uc_ref[...] = _dot(h, wc_ref[...])
    q_ref[...] = _dot(h, wq_ref[...])
    k_ref[...] = _dot(h, wk_ref[...])
    v_ref[...] = _dot(h, wv_ref[...]).astype(v_ref.dtype)
    r_ref[...] = _dot(h, wr_ref[...])
    g_low = _dot(h, wg_ref[...])
    z = jnp.dot(g_low, gw_ref[...], precision=HIGHEST, preferred_element_type=F32) + gb_ref[...]
    gl_ref[...] = (jnp.minimum(z, 0.0) - jnp.log(1.0 + jnp.exp(-jnp.abs(z)))) * (1.0 / GLA_GATE_TAU)


def _ab_in(x2, sc, sh, w_in, gate_w, gate_b, tiles_per_b):
    t, d = x2.shape
    cc2 = d
    kw = d // 4
    vw = d // 2
    rank = gate_w.shape[0]
    splits = [cc2, cc2 + kw, cc2 + 2 * kw, cc2 + 2 * kw + vw, cc2 + 2 * kw + 2 * vw]
    wb = w_in.astype(BF16)
    ws = [wb[:, :splits[0]], wb[:, splits[0]:splits[1]], wb[:, splits[1]:splits[2]],
          wb[:, splits[2]:splits[3]], wb[:, splits[3]:splits[4]], wb[:, splits[4]:]]
    tm = TOKEN_TILE
    full = lambda a: pl.BlockSpec(a.shape, lambda i: (0,) * a.ndim)
    row = lambda n: pl.BlockSpec((tm, n), lambda i: (i, 0))
    mod = pl.BlockSpec((1, 1, d), lambda i: (i // tiles_per_b, 0, 0))
    gb2 = gate_b.reshape(1, kw)
    widths = [cc2, kw, kw, vw, vw, kw]
    return pl.pallas_call(
        _ab_in_kernel,
        out_shape=[jax.ShapeDtypeStruct((t, n), BF16 if idx == 3 else F32) for idx, n in enumerate(widths)],
        grid=(t // tm,),
        in_specs=[row(d), mod, mod] + [full(w) for w in ws] + [full(gate_w), full(gb2)],
        out_specs=[row(n) for n in widths],
        compiler_params=_cparams(("parallel",)),
        name="ab_in",
    )(x2, sc, sh, *ws, gate_w, gb2)


def _conv_kernel(u_ref, halo_ref, cw_ref, cb_ref, lg_ref, lb_ref, o_ref, hp_ref, *, ts, cc):
    j = pl.program_id(1)

    def glu(u):
        return u[:, :cc] * _sigmoid(u[:, cc:])

    hp_ref[0, 0:CONV_HALO, :] = jnp.where(j > 0, glu(halo_ref[0]), 0.0)
    hp_ref[0, CONV_HALO:CONV_HALO + ts, :] = glu(u_ref[0])
    nrow = CONV_HALO + ts
    for b in range(1, SUBLANES):
        hp_ref[b, 0:nrow - SUBLANES, :] = hp_ref[0, b:nrow - SUBLANES + b, :]
    shift = CONV_HALO - (CONV_WIDTH - 1)
    for rb in range(ts // CONV_ROWS):
        r0 = rb * CONV_ROWS
        acc = jnp.zeros((CONV_ROWS, cc), F32)
        for tap in range(CONV_WIDTH):
            lo = r0 + shift + tap
            base = lo // SUBLANES * SUBLANES
            acc = acc + cw_ref[tap:tap + 1, :] * hp_ref[lo - base, base:base + CONV_ROWS, :]
        y = _layer_norm(acc + cb_ref[...], lg_ref[...], lb_ref[...])
        o_ref[0, r0:r0 + CONV_ROWS, :] = (y * _sigmoid(y)).astype(o_ref.dtype)


def _conv(u3, conv_w, conv_b, ln_g, ln_b):
    bsz, s, cc2 = u3.shape
    cc = cc2 // 2
    ts = TOKEN_TILE
    hb = ts // CONV_HALO
    vec = lambda a: pl.BlockSpec((1, cc), lambda b, j: (0, 0))
    return pl.pallas_call(
        functools.partial(_conv_kernel, ts=ts, cc=cc),
        out_shape=jax.ShapeDtypeStruct((bsz, s, cc), BF16),
        grid=(bsz, s // ts),
        in_specs=[pl.BlockSpec((1, ts, cc2), lambda b, j: (b, j, 0)),
                  pl.BlockSpec((1, CONV_HALO, cc2), lambda b, j: (b, jnp.maximum(j * hb - 1, 0), 0)),
                  pl.BlockSpec((CONV_WIDTH, cc), lambda b, j: (0, 0)),
                  vec(conv_b), vec(ln_g), vec(ln_b)],
        out_specs=pl.BlockSpec((1, ts, cc), lambda b, j: (b, j, 0)),
        scratch_shapes=[pltpu.VMEM((SUBLANES, CONV_HALO + ts, cc), F32)],
        compiler_params=_cparams(("parallel", "parallel")),
        name="conv",
    )(u3, u3, conv_w, conv_b.reshape(1, cc), ln_g.reshape(1, cc), ln_b.reshape(1, cc))


def _gla_kernel(q_ref, k_ref, v_ref, gl_ref, r_ref, ng_ref, o_ref, st_ref, *, nb, nh, dk, dv, gc, nchunks):
    @pl.when(pl.program_id(1) == 0)
    def _():
        st_ref[...] = jnp.zeros_like(st_ref)

    row = lax.broadcasted_iota(I32, (gc, gc), 0)
    col = lax.broadcasted_iota(I32, (gc, gc), 1)
    causal = col <= row
    tri = jnp.where(causal, 1.0, 0.0).astype(BF16)
    scale = dk ** -0.5

    ks = [slice(h * dk, (h + 1) * dk) for h in range(nh)]
    vs = [slice(h * dv, (h + 1) * dv) for h in range(nh)]
    streams = [(bb, h) for bb in range(nb) for h in range(nh)]

    def chunk(c, carry):
        r0 = pl.multiple_of(c * gc, gc)
        rows = pl.ds(r0, gc)
        bs = [_dot_01_f32(tri, gl_ref[bb, rows, :]) for bb in range(nb)]
        q_in, k_in, q_st, k_st, decay, v = [], [], [], [], [], []
        for bb in range(nb):
            b = bs[bb]
            b_last = b[gc - 1:gc, :]
            mid = 0.5 * b_last
            q = q_ref[bb, rows, :] * scale
            k = k_ref[bb, rows, :]
            v.append(v_ref[bb, rows, :].astype(BF16))
            q_in.append((q * jnp.exp(b - mid)).astype(BF16))
            k_in.append((k * jnp.exp(mid - b)).astype(BF16))
            q_st.append((q * jnp.exp(b)).astype(BF16))
            k_st.append((k * jnp.exp(b_last - b)).astype(BF16))
            decay.append(jnp.exp(b_last))
        sts = [st_ref[bb, h] for bb, h in streams]
        scores = [_dot_nt(q_in[bb][:, ks[h]], k_in[bb][:, ks[h]]) for bb, h in streams]
        inter = [_dot_nt(q_st[bb][:, ks[h]], st.astype(BF16)) for (bb, h), st in zip(streams, sts)]
        update = [_dot_tn(v[bb][:, vs[h]], k_st[bb][:, ks[h]]) for bb, h in streams]
        atts = [jnp.where(causal, sc, 0.0).astype(BF16) for sc in scores]
        outs = [_dot(att, v[bb][:, vs[h]]) + it for (bb, h), att, it in zip(streams, atts, inter)]
        for (bb, h), st, up, o in zip(streams, sts, update, outs):
            st_ref[bb, h] = st * decay[bb][:, ks[h]] + up
            r = r_ref[bb, rows, vs[h]]
            o = o * lax.rsqrt(jnp.mean(o * o, axis=-1, keepdims=True) + RMS_EPS) * ng_ref[:, vs[h]]
            o_ref[bb, rows, vs[h]] = (o * (r * _sigmoid(r))).astype(o_ref.dtype)
        return carry

    lax.fori_loop(0, nchunks, chunk, 0)


def _gla(q3, k3, v3, gl3, r3, norm_g):
    bsz, s, kw = q3.shape
    vw = v3.shape[-1]
    nh = GLA_HEADS
    dk, dv = kw // nh, vw // nh
    cb = GLA_BLOCK
    gc = GLA_CHUNK
    nb = GLA_BATCH_PER_STEP
    blk = lambda n: pl.BlockSpec((nb, cb, n), lambda b, j: (b, j, 0))
    return pl.pallas_call(
        functools.partial(_gla_kernel, nb=nb, nh=nh, dk=dk, dv=dv, gc=gc, nchunks=cb // gc),
        out_shape=jax.ShapeDtypeStruct((bsz, s, vw), BF16),
        grid=(bsz // nb, s // cb),
        in_specs=[blk(kw), blk(kw), blk(vw), blk(kw), blk(vw), pl.BlockSpec((1, vw), lambda b, j: (0, 0))],
        out_specs=blk(vw),
        scratch_shapes=[pltpu.VMEM((nb, nh, dv, dk), F32)],
        compiler_params=_cparams(("parallel", "arbitrary")),
        name="gla",
    )(q3, k3, v3, gl3, r3, norm_g.reshape(1, vw))


def _proj_ln_kernel(*refs, n_in, alpha):
    a_refs, w_refs = refs[:n_in], refs[n_in:2 * n_in]
    x_ref, gate_ref, lg_ref, lb_ref, o_ref = refs[2 * n_in:]
    mix = _dot(a_refs[0][...], w_refs[0][...])
    for a_ref, w_ref in zip(a_refs[1:], w_refs[1:]):
        mix = mix + _dot(a_ref[...], w_ref[...])
    y = alpha * x_ref[...] + (1.0 + gate_ref[0]) * mix
    o_ref[...] = _layer_norm(y, lg_ref[...], lb_ref[...])


def _proj_ln(acts, weights, x2, gate, ln_g, ln_b, tiles_per_b, alpha):
    t, d = x2.shape
    tm = TOKEN_TILE
    n_in = len(acts)
    row = lambda n: pl.BlockSpec((tm, n), lambda i: (i, 0))
    full = lambda a: pl.BlockSpec(a.shape, lambda i: (0,) * a.ndim)
    vec = pl.BlockSpec((1, d), lambda i: (0, 0))
    return pl.pallas_call(
        functools.partial(_proj_ln_kernel, n_in=n_in, alpha=alpha),
        out_shape=jax.ShapeDtypeStruct((t, d), F32),
        grid=(t // tm,),
        in_specs=[row(a.shape[1]) for a in acts] + [full(w) for w in weights]
                 + [row(d), pl.BlockSpec((1, 1, d), lambda i: (i // tiles_per_b, 0, 0)), vec, vec],
        out_specs=row(d),
        compiler_params=_cparams(("parallel",)),
        name="proj_ln",
    )(*acts, *weights, x2, gate, ln_g.reshape(1, d), ln_b.reshape(1, d))


def _mla_in_kernel(x_ref, sc_ref, sh_ref, pos_ref, invf_ref, sign_ref, win_ref, gq_ref, gkv_ref,
                   wqa_ref, wqb_ref, wk_ref, wvt_ref, q_ref, k_ref, vt_ref, *, nh, q_lora, kv_lora, scale, tk):
    h = (x_ref[...] * (1.0 + sc_ref[0]) + sh_ref[0]).astype(BF16)
    u = _dot(h, win_ref[...])
    cq = u[:, :q_lora]
    ckv = u[:, q_lora:q_lora + kv_lora]
    kr = u[:, q_lora + kv_lora:q_lora + kv_lora + LANES]
    kr_sw = u[:, q_lora + kv_lora + LANES:]
    cqn = (cq * lax.rsqrt(jnp.mean(cq * cq, axis=-1, keepdims=True) + RMS_EPS) * gq_ref[...]).astype(BF16)
    kvn = (ckv * lax.rsqrt(jnp.mean(ckv * ckv, axis=-1, keepdims=True) + RMS_EPS) * gkv_ref[...]).astype(BF16)
    ang = pos_ref[...].astype(F32) * invf_ref[...]
    cos = jnp.cos(ang)
    sin = jnp.sin(ang) * sign_ref[...]
    kr_rot = (kr * cos + kr_sw * sin).astype(BF16)
    qa = _dot(cqn, wqa_ref[...])
    qb = _dot(cqn, wqb_ref[...])
    kv = _dot(kvn, wk_ref[...])
    hw = 2 * LANES
    for hd in range(nh):
        q_ref[:, hd * hw:hd * hw + LANES] = (qa[:, hd * hw:hd * hw + LANES] * scale).astype(BF16)
        rope = qa[:, hd * hw + LANES:(hd + 1) * hw] * cos + qb[:, hd * LANES:(hd + 1) * LANES] * sin
        q_ref[:, hd * hw + LANES:(hd + 1) * hw] = (rope * scale).astype(BF16)
        k_ref[:, hd * hw:hd * hw + LANES] = kv[:, hd * LANES:(hd + 1) * LANES].astype(BF16)
        k_ref[:, hd * hw + LANES:(hd + 1) * hw] = kr_rot
    vt = _dot_nt(wvt_ref[...], kvn).astype(BF16)
    for c in range(vt.shape[1] // tk):
        vt_ref[0, c] = vt[:, c * tk:(c + 1) * tk]


def _mla_in(x2, sc, sh, pos2, w_in, gq, gkv, w_uq, w_ukv, tiles_per_b):
    t, d = x2.shape
    nh = MLA_HEADS
    q_lora, kv_lora = gq.shape[0], gkv.shape[0]
    half = MLA_ROPE // 2
    pad = LANES - MLA_ROPE
    kr_w = w_in[:, q_lora + kv_lora:]
    kr_sw = jnp.concatenate([kr_w[:, half:], kr_w[:, :half]], axis=1)
    zpad = jnp.zeros((d, pad), w_in.dtype)
    win_ext = jnp.concatenate([w_in[:, :q_lora + kv_lora], kr_w, zpad, kr_sw, zpad], axis=1).astype(BF16)
    wq = w_uq.reshape(q_lora, nh, MLA_NOPE + MLA_ROPE)
    q_nope, q_rope = wq[:, :, :MLA_NOPE], wq[:, :, MLA_NOPE:]
    q_rope_sw = jnp.concatenate([q_rope[:, :, half:], q_rope[:, :, :half]], axis=2)
    zq = jnp.zeros((q_lora, nh, pad), w_uq.dtype)
    wqa = jnp.concatenate([q_nope, q_rope, zq], axis=2).reshape(q_lora, nh * 2 * LANES).astype(BF16)
    wqb = jnp.concatenate([q_rope_sw, zq], axis=2).reshape(q_lora, nh * LANES).astype(BF16)
    wkv = w_ukv.reshape(kv_lora, nh, MLA_NOPE + MLA_V)
    wk = wkv[:, :, :MLA_NOPE].reshape(kv_lora, nh * MLA_NOPE).astype(BF16)
    wvt = wkv[:, :, MLA_NOPE:].reshape(kv_lora, nh * MLA_V).T.astype(BF16)
    inv_freq = 1.0 / (ROPE_THETA ** (jnp.arange(0, MLA_ROPE, 2, dtype=F32) / MLA_ROPE))
    invf = jnp.concatenate([inv_freq, inv_freq, jnp.zeros((pad,), F32)]).reshape(1, LANES)
    sign = jnp.concatenate([-jnp.ones((half,), F32), jnp.ones((half,), F32), jnp.zeros((pad,), F32)]).reshape(1, LANES)
    tm = TOKEN_TILE
    full = lambda a: pl.BlockSpec(a.shape, lambda i: (0,) * a.ndim)
    row = lambda n: pl.BlockSpec((tm, n), lambda i: (i, 0))
    mod = pl.BlockSpec((1, 1, d), lambda i: (i // tiles_per_b, 0, 0))
    gq2, gkv2 = gq.reshape(1, q_lora), gkv.reshape(1, kv_lora)
    scale = (MLA_NOPE + MLA_ROPE) ** -0.5
    tk = ATTN_TK
    kt_per_tile = tm // tk
    s = tiles_per_b * tm
    return pl.pallas_call(
        functools.partial(_mla_in_kernel, nh=nh, q_lora=q_lora, kv_lora=kv_lora, scale=scale, tk=tk),
        out_shape=[jax.ShapeDtypeStruct((t, nh * 2 * LANES), BF16), jax.ShapeDtypeStruct((t, nh * 2 * LANES), BF16),
                   jax.ShapeDtypeStruct((t // s, s // tk, nh * MLA_V, tk), BF16)],
        grid=(t // tm,),
        in_specs=[row(d), mod, mod, row(1), full(invf), full(sign), full(win_ext), full(gq2), full(gkv2),
                  full(wqa), full(wqb), full(wk), full(wvt)],
        out_specs=[row(nh * 2 * LANES), row(nh * 2 * LANES),
                   pl.BlockSpec((1, kt_per_tile, nh * MLA_V, tk),
                                lambda i: (i // tiles_per_b, i % tiles_per_b, 0, 0))],
        compiler_params=_cparams(("parallel",)),
        name="mla_in",
    )(x2, sc, sh, pos2, invf, sign, win_ext, gq2, gkv2, wqa, wqb, wk, wvt)


def _attn_kernel(q_ref, k_ref, vt_ref, o_ref, acc_ref, s0_ref, s1_ref, *, s, tq, tk, hb):
    kpq = tq // tk
    assert kpq == 2, "the pipeline below alternates two score buffers over pairs of key tiles"
    hw = 2 * LANES
    ones = jnp.ones((8, tk), BF16)

    def q_block(qi, carry):
        qrows = pl.ds(pl.multiple_of(qi * tq, tq), tq)
        acc_ref[...] = jnp.zeros_like(acc_ref)

        def scores(j, s_ref, c0=0, nc=tq):
            krows = pl.ds(pl.multiple_of(j * tk, tk), tk)
            cols = pl.ds(pl.multiple_of(qi * tq + c0, tk), nc)
            for h in range(hb):
                s_ref[h, :, c0:c0 + nc] = _dot_nt(k_ref[krows, h * hw:(h + 1) * hw], q_ref[cols, h * hw:(h + 1) * hw])

        def tile(j, s_ref, stats, masked, c0=0, nc=tq):
            def put(full, part):
                pieces = ([full[:, :c0]] if c0 else []) + [part] + ([full[:, c0 + nc:]] if c0 + nc < tq else [])
                return pieces[0] if len(pieces) == 1 else jnp.concatenate(pieces, axis=1)
            ps, alphas, out = [], [], []
            for h in range(hb):
                m, st = stats[2 * h][:, c0:c0 + nc], s_ref[h, :, c0:c0 + nc]
                if masked:
                    key = j * tk + lax.broadcasted_iota(I32, (tk, nc), 0)
                    qry = qi * tq + c0 + lax.broadcasted_iota(I32, (tk, nc), 1)
                    st = jnp.where(key <= qry, st, -jnp.inf)
                m_new = jnp.maximum(m, jnp.max(st, axis=0, keepdims=True))
                ps.append(jnp.exp(st - m_new).astype(BF16))
                alphas.append(jnp.exp(m - m_new))
                out.append(put(stats[2 * h], m_new))
            for h in range(hb):
                acc_ref[h, :, c0:c0 + nc] = (alphas[h] * acc_ref[h, :, c0:c0 + nc]
                                             + _dot(vt_ref[0, j, h * MLA_V:(h + 1) * MLA_V, :], ps[h]))
                l_new = alphas[h] * stats[2 * h + 1][:, c0:c0 + nc] + _dot(ones, ps[h])[0:1]
                out.insert(2 * h + 1, put(stats[2 * h + 1], l_new))
            return tuple(out)

        stats = (jnp.full((1, tq), -jnp.inf, F32), jnp.zeros((1, tq), F32)) * hb
        scores(0, s0_ref)

        def pair(jj, c):
            j = 2 * jj
            scores(j + 1, s1_ref)
            c = tile(j, s0_ref, c, False)
            scores(j + 2, s0_ref)
            return tile(j + 1, s1_ref, c, False)

        stats = lax.fori_loop(0, qi, pair, stats)
        scores(2 * qi + 1, s1_ref, tk, tk)
        stats = tile(2 * qi, s0_ref, stats, True, 0, tk)
        stats = tile(2 * qi, s0_ref, stats, False, tk, tk)
        stats = tile(2 * qi + 1, s1_ref, stats, True, tk, tk)
        for h in range(hb):
            o_ref[qrows, h * MLA_V:(h + 1) * MLA_V] = (acc_ref[h] / stats[2 * h + 1]).T.astype(o_ref.dtype)
        return carry

    lax.fori_loop(0, s // tq, q_block, 0)


def _attn(q, k, vt, bsz, s):
    nh = MLA_HEADS
    tq, tk = ATTN_TQ, ATTN_TK
    hb = ATTN_HEADS_PER_STEP
    hw = 2 * LANES
    return pl.pallas_call(
        functools.partial(_attn_kernel, s=s, tq=tq, tk=tk, hb=hb),
        out_shape=jax.ShapeDtypeStruct((bsz * s, nh * MLA_V), BF16),
        grid=(bsz, nh // hb),
        in_specs=[pl.BlockSpec((s, hb * hw), lambda b, h: (b, h)),
                  pl.BlockSpec((s, hb * hw), lambda b, h: (b, h)),
                  pl.BlockSpec((1, s // tk, hb * MLA_V, tk), lambda b, h: (b, 0, h, 0))],
        out_specs=pl.BlockSpec((s, hb * MLA_V), lambda b, h: (b, h)),
        scratch_shapes=[pltpu.VMEM((hb, MLA_V, tq), F32), pltpu.VMEM((hb, tk, tq), F32),
                        pltpu.VMEM((hb, tk, tq), F32)],
        compiler_params=_cparams(("parallel", "parallel")),
        name="attn",
    )(q, k, vt)


def _route_kernel(x_ref, sc_ref, sh_ref, w_ref, b_ref, lp_ref, wt_ref, tm_ref, offs_ref, te_ref, na_ref,
                  upper_ref, carry_ref, *, tm, ne, ng, row_tile, nt_pad):
    i = pl.program_id(0)
    epg = ne // ng

    @pl.when(i == 0)
    def _():
        r = lax.broadcasted_iota(I32, (tm, tm), 0)
        c = lax.broadcasted_iota(I32, (tm, tm), 1)
        upper_ref[...] = jnp.where(r < c, 1.0, 0.0).astype(BF16)
        carry_ref[...] = jnp.zeros_like(carry_ref)

    h = x_ref[...] * (1.0 + sc_ref[0]) + sh_ref[0]
    logits = jnp.dot(h, w_ref[...], precision=HIGHEST, preferred_element_type=F32) + b_ref[...]
    lt = logits.T
    lr = lt[0:ne]
    grow = lax.broadcasted_iota(I32, (8, tm), 0).astype(F32)
    lg = jnp.where(grow < ng, lt[ne:ne + 8], -jnp.inf)
    gmax = jnp.max(lg, axis=0, keepdims=True)
    g_idx = jnp.min(jnp.where(lg == gmax, grow, 1e9), axis=0, keepdims=True)
    g_w = 1.0 / jnp.sum(jnp.exp(lg - gmax), axis=0, keepdims=True)
    erow = lax.broadcasted_iota(I32, (ne, tm), 0).astype(F32)
    in_group = jnp.floor(erow * (1.0 / epg)) == g_idx
    sel = jnp.where(in_group, lr, -jnp.inf)
    v1 = jnp.max(sel, axis=0, keepdims=True)
    i1 = jnp.min(jnp.where(sel == v1, erow, 1e9), axis=0, keepdims=True)
    sel2 = jnp.where(erow == i1, -jnp.inf, sel)
    v2 = jnp.max(sel2, axis=0, keepdims=True)
    i2 = jnp.min(jnp.where(sel2 == v2, erow, 1e9), axis=0, keepdims=True)
    t = jnp.exp(v2 - v1)
    w1 = g_w / (1.0 + t)
    w2 = g_w * t / (1.0 + t)
    oh1 = erow == i1
    oh2 = erow == i2
    member = jnp.where(oh1 | oh2, 1.0, 0.0)
    lcnt = jnp.sum(member, axis=1, keepdims=True)
    lcnt = jnp.floor((lcnt + (RUN_ALIGN - 1)) * (1.0 / RUN_ALIGN)) * RUN_ALIGN
    er_ = lax.broadcasted_iota(I32, (ne, ne), 0)
    ec_ = lax.broadcasted_iota(I32, (ne, ne), 1)
    lstart = jnp.dot(jnp.where(ec_ < er_, 1.0, 0.0).astype(F32), jnp.broadcast_to(lcnt, (ne, LANES)),
                     precision=HIGHEST, preferred_element_type=F32)[:, 0:1]
    lrank = _dot(member.astype(BF16), upper_ref[...]) + lstart
    p1 = jnp.sum(jnp.where(oh1, lrank, 0.0), axis=0, keepdims=True)
    p2 = jnp.sum(jnp.where(oh2, lrank, 0.0), axis=0, keepdims=True)

    orow = lax.broadcasted_iota(I32, (8, tm), 0)
    lp_ref[...] = jnp.where(orow == 0, p1, jnp.where(orow == 1, p2, 0.0)).astype(I32)
    wrow = lax.broadcasted_iota(I32, (LANES, tm), 0)
    wt_ref[...] = jnp.where(wrow == 0, w1, jnp.where(wrow == 1, w2,
                                                     jnp.where(wrow == 2, p1, jnp.where(wrow == 3, p2, 0.0)))).T
    mr = lax.broadcasted_iota(I32, (ne, LANES), 0)
    mc = lax.broadcasted_iota(I32, (ne, LANES), 1)
    to_row = lambda col, lane0: jnp.sum(jnp.where(mr + lane0 == mc, col, 0.0), axis=0, keepdims=True)
    total = jnp.sum(lcnt, axis=0, keepdims=True)
    lane = lax.broadcasted_iota(I32, (1, LANES), 1)
    packed = (to_row(lstart, 0) + to_row(lcnt, ne) + to_row(carry_ref[...], 2 * ne)
              + jnp.where(lane == 3 * ne, total, 0.0))
    trow = lax.broadcasted_iota(I32, (8, LANES), 0)
    tm_ref[...] = jnp.where(trow == 0, packed, 0.0).astype(I32)
    carry_ref[...] = carry_ref[...] + lcnt

    @pl.when(i == pl.num_programs(0) - 1)
    def _():
        cnt = carry_ref[...]
        ntl = jnp.floor((cnt + (row_tile - 1)) * (1.0 / row_tile))
        incl = jnp.where(ec_ <= er_, 1.0, 0.0).astype(F32)
        ends = jnp.dot(incl, jnp.broadcast_to(ntl, (ne, LANES)), precision=HIGHEST,
                       preferred_element_type=F32)
        starts = ends - ntl
        offs_ref[...] = jnp.concatenate([starts * row_tile, ends * row_tile, jnp.broadcast_to(ntl, (ne, LANES)),
                                         jnp.zeros((8, LANES), F32)], axis=0).astype(I32)
        tile = lax.broadcasted_iota(I32, (ne, nt_pad), 1).astype(F32)
        te = jnp.sum(jnp.where(ends[:, 0:1] <= tile, 1.0, 0.0), axis=0, keepdims=True)
        te_ref[...] = jnp.broadcast_to(jnp.minimum(te, ne - 1.0), (8, nt_pad)).astype(I32)
        na_ref[...] = jnp.broadcast_to(ends[ne - 1:ne, :], (8, LANES)).astype(I32)


def _moe_tiles(t):
    rows = 2 * t + (RUN_ALIGN - 1) * MOE_EXPERTS * (t // TOKEN_TILE)
    nt_max = -(-rows // MOE_ROW_TILE) + MOE_EXPERTS
    nt_pad = -(-nt_max // LANES) * LANES
    return nt_max, nt_pad


def _sorted_rows(tm):
    return -(-(2 * tm + (RUN_ALIGN - 1) * MOE_EXPERTS) // LANES) * LANES


def _route(x2, sc, sh, w_group, b_group, w_router, b_router, tiles_per_b):
    t, d = x2.shape
    ne, ng = MOE_EXPERTS, MOE_GROUPS
    tm = TOKEN_TILE
    _, nt_pad = _moe_tiles(t)
    wcat = jnp.concatenate([w_router, w_group, jnp.zeros((d, LANES - ne - ng), F32)], axis=1)
    bcat = jnp.concatenate([b_router, b_group, jnp.zeros((LANES - ne - ng,), F32)]).reshape(1, LANES)
    const = lambda shp: pl.BlockSpec(shp, lambda i: (0,) * len(shp))
    return pl.pallas_call(
        functools.partial(_route_kernel, tm=tm, ne=ne, ng=ng, row_tile=MOE_ROW_TILE, nt_pad=nt_pad),
        out_shape=[jax.ShapeDtypeStruct((8, t), I32), jax.ShapeDtypeStruct((t, LANES), F32),
                   jax.ShapeDtypeStruct((8 * (t // tm), LANES), I32),
                   jax.ShapeDtypeStruct((3 * ne + 8, LANES), I32), jax.ShapeDtypeStruct((8, nt_pad), I32),
                   jax.ShapeDtypeStruct((8, LANES), I32)],
        grid=(t // tm,),
        in_specs=[pl.BlockSpec((tm, d), lambda i: (i, 0)),
                  pl.BlockSpec((1, 1, d), lambda i: (i // tiles_per_b, 0, 0)),
                  pl.BlockSpec((1, 1, d), lambda i: (i // tiles_per_b, 0, 0)),
                  const((d, LANES)), const((1, LANES))],
        out_specs=[pl.BlockSpec((8, tm), lambda i: (0, i)), pl.BlockSpec((tm, LANES), lambda i: (i, 0)),
                   pl.BlockSpec((8, LANES), lambda i: (i, 0)),
                   const((3 * ne + 8, LANES)), const((8, nt_pad)), const((8, LANES))],
        scratch_shapes=[pltpu.VMEM((tm, tm), BF16), pltpu.VMEM((ne, 1), F32)],
        compiler_params=_cparams(("arbitrary",)),
        name="moe_route",
    )(x2, sc, sh, wcat, bcat)


def _start_runs(tm_ref, tile, ne, copy):
    for e in range(ne):
        lstart = tm_ref[tile, e]
        n = tm_ref[tile, ne + e]
        before = tm_ref[tile, 2 * ne + e]
        _binary_pieces(n, TOKEN_TILE, lambda off, rows: copy(
            e, pl.multiple_of(lstart + off, RUN_ALIGN), pl.multiple_of(before + off, RUN_ALIGN), rows))


def _binary_pieces(n, n_max, copy):
    del n_max
    nbig = n >> (RUN_PIECE.bit_length() - 1)

    def big_piece(c, carry):
        copy(c * RUN_PIECE, RUN_PIECE).start()
        return carry

    lax.fori_loop(0, nbig, big_piece, 0)
    off = nbig * RUN_PIECE
    p = RUN_PIECE // 2
    while p >= RUN_ALIGN:
        @pl.when((n & p) != 0)
        def _():
            copy(off, p).start()
        off = off + (n & p)
        p //= 2


def _dispatch_kernel(offs_ref, gend_ref, ntl_ref, tm_ref, x_ref, sc_ref, sh_ref, lp_ref, wt_ref, xs_ref,
                     h_ref, z_ref, sem_z, sem_r, *, tm, ne, row_tile, nt_max, ntile, srows):
    i = pl.program_id(0)
    tile = i - 1

    @pl.when(i == 0)
    def _():
        z_ref[...] = jnp.zeros_like(z_ref)
        for e in range(ne):
            @pl.when(ntl_ref[e] > 0)
            def _():
                start = pl.multiple_of(gend_ref[e] - row_tile, row_tile)
                cp = pltpu.make_async_copy(z_ref, xs_ref.at[pl.ds(start, row_tile), :], sem_z)
                cp.start()
                cp.wait()
        for back in range(1, nt_max - (2 * tm * ntile) // row_tile + 1):
            @pl.when(nt_max - back >= gend_ref[ne - 1] // row_tile)
            def _():
                cp = pltpu.make_async_copy(z_ref, xs_ref.at[pl.ds((nt_max - back) * row_tile, row_tile), :], sem_z)
                cp.start()
                cp.wait()
        for spill in range(2):
            cp = pltpu.make_async_copy(z_ref, xs_ref.at[pl.ds((nt_max + spill) * row_tile, row_tile), :], sem_z)
            cp.start()
            cp.wait()

    def wait_tile(t):
        slot = t % 2
        pltpu.make_async_copy(h_ref.at[slot], xs_ref.at[pl.ds(0, srows), :], sem_r.at[slot]).wait()

    @pl.when((tile >= 2) & (tile <= ntile))
    def _():
        wait_tile(tile - 2)

    @pl.when(tile == ntile)
    def _():
        wait_tile(tile - 1)

    @pl.when((tile >= 0) & (tile < ntile))
    def _():
        slot = tile % 2
        h = (x_ref[...] * (1.0 + sc_ref[0]) + sh_ref[0]).astype(BF16)
        row = lax.broadcasted_iota(I32, (srows, tm), 0)
        lp = lp_ref[...]
        oh1 = jnp.where(row == lp[0:1, :], 1.0, 0.0).astype(BF16)
        oh2 = jnp.where(row == lp[1:2, :], 1.0, 0.0).astype(BF16)
        d = h.shape[1]
        h_ref[slot, :, :d] = _dot(oh1 + oh2, h).astype(BF16)
        lane = lax.broadcasted_iota(I32, (tm, LANES), 1)

        def weight_terms(w):
            hi = w.astype(BF16).astype(F32)
            mid = (w - hi).astype(BF16).astype(F32)
            lo = w - hi - mid
            return jnp.where(lane == 0, hi, jnp.where(lane == 1, mid, jnp.where(lane == 2, lo, 0.0))).astype(BF16)

        wt = wt_ref[...]
        h_ref[slot, :, d:] = (_dot(oh1, weight_terms(wt[:, 0:1])) + _dot(oh2, weight_terms(wt[:, 1:2]))).astype(BF16)

        def copy(e, local_row, rows_before, rows):
            dst = pl.multiple_of(offs_ref[e] + rows_before, RUN_ALIGN)
            return pltpu.make_async_copy(h_ref.at[slot, pl.ds(local_row, rows), :], xs_ref.at[pl.ds(dst, rows), :],
                                         sem_r.at[slot])

        _start_runs(tm_ref, tile, ne, copy)
        used = tm_ref[tile, 3 * ne]
        _binary_pieces(srows - used, srows - 2 * tm, lambda off, rows: pltpu.make_async_copy(
            h_ref.at[slot, pl.ds(pl.multiple_of(used + off, RUN_ALIGN), rows), :],
            xs_ref.at[pl.ds(pl.multiple_of((nt_max + slot) * row_tile + off, RUN_ALIGN), rows), :], sem_r.at[slot]))


def _dispatch(x2, sc, sh, lp, wt, tmeta, offs, gend, ntl, tiles_per_b):
    t, d = x2.shape
    tm = TOKEN_TILE
    nt_max, _ = _moe_tiles(t)
    ntile = t // tm
    srows = _sorted_rows(tm)
    cur = lambda i: jnp.clip(i - 1, 0, ntile - 1)
    grid_spec = pltpu.PrefetchScalarGridSpec(
        num_scalar_prefetch=4,
        grid=(ntile + 2,),
        in_specs=[pl.BlockSpec((tm, d), lambda i, *_: (cur(i), 0)),
                  pl.BlockSpec((1, 1, d), lambda i, *_: (cur(i) // tiles_per_b, 0, 0)),
                  pl.BlockSpec((1, 1, d), lambda i, *_: (cur(i) // tiles_per_b, 0, 0)),
                  pl.BlockSpec((8, tm), lambda i, *_: (0, cur(i))),
                  pl.BlockSpec((tm, LANES), lambda i, *_: (cur(i), 0))],
        out_specs=pl.BlockSpec(memory_space=pl.ANY),
        scratch_shapes=[pltpu.VMEM((2, srows, d + LANES), BF16), pltpu.VMEM((MOE_ROW_TILE, d + LANES), BF16),
                        pltpu.SemaphoreType.DMA, pltpu.SemaphoreType.DMA((2,))])
    return pl.pallas_call(
        functools.partial(_dispatch_kernel, tm=tm, ne=MOE_EXPERTS, row_tile=MOE_ROW_TILE, nt_max=nt_max,
                          ntile=ntile, srows=srows),
        out_shape=jax.ShapeDtypeStruct(((nt_max + 2) * MOE_ROW_TILE, d + LANES), BF16),
        grid_spec=grid_spec,
        compiler_params=_cparams(("arbitrary",)),
        name="moe_dispatch",
    )(offs, gend, ntl, tmeta, x2, sc, sh, lp, wt)


def _gmm_kernel(te_ref, na_ref, xs_ref, wg_ref, wu_ref, wd_ref, ys_ref, wgu_buf, wd_buf, *, ff, tr):
    j = pl.program_id(0)

    @pl.when(j < na_ref[0])
    def _():
        changed = (j == 0) | (te_ref[j] != te_ref[jnp.maximum(j - 1, 0)])

        @pl.when(changed)
        def _():
            wgu_buf[:, :ff] = wg_ref[0, 0].astype(BF16)
            wgu_buf[:, ff:] = wu_ref[0, 0].astype(BF16)
            wd_buf[...] = wd_ref[0, 0].astype(BF16)

        halves = [slice(c * (tr // 2), (c + 1) * (tr // 2)) for c in range(2)]
        d = wgu_buf.shape[0]
        gus = [_dot(xs_ref[rows, :d], wgu_buf[...]) for rows in halves]
        hids = [(gu[:, :ff] * _sigmoid(gu[:, :ff]) * gu[:, ff:]).astype(BF16) for gu in gus]
        for rows, hid in zip(halves, hids):
            wl = xs_ref[rows, d:].astype(F32)
            w = wl[:, 0:1] + wl[:, 1:2] + wl[:, 2:3]
            ys_ref[rows, :] = (w * _dot(hid, wd_buf[...])).astype(ys_ref.dtype)

    @pl.when(j >= na_ref[0])
    def _():
        ys_ref[...] = jnp.zeros_like(ys_ref)


def _gmm(xs, te, na, w_gate, w_up, w_down, layer):
    d, ff = w_gate.shape[-2:]
    tr = MOE_ROW_TILE
    ns = te.shape[0] * tr
    act = lambda j, te_ref, na_ref: jnp.minimum(j, na_ref[0] - 1)
    grid_spec = pltpu.PrefetchScalarGridSpec(
        num_scalar_prefetch=2,
        grid=(ns // tr,),
        in_specs=[pl.BlockSpec((tr, d + LANES), lambda j, te_ref, na_ref: (act(j, te_ref, na_ref), 0)),
                  pl.BlockSpec((1, 1, d, ff), lambda j, te_ref, na_ref: (layer, te_ref[act(j, te_ref, na_ref)], 0, 0)),
                  pl.BlockSpec((1, 1, d, ff), lambda j, te_ref, na_ref: (layer, te_ref[act(j, te_ref, na_ref)], 0, 0)),
                  pl.BlockSpec((1, 1, ff, d), lambda j, te_ref, na_ref: (layer, te_ref[act(j, te_ref, na_ref)], 0, 0))],
        out_specs=pl.BlockSpec((tr, d), lambda j, te_ref, na_ref: (j, 0)),
        scratch_shapes=[pltpu.VMEM((d, 2 * ff), BF16), pltpu.VMEM((ff, d), BF16)])
    return pl.pallas_call(
        functools.partial(_gmm_kernel, ff=ff, tr=tr),
        out_shape=jax.ShapeDtypeStruct((ns, d), BF16),
        grid_spec=grid_spec,
        compiler_params=_cparams(("arbitrary",)),
        name="moe_gmm",
    )(te, na, xs, w_gate, w_up, w_down)


def _combine_kernel(offs_ref, tm_ref, wt_ref, x_ref, gate_ref, lg_ref, lb_ref, ys_ref, o_ref,
                    buf_ref, sem_r, *, tm, ne, ntile, srows, alpha):
    i = pl.program_id(0)

    def fetch(tile):
        slot = tile % 2

        def copy(e, local_row, rows_before, rows):
            src = pl.multiple_of(offs_ref[e] + rows_before, RUN_ALIGN)
            return pltpu.make_async_copy(ys_ref.at[pl.ds(src, rows), :], buf_ref.at[slot, pl.ds(local_row, rows), :],
                                         sem_r.at[slot])

        _start_runs(tm_ref, tile, ne, copy)
        used = tm_ref[tile, 3 * ne]
        _binary_pieces(srows - used, srows - 2 * tm, lambda off, rows: pltpu.make_async_copy(
            ys_ref.at[pl.ds(pl.multiple_of(off, RUN_ALIGN), rows), :],
            buf_ref.at[slot, pl.ds(pl.multiple_of(used + off, RUN_ALIGN), rows), :], sem_r.at[slot]))

    def wait(tile):
        slot = tile % 2
        pltpu.make_async_copy(ys_ref.at[pl.ds(0, srows), :], buf_ref.at[slot], sem_r.at[slot]).wait()

    def finish(tile):
        w = wt_ref[...]
        rows = buf_ref[tile % 2]
        col = lax.broadcasted_iota(I32, (tm, srows), 1).astype(F32)
        both = jnp.where(col == w[:, 2:3], 1.0, jnp.where(col == w[:, 3:4], 1.0, 0.0)).astype(BF16)
        ffn = _dot(both, rows)
        y = alpha * x_ref[...] + (1.0 + gate_ref[0]) * ffn
        o_ref[...] = _layer_norm(y, lg_ref[...], lb_ref[...])

    @pl.when(i == 0)
    def _():
        fetch(i)

    @pl.when((i > 0) & (i < ntile))
    def _():
        wait(i - 1)
        fetch(i)
        finish(i - 1)

    @pl.when(i == ntile)
    def _():
        wait(i - 1)
        finish(i - 1)


def _combine(ys, tmeta, wt, x2, gate, ln_g, ln_b, offs, tiles_per_b, alpha):
    t, d = x2.shape
    tm = TOKEN_TILE
    ntile = t // tm
    srows = _sorted_rows(tm)
    prev = lambda i: jnp.maximum(i - 1, 0)
    vec = pl.BlockSpec((1, d), lambda i, *_: (0, 0))
    grid_spec = pltpu.PrefetchScalarGridSpec(
        num_scalar_prefetch=2,
        grid=(ntile + 1,),
        in_specs=[pl.BlockSpec((tm, LANES), lambda i, *_: (prev(i), 0)),
                  pl.BlockSpec((tm, d), lambda i, *_: (prev(i), 0)),
                  pl.BlockSpec((1, 1, d), lambda i, *_: (prev(i) // tiles_per_b, 0, 0)),
                  vec, vec,
                  pl.BlockSpec(memory_space=pl.ANY)],
        out_specs=pl.BlockSpec((tm, d), lambda i, *_: (prev(i), 0)),
        scratch_shapes=[pltpu.VMEM((2, srows, d), BF16), pltpu.SemaphoreType.DMA((2,))])
    return pl.pallas_call(
        functools.partial(_combine_kernel, tm=tm, ne=MOE_EXPERTS, ntile=ntile, srows=srows, alpha=alpha),
        out_shape=jax.ShapeDtypeStruct((t, d), F32),
        grid_spec=grid_spec,
        compiler_params=_cparams(("arbitrary",)),
        name="moe_combine",
    )(offs, tmeta, wt, x2, gate, ln_g.reshape(1, d), ln_b.reshape(1, d), ys)


def _moe_block(x2, sc, sh, gate, ln_g, ln_b, w_group, b_group, w_router, b_router, w_gate, w_up, w_down,
               layer, tiles_per_b, alpha):
    ne = MOE_EXPERTS
    nt_max, _ = _moe_tiles(x2.shape[0])
    lp, wt, tmeta, meta, te, na = _route(x2, sc, sh, w_group, b_group, w_router, b_router, tiles_per_b)
    offs, gend, ntl = meta[:ne, 0], meta[ne:2 * ne, 0], meta[2 * ne:3 * ne, 0]
    tmeta = tmeta.reshape(-1, SUBLANES, LANES)[:, 0, :]
    xs = _dispatch(x2, sc, sh, lp, wt, tmeta, offs, gend, ntl, tiles_per_b)
    ys = _gmm(xs, te[0, :nt_max], na[0, :1], w_gate, w_up, w_down, layer)
    return _combine(ys, tmeta, wt, x2, gate, ln_g, ln_b, offs, tiles_per_b, alpha)


def kernel(x, c, positions, ada_w, ada_b, ln_mix_g, ln_mix_b, ln_ffn_g, ln_ffn_b, ab_w_in, conv_w, conv_b, conv_ln_g, conv_ln_b, gla_gate_w, gla_gate_b, gla_norm_g, ab_w_out, mla_w_in, mla_q_norm_g, mla_kv_norm_g, mla_w_uq, mla_w_ukv, mla_w_out, moe_w_group, moe_b_group, moe_w_router, moe_b_router, moe_w_gate, moe_w_up, moe_w_down):
    bsz, s, d = x.shape
    depth = ada_w.shape[0]
    t = bsz * s
    tiles_per_b = s // TOKEN_TILE
    alpha = (2 * depth) ** 0.25
    mod = _ada(c, ada_w, ada_b).reshape(depth, bsz, 6, 1, d)
    x2 = x.reshape(t, d)
    for layer in range(depth):
        sh_m, sc_m, g_m, sh_f, sc_f, g_f = (mod[layer, :, n] for n in range(6))
        i = layer // 2
        if layer % 2 == 0:
            uc, q, k, v, r, gl = _ab_in(x2, sc_m, sh_m, ab_w_in[i], gla_gate_w[i], gla_gate_b[i], tiles_per_b)
            y_a = _conv(uc.reshape(bsz, s, -1), conv_w[i], conv_b[i], conv_ln_g[i], conv_ln_b[i])
            b3 = lambda a: a.reshape(bsz, s, -1)
            y_b = _gla(b3(q), b3(k), b3(v), b3(gl), b3(r), gla_norm_g[i])
            w_out = ab_w_out[i].astype(BF16)
            cc = y_a.shape[-1]
            acts = [y_a.reshape(t, cc), y_b.reshape(t, -1)]
            weights = [w_out[:cc], w_out[cc:]]
        else:
            qc, kc, vv = _mla_in(x2, sc_m, sh_m, positions.reshape(t, 1), mla_w_in[i], mla_q_norm_g[i],
                                 mla_kv_norm_g[i], mla_w_uq[i], mla_w_ukv[i], tiles_per_b)
            acts = [_attn(qc, kc, vv, bsz, s)]
            weights = [mla_w_out[i].astype(BF16)]
        x2 = _proj_ln(acts, weights, x2, g_m, ln_mix_g[layer], ln_mix_b[layer], tiles_per_b, alpha)
        x2 = _moe_block(x2, sc_f, sh_f, g_f, ln_ffn_g[layer], ln_ffn_b[layer], moe_w_group[layer], moe_b_group[layer],
                        moe_w_router[layer], moe_b_router[layer], moe_w_gate, moe_w_up, moe_w_down,
                        layer, tiles_per_b, alpha)
    return x2.reshape(bsz, s, d)
```

```python
import functools

import jax
import jax.numpy as jnp
from jax import lax
from jax.experimental import pallas as pl
from jax.experimental.pallas import tpu as pltpu

F32 = jnp.float32
BF16 = jnp.bfloat16
I32 = jnp.int32
HIGHEST = lax.Precision.HIGHEST

LN_EPS = 1e-5
RMS_EPS = 1e-6
CONV_WIDTH = 31
GLA_HEADS = 4
GLA_GATE_TAU = 16.0
MLA_HEADS = 8
MLA_NOPE = 128
MLA_ROPE = 64
MLA_V = 128
ROPE_THETA = 10000.0
MOE_GROUPS = 4
MOE_EXPERTS_PER_GROUP = 8
MOE_EXPERTS = MOE_GROUPS * MOE_EXPERTS_PER_GROUP

LANES = 128
SUBLANES = 8
TOKEN_TILE = 512
GLA_CHUNK = 128
GLA_BLOCK = 512
GLA_BATCH_PER_STEP = 2
CONV_ROWS = 32
CONV_HALO = 32
ATTN_TQ = 512
ATTN_TK = 256
ATTN_HEADS_PER_STEP = 4
MOE_ROW_TILE = 512
RUN_PIECE = 64
RUN_ALIGN = 16
VMEM_LIMIT = 48 * 1024 * 1024


def _cparams(sem):
    return pltpu.CompilerParams(dimension_semantics=sem, vmem_limit_bytes=VMEM_LIMIT)


def _sigmoid(x):
    return 1.0 / (1.0 + jnp.exp(-x))


def _dot(a, b):
    return jnp.dot(a, b, preferred_element_type=F32)


def _dot_nt(a, b):
    return lax.dot_general(a, b, (((1,), (1,)), ((), ())), preferred_element_type=F32)


def _dot_tn(a, b):
    return lax.dot_general(a, b, (((0,), (0,)), ((), ())), preferred_element_type=F32)


def _split3(x):
    hi = x.astype(BF16)
    r1 = x - hi.astype(F32)
    mid = r1.astype(BF16)
    lo = (r1 - mid.astype(F32)).astype(BF16)
    return hi, mid, lo


def _dot_01_f32(a01, x):
    hi, mid, lo = _split3(x)
    return _dot(a01, hi) + (_dot(a01, mid) + _dot(a01, lo))


def _layer_norm(y, g, b):
    mu = jnp.mean(y, axis=-1, keepdims=True)
    d = y - mu
    var = jnp.mean(d * d, axis=-1, keepdims=True)
    return d * lax.rsqrt(var + LN_EPS) * g + b


def _ada_kernel(c_ref, w_ref, b_ref, o_ref):
    c = c_ref[...]
    o_ref[0] = jnp.dot(c * _sigmoid(c), w_ref[0], precision=HIGHEST, preferred_element_type=F32) + b_ref[0]


def _ada(c, ada_w, ada_b):
    depth, d, n = ada_w.shape
    bsz = c.shape[0]
    tn = 1536
    return pl.pallas_call(
        _ada_kernel,
        out_shape=jax.ShapeDtypeStruct((depth, bsz, n), F32),
        grid=(depth, n // tn),
        in_specs=[pl.BlockSpec((bsz, d), lambda l, j: (0, 0)),
                  pl.BlockSpec((1, d, tn), lambda l, j: (l, 0, j)),
                  pl.BlockSpec((1, 1, tn), lambda l, j: (l, 0, j))],
        out_specs=pl.BlockSpec((1, bsz, tn), lambda l, j: (l, 0, j)),
        compiler_params=_cparams(("parallel", "parallel")),
        name="ada",
    )(c, ada_w, ada_b.reshape(depth, 1, n))


def _ab_in_kernel(x_ref, sc_ref, sh_ref, wc_ref, wq_ref, wk_ref, wv_ref, wr_ref, wg_ref, gw_ref, gb_ref,
                  uc_ref, q_ref, k_ref, v_ref, r_ref, gl_ref):
    h = (x_ref[...] * (1.0 + sc_ref[0]) + sh_ref[0]).astype(BF16)
    uc_ref[...] = _dot(h, wc_ref[...]).astype(uc_ref.dtype)
    q_ref[...] = _dot(h, wq_ref[...])
    k_ref[...] = _dot(h, wk_ref[...])
    v_ref[...] = _dot(h, wv_ref[...]).astype(v_ref.dtype)
    r_ref[...] = _dot(h, wr_ref[...]).astype(r_ref.dtype)
    g_low = _dot(h, wg_ref[...])
    z = jnp.dot(g_low, gw_ref[...], precision=HIGHEST, preferred_element_type=F32) + gb_ref[...]
    gl_ref[...] = (jnp.minimum(z, 0.0) - jnp.log(1.0 + jnp.exp(-jnp.abs(z)))) * (1.0 / GLA_GATE_TAU)


def _ab_in(x2, sc, sh, w_in, gate_w, gate_b, tiles_per_b):
    t, d = x2.shape
    cc2 = d
    kw = d // 4
    vw = d // 2
    rank = gate_w.shape[0]
    splits = [cc2, cc2 + kw, cc2 + 2 * kw, cc2 + 2 * kw + vw, cc2 + 2 * kw + 2 * vw]
    wb = w_in.astype(BF16)
    ws = [wb[:, :splits[0]], wb[:, splits[0]:splits[1]], wb[:, splits[1]:splits[2]],
          wb[:, splits[2]:splits[3]], wb[:, splits[3]:splits[4]], wb[:, splits[4]:]]
    tm = TOKEN_TILE
    full = lambda a: pl.BlockSpec(a.shape, lambda i: (0,) * a.ndim)
    row = lambda n: pl.BlockSpec((tm, n), lambda i: (i, 0))
    mod = pl.BlockSpec((1, 1, d), lambda i: (i // tiles_per_b, 0, 0))
    gb2 = gate_b.reshape(1, kw)
    widths = [cc2, kw, kw, vw, vw, kw]
    return pl.pallas_call(
        _ab_in_kernel,
        out_shape=[jax.ShapeDtypeStruct((t, n), BF16 if idx in (0, 3, 4) else F32) for idx, n in enumerate(widths)],
        grid=(t // tm,),
        in_specs=[row(d), mod, mod] + [full(w) for w in ws] + [full(gate_w), full(gb2)],
        out_specs=[row(n) for n in widths],
        compiler_params=_cparams(("parallel",)),
        name="ab_in",
    )(x2, sc, sh, *ws, gate_w, gb2)


def _conv_kernel(u_ref, halo_ref, cw_ref, cb_ref, lg_ref, lb_ref, o_ref, hp_ref, *, ts, cc):
    j = pl.program_id(1)

    def glu(u):
        u = u.astype(F32)
        return u[:, :cc] * _sigmoid(u[:, cc:])

    hp_ref[0, 0:CONV_HALO, :] = jnp.where(j > 0, glu(halo_ref[0]), 0.0)
    hp_ref[0, CONV_HALO:CONV_HALO + ts, :] = glu(u_ref[0])
    nrow = CONV_HALO + ts
    for b in range(1, SUBLANES):
        hp_ref[b, 0:nrow - SUBLANES, :] = hp_ref[0, b:nrow - SUBLANES + b, :]
    shift = CONV_HALO - (CONV_WIDTH - 1)
    for rb in range(ts // CONV_ROWS):
        r0 = rb * CONV_ROWS
        acc = jnp.zeros((CONV_ROWS, cc), F32)
        for tap in range(CONV_WIDTH):
            lo = r0 + shift + tap
            base = lo // SUBLANES * SUBLANES
            acc = acc + cw_ref[tap:tap + 1, :] * hp_ref[lo - base, base:base + CONV_ROWS, :]
        y = _layer_norm(acc + cb_ref[...], lg_ref[...], lb_ref[...])
        o_ref[0, r0:r0 + CONV_ROWS, :] = (y * _sigmoid(y)).astype(o_ref.dtype)


def _conv(u3, conv_w, conv_b, ln_g, ln_b):
    bsz, s, cc2 = u3.shape
    cc = cc2 // 2
    ts = TOKEN_TILE
    hb = ts // CONV_HALO
    vec = lambda a: pl.BlockSpec((1, cc), lambda b, j: (0, 0))
    return pl.pallas_call(
        functools.partial(_conv_kernel, ts=ts, cc=cc),
        out_shape=jax.ShapeDtypeStruct((bsz, s, cc), BF16),
        grid=(bsz, s // ts),
        in_specs=[pl.BlockSpec((1, ts, cc2), lambda b, j: (b, j, 0)),
                  pl.BlockSpec((1, CONV_HALO, cc2), lambda b, j: (b, jnp.maximum(j * hb - 1, 0), 0)),
                  pl.BlockSpec((CONV_WIDTH, cc), lambda b, j: (0, 0)),
                  vec(conv_b), vec(ln_g), vec(ln_b)],
        out_specs=pl.BlockSpec((1, ts, cc), lambda b, j: (b, j, 0)),
        scratch_shapes=[pltpu.VMEM((SUBLANES, CONV_HALO + ts, cc), F32)],
        compiler_params=_cparams(("parallel", "parallel")),
        name="conv",
    )(u3, u3, conv_w, conv_b.reshape(1, cc), ln_g.reshape(1, cc), ln_b.reshape(1, cc))


def _gla_kernel(q_ref, k_ref, v_ref, gl_ref, r_ref, ng_ref, o_ref, st_ref, *, nb, nh, dk, dv, gc, nchunks):
    @pl.when(pl.program_id(1) == 0)
    def _():
        st_ref[...] = jnp.zeros_like(st_ref)

    row = lax.broadcasted_iota(I32, (gc, gc), 0)
    col = lax.broadcasted_iota(I32, (gc, gc), 1)
    causal = col <= row
    tri = jnp.where(causal, 1.0, 0.0).astype(BF16)
    scale = dk ** -0.5

    ks = [slice(h * dk, (h + 1) * dk) for h in range(nh)]
    vs = [slice(h * dv, (h + 1) * dv) for h in range(nh)]
    streams = [(bb, h) for bb in range(nb) for h in range(nh)]

    def chunk(c, carry):
        r0 = pl.multiple_of(c * gc, gc)
        rows = pl.ds(r0, gc)
        bs = [_dot_01_f32(tri, gl_ref[bb, rows, :]) for bb in range(nb)]
        q_in, k_in, q_st, k_st, decay, v = [], [], [], [], [], []
        for bb in range(nb):
            b = bs[bb]
            b_last = b[gc - 1:gc, :]
            mid = 0.5 * b_last
            q = q_ref[bb, rows, :] * scale
            k = k_ref[bb, rows, :]
            v.append(v_ref[bb, rows, :].astype(BF16))
            q_in.append((q * jnp.exp(b - mid)).astype(BF16))
            k_in.append((k * jnp.exp(mid - b)).astype(BF16))
            q_st.append((q * jnp.exp(b)).astype(BF16))
            k_st.append((k * jnp.exp(b_last - b)).astype(BF16))
            decay.append(jnp.exp(b_last))
        sts = [st_ref[bb, h] for bb, h in streams]
        scores = [_dot_nt(q_in[bb][:, ks[h]], k_in[bb][:, ks[h]]) for bb, h in streams]
        inter = [_dot_nt(q_st[bb][:, ks[h]], st.astype(BF16)) for (bb, h), st in zip(streams, sts)]
        update = [_dot_tn(v[bb][:, vs[h]], k_st[bb][:, ks[h]]) for bb, h in streams]
        atts = [jnp.where(causal, sc, 0.0).astype(BF16) for sc in scores]
        outs = [_dot(att, v[bb][:, vs[h]]) + it for (bb, h), att, it in zip(streams, atts, inter)]
        for (bb, h), st, up, o in zip(streams, sts, update, outs):
            st_ref[bb, h] = st * decay[bb][:, ks[h]] + up
            r = r_ref[bb, rows, vs[h]].astype(F32)
            o = o * lax.rsqrt(jnp.mean(o * o, axis=-1, keepdims=True) + RMS_EPS) * ng_ref[:, vs[h]]
            o_ref[bb, rows, vs[h]] = (o * (r * _sigmoid(r))).astype(o_ref.dtype)
        return carry

    lax.fori_loop(0, nchunks, chunk, 0)


def _gla(q3, k3, v3, gl3, r3, norm_g):
    bsz, s, kw = q3.shape
    vw = v3.shape[-1]
    nh = GLA_HEADS
    dk, dv = kw // nh, vw // nh
    cb = GLA_BLOCK
    gc = GLA_CHUNK
    nb = GLA_BATCH_PER_STEP
    blk = lambda n: pl.BlockSpec((nb, cb, n), lambda b, j: (b, j, 0))
    return pl.pallas_call(
        functools.partial(_gla_kernel, nb=nb, nh=nh, dk=dk, dv=dv, gc=gc, nchunks=cb // gc),
        out_shape=jax.ShapeDtypeStruct((bsz, s, vw), BF16),
        grid=(bsz // nb, s // cb),
        in_specs=[blk(kw), blk(kw), blk(vw), blk(kw), blk(vw), pl.BlockSpec((1, vw), lambda b, j: (0, 0))],
        out_specs=blk(vw),
        scratch_shapes=[pltpu.VMEM((nb, nh, dv, dk), F32)],
        compiler_params=_cparams(("parallel", "arbitrary")),
        name="gla",
    )(q3, k3, v3, gl3, r3, norm_g.reshape(1, vw))


def _proj_ln_kernel(*refs, n_in, alpha):
    a_refs, w_refs = refs[:n_in], refs[n_in:2 * n_in]
    x_ref, gate_ref, lg_ref, lb_ref, o_ref = refs[2 * n_in:]
    mix = _dot(a_refs[0][...], w_refs[0][...])
    for a_ref, w_ref in zip(a_refs[1:], w_refs[1:]):
        mix = mix + _dot(a_ref[...], w_ref[...])
    y = alpha * x_ref[...] + (1.0 + gate_ref[0]) * mix
    o_ref[...] = _layer_norm(y, lg_ref[...], lb_ref[...])


def _proj_ln(acts, weights, x2, gate, ln_g, ln_b, tiles_per_b, alpha):
    t, d = x2.shape
    tm = TOKEN_TILE
    n_in = len(acts)
    row = lambda n: pl.BlockSpec((tm, n), lambda i: (i, 0))
    full = lambda a: pl.BlockSpec(a.shape, lambda i: (0,) * a.ndim)
    vec = pl.BlockSpec((1, d), lambda i: (0, 0))
    return pl.pallas_call(
        functools.partial(_proj_ln_kernel, n_in=n_in, alpha=alpha),
        out_shape=jax.ShapeDtypeStruct((t, d), F32),
        grid=(t // tm,),
        in_specs=[row(a.shape[1]) for a in acts] + [full(w) for w in weights]
                 + [row(d), pl.BlockSpec((1, 1, d), lambda i: (i // tiles_per_b, 0, 0)), vec, vec],
        out_specs=row(d),
        compiler_params=_cparams(("parallel",)),
        name="proj_ln",
    )(*acts, *weights, x2, gate, ln_g.reshape(1, d), ln_b.reshape(1, d))


def _mla_in_kernel(x_ref, sc_ref, sh_ref, pos_ref, invf_ref, sign_ref, win_ref, gq_ref, gkv_ref,
                   wqa_ref, wqb_ref, wk_ref, wvt_ref, q_ref, k_ref, vt_ref, *, nh, q_lora, kv_lora, scale, tk):
    h = (x_ref[...] * (1.0 + sc_ref[0]) + sh_ref[0]).astype(BF16)
    u = _dot(h, win_ref[...])
    cq = u[:, :q_lora]
    ckv = u[:, q_lora:q_lora + kv_lora]
    kr = u[:, q_lora + kv_lora:q_lora + kv_lora + LANES]
    kr_sw = u[:, q_lora + kv_lora + LANES:]
    cqn = (cq * lax.rsqrt(jnp.mean(cq * cq, axis=-1, keepdims=True) + RMS_EPS) * gq_ref[...]).astype(BF16)
    kvn = (ckv * lax.rsqrt(jnp.mean(ckv * ckv, axis=-1, keepdims=True) + RMS_EPS) * gkv_ref[...]).astype(BF16)
    ang = pos_ref[...].astype(F32) * invf_ref[...]
    cos = jnp.cos(ang)
    sin = jnp.sin(ang) * sign_ref[...]
    kr_rot = (kr * cos + kr_sw * sin).astype(BF16)
    qa = _dot(cqn, wqa_ref[...])
    qb = _dot(cqn, wqb_ref[...])
    kv = _dot(kvn, wk_ref[...])
    hw = 2 * LANES
    for hd in range(nh):
        q_ref[:, hd * hw:hd * hw + LANES] = (qa[:, hd * hw:hd * hw + LANES] * scale).astype(BF16)
        rope = qa[:, hd * hw + LANES:(hd + 1) * hw] * cos + qb[:, hd * LANES:(hd + 1) * LANES] * sin
        q_ref[:, hd * hw + LANES:(hd + 1) * hw] = (rope * scale).astype(BF16)
        k_ref[:, hd * hw:hd * hw + LANES] = kv[:, hd * LANES:(hd + 1) * LANES].astype(BF16)
        k_ref[:, hd * hw + LANES:(hd + 1) * hw] = kr_rot
    vt = _dot_nt(wvt_ref[...], kvn).astype(BF16)
    for c in range(vt.shape[1] // tk):
        vt_ref[0, c] = vt[:, c * tk:(c + 1) * tk]


def _mla_in(x2, sc, sh, pos2, w_in, gq, gkv, w_uq, w_ukv, tiles_per_b):
    t, d = x2.shape
    nh = MLA_HEADS
    q_lora, kv_lora = gq.shape[0], gkv.shape[0]
    half = MLA_ROPE // 2
    pad = LANES - MLA_ROPE
    kr_w = w_in[:, q_lora + kv_lora:]
    kr_sw = jnp.concatenate([kr_w[:, half:], kr_w[:, :half]], axis=1)
    zpad = jnp.zeros((d, pad), w_in.dtype)
    win_ext = jnp.concatenate([w_in[:, :q_lora + kv_lora], kr_w, zpad, kr_sw, zpad], axis=1).astype(BF16)
    wq = w_uq.reshape(q_lora, nh, MLA_NOPE + MLA_ROPE)
    q_nope, q_rope = wq[:, :, :MLA_NOPE], wq[:, :, MLA_NOPE:]
    q_rope_sw = jnp.concatenate([q_rope[:, :, half:], q_rope[:, :, :half]], axis=2)
    zq = jnp.zeros((q_lora, nh, pad), w_uq.dtype)
    wqa = jnp.concatenate([q_nope, q_rope, zq], axis=2).reshape(q_lora, nh * 2 * LANES).astype(BF16)
    wqb = jnp.concatenate([q_rope_sw, zq], axis=2).reshape(q_lora, nh * LANES).astype(BF16)
    wkv = w_ukv.reshape(kv_lora, nh, MLA_NOPE + MLA_V)
    wk = wkv[:, :, :MLA_NOPE].reshape(kv_lora, nh * MLA_NOPE).astype(BF16)
    wvt = wkv[:, :, MLA_NOPE:].reshape(kv_lora, nh * MLA_V).T.astype(BF16)
    inv_freq = 1.0 / (ROPE_THETA ** (jnp.arange(0, MLA_ROPE, 2, dtype=F32) / MLA_ROPE))
    invf = jnp.concatenate([inv_freq, inv_freq, jnp.zeros((pad,), F32)]).reshape(1, LANES)
    sign = jnp.concatenate([-jnp.ones((half,), F32), jnp.ones((half,), F32), jnp.zeros((pad,), F32)]).reshape(1, LANES)
    tm = TOKEN_TILE
    full = lambda a: pl.BlockSpec(a.shape, lambda i: (0,) * a.ndim)
    row = lambda n: pl.BlockSpec((tm, n), lambda i: (i, 0))
    mod = pl.BlockSpec((1, 1, d), lambda i: (i // tiles_per_b, 0, 0))
    gq2, gkv2 = gq.reshape(1, q_lora), gkv.reshape(1, kv_lora)
    scale = (MLA_NOPE + MLA_ROPE) ** -0.5 * 1.4426950408889634
    tk = ATTN_TK
    kt_per_tile = tm // tk
    s = tiles_per_b * tm
    return pl.pallas_call(
        functools.partial(_mla_in_kernel, nh=nh, q_lora=q_lora, kv_lora=kv_lora, scale=scale, tk=tk),
        out_shape=[jax.ShapeDtypeStruct((t, nh * 2 * LANES), BF16), jax.ShapeDtypeStruct((t, nh * 2 * LANES), BF16),
                   jax.ShapeDtypeStruct((t // s, s // tk, nh * MLA_V, tk), BF16)],
        grid=(t // tm,),
        in_specs=[row(d), mod, mod, row(1), full(invf), full(sign), full(win_ext), full(gq2), full(gkv2),
                  full(wqa), full(wqb), full(wk), full(wvt)],
        out_specs=[row(nh * 2 * LANES), row(nh * 2 * LANES),
                   pl.BlockSpec((1, kt_per_tile, nh * MLA_V, tk),
                                lambda i: (i // tiles_per_b, i % tiles_per_b, 0, 0))],
        compiler_params=_cparams(("parallel",)),
        name="mla_in",
    )(x2, sc, sh, pos2, invf, sign, win_ext, gq2, gkv2, wqa, wqb, wk, wvt)


def _attn_kernel(q_ref, k_ref, vt_ref, o_ref, acc_ref, s0_ref, s1_ref, *, s, tq, tk, hb):
    kpq = tq // tk
    assert kpq == 2, "the pipeline below alternates two score buffers over pairs of key tiles"
    hw = 2 * LANES
    ones = jnp.ones((8, tk), BF16)

    def q_block(qi, carry):
        qrows = pl.ds(pl.multiple_of(qi * tq, tq), tq)
        acc_ref[...] = jnp.zeros_like(acc_ref)

        def scores(j, s_ref, c0=0, nc=tq):
            krows = pl.ds(pl.multiple_of(j * tk, tk), tk)
            cols = pl.ds(pl.multiple_of(qi * tq + c0, tk), nc)
            for h in range(hb):
                s_ref[h, :, c0:c0 + nc] = _dot_nt(k_ref[krows, h * hw:(h + 1) * hw], q_ref[cols, h * hw:(h + 1) * hw])

        def tile(j, s_ref, stats, masked, c0=0, nc=tq):
            def put(full, part):
                pieces = ([full[:, :c0]] if c0 else []) + [part] + ([full[:, c0 + nc:]] if c0 + nc < tq else [])
                return pieces[0] if len(pieces) == 1 else jnp.concatenate(pieces, axis=1)
            ps, alphas, out = [], [], []
            for h in range(hb):
                m, st = stats[2 * h][:, c0:c0 + nc], s_ref[h, :, c0:c0 + nc]
                if masked:
                    key = j * tk + lax.broadcasted_iota(I32, (tk, nc), 0)
                    qry = qi * tq + c0 + lax.broadcasted_iota(I32, (tk, nc), 1)
                    st = jnp.where(key <= qry, st, -jnp.inf)
                m_new = jnp.maximum(m, jnp.max(st, axis=0, keepdims=True))
                ps.append(jnp.exp2(st - m_new).astype(BF16))
                alphas.append(jnp.exp2(m - m_new))
                out.append(put(stats[2 * h], m_new))
            for h in range(hb):
                acc_ref[h, :, c0:c0 + nc] = (alphas[h] * acc_ref[h, :, c0:c0 + nc]
                                             + _dot(vt_ref[0, j, h * MLA_V:(h + 1) * MLA_V, :], ps[h]))
                l_new = alphas[h] * stats[2 * h + 1][:, c0:c0 + nc] + _dot(ones, ps[h])[0:1]
                out.insert(2 * h + 1, put(stats[2 * h + 1], l_new))
            return tuple(out)

        stats = (jnp.full((1, tq), -jnp.inf, F32), jnp.zeros((1, tq), F32)) * hb
        scores(0, s0_ref)

        def pair(jj, c):
            j = 2 * jj
            scores(j + 1, s1_ref)
            c = tile(j, s0_ref, c, False)
            scores(j + 2, s0_ref)
            return tile(j + 1, s1_ref, c, False)

        stats = lax.fori_loop(0, qi, pair, stats)
        scores(2 * qi + 1, s1_ref, tk, tk)
        stats = tile(2 * qi, s0_ref, stats, True, 0, tk)
        stats = tile(2 * qi, s0_ref, stats, False, tk, tk)
        stats = tile(2 * qi + 1, s1_ref, stats, True, tk, tk)
        for h in range(hb):
            o_ref[qrows, h * MLA_V:(h + 1) * MLA_V] = (acc_ref[h] / stats[2 * h + 1]).T.astype(o_ref.dtype)
        return carry

    lax.fori_loop(0, s // tq, q_block, 0)


def _attn(q, k, vt, bsz, s):
    nh = MLA_HEADS
    tq, tk = ATTN_TQ, ATTN_TK
    hb = ATTN_HEADS_PER_STEP
    hw = 2 * LANES
    return pl.pallas_call(
        functools.partial(_attn_kernel, s=s, tq=tq, tk=tk, hb=hb),
        out_shape=jax.ShapeDtypeStruct((bsz * s, nh * MLA_V), BF16),
        grid=(bsz, nh // hb),
        in_specs=[pl.BlockSpec((s, hb * hw), lambda b, h: (b, h)),
                  pl.BlockSpec((s, hb * hw), lambda b, h: (b, h)),
                  pl.BlockSpec((1, s // tk, hb * MLA_V, tk), lambda b, h: (b, 0, h, 0))],
        out_specs=pl.BlockSpec((s, hb * MLA_V), lambda b, h: (b, h)),
        scratch_shapes=[pltpu.VMEM((hb, MLA_V, tq), F32), pltpu.VMEM((hb, tk, tq), F32),
                        pltpu.VMEM((hb, tk, tq), F32)],
        compiler_params=_cparams(("parallel", "parallel")),
        name="attn",
    )(q, k, vt)


def _route_kernel(x_ref, sc_ref, sh_ref, w_ref, b_ref, lp_ref, wt_ref, tm_ref, offs_ref, te_ref, na_ref,
                  upper_ref, carry_ref, *, tm, ne, ng, row_tile, nt_pad):
    i = pl.program_id(0)
    epg = ne // ng

    @pl.when(i == 0)
    def _():
        r = lax.broadcasted_iota(I32, (tm, tm), 0)
        c = lax.broadcasted_iota(I32, (tm, tm), 1)
        upper_ref[...] = jnp.where(r < c, 1.0, 0.0).astype(BF16)
        carry_ref[...] = jnp.zeros_like(carry_ref)

    h = x_ref[...] * (1.0 + sc_ref[0]) + sh_ref[0]
    h_hi, h_mid, h_lo = _split3(h)
    pa = _dot(h_hi, w_ref[...])
    pb = _dot(h_mid, w_ref[:, :2 * LANES])
    pc = _dot(h_lo, w_ref[:, :LANES])
    small = (pa[:, 2 * LANES:] + pc) + pb[:, LANES:]
    logits = pa[:, :LANES] + ((pa[:, LANES:2 * LANES] + pb[:, :LANES]) + small) + b_ref[...]
    lt = logits.T
    lr = lt[0:ne]
    grow = lax.broadcasted_iota(I32, (8, tm), 0).astype(F32)
    lg = jnp.where(grow < ng, lt[ne:ne + 8], -jnp.inf)
    gmax = jnp.max(lg, axis=0, keepdims=True)
    g_idx = jnp.min(jnp.where(lg == gmax, grow, 1e9), axis=0, keepdims=True)
    g_w = 1.0 / jnp.sum(jnp.exp(lg - gmax), axis=0, keepdims=True)
    erow = lax.broadcasted_iota(I32, (ne, tm), 0).astype(F32)
    in_group = jnp.floor(erow * (1.0 / epg)) == g_idx
    sel = jnp.where(in_group, lr, -jnp.inf)
    v1 = jnp.max(sel, axis=0, keepdims=True)
    i1 = jnp.min(jnp.where(sel == v1, erow, 1e9), axis=0, keepdims=True)
    sel2 = jnp.where(erow == i1, -jnp.inf, sel)
    v2 = jnp.max(sel2, axis=0, keepdims=True)
    i2 = jnp.min(jnp.where(sel2 == v2, erow, 1e9), axis=0, keepdims=True)
    t = jnp.exp(v2 - v1)
    w1 = g_w / (1.0 + t)
    w2 = g_w * t / (1.0 + t)
    oh1 = erow == i1
    oh2 = erow == i2
    member = jnp.where(oh1 | oh2, 1.0, 0.0)
    lcnt = jnp.sum(member, axis=1, keepdims=True)
    lcnt = jnp.floor((lcnt + (RUN_ALIGN - 1)) * (1.0 / RUN_ALIGN)) * RUN_ALIGN
    er_ = lax.broadcasted_iota(I32, (ne, ne), 0)
    ec_ = lax.broadcasted_iota(I32, (ne, ne), 1)
    lstart = jnp.dot(jnp.where(ec_ < er_, 1.0, 0.0).astype(F32), jnp.broadcast_to(lcnt, (ne, LANES)),
                     precision=HIGHEST, preferred_element_type=F32)[:, 0:1]
    lrank = _dot(member.astype(BF16), upper_ref[...]) + lstart
    p1 = jnp.sum(jnp.where(oh1, lrank, 0.0), axis=0, keepdims=True)
    p2 = jnp.sum(jnp.where(oh2, lrank, 0.0), axis=0, keepdims=True)

    orow = lax.broadcasted_iota(I32, (8, tm), 0)
    lp_ref[...] = jnp.where(orow == 0, p1, jnp.where(orow == 1, p2, 0.0)).astype(I32)
    wrow = lax.broadcasted_iota(I32, (LANES, tm), 0)
    wt_ref[...] = jnp.where(wrow == 0, w1, jnp.where(wrow == 1, w2,
                                                     jnp.where(wrow == 2, p1, jnp.where(wrow == 3, p2, 0.0)))).T
    mr = lax.broadcasted_iota(I32, (ne, LANES), 0)
    mc = lax.broadcasted_iota(I32, (ne, LANES), 1)
    to_row = lambda col, lane0: jnp.sum(jnp.where(mr + lane0 == mc, col, 0.0), axis=0, keepdims=True)
    total = jnp.sum(lcnt, axis=0, keepdims=True)
    lane = lax.broadcasted_iota(I32, (1, LANES), 1)
    packed = (to_row(lstart, 0) + to_row(lcnt, ne) + to_row(carry_ref[...], 2 * ne)
              + jnp.where(lane == 3 * ne, total, 0.0))
    trow = lax.broadcasted_iota(I32, (8, LANES), 0)
    tm_ref[...] = jnp.where(trow == 0, packed, 0.0).astype(I32)
    carry_ref[...] = carry_ref[...] + lcnt

    @pl.when(i == pl.num_programs(0) - 1)
    def _():
        cnt = carry_ref[...]
        ntl = jnp.floor((cnt + (row_tile - 1)) * (1.0 / row_tile))
        incl = jnp.where(ec_ <= er_, 1.0, 0.0).astype(F32)
        ends = jnp.dot(incl, jnp.broadcast_to(ntl, (ne, LANES)), precision=HIGHEST,
                       preferred_element_type=F32)
        starts = ends - ntl
        offs_ref[...] = jnp.concatenate([starts * row_tile, ends * row_tile, jnp.broadcast_to(ntl, (ne, LANES)),
                                         jnp.zeros((8, LANES), F32)], axis=0).astype(I32)
        tile = lax.broadcasted_iota(I32, (ne, nt_pad), 1).astype(F32)
        te = jnp.sum(jnp.where(ends[:, 0:1] <= tile, 1.0, 0.0), axis=0, keepdims=True)
        te_ref[...] = jnp.broadcast_to(jnp.minimum(te, ne - 1.0), (8, nt_pad)).astype(I32)
        na_ref[...] = jnp.broadcast_to(ends[ne - 1:ne, :], (8, LANES)).astype(I32)


def _moe_tiles(t):
    rows = 2 * t + (RUN_ALIGN - 1) * MOE_EXPERTS * (t // TOKEN_TILE)
    nt_max = -(-rows // MOE_ROW_TILE) + MOE_EXPERTS
    nt_pad = -(-nt_max // LANES) * LANES
    return nt_max, nt_pad


def _sorted_rows(tm):
    return -(-(2 * tm + (RUN_ALIGN - 1) * MOE_EXPERTS) // LANES) * LANES


def _route(x2, sc, sh, w_group, b_group, w_router, b_router, tiles_per_b):
    t, d = x2.shape
    ne, ng = MOE_EXPERTS, MOE_GROUPS
    tm = TOKEN_TILE
    _, nt_pad = _moe_tiles(t)
    wcat = jnp.concatenate([w_router, w_group, jnp.zeros((d, LANES - ne - ng), F32)], axis=1)
    w_hi = wcat.astype(BF16)
    w_mid = (wcat - w_hi.astype(F32)).astype(BF16)
    w_lo = (wcat - w_hi.astype(F32) - w_mid.astype(F32)).astype(BF16)
    wcat = jnp.concatenate([w_hi, w_mid, w_lo], axis=1)
    bcat = jnp.concatenate([b_router, b_group, jnp.zeros((LANES - ne - ng,), F32)]).reshape(1, LANES)
    const = lambda shp: pl.BlockSpec(shp, lambda i: (0,) * len(shp))
    return pl.pallas_call(
        functools.partial(_route_kernel, tm=tm, ne=ne, ng=ng, row_tile=MOE_ROW_TILE, nt_pad=nt_pad),
        out_shape=[jax.ShapeDtypeStruct((8, t), I32), jax.ShapeDtypeStruct((t, LANES), F32),
                   jax.ShapeDtypeStruct((8 * (t // tm), LANES), I32),
                   jax.ShapeDtypeStruct((3 * ne + 8, LANES), I32), jax.ShapeDtypeStruct((8, nt_pad), I32),
                   jax.ShapeDtypeStruct((8, LANES), I32)],
        grid=(t // tm,),
        in_specs=[pl.BlockSpec((tm, d), lambda i: (i, 0)),
                  pl.BlockSpec((1, 1, d), lambda i: (i // tiles_per_b, 0, 0)),
                  pl.BlockSpec((1, 1, d), lambda i: (i // tiles_per_b, 0, 0)),
                  const((d, 3 * LANES)), const((1, LANES))],
        out_specs=[pl.BlockSpec((8, tm), lambda i: (0, i)), pl.BlockSpec((tm, LANES), lambda i: (i, 0)),
                   pl.BlockSpec((8, LANES), lambda i: (i, 0)),
                   const((3 * ne + 8, LANES)), const((8, nt_pad)), const((8, LANES))],
        scratch_shapes=[pltpu.VMEM((tm, tm), BF16), pltpu.VMEM((ne, 1), F32)],
        compiler_params=_cparams(("arbitrary",)),
        name="moe_route",
    )(x2, sc, sh, wcat, bcat)


def _start_runs(tm_ref, tile, ne, copy):
    for e in range(ne):
        lstart = tm_ref[tile, e]
        n = tm_ref[tile, ne + e]
        before = tm_ref[tile, 2 * ne + e]
        _binary_pieces(n, TOKEN_TILE, lambda off, rows: copy(
            e, pl.multiple_of(lstart + off, RUN_ALIGN), pl.multiple_of(before + off, RUN_ALIGN), rows))


def _binary_pieces(n, n_max, copy):
    del n_max
    nbig = n >> (RUN_PIECE.bit_length() - 1)

    def big_piece(c, carry):
        copy(c * RUN_PIECE, RUN_PIECE).start()
        return carry

    lax.fori_loop(0, nbig, big_piece, 0)
    off = nbig * RUN_PIECE
    p = RUN_PIECE // 2
    while p >= RUN_ALIGN:
        @pl.when((n & p) != 0)
        def _():
            copy(off, p).start()
        off = off + (n & p)
        p //= 2


def _dispatch_kernel(offs_ref, gend_ref, ntl_ref, tm_ref, x_ref, sc_ref, sh_ref, lp_ref, wt_ref, xs_ref,
                     h_ref, z_ref, sem_z, sem_r, *, tm, ne, row_tile, nt_max, ntile, srows):
    i = pl.program_id(0)
    tile = i - 1

    @pl.when(i == 0)
    def _():
        z_ref[...] = jnp.zeros_like(z_ref)
        for e in range(ne):
            @pl.when(ntl_ref[e] > 0)
            def _():
                start = pl.multiple_of(gend_ref[e] - row_tile, row_tile)
                cp = pltpu.make_async_copy(z_ref, xs_ref.at[pl.ds(start, row_tile), :], sem_z)
                cp.start()
                cp.wait()
        for back in range(1, nt_max - (2 * tm * ntile) // row_tile + 1):
            @pl.when(nt_max - back >= gend_ref[ne - 1] // row_tile)
            def _():
                cp = pltpu.make_async_copy(z_ref, xs_ref.at[pl.ds((nt_max - back) * row_tile, row_tile), :], sem_z)
                cp.start()
                cp.wait()
        for spill in range(2):
            cp = pltpu.make_async_copy(z_ref, xs_ref.at[pl.ds((nt_max + spill) * row_tile, row_tile), :], sem_z)
            cp.start()
            cp.wait()

    def wait_tile(t):
        slot = t % 2
        pltpu.make_async_copy(h_ref.at[slot], xs_ref.at[pl.ds(0, srows), :], sem_r.at[slot]).wait()

    @pl.when((tile >= 2) & (tile <= ntile))
    def _():
        wait_tile(tile - 2)

    @pl.when(tile == ntile)
    def _():
        wait_tile(tile - 1)

    @pl.when((tile >= 0) & (tile < ntile))
    def _():
        slot = tile % 2
        h = (x_ref[...] * (1.0 + sc_ref[0]) + sh_ref[0]).astype(BF16)
        row = lax.broadcasted_iota(I32, (srows, tm), 0)
        lp = lp_ref[...]
        oh1 = jnp.where(row == lp[0:1, :], 1.0, 0.0).astype(BF16)
        oh2 = jnp.where(row == lp[1:2, :], 1.0, 0.0).astype(BF16)
        d = h.shape[1]
        h_ref[slot, :, :d] = _dot(oh1 + oh2, h).astype(BF16)
        lane = lax.broadcasted_iota(I32, (tm, LANES), 1)

        def weight_terms(w):
            hi = w.astype(BF16).astype(F32)
            mid = (w - hi).astype(BF16).astype(F32)
            lo = w - hi - mid
            return jnp.where(lane == 0, hi, jnp.where(lane == 1, mid, jnp.where(lane == 2, lo, 0.0))).astype(BF16)

        wt = wt_ref[...]
        h_ref[slot, :, d:] = (_dot(oh1, weight_terms(wt[:, 0:1])) + _dot(oh2, weight_terms(wt[:, 1:2]))).astype(BF16)

        def copy(e, local_row, rows_before, rows):
            dst = pl.multiple_of(offs_ref[e] + rows_before, RUN_ALIGN)
            return pltpu.make_async_copy(h_ref.at[slot, pl.ds(local_row, rows), :], xs_ref.at[pl.ds(dst, rows), :],
                                         sem_r.at[slot])

        _start_runs(tm_ref, tile, ne, copy)
        used = tm_ref[tile, 3 * ne]
        _binary_pieces(srows - used, srows - 2 * tm, lambda off, rows: pltpu.make_async_copy(
            h_ref.at[slot, pl.ds(pl.multiple_of(used + off, RUN_ALIGN), rows), :],
            xs_ref.at[pl.ds(pl.multiple_of((nt_max + slot) * row_tile + off, RUN_ALIGN), rows), :], sem_r.at[slot]))


def _dispatch(x2, sc, sh, lp, wt, tmeta, offs, gend, ntl, tiles_per_b):
    t, d = x2.shape
    tm = TOKEN_TILE
    nt_max, _ = _moe_tiles(t)
    ntile = t // tm
    srows = _sorted_rows(tm)
    cur = lambda i: jnp.clip(i - 1, 0, ntile - 1)
    grid_spec = pltpu.PrefetchScalarGridSpec(
        num_scalar_prefetch=4,
        grid=(ntile + 2,),
        in_specs=[pl.BlockSpec((tm, d), lambda i, *_: (cur(i), 0)),
                  pl.BlockSpec((1, 1, d), lambda i, *_: (cur(i) // tiles_per_b, 0, 0)),
                  pl.BlockSpec((1, 1, d), lambda i, *_: (cur(i) // tiles_per_b, 0, 0)),
                  pl.BlockSpec((8, tm), lambda i, *_: (0, cur(i))),
                  pl.BlockSpec((tm, LANES), lambda i, *_: (cur(i), 0))],
        out_specs=pl.BlockSpec(memory_space=pl.ANY),
        scratch_shapes=[pltpu.VMEM((2, srows, d + LANES), BF16), pltpu.VMEM((MOE_ROW_TILE, d + LANES), BF16),
                        pltpu.SemaphoreType.DMA, pltpu.SemaphoreType.DMA((2,))])
    return pl.pallas_call(
        functools.partial(_dispatch_kernel, tm=tm, ne=MOE_EXPERTS, row_tile=MOE_ROW_TILE, nt_max=nt_max,
                          ntile=ntile, srows=srows),
        out_shape=jax.ShapeDtypeStruct(((nt_max + 2) * MOE_ROW_TILE, d + LANES), BF16),
        grid_spec=grid_spec,
        compiler_params=_cparams(("arbitrary",)),
        name="moe_dispatch",
    )(offs, gend, ntl, tmeta, x2, sc, sh, lp, wt)


def _gmm_kernel(te_ref, na_ref, xs_ref, wg_ref, wu_ref, wd_ref, ys_ref, wgu_buf, wd_buf, *, ff, tr):
    j = pl.program_id(0)

    @pl.when(j < na_ref[0])
    def _():
        changed = (j == 0) | (te_ref[j] != te_ref[jnp.maximum(j - 1, 0)])

        @pl.when(changed)
        def _():
            wgu_buf[:, :ff] = wg_ref[0, 0].astype(BF16)
            wgu_buf[:, ff:] = wu_ref[0, 0].astype(BF16)
            wd_buf[...] = wd_ref[0, 0].astype(BF16)

        halves = [slice(c * (tr // 2), (c + 1) * (tr // 2)) for c in range(2)]
        d = wgu_buf.shape[0]
        gus = [_dot(xs_ref[rows, :d], wgu_buf[...]) for rows in halves]
        hids = [(gu[:, :ff] * _sigmoid(gu[:, :ff]) * gu[:, ff:]).astype(BF16) for gu in gus]
        for rows, hid in zip(halves, hids):
            wl = xs_ref[rows, d:].astype(F32)
            w = wl[:, 0:1] + wl[:, 1:2] + wl[:, 2:3]
            ys_ref[rows, :] = (w * _dot(hid, wd_buf[...])).astype(ys_ref.dtype)

    @pl.when(j >= na_ref[0])
    def _():
        ys_ref[...] = jnp.zeros_like(ys_ref)


def _gmm(xs, te, na, w_gate, w_up, w_down, layer):
    d, ff = w_gate.shape[-2:]
    tr = MOE_ROW_TILE
    ns = te.shape[0] * tr
    act = lambda j, te_ref, na_ref: jnp.minimum(j, na_ref[0] - 1)
    grid_spec = pltpu.PrefetchScalarGridSpec(
        num_scalar_prefetch=2,
        grid=(ns // tr,),
        in_specs=[pl.BlockSpec((tr, d + LANES), lambda j, te_ref, na_ref: (act(j, te_ref, na_ref), 0)),
                  pl.BlockSpec((1, 1, d, ff), lambda j, te_ref, na_ref: (layer, te_ref[act(j, te_ref, na_ref)], 0, 0)),
                  pl.BlockSpec((1, 1, d, ff), lambda j, te_ref, na_ref: (layer, te_ref[act(j, te_ref, na_ref)], 0, 0)),
                  pl.BlockSpec((1, 1, ff, d), lambda j, te_ref, na_ref: (layer, te_ref[act(j, te_ref, na_ref)], 0, 0))],
        out_specs=pl.BlockSpec((tr, d), lambda j, te_ref, na_ref: (j, 0)),
        scratch_shapes=[pltpu.VMEM((d, 2 * ff), BF16), pltpu.VMEM((ff, d), BF16)])
    return pl.pallas_call(
        functools.partial(_gmm_kernel, ff=ff, tr=tr),
        out_shape=jax.ShapeDtypeStruct((ns, d), BF16),
        grid_spec=grid_spec,
        compiler_params=_cparams(("arbitrary",)),
        name="moe_gmm",
    )(te, na, xs, w_gate, w_up, w_down)


def _combine_kernel(offs_ref, tm_ref, wt_ref, x_ref, gate_ref, lg_ref, lb_ref, ys_ref, o_ref,
                    buf_ref, sem_r, *, tm, ne, ntile, srows, alpha):
    i = pl.program_id(0)

    def fetch(tile):
        slot = tile % 2

        def copy(e, local_row, rows_before, rows):
            src = pl.multiple_of(offs_ref[e] + rows_before, RUN_ALIGN)
            return pltpu.make_async_copy(ys_ref.at[pl.ds(src, rows), :], buf_ref.at[slot, pl.ds(local_row, rows), :],
                                         sem_r.at[slot])

        _start_runs(tm_ref, tile, ne, copy)
        used = tm_ref[tile, 3 * ne]
        _binary_pieces(srows - used, srows - 2 * tm, lambda off, rows: pltpu.make_async_copy(
            ys_ref.at[pl.ds(pl.multiple_of(off, RUN_ALIGN), rows), :],
            buf_ref.at[slot, pl.ds(pl.multiple_of(used + off, RUN_ALIGN), rows), :], sem_r.at[slot]))

    def wait(tile):
        slot = tile % 2
        pltpu.make_async_copy(ys_ref.at[pl.ds(0, srows), :], buf_ref.at[slot], sem_r.at[slot]).wait()

    def finish(tile):
        w = wt_ref[...]
        rows = buf_ref[tile % 2]
        col = lax.broadcasted_iota(I32, (tm, srows), 1).astype(F32)
        both = jnp.where(col == w[:, 2:3], 1.0, jnp.where(col == w[:, 3:4], 1.0, 0.0)).astype(BF16)
        ffn = _dot(both, rows)
        y = alpha * x_ref[...] + (1.0 + gate_ref[0]) * ffn
        o_ref[...] = _layer_norm(y, lg_ref[...], lb_ref[...])

    @pl.when(i == 0)
    def _():
        fetch(i)

    @pl.when((i > 0) & (i < ntile))
    def _():
        wait(i - 1)
        fetch(i)
        finish(i - 1)

    @pl.when(i == ntile)
    def _():
        wait(i - 1)
        finish(i - 1)


def _combine(ys, tmeta, wt, x2, gate, ln_g, ln_b, offs, tiles_per_b, alpha):
    t, d = x2.shape
    tm = TOKEN_TILE
    ntile = t // tm
    srows = _sorted_rows(tm)
    prev = lambda i: jnp.maximum(i - 1, 0)
    vec = pl.BlockSpec((1, d), lambda i, *_: (0, 0))
    grid_spec = pltpu.PrefetchScalarGridSpec(
        num_scalar_prefetch=2,
        grid=(ntile + 1,),
        in_specs=[pl.BlockSpec((tm, LANES), lambda i, *_: (prev(i), 0)),
                  pl.BlockSpec((tm, d), lambda i, *_: (prev(i), 0)),
                  pl.BlockSpec((1, 1, d), lambda i, *_: (prev(i) // tiles_per_b, 0, 0)),
                  vec, vec,
                  pl.BlockSpec(memory_space=pl.ANY)],
        out_specs=pl.BlockSpec((tm, d), lambda i, *_: (prev(i), 0)),
        scratch_shapes=[pltpu.VMEM((2, srows, d), BF16), pltpu.SemaphoreType.DMA((2,))])
    return pl.pallas_call(
        functools.partial(_combine_kernel, tm=tm, ne=MOE_EXPERTS, ntile=ntile, srows=srows, alpha=alpha),
        out_shape=jax.ShapeDtypeStruct((t, d), F32),
        grid_spec=grid_spec,
        compiler_params=_cparams(("arbitrary",)),
        name="moe_combine",
    )(offs, tmeta, wt, x2, gate, ln_g.reshape(1, d), ln_b.reshape(1, d), ys)


def _moe_block(x2, sc, sh, gate, ln_g, ln_b, w_group, b_group, w_router, b_router, w_gate, w_up, w_down,
               layer, tiles_per_b, alpha):
    ne = MOE_EXPERTS
    nt_max, _ = _moe_tiles(x2.shape[0])
    lp, wt, tmeta, meta, te, na = _route(x2, sc, sh, w_group, b_group, w_router, b_router, tiles_per_b)
    offs, gend, ntl = meta[:ne, 0], meta[ne:2 * ne, 0], meta[2 * ne:3 * ne, 0]
    tmeta = tmeta.reshape(-1, SUBLANES, LANES)[:, 0, :]
    xs = _dispatch(x2, sc, sh, lp, wt, tmeta, offs, gend, ntl, tiles_per_b)
    ys = _gmm(xs, te[0, :nt_max], na[0, :1], w_gate, w_up, w_down, layer)
    return _combine(ys, tmeta, wt, x2, gate, ln_g, ln_b, offs, tiles_per_b, alpha)


def kernel(x, c, positions, ada_w, ada_b, ln_mix_g, ln_mix_b, ln_ffn_g, ln_ffn_b, ab_w_in, conv_w, conv_b, conv_ln_g, conv_ln_b, gla_gate_w, gla_gate_b, gla_norm_g, ab_w_out, mla_w_in, mla_q_norm_g, mla_kv_norm_g, mla_w_uq, mla_w_ukv, mla_w_out, moe_w_group, moe_b_group, moe_w_router, moe_b_router, moe_w_gate, moe_w_up, moe_w_down):
    bsz, s, d = x.shape
    depth = ada_w.shape[0]
    t = bsz * s
    tiles_per_b = s // TOKEN_TILE
    alpha = (2 * depth) ** 0.25
    mod = _ada(c, ada_w, ada_b).reshape(depth, bsz, 6, 1, d)
    x2 = x.reshape(t, d)
    for layer in range(depth):
        sh_m, sc_m, g_m, sh_f, sc_f, g_f = (mod[layer, :, n] for n in range(6))
        i = layer // 2
        if layer % 2 == 0:
            uc, q, k, v, r, gl = _ab_in(x2, sc_m, sh_m, ab_w_in[i], gla_gate_w[i], gla_gate_b[i], tiles_per_b)
            y_a = _conv(uc.reshape(bsz, s, -1), conv_w[i], conv_b[i], conv_ln_g[i], conv_ln_b[i])
            b3 = lambda a: a.reshape(bsz, s, -1)
            y_b = _gla(b3(q), b3(k), b3(v), b3(gl), b3(r), gla_norm_g[i])
            w_out = ab_w_out[i].astype(BF16)
            cc = y_a.shape[-1]
            acts = [y_a.reshape(t, cc), y_b.reshape(t, -1)]
            weights = [w_out[:cc], w_out[cc:]]
        else:
            qc, kc, vv = _mla_in(x2, sc_m, sh_m, positions.reshape(t, 1), mla_w_in[i], mla_q_norm_g[i],
                                 mla_kv_norm_g[i], mla_w_uq[i], mla_w_ukv[i], tiles_per_b)
            acts = [_attn(qc, kc, vv, bsz, s)]
            weights = [mla_w_out[i].astype(BF16)]
        x2 = _proj_ln(acts, weights, x2, g_m, ln_mix_g[layer], ln_mix_b[layer], tiles_per_b, alpha)
        x2 = _moe_block(x2, sc_f, sh_f, g_f, ln_ffn_g[layer], ln_ffn_b[layer], moe_w_group[layer], moe_b_group[layer],
                        moe_w_router[layer], moe_b_router[layer], moe_w_gate, moe_w_up, moe_w_down,
                        layer, tiles_per_b, alpha)
    return x2.reshape(bsz, s, d)
```

```python
import functools

import jax
import jax.numpy as jnp
from jax import lax
from jax.experimental import pallas as pl
from jax.experimental.pallas import tpu as pltpu

F32 = jnp.float32
BF16 = jnp.bfloat16
I32 = jnp.int32
HIGHEST = lax.Precision.HIGHEST

LN_EPS = 1e-5
RMS_EPS = 1e-6
CONV_WIDTH = 31
GLA_HEADS = 4
GLA_GATE_TAU = 16.0
MLA_HEADS = 8
MLA_NOPE = 128
MLA_ROPE = 64
MLA_V = 128
ROPE_THETA = 10000.0
MOE_GROUPS = 4
MOE_EXPERTS_PER_GROUP = 8
MOE_EXPERTS = MOE_GROUPS * MOE_EXPERTS_PER_GROUP

LANES = 128
SUBLANES = 8
TOKEN_TILE = 512
GLA_CHUNK = 128
GLA_BLOCK = 512
GLA_BATCH_PER_STEP = 4
CONV_ROWS = 32
CONV_HALO = 32
ATTN_TQ = 512
ATTN_TK = 256
ATTN_HEADS_PER_STEP = 4
MOE_ROW_TILE = 512
RUN_PIECE = 64
RUN_ALIGN = 16
SIGN_LANE = 6
VMEM_LIMIT = 48 * 1024 * 1024


def _cparams(sem):
    return pltpu.CompilerParams(dimension_semantics=sem, vmem_limit_bytes=VMEM_LIMIT)


def _sigmoid(x):
    return 1.0 / (1.0 + jnp.exp(-x))


def _dot(a, b):
    return jnp.dot(a, b, preferred_element_type=F32)


def _dot_nt(a, b):
    return lax.dot_general(a, b, (((1,), (1,)), ((), ())), preferred_element_type=F32)


def _dot_tn(a, b):
    return lax.dot_general(a, b, (((0,), (0,)), ((), ())), preferred_element_type=F32)


def _split3(x):
    hi = x.astype(BF16)
    r1 = x - hi.astype(F32)
    mid = r1.astype(BF16)
    lo = (r1 - mid.astype(F32)).astype(BF16)
    return hi, mid, lo


def _dot_01_f32(a01, x):
    hi, mid, lo = _split3(x)
    return _dot(a01, hi) + (_dot(a01, mid) + _dot(a01, lo))


def _layer_norm(y, g, b):
    mu = jnp.mean(y, axis=-1, keepdims=True)
    d = y - mu
    var = jnp.mean(d * d, axis=-1, keepdims=True)
    return d * lax.rsqrt(var + LN_EPS) * g + b


def _ada_kernel(c_ref, w_ref, b_ref, o_ref):
    c = c_ref[...]
    o_ref[0] = jnp.dot(c * _sigmoid(c), w_ref[0], precision=HIGHEST, preferred_element_type=F32) + b_ref[0]


def _ada(c, ada_w, ada_b):
    depth, d, n = ada_w.shape
    bsz = c.shape[0]
    tn = 1536
    return pl.pallas_call(
        _ada_kernel,
        out_shape=jax.ShapeDtypeStruct((depth, bsz, n), F32),
        grid=(depth, n // tn),
        in_specs=[pl.BlockSpec((bsz, d), lambda l, j: (0, 0)),
                  pl.BlockSpec((1, d, tn), lambda l, j: (l, 0, j)),
                  pl.BlockSpec((1, 1, tn), lambda l, j: (l, 0, j))],
        out_specs=pl.BlockSpec((1, bsz, tn), lambda l, j: (l, 0, j)),
        compiler_params=_cparams(("parallel", "parallel")),
        name="ada",
    )(c, ada_w, ada_b.reshape(depth, 1, n))


def _ab_in_kernel(x_ref, sc_ref, sh_ref, wc_ref, wq_ref, wk_ref, wv_ref, wr_ref, wg_ref, gw_ref, gb_ref,
                  uc_ref, q_ref, k_ref, v_ref, r_ref, gl_ref):
    h = (x_ref[...] * (1.0 + sc_ref[0]) + sh_ref[0]).astype(BF16)
    uc_ref[...] = _dot(h, wc_ref[...]).astype(uc_ref.dtype)
    q_ref[...] = _dot(h, wq_ref[...])
    k_ref[...] = _dot(h, wk_ref[...])
    v_ref[...] = _dot(h, wv_ref[...]).astype(v_ref.dtype)
    r_ref[...] = _dot(h, wr_ref[...]).astype(r_ref.dtype)
    g_low = _dot(h, wg_ref[...])
    z = jnp.dot(g_low, gw_ref[...], precision=HIGHEST, preferred_element_type=F32) + gb_ref[...]
    gl_ref[...] = (jnp.minimum(z, 0.0) - jnp.log(1.0 + jnp.exp(-jnp.abs(z)))) * (1.0 / GLA_GATE_TAU)


def _ab_in(x2, sc, sh, w_in, gate_w, gate_b, tiles_per_b):
    t, d = x2.shape
    cc2 = d
    kw = d // 4
    vw = d // 2
    rank = gate_w.shape[0]
    splits = [cc2, cc2 + kw, cc2 + 2 * kw, cc2 + 2 * kw + vw, cc2 + 2 * kw + 2 * vw]
    wb = w_in.astype(BF16)
    ws = [wb[:, :splits[0]], wb[:, splits[0]:splits[1]], wb[:, splits[1]:splits[2]],
          wb[:, splits[2]:splits[3]], wb[:, splits[3]:splits[4]], wb[:, splits[4]:]]
    tm = TOKEN_TILE
    full = lambda a: pl.BlockSpec(a.shape, lambda i: (0,) * a.ndim)
    row = lambda n: pl.BlockSpec((tm, n), lambda i: (i, 0))
    mod = pl.BlockSpec((1, 1, d), lambda i: (i // tiles_per_b, 0, 0))
    gb2 = gate_b.reshape(1, kw)
    widths = [cc2, kw, kw, vw, vw, kw]
    return pl.pallas_call(
        _ab_in_kernel,
        out_shape=[jax.ShapeDtypeStruct((t, n), BF16 if idx in (0, 3, 4) else F32) for idx, n in enumerate(widths)],
        grid=(t // tm,),
        in_specs=[row(d), mod, mod] + [full(w) for w in ws] + [full(gate_w), full(gb2)],
        out_specs=[row(n) for n in widths],
        compiler_params=_cparams(("parallel",)),
        name="ab_in",
    )(x2, sc, sh, *ws, gate_w, gb2)


def _conv_kernel(u_ref, halo_ref, cw_ref, cb_ref, lg_ref, lb_ref, o_ref, hp_ref, *, ts, cc):
    j = pl.program_id(1)

    def glu(u):
        u = u.astype(F32)
        return u[:, :cc] * _sigmoid(u[:, cc:])

    hp_ref[0, 0:CONV_HALO, :] = jnp.where(j > 0, glu(halo_ref[0]), 0.0)
    hp_ref[0, CONV_HALO:CONV_HALO + ts, :] = glu(u_ref[0])
    nrow = CONV_HALO + ts
    for b in range(1, SUBLANES):
        hp_ref[b, 0:nrow - SUBLANES, :] = hp_ref[0, b:nrow - SUBLANES + b, :]
    shift = CONV_HALO - (CONV_WIDTH - 1)
    for rb in range(ts // CONV_ROWS):
        r0 = rb * CONV_ROWS
        acc = jnp.zeros((CONV_ROWS, cc), F32)
        for tap in range(CONV_WIDTH):
            lo = r0 + shift + tap
            base = lo // SUBLANES * SUBLANES
            acc = acc + cw_ref[tap:tap + 1, :] * hp_ref[lo - base, base:base + CONV_ROWS, :]
        y = _layer_norm(acc + cb_ref[...], lg_ref[...], lb_ref[...])
        o_ref[0, r0:r0 + CONV_ROWS, :] = (y * _sigmoid(y)).astype(o_ref.dtype)


def _conv(u3, conv_w, conv_b, ln_g, ln_b):
    bsz, s, cc2 = u3.shape
    cc = cc2 // 2
    ts = TOKEN_TILE
    hb = ts // CONV_HALO
    vec = lambda a: pl.BlockSpec((1, cc), lambda b, j: (0, 0))
    return pl.pallas_call(
        functools.partial(_conv_kernel, ts=ts, cc=cc),
        out_shape=jax.ShapeDtypeStruct((bsz, s, cc), BF16),
        grid=(bsz, s // ts),
        in_specs=[pl.BlockSpec((1, ts, cc2), lambda b, j: (b, j, 0)),
                  pl.BlockSpec((1, CONV_HALO, cc2), lambda b, j: (b, jnp.maximum(j * hb - 1, 0), 0)),
                  pl.BlockSpec((CONV_WIDTH, cc), lambda b, j: (0, 0)),
                  vec(conv_b), vec(ln_g), vec(ln_b)],
        out_specs=pl.BlockSpec((1, ts, cc), lambda b, j: (b, j, 0)),
        scratch_shapes=[pltpu.VMEM((SUBLANES, CONV_HALO + ts, cc), F32)],
        compiler_params=_cparams(("parallel", "parallel")),
        name="conv",
    )(u3, u3, conv_w, conv_b.reshape(1, cc), ln_g.reshape(1, cc), ln_b.reshape(1, cc))


def _gla_kernel(q_ref, k_ref, v_ref, gl_ref, r_ref, ng_ref, o_ref, st_ref, *, nb, nh, dk, dv, gc, nchunks):
    @pl.when(pl.program_id(1) == 0)
    def _():
        st_ref[...] = jnp.zeros_like(st_ref)

    row = lax.broadcasted_iota(I32, (gc, gc), 0)
    col = lax.broadcasted_iota(I32, (gc, gc), 1)
    causal = col <= row
    tri = jnp.where(causal, 1.0, 0.0).astype(BF16)
    scale = dk ** -0.5

    ks = [slice(h * dk, (h + 1) * dk) for h in range(nh)]
    vs = [slice(h * dv, (h + 1) * dv) for h in range(nh)]
    streams = [(bb, h) for bb in range(nb) for h in range(nh)]

    def chunk(c, carry):
        r0 = pl.multiple_of(c * gc, gc)
        rows = pl.ds(r0, gc)
        bs = [_dot_01_f32(tri, gl_ref[bb, rows, :]) for bb in range(nb)]
        q_in, k_in, q_st, k_st, decay, v = [], [], [], [], [], []
        for bb in range(nb):
            b = bs[bb]
            b_last = b[gc - 1:gc, :]
            mid = 0.5 * b_last
            q = q_ref[bb, rows, :] * scale
            k = k_ref[bb, rows, :]
            v.append(v_ref[bb, rows, :].astype(BF16))
            q_in.append((q * jnp.exp(b - mid)).astype(BF16))
            k_in.append((k * jnp.exp(mid - b)).astype(BF16))
            q_st.append((q * jnp.exp(b)).astype(BF16))
            k_st.append((k * jnp.exp(b_last - b)).astype(BF16))
            decay.append(jnp.exp(b_last))
        sts = [st_ref[bb, h] for bb, h in streams]
        scores = [_dot_nt(q_in[bb][:, ks[h]], k_in[bb][:, ks[h]]) for bb, h in streams]
        inter = [_dot_nt(q_st[bb][:, ks[h]], st.astype(BF16)) for (bb, h), st in zip(streams, sts)]
        update = [_dot_tn(v[bb][:, vs[h]], k_st[bb][:, ks[h]]) for bb, h in streams]
        atts = [jnp.where(causal, sc, 0.0).astype(BF16) for sc in scores]
        outs = [_dot(att, v[bb][:, vs[h]]) + it for (bb, h), att, it in zip(streams, atts, inter)]
        for (bb, h), st, up, o in zip(streams, sts, update, outs):
            st_ref[bb, h] = st * decay[bb][:, ks[h]] + up
            r = r_ref[bb, rows, vs[h]].astype(F32)
            o = o * lax.rsqrt(jnp.mean(o * o, axis=-1, keepdims=True) + RMS_EPS) * ng_ref[:, vs[h]]
            o_ref[bb, rows, vs[h]] = (o * (r * _sigmoid(r))).astype(o_ref.dtype)
        return carry

    lax.fori_loop(0, nchunks, chunk, 0)


def _gla(q3, k3, v3, gl3, r3, norm_g):
    bsz, s, kw = q3.shape
    vw = v3.shape[-1]
    nh = GLA_HEADS
    dk, dv = kw // nh, vw // nh
    cb = GLA_BLOCK
    gc = GLA_CHUNK
    nb = GLA_BATCH_PER_STEP
    blk = lambda n: pl.BlockSpec((nb, cb, n), lambda b, j: (b, j, 0))
    return pl.pallas_call(
        functools.partial(_gla_kernel, nb=nb, nh=nh, dk=dk, dv=dv, gc=gc, nchunks=cb // gc),
        out_shape=jax.ShapeDtypeStruct((bsz, s, vw), BF16),
        grid=(bsz // nb, s // cb),
        in_specs=[blk(kw), blk(kw), blk(vw), blk(kw), blk(vw), pl.BlockSpec((1, vw), lambda b, j: (0, 0))],
        out_specs=blk(vw),
        scratch_shapes=[pltpu.VMEM((nb, nh, dv, dk), F32)],
        compiler_params=_cparams(("parallel", "arbitrary")),
        name="gla",
    )(q3, k3, v3, gl3, r3, norm_g.reshape(1, vw))


def _proj_ln_kernel(*refs, n_in, alpha):
    a_refs, w_refs = refs[:n_in], refs[n_in:2 * n_in]
    x_ref, gate_ref, lg_ref, lb_ref, o_ref = refs[2 * n_in:]
    mix = _dot(a_refs[0][...], w_refs[0][...])
    for a_ref, w_ref in zip(a_refs[1:], w_refs[1:]):
        mix = mix + _dot(a_ref[...], w_ref[...])
    y = alpha * x_ref[...] + (1.0 + gate_ref[0]) * mix
    o_ref[...] = _layer_norm(y, lg_ref[...], lb_ref[...])


def _proj_ln(acts, weights, x2, gate, ln_g, ln_b, tiles_per_b, alpha):
    t, d = x2.shape
    tm = TOKEN_TILE
    n_in = len(acts)
    row = lambda n: pl.BlockSpec((tm, n), lambda i: (i, 0))
    full = lambda a: pl.BlockSpec(a.shape, lambda i: (0,) * a.ndim)
    vec = pl.BlockSpec((1, d), lambda i: (0, 0))
    return pl.pallas_call(
        functools.partial(_proj_ln_kernel, n_in=n_in, alpha=alpha),
        out_shape=jax.ShapeDtypeStruct((t, d), F32),
        grid=(t // tm,),
        in_specs=[row(a.shape[1]) for a in acts] + [full(w) for w in weights]
                 + [row(d), pl.BlockSpec((1, 1, d), lambda i: (i // tiles_per_b, 0, 0)), vec, vec],
        out_specs=row(d),
        compiler_params=_cparams(("parallel",)),
        name="proj_ln",
    )(*acts, *weights, x2, gate, ln_g.reshape(1, d), ln_b.reshape(1, d))


def _mla_in_kernel(x_ref, sc_ref, sh_ref, pos_ref, invf_ref, sign_ref, win_ref, gq_ref, gkv_ref,
                   wqa_ref, wqb_ref, wk_ref, wvt_ref, q_ref, k_ref, vt_ref, *, nh, q_lora, kv_lora, scale, tk):
    h = (x_ref[...] * (1.0 + sc_ref[0]) + sh_ref[0]).astype(BF16)
    u = _dot(h, win_ref[...])
    cq = u[:, :q_lora]
    ckv = u[:, q_lora:q_lora + kv_lora]
    kr = u[:, q_lora + kv_lora:q_lora + kv_lora + LANES]
    kr_sw = u[:, q_lora + kv_lora + LANES:]
    cqn = (cq * lax.rsqrt(jnp.mean(cq * cq, axis=-1, keepdims=True) + RMS_EPS) * gq_ref[...]).astype(BF16)
    kvn = (ckv * lax.rsqrt(jnp.mean(ckv * ckv, axis=-1, keepdims=True) + RMS_EPS) * gkv_ref[...]).astype(BF16)
    ang = pos_ref[...].astype(F32) * invf_ref[...]
    cos = jnp.cos(ang)
    sin = jnp.sin(ang) * sign_ref[...]
    kr_rot = (kr * cos + kr_sw * sin).astype(BF16)
    qa = _dot(cqn, wqa_ref[...])
    qb = _dot(cqn, wqb_ref[...])
    kv = _dot(kvn, wk_ref[...])
    hw = 2 * LANES
    for hd in range(nh):
        q_ref[:, hd * hw:hd * hw + LANES] = (qa[:, hd * hw:hd * hw + LANES] * scale).astype(BF16)
        rope = qa[:, hd * hw + LANES:(hd + 1) * hw] * cos + qb[:, hd * LANES:(hd + 1) * LANES] * sin
        q_ref[:, hd * hw + LANES:(hd + 1) * hw] = (rope * scale).astype(BF16)
        k_ref[:, hd * hw:hd * hw + LANES] = kv[:, hd * LANES:(hd + 1) * LANES].astype(BF16)
        k_ref[:, hd * hw + LANES:(hd + 1) * hw] = kr_rot
    vt = _dot_nt(wvt_ref[...], kvn).astype(BF16)
    for c in range(vt.shape[1] // tk):
        vt_ref[0, c] = vt[:, c * tk:(c + 1) * tk]


def _mla_in(x2, sc, sh, pos2, w_in, gq, gkv, w_uq, w_ukv, tiles_per_b):
    t, d = x2.shape
    nh = MLA_HEADS
    q_lora, kv_lora = gq.shape[0], gkv.shape[0]
    half = MLA_ROPE // 2
    pad = LANES - MLA_ROPE
    kr_w = w_in[:, q_lora + kv_lora:]
    kr_sw = jnp.concatenate([kr_w[:, half:], kr_w[:, :half]], axis=1)
    zpad = jnp.zeros((d, pad), w_in.dtype)
    win_ext = jnp.concatenate([w_in[:, :q_lora + kv_lora], kr_w, zpad, kr_sw, zpad], axis=1).astype(BF16)
    wq = w_uq.reshape(q_lora, nh, MLA_NOPE + MLA_ROPE)
    q_nope, q_rope = wq[:, :, :MLA_NOPE], wq[:, :, MLA_NOPE:]
    q_rope_sw = jnp.concatenate([q_rope[:, :, half:], q_rope[:, :, :half]], axis=2)
    zq = jnp.zeros((q_lora, nh, pad), w_uq.dtype)
    wqa = jnp.concatenate([q_nope, q_rope, zq], axis=2).reshape(q_lora, nh * 2 * LANES).astype(BF16)
    wqb = jnp.concatenate([q_rope_sw, zq], axis=2).reshape(q_lora, nh * LANES).astype(BF16)
    wkv = w_ukv.reshape(kv_lora, nh, MLA_NOPE + MLA_V)
    wk = wkv[:, :, :MLA_NOPE].reshape(kv_lora, nh * MLA_NOPE).astype(BF16)
    wvt = wkv[:, :, MLA_NOPE:].reshape(kv_lora, nh * MLA_V).T.astype(BF16)
    inv_freq = 1.0 / (ROPE_THETA ** (jnp.arange(0, MLA_ROPE, 2, dtype=F32) / MLA_ROPE))
    invf = jnp.concatenate([inv_freq, inv_freq, jnp.zeros((pad,), F32)]).reshape(1, LANES)
    sign = jnp.concatenate([-jnp.ones((half,), F32), jnp.ones((half,), F32), jnp.zeros((pad,), F32)]).reshape(1, LANES)
    tm = TOKEN_TILE
    full = lambda a: pl.BlockSpec(a.shape, lambda i: (0,) * a.ndim)
    row = lambda n: pl.BlockSpec((tm, n), lambda i: (i, 0))
    mod = pl.BlockSpec((1, 1, d), lambda i: (i // tiles_per_b, 0, 0))
    gq2, gkv2 = gq.reshape(1, q_lora), gkv.reshape(1, kv_lora)
    scale = (MLA_NOPE + MLA_ROPE) ** -0.5 * 1.4426950408889634
    tk = ATTN_TK
    kt_per_tile = tm // tk
    s = tiles_per_b * tm
    return pl.pallas_call(
        functools.partial(_mla_in_kernel, nh=nh, q_lora=q_lora, kv_lora=kv_lora, scale=scale, tk=tk),
        out_shape=[jax.ShapeDtypeStruct((t, nh * 2 * LANES), BF16), jax.ShapeDtypeStruct((t, nh * 2 * LANES), BF16),
                   jax.ShapeDtypeStruct((t // s, s // tk, nh * MLA_V, tk), BF16)],
        grid=(t // tm,),
        in_specs=[row(d), mod, mod, row(1), full(invf), full(sign), full(win_ext), full(gq2), full(gkv2),
                  full(wqa), full(wqb), full(wk), full(wvt)],
        out_specs=[row(nh * 2 * LANES), row(nh * 2 * LANES),
                   pl.BlockSpec((1, kt_per_tile, nh * MLA_V, tk),
                                lambda i: (i // tiles_per_b, i % tiles_per_b, 0, 0))],
        compiler_params=_cparams(("parallel",)),
        name="mla_in",
    )(x2, sc, sh, pos2, invf, sign, win_ext, gq2, gkv2, wqa, wqb, wk, wvt)


def _attn_kernel(q_ref, k_ref, vt_ref, o_ref, acc_ref, s0_ref, s1_ref, *, s, tq, tk, hb):
    kpq = tq // tk
    assert kpq == 2, "the pipeline below alternates two score buffers over pairs of key tiles"
    hw = 2 * LANES
    ones = jnp.ones((8, tk), BF16)

    def q_block(qi, carry):
        qrows = pl.ds(pl.multiple_of(qi * tq, tq), tq)
        acc_ref[...] = jnp.zeros_like(acc_ref)

        def scores(j, s_ref, c0=0, nc=tq):
            krows = pl.ds(pl.multiple_of(j * tk, tk), tk)
            cols = pl.ds(pl.multiple_of(qi * tq + c0, tk), nc)
            for h in range(hb):
                s_ref[h, :, c0:c0 + nc] = _dot_nt(k_ref[krows, h * hw:(h + 1) * hw], q_ref[cols, h * hw:(h + 1) * hw])

        def tile(j, s_ref, stats, masked, c0=0, nc=tq):
            def put(full, part):
                pieces = ([full[:, :c0]] if c0 else []) + [part] + ([full[:, c0 + nc:]] if c0 + nc < tq else [])
                return pieces[0] if len(pieces) == 1 else jnp.concatenate(pieces, axis=1)
            ps, alphas, out = [], [], []
            for h in range(hb):
                m, st = stats[2 * h][:, c0:c0 + nc], s_ref[h, :, c0:c0 + nc]
                if masked:
                    key = j * tk + lax.broadcasted_iota(I32, (tk, nc), 0)
                    qry = qi * tq + c0 + lax.broadcasted_iota(I32, (tk, nc), 1)
                    st = jnp.where(key <= qry, st, -jnp.inf)
                m_new = jnp.maximum(m, jnp.max(st, axis=0, keepdims=True))
                ps.append(jnp.exp2(st - m_new).astype(BF16))
                alphas.append(jnp.exp2(m - m_new))
                out.append(put(stats[2 * h], m_new))
            for h in range(hb):
                acc_ref[h, :, c0:c0 + nc] = (alphas[h] * acc_ref[h, :, c0:c0 + nc]
                                             + _dot(vt_ref[0, j, h * MLA_V:(h + 1) * MLA_V, :], ps[h]))
                l_new = alphas[h] * stats[2 * h + 1][:, c0:c0 + nc] + _dot(ones, ps[h])[0:1]
                out.insert(2 * h + 1, put(stats[2 * h + 1], l_new))
            return tuple(out)

        stats = (jnp.full((1, tq), -jnp.inf, F32), jnp.zeros((1, tq), F32)) * hb
        scores(0, s0_ref)

        def pair(jj, c):
            j = 2 * jj
            scores(j + 1, s1_ref)
            c = tile(j, s0_ref, c, False)
            scores(j + 2, s0_ref)
            return tile(j + 1, s1_ref, c, False)

        stats = lax.fori_loop(0, qi, pair, stats)
        scores(2 * qi + 1, s1_ref, tk, tk)
        stats = tile(2 * qi, s0_ref, stats, True, 0, tk)
        stats = tile(2 * qi, s0_ref, stats, False, tk, tk)
        stats = tile(2 * qi + 1, s1_ref, stats, True, tk, tk)
        for h in range(hb):
            o_ref[qrows, h * MLA_V:(h + 1) * MLA_V] = (acc_ref[h] / stats[2 * h + 1]).T.astype(o_ref.dtype)
        return carry

    lax.fori_loop(0, s // tq, q_block, 0)


def _attn(q, k, vt, bsz, s):
    nh = MLA_HEADS
    tq, tk = ATTN_TQ, ATTN_TK
    hb = ATTN_HEADS_PER_STEP
    hw = 2 * LANES
    return pl.pallas_call(
        functools.partial(_attn_kernel, s=s, tq=tq, tk=tk, hb=hb),
        out_shape=jax.ShapeDtypeStruct((bsz * s, nh * MLA_V), BF16),
        grid=(bsz, nh // hb),
        in_specs=[pl.BlockSpec((s, hb * hw), lambda b, h: (b, h)),
                  pl.BlockSpec((s, hb * hw), lambda b, h: (b, h)),
                  pl.BlockSpec((1, s // tk, hb * MLA_V, tk), lambda b, h: (b, 0, h, 0))],
        out_specs=pl.BlockSpec((s, hb * MLA_V), lambda b, h: (b, h)),
        scratch_shapes=[pltpu.VMEM((hb, MLA_V, tq), F32), pltpu.VMEM((hb, tk, tq), F32),
                        pltpu.VMEM((hb, tk, tq), F32)],
        compiler_params=_cparams(("parallel", "parallel")),
        name="attn",
    )(q, k, vt)


def _route_kernel(x_ref, sc_ref, sh_ref, w_ref, b_ref, lp_ref, wt_ref, tm_ref, offs_ref, te_ref, na_ref,
                  upper_ref, carry_ref, *, tm, ne, ng, row_tile, nt_pad):
    i = pl.program_id(0)
    epg = ne // ng

    @pl.when(i == 0)
    def _():
        r = lax.broadcasted_iota(I32, (tm, tm), 0)
        c = lax.broadcasted_iota(I32, (tm, tm), 1)
        upper_ref[...] = jnp.where(r < c, 1.0, 0.0).astype(BF16)
        carry_ref[...] = jnp.zeros_like(carry_ref)

    h = x_ref[...] * (1.0 + sc_ref[0]) + sh_ref[0]
    h_hi, h_mid, h_lo = _split3(h)
    pa = _dot(h_hi, w_ref[...])
    pb = _dot(h_mid, w_ref[:, :2 * LANES])
    pc = _dot(h_lo, w_ref[:, :LANES])
    small = (pa[:, 2 * LANES:] + pc) + pb[:, LANES:]
    logits = pa[:, :LANES] + ((pa[:, LANES:2 * LANES] + pb[:, :LANES]) + small) + b_ref[...]
    lt = logits.T
    lr = lt[0:ne]
    grow = lax.broadcasted_iota(I32, (8, tm), 0).astype(F32)
    lg = jnp.where(grow < ng, lt[ne:ne + 8], -jnp.inf)
    gmax = jnp.max(lg, axis=0, keepdims=True)
    g_idx = jnp.min(jnp.where(lg == gmax, grow, 1e9), axis=0, keepdims=True)
    g_w = 1.0 / jnp.sum(jnp.exp(lg - gmax), axis=0, keepdims=True)
    erow = lax.broadcasted_iota(I32, (ne, tm), 0).astype(F32)
    in_group = jnp.floor(erow * (1.0 / epg)) == g_idx
    sel = jnp.where(in_group, lr, -jnp.inf)
    v1 = jnp.max(sel, axis=0, keepdims=True)
    i1 = jnp.min(jnp.where(sel == v1, erow, 1e9), axis=0, keepdims=True)
    sel2 = jnp.where(erow == i1, -jnp.inf, sel)
    v2 = jnp.max(sel2, axis=0, keepdims=True)
    i2 = jnp.min(jnp.where(sel2 == v2, erow, 1e9), axis=0, keepdims=True)
    t = jnp.exp(v2 - v1)
    w1 = g_w / (1.0 + t)
    w2 = g_w * t / (1.0 + t)
    oh1 = erow == i1
    oh2 = erow == i2
    member = jnp.where(oh1 | oh2, 1.0, 0.0)
    lcnt = jnp.sum(member, axis=1, keepdims=True)
    lcnt = jnp.floor((lcnt + (RUN_ALIGN - 1)) * (1.0 / RUN_ALIGN)) * RUN_ALIGN
    er_ = lax.broadcasted_iota(I32, (ne, ne), 0)
    ec_ = lax.broadcasted_iota(I32, (ne, ne), 1)
    lstart = jnp.dot(jnp.where(ec_ < er_, 1.0, 0.0).astype(F32), jnp.broadcast_to(lcnt, (ne, LANES)),
                     precision=HIGHEST, preferred_element_type=F32)[:, 0:1]
    lrank = _dot(member.astype(BF16), upper_ref[...]) + lstart
    p1 = jnp.sum(jnp.where(oh1, lrank, 0.0), axis=0, keepdims=True)
    p2 = jnp.sum(jnp.where(oh2, lrank, 0.0), axis=0, keepdims=True)

    orow = lax.broadcasted_iota(I32, (8, tm), 0)
    lp_ref[...] = jnp.where(orow == 0, p1, jnp.where(orow == 1, p2, 0.0)).astype(I32)
    wrow = lax.broadcasted_iota(I32, (LANES, tm), 0)
    wt_ref[...] = jnp.where(wrow == 0, w1, jnp.where(wrow == 1, w2,
                                                     jnp.where(wrow == 2, p1, jnp.where(wrow == 3, p2, 0.0)))).T
    mr = lax.broadcasted_iota(I32, (ne, LANES), 0)
    mc = lax.broadcasted_iota(I32, (ne, LANES), 1)
    to_row = lambda col, lane0: jnp.sum(jnp.where(mr + lane0 == mc, col, 0.0), axis=0, keepdims=True)
    total = jnp.sum(lcnt, axis=0, keepdims=True)
    lane = lax.broadcasted_iota(I32, (1, LANES), 1)
    packed = (to_row(lstart, 0) + to_row(lcnt, ne) + to_row(carry_ref[...], 2 * ne)
              + jnp.where(lane == 3 * ne, total, 0.0))
    trow = lax.broadcasted_iota(I32, (8, LANES), 0)
    tm_ref[...] = jnp.where(trow == 0, packed, 0.0).astype(I32)
    carry_ref[...] = carry_ref[...] + lcnt

    @pl.when(i == pl.num_programs(0) - 1)
    def _():
        cnt = carry_ref[...]
        ntl = jnp.floor((cnt + (row_tile - 1)) * (1.0 / row_tile))
        incl = jnp.where(ec_ <= er_, 1.0, 0.0).astype(F32)
        ends = jnp.dot(incl, jnp.broadcast_to(ntl, (ne, LANES)), precision=HIGHEST,
                       preferred_element_type=F32)
        starts = ends - ntl
        offs_ref[...] = jnp.concatenate([starts * row_tile, ends * row_tile, jnp.broadcast_to(ntl, (ne, LANES)),
                                         jnp.zeros((8, LANES), F32)], axis=0).astype(I32)
        tile = lax.broadcasted_iota(I32, (ne, nt_pad), 1).astype(F32)
        te = jnp.sum(jnp.where(ends[:, 0:1] <= tile, 1.0, 0.0), axis=0, keepdims=True)
        te_ref[...] = jnp.broadcast_to(jnp.minimum(te, ne - 1.0), (8, nt_pad)).astype(I32)
        na_ref[...] = jnp.broadcast_to(ends[ne - 1:ne, :], (8, LANES)).astype(I32)


def _moe_tiles(t):
    rows = 2 * t + (RUN_ALIGN - 1) * MOE_EXPERTS * (t // TOKEN_TILE)
    nt_max = -(-rows // MOE_ROW_TILE) + MOE_EXPERTS
    nt_pad = -(-nt_max // LANES) * LANES
    return nt_max, nt_pad


def _sorted_rows(tm):
    return -(-(2 * tm + (RUN_ALIGN - 1) * MOE_EXPERTS) // LANES) * LANES


def _route(x2, sc, sh, w_group, b_group, w_router, b_router, tiles_per_b):
    t, d = x2.shape
    ne, ng = MOE_EXPERTS, MOE_GROUPS
    tm = TOKEN_TILE
    _, nt_pad = _moe_tiles(t)
    wcat = jnp.concatenate([w_router, w_group, jnp.zeros((d, LANES - ne - ng), F32)], axis=1)
    w_hi = wcat.astype(BF16)
    w_mid = (wcat - w_hi.astype(F32)).astype(BF16)
    w_lo = (wcat - w_hi.astype(F32) - w_mid.astype(F32)).astype(BF16)
    wcat = jnp.concatenate([w_hi, w_mid, w_lo], axis=1)
    bcat = jnp.concatenate([b_router, b_group, jnp.zeros((LANES - ne - ng,), F32)]).reshape(1, LANES)
    const = lambda shp: pl.BlockSpec(shp, lambda i: (0,) * len(shp))
    return pl.pallas_call(
        functools.partial(_route_kernel, tm=tm, ne=ne, ng=ng, row_tile=MOE_ROW_TILE, nt_pad=nt_pad),
        out_shape=[jax.ShapeDtypeStruct((8, t), I32), jax.ShapeDtypeStruct((t, LANES), F32),
                   jax.ShapeDtypeStruct((8 * (t // tm), LANES), I32),
                   jax.ShapeDtypeStruct((3 * ne + 8, LANES), I32), jax.ShapeDtypeStruct((8, nt_pad), I32),
                   jax.ShapeDtypeStruct((8, LANES), I32)],
        grid=(t // tm,),
        in_specs=[pl.BlockSpec((tm, d), lambda i: (i, 0)),
                  pl.BlockSpec((1, 1, d), lambda i: (i // tiles_per_b, 0, 0)),
                  pl.BlockSpec((1, 1, d), lambda i: (i // tiles_per_b, 0, 0)),
                  const((d, 3 * LANES)), const((1, LANES))],
        out_specs=[pl.BlockSpec((8, tm), lambda i: (0, i)), pl.BlockSpec((tm, LANES), lambda i: (i, 0)),
                   pl.BlockSpec((8, LANES), lambda i: (i, 0)),
                   const((3 * ne + 8, LANES)), const((8, nt_pad)), const((8, LANES))],
        scratch_shapes=[pltpu.VMEM((tm, tm), BF16), pltpu.VMEM((ne, 1), F32)],
        compiler_params=_cparams(("arbitrary",)),
        name="moe_route",
    )(x2, sc, sh, wcat, bcat)


def _start_runs(tm_ref, tile, ne, copy):
    for e in range(ne):
        lstart = tm_ref[tile, e]
        n = tm_ref[tile, ne + e]
        before = tm_ref[tile, 2 * ne + e]
        _binary_pieces(n, TOKEN_TILE, lambda off, rows: copy(
            e, pl.multiple_of(lstart + off, RUN_ALIGN), pl.multiple_of(before + off, RUN_ALIGN), rows))


def _binary_pieces(n, n_max, copy):
    del n_max
    nbig = n >> (RUN_PIECE.bit_length() - 1)

    def big_piece(c, carry):
        copy(c * RUN_PIECE, RUN_PIECE).start()
        return carry

    lax.fori_loop(0, nbig, big_piece, 0)
    off = nbig * RUN_PIECE
    p = RUN_PIECE // 2
    while p >= RUN_ALIGN:
        @pl.when((n & p) != 0)
        def _():
            copy(off, p).start()
        off = off + (n & p)
        p //= 2


def _dispatch_kernel(offs_ref, gend_ref, ntl_ref, tm_ref, x_ref, sc_ref, sh_ref, lp_ref, wt_ref, xs_ref,
                     h_ref, z_ref, sem_z, sem_r, *, tm, ne, row_tile, nt_max, ntile, srows):
    i = pl.program_id(0)
    tile = i - 1

    @pl.when(i == 0)
    def _():
        z_ref[...] = jnp.zeros_like(z_ref)
        for e in range(ne):
            @pl.when(ntl_ref[e] > 0)
            def _():
                start = pl.multiple_of(gend_ref[e] - row_tile, row_tile)
                cp = pltpu.make_async_copy(z_ref, xs_ref.at[pl.ds(start, row_tile), :], sem_z)
                cp.start()
                cp.wait()
        for back in range(1, nt_max - (2 * tm * ntile) // row_tile + 1):
            @pl.when(nt_max - back >= gend_ref[ne - 1] // row_tile)
            def _():
                cp = pltpu.make_async_copy(z_ref, xs_ref.at[pl.ds((nt_max - back) * row_tile, row_tile), :], sem_z)
                cp.start()
                cp.wait()
        for spill in range(2):
            cp = pltpu.make_async_copy(z_ref, xs_ref.at[pl.ds((nt_max + spill) * row_tile, row_tile), :], sem_z)
            cp.start()
            cp.wait()

    def wait_tile(t):
        slot = t % 2
        pltpu.make_async_copy(h_ref.at[slot], xs_ref.at[pl.ds(0, srows), :], sem_r.at[slot]).wait()

    @pl.when((tile >= 2) & (tile <= ntile))
    def _():
        wait_tile(tile - 2)

    @pl.when(tile == ntile)
    def _():
        wait_tile(tile - 1)

    @pl.when((tile >= 0) & (tile < ntile))
    def _():
        slot = tile % 2
        h = (x_ref[...] * (1.0 + sc_ref[0]) + sh_ref[0]).astype(BF16)
        row = lax.broadcasted_iota(I32, (srows, tm), 0)
        lp = lp_ref[...]
        signed = jnp.where(row == lp[0:1, :], 1.0, jnp.where(row == lp[1:2, :], -1.0, 0.0)).astype(BF16)
        lane = lax.broadcasted_iota(I32, (tm, LANES), 1)
        wt = wt_ref[...]
        terms = []
        for k in range(2):
            w = wt[:, k:k + 1]
            hi = w.astype(BF16).astype(F32)
            mid = (w - hi).astype(BF16).astype(F32)
            terms += [hi, mid, w - hi - mid]
        extra = jnp.where(lane == SIGN_LANE, 1.0, 0.0)
        for idx, term in enumerate(terms):
            extra = jnp.where(lane == idx, term, extra)
        h_ref[slot] = _dot(signed, jnp.concatenate([h, extra.astype(BF16)], axis=1)).astype(BF16)

        def copy(e, local_row, rows_before, rows):
            dst = pl.multiple_of(offs_ref[e] + rows_before, RUN_ALIGN)
            return pltpu.make_async_copy(h_ref.at[slot, pl.ds(local_row, rows), :], xs_ref.at[pl.ds(dst, rows), :],
                                         sem_r.at[slot])

        _start_runs(tm_ref, tile, ne, copy)
        used = tm_ref[tile, 3 * ne]
        _binary_pieces(srows - used, srows - 2 * tm, lambda off, rows: pltpu.make_async_copy(
            h_ref.at[slot, pl.ds(pl.multiple_of(used + off, RUN_ALIGN), rows), :],
            xs_ref.at[pl.ds(pl.multiple_of((nt_max + slot) * row_tile + off, RUN_ALIGN), rows), :], sem_r.at[slot]))


def _dispatch(x2, sc, sh, lp, wt, tmeta, offs, gend, ntl, tiles_per_b):
    t, d = x2.shape
    tm = TOKEN_TILE
    nt_max, _ = _moe_tiles(t)
    ntile = t // tm
    srows = _sorted_rows(tm)
    cur = lambda i: jnp.clip(i - 1, 0, ntile - 1)
    grid_spec = pltpu.PrefetchScalarGridSpec(
        num_scalar_prefetch=4,
        grid=(ntile + 2,),
        in_specs=[pl.BlockSpec((tm, d), lambda i, *_: (cur(i), 0)),
                  pl.BlockSpec((1, 1, d), lambda i, *_: (cur(i) // tiles_per_b, 0, 0)),
                  pl.BlockSpec((1, 1, d), lambda i, *_: (cur(i) // tiles_per_b, 0, 0)),
                  pl.BlockSpec((8, tm), lambda i, *_: (0, cur(i))),
                  pl.BlockSpec((tm, LANES), lambda i, *_: (cur(i), 0))],
        out_specs=pl.BlockSpec(memory_space=pl.ANY),
        scratch_shapes=[pltpu.VMEM((2, srows, d + LANES), BF16), pltpu.VMEM((MOE_ROW_TILE, d + LANES), BF16),
                        pltpu.SemaphoreType.DMA, pltpu.SemaphoreType.DMA((2,))])
    return pl.pallas_call(
        functools.partial(_dispatch_kernel, tm=tm, ne=MOE_EXPERTS, row_tile=MOE_ROW_TILE, nt_max=nt_max,
                          ntile=ntile, srows=srows),
        out_shape=jax.ShapeDtypeStruct(((nt_max + 2) * MOE_ROW_TILE, d + LANES), BF16),
        grid_spec=grid_spec,
        compiler_params=_cparams(("arbitrary",)),
        name="moe_dispatch",
    )(offs, gend, ntl, tmeta, x2, sc, sh, lp, wt)


def _gmm_kernel(te_ref, na_ref, xs_ref, wg_ref, wu_ref, wd_ref, ys_ref, wgu_buf, wd_buf, *, ff, tr):
    j = pl.program_id(0)

    @pl.when(j < na_ref[0])
    def _():
        changed = (j == 0) | (te_ref[j] != te_ref[jnp.maximum(j - 1, 0)])

        @pl.when(changed)
        def _():
            wgu_buf[:, :ff] = wg_ref[0, 0].astype(BF16)
            wgu_buf[:, ff:] = wu_ref[0, 0].astype(BF16)
            wd_buf[...] = wd_ref[0, 0].astype(BF16)

        halves = [slice(c * (tr // 2), (c + 1) * (tr // 2)) for c in range(2)]
        d = wgu_buf.shape[0]
        riders = [xs_ref[rows, d:].astype(F32) for rows in halves]
        signs = [r[:, SIGN_LANE:SIGN_LANE + 1] for r in riders]
        gus = [_dot(xs_ref[rows, :d] * sg.astype(BF16), wgu_buf[...]) for rows, sg in zip(halves, signs)]
        hids = [(gu[:, :ff] * _sigmoid(gu[:, :ff]) * gu[:, ff:]).astype(BF16) for gu in gus]
        for rows, hid, r, sg in zip(halves, hids, riders, signs):
            w = jnp.where(sg > 0.0, r[:, 0:1] + r[:, 1:2] + r[:, 2:3], -(r[:, 3:4] + r[:, 4:5] + r[:, 5:6]))
            ys_ref[rows, :] = (w * _dot(hid, wd_buf[...])).astype(ys_ref.dtype)

    @pl.when(j >= na_ref[0])
    def _():
        ys_ref[...] = jnp.zeros_like(ys_ref)


def _gmm(xs, te, na, w_gate, w_up, w_down, layer):
    d, ff = w_gate.shape[-2:]
    tr = MOE_ROW_TILE
    ns = te.shape[0] * tr
    act = lambda j, te_ref, na_ref: jnp.minimum(j, na_ref[0] - 1)
    grid_spec = pltpu.PrefetchScalarGridSpec(
        num_scalar_prefetch=2,
        grid=(ns // tr,),
        in_specs=[pl.BlockSpec((tr, d + LANES), lambda j, te_ref, na_ref: (act(j, te_ref, na_ref), 0)),
                  pl.BlockSpec((1, 1, d, ff), lambda j, te_ref, na_ref: (layer, te_ref[act(j, te_ref, na_ref)], 0, 0)),
                  pl.BlockSpec((1, 1, d, ff), lambda j, te_ref, na_ref: (layer, te_ref[act(j, te_ref, na_ref)], 0, 0)),
                  pl.BlockSpec((1, 1, ff, d), lambda j, te_ref, na_ref: (layer, te_ref[act(j, te_ref, na_ref)], 0, 0))],
        out_specs=pl.BlockSpec((tr, d), lambda j, te_ref, na_ref: (j, 0)),
        scratch_shapes=[pltpu.VMEM((d, 2 * ff), BF16), pltpu.VMEM((ff, d), BF16)])
    return pl.pallas_call(
        functools.partial(_gmm_kernel, ff=ff, tr=tr),
        out_shape=jax.ShapeDtypeStruct((ns, d), BF16),
        grid_spec=grid_spec,
        compiler_params=_cparams(("arbitrary",)),
        name="moe_gmm",
    )(te, na, xs, w_gate, w_up, w_down)


def _combine_kernel(offs_ref, tm_ref, wt_ref, x_ref, gate_ref, lg_ref, lb_ref, ys_ref, o_ref,
                    buf_ref, sem_r, *, tm, ne, ntile, srows, alpha):
    i = pl.program_id(0)

    def fetch(tile):
        slot = tile % 2

        def copy(e, local_row, rows_before, rows):
            src = pl.multiple_of(offs_ref[e] + rows_before, RUN_ALIGN)
            return pltpu.make_async_copy(ys_ref.at[pl.ds(src, rows), :], buf_ref.at[slot, pl.ds(local_row, rows), :],
                                         sem_r.at[slot])

        _start_runs(tm_ref, tile, ne, copy)
        used = tm_ref[tile, 3 * ne]
        _binary_pieces(srows - used, srows - 2 * tm, lambda off, rows: pltpu.make_async_copy(
            ys_ref.at[pl.ds(pl.multiple_of(off, RUN_ALIGN), rows), :],
            buf_ref.at[slot, pl.ds(pl.multiple_of(used + off, RUN_ALIGN), rows), :], sem_r.at[slot]))

    def wait(tile):
        slot = tile % 2
        pltpu.make_async_copy(ys_ref.at[pl.ds(0, srows), :], buf_ref.at[slot], sem_r.at[slot]).wait()

    def finish(tile):
        w = wt_ref[...]
        rows = buf_ref[tile % 2]
        col = lax.broadcasted_iota(I32, (tm, srows), 1).astype(F32)
        both = jnp.where(col == w[:, 2:3], 1.0, jnp.where(col == w[:, 3:4], 1.0, 0.0)).astype(BF16)
        ffn = _dot(both, rows)
        y = alpha * x_ref[...] + (1.0 + gate_ref[0]) * ffn
        o_ref[...] = _layer_norm(y, lg_ref[...], lb_ref[...])

    @pl.when(i == 0)
    def _():
        fetch(i)

    @pl.when((i > 0) & (i < ntile))
    def _():
        wait(i - 1)
        fetch(i)
        finish(i - 1)

    @pl.when(i == ntile)
    def _():
        wait(i - 1)
        finish(i - 1)


def _combine(ys, tmeta, wt, x2, gate, ln_g, ln_b, offs, tiles_per_b, alpha):
    t, d = x2.shape
    tm = TOKEN_TILE
    ntile = t // tm
    srows = _sorted_rows(tm)
    prev = lambda i: jnp.maximum(i - 1, 0)
    vec = pl.BlockSpec((1, d), lambda i, *_: (0, 0))
    grid_spec = pltpu.PrefetchScalarGridSpec(
        num_scalar_prefetch=2,
        grid=(ntile + 1,),
        in_specs=[pl.BlockSpec((tm, LANES), lambda i, *_: (prev(i), 0)),
                  pl.BlockSpec((tm, d), lambda i, *_: (prev(i), 0)),
                  pl.BlockSpec((1, 1, d), lambda i, *_: (prev(i) // tiles_per_b, 0, 0)),
                  vec, vec,
                  pl.BlockSpec(memory_space=pl.ANY)],
        out_specs=pl.BlockSpec((tm, d), lambda i, *_: (prev(i), 0)),
        scratch_shapes=[pltpu.VMEM((2, srows, d), BF16), pltpu.SemaphoreType.DMA((2,))])
    return pl.pallas_call(
        functools.partial(_combine_kernel, tm=tm, ne=MOE_EXPERTS, ntile=ntile, srows=srows, alpha=alpha),
        out_shape=jax.ShapeDtypeStruct((t, d), F32),
        grid_spec=grid_spec,
        compiler_params=_cparams(("arbitrary",)),
        name="moe_combine",
    )(offs, tmeta, wt, x2, gate, ln_g.reshape(1, d), ln_b.reshape(1, d), ys)


def _moe_block(x2, sc, sh, gate, ln_g, ln_b, w_group, b_group, w_router, b_router, w_gate, w_up, w_down,
               layer, tiles_per_b, alpha):
    ne = MOE_EXPERTS
    nt_max, _ = _moe_tiles(x2.shape[0])
    lp, wt, tmeta, meta, te, na = _route(x2, sc, sh, w_group, b_group, w_router, b_router, tiles_per_b)
    offs, gend, ntl = meta[:ne, 0], meta[ne:2 * ne, 0], meta[2 * ne:3 * ne, 0]
    tmeta = tmeta.reshape(-1, SUBLANES, LANES)[:, 0, :]
    xs = _dispatch(x2, sc, sh, lp, wt, tmeta, offs, gend, ntl, tiles_per_b)
    ys = _gmm(xs, te[0, :nt_max], na[0, :1], w_gate, w_up, w_down, layer)
    return _combine(ys, tmeta, wt, x2, gate, ln_g, ln_b, offs, tiles_per_b, alpha)


def kernel(x, c, positions, ada_w, ada_b, ln_mix_g, ln_mix_b, ln_ffn_g, ln_ffn_b, ab_w_in, conv_w, conv_b, conv_ln_g, conv_ln_b, gla_gate_w, gla_gate_b, gla_norm_g, ab_w_out, mla_w_in, mla_q_norm_g, mla_kv_norm_g, mla_w_uq, mla_w_ukv, mla_w_out, moe_w_group, moe_b_group, moe_w_router, moe_b_router, moe_w_gate, moe_w_up, moe_w_down):
    bsz, s, d = x.shape
    depth = ada_w.shape[0]
    t = bsz * s
    tiles_per_b = s // TOKEN_TILE
    alpha = (2 * depth) ** 0.25
    mod = _ada(c, ada_w, ada_b).reshape(depth, bsz, 6, 1, d)
    x2 = x.reshape(t, d)
    for layer in range(depth):
        sh_m, sc_m, g_m, sh_f, sc_f, g_f = (mod[layer, :, n] for n in range(6))
        i = layer // 2
        if layer % 2 == 0:
            uc, q, k, v, r, gl = _ab_in(x2, sc_m, sh_m, ab_w_in[i], gla_gate_w[i], gla_gate_b[i], tiles_per_b)
            y_a = _conv(uc.reshape(bsz, s, -1), conv_w[i], conv_b[i], conv_ln_g[i], conv_ln_b[i])
            b3 = lambda a: a.reshape(bsz, s, -1)
            y_b = _gla(b3(q), b3(k), b3(v), b3(gl), b3(r), gla_norm_g[i])
            w_out = ab_w_out[i].astype(BF16)
            cc = y_a.shape[-1]
            acts = [y_a.reshape(t, cc), y_b.reshape(t, -1)]
            weights = [w_out[:cc], w_out[cc:]]
        else:
            qc, kc, vv = _mla_in(x2, sc_m, sh_m, positions.reshape(t, 1), mla_w_in[i], mla_q_norm_g[i],
                                 mla_kv_norm_g[i], mla_w_uq[i], mla_w_ukv[i], tiles_per_b)
            acts = [_attn(qc, kc, vv, bsz, s)]
            weights = [mla_w_out[i].astype(BF16)]
        x2 = _proj_ln(acts, weights, x2, g_m, ln_mix_g[layer], ln_mix_b[layer], tiles_per_b, alpha)
        x2 = _moe_block(x2, sc_f, sh_f, g_f, ln_ffn_g[layer], ln_ffn_b[layer], moe_w_group[layer], moe_b_group[layer],
                        moe_w_router[layer], moe_b_router[layer], moe_w_gate, moe_w_up, moe_w_down,
                        layer, tiles_per_b, alpha)
    return x2.reshape(bsz, s, d)
```

```python
import functools

import jax
import jax.numpy as jnp
from jax import lax
from jax.experimental import pallas as pl
from jax.experimental.pallas import tpu as pltpu

F32 = jnp.float32
BF16 = jnp.bfloat16
I32 = jnp.int32
HIGHEST = lax.Precision.HIGHEST

LN_EPS = 1e-5
RMS_EPS = 1e-6
CONV_WIDTH = 31
GLA_HEADS = 4
GLA_GATE_TAU = 16.0
MLA_HEADS = 8
MLA_NOPE = 128
MLA_ROPE = 64
MLA_V = 128
ROPE_THETA = 10000.0
MOE_GROUPS = 4
MOE_EXPERTS_PER_GROUP = 8
MOE_EXPERTS = MOE_GROUPS * MOE_EXPERTS_PER_GROUP

LANES = 128
SUBLANES = 8
TOKEN_TILE = 512
GLA_CHUNK = 128
GLA_BLOCK = 512
GLA_BATCH_PER_STEP = 4
CONV_ROWS = 32
CONV_HALO = 32
ATTN_TQ = 512
ATTN_TK = 256
ATTN_HEADS_PER_STEP = 4
MOE_ROW_TILE = 512
RUN_PIECE = 64
RUN_ALIGN = 16
SIGN_LANE = 6
VMEM_LIMIT = 48 * 1024 * 1024


def _cparams(sem):
    return pltpu.CompilerParams(dimension_semantics=sem, vmem_limit_bytes=VMEM_LIMIT)


def _sigmoid(x):
    return 1.0 / (1.0 + jnp.exp(-x))


def _dot(a, b):
    return jnp.dot(a, b, preferred_element_type=F32)


def _dot_nt(a, b):
    return lax.dot_general(a, b, (((1,), (1,)), ((), ())), preferred_element_type=F32)


def _dot_tn(a, b):
    return lax.dot_general(a, b, (((0,), (0,)), ((), ())), preferred_element_type=F32)


def _split3(x):
    hi = x.astype(BF16)
    r1 = x - hi.astype(F32)
    mid = r1.astype(BF16)
    lo = (r1 - mid.astype(F32)).astype(BF16)
    return hi, mid, lo


def _dot_01_f32(a01, x):
    hi, mid, lo = _split3(x)
    return _dot(a01, hi) + (_dot(a01, mid) + _dot(a01, lo))


def _layer_norm(y, g, b):
    mu = jnp.mean(y, axis=-1, keepdims=True)
    d = y - mu
    var = jnp.mean(d * d, axis=-1, keepdims=True)
    return d * lax.rsqrt(var + LN_EPS) * g + b


def _ada_kernel(c_ref, w_ref, b_ref, o_ref):
    c = c_ref[...]
    o_ref[0] = jnp.dot(c * _sigmoid(c), w_ref[0], precision=HIGHEST, preferred_element_type=F32) + b_ref[0]


def _ada(c, ada_w, ada_b):
    depth, d, n = ada_w.shape
    bsz = c.shape[0]
    tn = 1536
    return pl.pallas_call(
        _ada_kernel,
        out_shape=jax.ShapeDtypeStruct((depth, bsz, n), F32),
        grid=(depth, n // tn),
        in_specs=[pl.BlockSpec((bsz, d), lambda l, j: (0, 0)),
                  pl.BlockSpec((1, d, tn), lambda l, j: (l, 0, j)),
                  pl.BlockSpec((1, 1, tn), lambda l, j: (l, 0, j))],
        out_specs=pl.BlockSpec((1, bsz, tn), lambda l, j: (l, 0, j)),
        compiler_params=_cparams(("parallel", "parallel")),
        name="ada",
    )(c, ada_w, ada_b.reshape(depth, 1, n))


def _ab_in_kernel(x_ref, sc_ref, sh_ref, wc_ref, wq_ref, wk_ref, wv_ref, wr_ref, wg_ref, gw_ref, gb_ref,
                  uc_ref, q_ref, k_ref, v_ref, r_ref, gl_ref):
    h = (x_ref[...] * (1.0 + sc_ref[0]) + sh_ref[0]).astype(BF16)
    uc_ref[...] = _dot(h, wc_ref[...]).astype(uc_ref.dtype)
    q_ref[...] = _dot(h, wq_ref[...])
    k_ref[...] = _dot(h, wk_ref[...])
    v_ref[...] = _dot(h, wv_ref[...]).astype(v_ref.dtype)
    r_ref[...] = _dot(h, wr_ref[...]).astype(r_ref.dtype)
    g_low = _dot(h, wg_ref[...])
    z = jnp.dot(g_low, gw_ref[...], precision=HIGHEST, preferred_element_type=F32) + gb_ref[...]
    gl_ref[...] = (jnp.minimum(z, 0.0) - jnp.log(1.0 + jnp.exp(-jnp.abs(z)))) * (1.0 / GLA_GATE_TAU)


def _ab_in(x2, sc, sh, w_in, gate_w, gate_b, tiles_per_b):
    t, d = x2.shape
    cc2 = d
    kw = d // 4
    vw = d // 2
    rank = gate_w.shape[0]
    splits = [cc2, cc2 + kw, cc2 + 2 * kw, cc2 + 2 * kw + vw, cc2 + 2 * kw + 2 * vw]
    wb = w_in.astype(BF16)
    ws = [wb[:, :splits[0]], wb[:, splits[0]:splits[1]], wb[:, splits[1]:splits[2]],
          wb[:, splits[2]:splits[3]], wb[:, splits[3]:splits[4]], wb[:, splits[4]:]]
    tm = TOKEN_TILE
    full = lambda a: pl.BlockSpec(a.shape, lambda i: (0,) * a.ndim)
    row = lambda n: pl.BlockSpec((tm, n), lambda i: (i, 0))
    mod = pl.BlockSpec((1, 1, d), lambda i: (i // tiles_per_b, 0, 0))
    gb2 = gate_b.reshape(1, kw)
    widths = [cc2, kw, kw, vw, vw, kw]
    return pl.pallas_call(
        _ab_in_kernel,
        out_shape=[jax.ShapeDtypeStruct((t, n), BF16 if idx in (0, 3, 4) else F32) for idx, n in enumerate(widths)],
        grid=(t // tm,),
        in_specs=[row(d), mod, mod] + [full(w) for w in ws] + [full(gate_w), full(gb2)],
        out_specs=[row(n) for n in widths],
        compiler_params=_cparams(("parallel",)),
        name="ab_in",
    )(x2, sc, sh, *ws, gate_w, gb2)


def _conv_kernel(u_ref, halo_ref, cw_ref, cb_ref, lg_ref, lb_ref, o_ref, hp_ref, *, ts, cc):
    j = pl.program_id(1)

    def glu(u):
        u = u.astype(F32)
        return u[:, :cc] * _sigmoid(u[:, cc:])

    hp_ref[0, 0:CONV_HALO, :] = jnp.where(j > 0, glu(halo_ref[0]), 0.0)
    hp_ref[0, CONV_HALO:CONV_HALO + ts, :] = glu(u_ref[0])
    nrow = CONV_HALO + ts
    for b in range(1, SUBLANES):
        hp_ref[b, 0:nrow - SUBLANES, :] = hp_ref[0, b:nrow - SUBLANES + b, :]
    shift = CONV_HALO - (CONV_WIDTH - 1)
    for rb in range(ts // CONV_ROWS):
        r0 = rb * CONV_ROWS
        acc = jnp.zeros((CONV_ROWS, cc), F32)
        for tap in range(CONV_WIDTH):
            lo = r0 + shift + tap
            base = lo // SUBLANES * SUBLANES
            acc = acc + cw_ref[tap:tap + 1, :] * hp_ref[lo - base, base:base + CONV_ROWS, :]
        y = _layer_norm(acc + cb_ref[...], lg_ref[...], lb_ref[...])
        o_ref[0, r0:r0 + CONV_ROWS, :] = (y * _sigmoid(y)).astype(o_ref.dtype)


def _conv(u3, conv_w, conv_b, ln_g, ln_b):
    bsz, s, cc2 = u3.shape
    cc = cc2 // 2
    ts = TOKEN_TILE
    hb = ts // CONV_HALO
    vec = lambda a: pl.BlockSpec((1, cc), lambda b, j: (0, 0))
    return pl.pallas_call(
        functools.partial(_conv_kernel, ts=ts, cc=cc),
        out_shape=jax.ShapeDtypeStruct((bsz, s, cc), BF16),
        grid=(bsz, s // ts),
        in_specs=[pl.BlockSpec((1, ts, cc2), lambda b, j: (b, j, 0)),
                  pl.BlockSpec((1, CONV_HALO, cc2), lambda b, j: (b, jnp.maximum(j * hb - 1, 0), 0)),
                  pl.BlockSpec((CONV_WIDTH, cc), lambda b, j: (0, 0)),
                  vec(conv_b), vec(ln_g), vec(ln_b)],
        out_specs=pl.BlockSpec((1, ts, cc), lambda b, j: (b, j, 0)),
        scratch_shapes=[pltpu.VMEM((SUBLANES, CONV_HALO + ts, cc), F32)],
        compiler_params=_cparams(("parallel", "parallel")),
        name="conv",
    )(u3, u3, conv_w, conv_b.reshape(1, cc), ln_g.reshape(1, cc), ln_b.reshape(1, cc))


def _gla_kernel(q_ref, k_ref, v_ref, gl_ref, r_ref, ng_ref, o_ref, st_ref, *, nb, nh, dk, dv, gc, nchunks):
    @pl.when(pl.program_id(1) == 0)
    def _():
        st_ref[...] = jnp.zeros_like(st_ref)

    row = lax.broadcasted_iota(I32, (gc, gc), 0)
    col = lax.broadcasted_iota(I32, (gc, gc), 1)
    causal = col <= row
    tri = jnp.where(causal, 1.0, 0.0).astype(BF16)
    scale = dk ** -0.5

    ks = [slice(h * dk, (h + 1) * dk) for h in range(nh)]
    vs = [slice(h * dv, (h + 1) * dv) for h in range(nh)]
    streams = [(bb, h) for bb in range(nb) for h in range(nh)]

    def chunk(c, carry):
        r0 = pl.multiple_of(c * gc, gc)
        rows = pl.ds(r0, gc)
        bs = [_dot_01_f32(tri, gl_ref[bb, rows, :]) for bb in range(nb)]
        q_in, k_in, q_st, k_st, decay, v = [], [], [], [], [], []
        for bb in range(nb):
            b = bs[bb]
            b_last = b[gc - 1:gc, :]
            mid = 0.5 * b_last
            q = q_ref[bb, rows, :] * scale
            k = k_ref[bb, rows, :]
            v.append(v_ref[bb, rows, :].astype(BF16))
            q_in.append((q * jnp.exp(b - mid)).astype(BF16))
            k_in.append((k * jnp.exp(mid - b)).astype(BF16))
            q_st.append((q * jnp.exp(b)).astype(BF16))
            k_st.append((k * jnp.exp(b_last - b)).astype(BF16))
            decay.append(jnp.exp(b_last))
        sts = [st_ref[bb, h] for bb, h in streams]
        scores = [_dot_nt(q_in[bb][:, ks[h]], k_in[bb][:, ks[h]]) for bb, h in streams]
        inter = [_dot_nt(q_st[bb][:, ks[h]], st.astype(BF16)) for (bb, h), st in zip(streams, sts)]
        update = [_dot_tn(v[bb][:, vs[h]], k_st[bb][:, ks[h]]) for bb, h in streams]
        atts = [jnp.where(causal, sc, 0.0).astype(BF16) for sc in scores]
        outs = [_dot(att, v[bb][:, vs[h]]) + it for (bb, h), att, it in zip(streams, atts, inter)]
        for (bb, h), st, up, o in zip(streams, sts, update, outs):
            st_ref[bb, h] = st * decay[bb][:, ks[h]] + up
            r = r_ref[bb, rows, vs[h]].astype(F32)
            o = o * lax.rsqrt(jnp.mean(o * o, axis=-1, keepdims=True) + RMS_EPS) * ng_ref[:, vs[h]]
            o_ref[bb, rows, vs[h]] = (o * (r * _sigmoid(r))).astype(o_ref.dtype)
        return carry

    lax.fori_loop(0, nchunks, chunk, 0)


def _gla(q3, k3, v3, gl3, r3, norm_g):
    bsz, s, kw = q3.shape
    vw = v3.shape[-1]
    nh = GLA_HEADS
    dk, dv = kw // nh, vw // nh
    cb = GLA_BLOCK
    gc = GLA_CHUNK
    nb = GLA_BATCH_PER_STEP
    blk = lambda n: pl.BlockSpec((nb, cb, n), lambda b, j: (b, j, 0))
    return pl.pallas_call(
        functools.partial(_gla_kernel, nb=nb, nh=nh, dk=dk, dv=dv, gc=gc, nchunks=cb // gc),
        out_shape=jax.ShapeDtypeStruct((bsz, s, vw), BF16),
        grid=(bsz // nb, s // cb),
        in_specs=[blk(kw), blk(kw), blk(vw), blk(kw), blk(vw), pl.BlockSpec((1, vw), lambda b, j: (0, 0))],
        out_specs=blk(vw),
        scratch_shapes=[pltpu.VMEM((nb, nh, dv, dk), F32)],
        compiler_params=_cparams(("parallel", "arbitrary")),
        name="gla",
    )(q3, k3, v3, gl3, r3, norm_g.reshape(1, vw))


def _router_weights(w_group, b_group, w_router, b_router):
    d = w_router.shape[0]
    fill = LANES - MOE_EXPERTS - MOE_GROUPS
    wcat = jnp.concatenate([w_router, w_group, jnp.zeros((d, fill), F32)], axis=1)
    w_hi = wcat.astype(BF16)
    w_mid = (wcat - w_hi.astype(F32)).astype(BF16)
    w_lo = (wcat - w_hi.astype(F32) - w_mid.astype(F32)).astype(BF16)
    bcat = jnp.concatenate([b_router, b_group, jnp.zeros((fill,), F32)]).reshape(1, LANES)
    return jnp.concatenate([w_hi, w_mid, w_lo], axis=1), bcat


def _proj_ln_kernel(*refs, n_in, alpha):
    a_refs, w_refs = refs[:n_in], refs[n_in:2 * n_in]
    x_ref, gate_ref, lg_ref, lb_ref, sc_ref, sh_ref, wr_ref, br_ref, o_ref, lo_ref = refs[2 * n_in:]
    mix = _dot(a_refs[0][...], w_refs[0][...])
    for a_ref, w_ref in zip(a_refs[1:], w_refs[1:]):
        mix = mix + _dot(a_ref[...], w_ref[...])
    y = alpha * x_ref[...] + (1.0 + gate_ref[0]) * mix
    x1 = _layer_norm(y, lg_ref[...], lb_ref[...])
    o_ref[...] = x1
    h_hi, h_mid, h_lo = _split3(x1 * (1.0 + sc_ref[0]) + sh_ref[0])
    pa = _dot(h_hi, wr_ref[...])
    pb = _dot(h_mid, wr_ref[:, :2 * LANES])
    pc = _dot(h_lo, wr_ref[:, :LANES])
    small = (pa[:, 2 * LANES:] + pc) + pb[:, LANES:]
    lo_ref[...] = pa[:, :LANES] + ((pa[:, LANES:2 * LANES] + pb[:, :LANES]) + small) + br_ref[...]


def _proj_ln(acts, weights, x2, gate, ln_g, ln_b, sc_f, sh_f, w_route, b_route, tiles_per_b, alpha):
    t, d = x2.shape
    tm = TOKEN_TILE
    n_in = len(acts)
    row = lambda n: pl.BlockSpec((tm, n), lambda i: (i, 0))
    full = lambda a: pl.BlockSpec(a.shape, lambda i: (0,) * a.ndim)
    vec = pl.BlockSpec((1, d), lambda i: (0, 0))
    mod = pl.BlockSpec((1, 1, d), lambda i: (i // tiles_per_b, 0, 0))
    return pl.pallas_call(
        functools.partial(_proj_ln_kernel, n_in=n_in, alpha=alpha),
        out_shape=[jax.ShapeDtypeStruct((t, d), F32), jax.ShapeDtypeStruct((t, LANES), F32)],
        grid=(t // tm,),
        in_specs=[row(a.shape[1]) for a in acts] + [full(w) for w in weights]
                 + [row(d), mod, vec, vec, mod, mod, full(w_route), full(b_route)],
        out_specs=[row(d), row(LANES)],
        compiler_params=_cparams(("parallel",)),
        name="proj_ln",
    )(*acts, *weights, x2, gate, ln_g.reshape(1, d), ln_b.reshape(1, d), sc_f, sh_f, w_route, b_route)


def _mla_in_kernel(x_ref, sc_ref, sh_ref, pos_ref, invf_ref, sign_ref, win_ref, gq_ref, gkv_ref,
                   wqa_ref, wqb_ref, wk_ref, wvt_ref, q_ref, k_ref, vt_ref, *, nh, q_lora, kv_lora, scale, tk):
    h = (x_ref[...] * (1.0 + sc_ref[0]) + sh_ref[0]).astype(BF16)
    u = _dot(h, win_ref[...])
    cq = u[:, :q_lora]
    ckv = u[:, q_lora:q_lora + kv_lora]
    kr = u[:, q_lora + kv_lora:q_lora + kv_lora + LANES]
    kr_sw = u[:, q_lora + kv_lora + LANES:]
    cqn = (cq * lax.rsqrt(jnp.mean(cq * cq, axis=-1, keepdims=True) + RMS_EPS) * gq_ref[...]).astype(BF16)
    kvn = (ckv * lax.rsqrt(jnp.mean(ckv * ckv, axis=-1, keepdims=True) + RMS_EPS) * gkv_ref[...]).astype(BF16)
    ang = pos_ref[...].astype(F32) * invf_ref[...]
    cos = jnp.cos(ang)
    sin = jnp.sin(ang) * sign_ref[...]
    kr_rot = (kr * cos + kr_sw * sin).astype(BF16)
    qa = _dot(cqn, wqa_ref[...])
    qb = _dot(cqn, wqb_ref[...])
    kv = _dot(kvn, wk_ref[...])
    hw = 2 * LANES
    for hd in range(nh):
        q_ref[:, hd * hw:hd * hw + LANES] = (qa[:, hd * hw:hd * hw + LANES] * scale).astype(BF16)
        rope = qa[:, hd * hw + LANES:(hd + 1) * hw] * cos + qb[:, hd * LANES:(hd + 1) * LANES] * sin
        q_ref[:, hd * hw + LANES:(hd + 1) * hw] = (rope * scale).astype(BF16)
        k_ref[:, hd * hw:hd * hw + LANES] = kv[:, hd * LANES:(hd + 1) * LANES].astype(BF16)
        k_ref[:, hd * hw + LANES:(hd + 1) * hw] = kr_rot
    vt = _dot_nt(wvt_ref[...], kvn).astype(BF16)
    for c in range(vt.shape[1] // tk):
        vt_ref[0, c] = vt[:, c * tk:(c + 1) * tk]


def _mla_in(x2, sc, sh, pos2, w_in, gq, gkv, w_uq, w_ukv, tiles_per_b):
    t, d = x2.shape
    nh = MLA_HEADS
    q_lora, kv_lora = gq.shape[0], gkv.shape[0]
    half = MLA_ROPE // 2
    pad = LANES - MLA_ROPE
    kr_w = w_in[:, q_lora + kv_lora:]
    kr_sw = jnp.concatenate([kr_w[:, half:], kr_w[:, :half]], axis=1)
    zpad = jnp.zeros((d, pad), w_in.dtype)
    win_ext = jnp.concatenate([w_in[:, :q_lora + kv_lora], kr_w, zpad, kr_sw, zpad], axis=1).astype(BF16)
    wq = w_uq.reshape(q_lora, nh, MLA_NOPE + MLA_ROPE)
    q_nope, q_rope = wq[:, :, :MLA_NOPE], wq[:, :, MLA_NOPE:]
    q_rope_sw = jnp.concatenate([q_rope[:, :, half:], q_rope[:, :, :half]], axis=2)
    zq = jnp.zeros((q_lora, nh, pad), w_uq.dtype)
    wqa = jnp.concatenate([q_nope, q_rope, zq], axis=2).reshape(q_lora, nh * 2 * LANES).astype(BF16)
    wqb = jnp.concatenate([q_rope_sw, zq], axis=2).reshape(q_lora, nh * LANES).astype(BF16)
    wkv = w_ukv.reshape(kv_lora, nh, MLA_NOPE + MLA_V)
    wk = wkv[:, :, :MLA_NOPE].reshape(kv_lora, nh * MLA_NOPE).astype(BF16)
    wvt = wkv[:, :, MLA_NOPE:].reshape(kv_lora, nh * MLA_V).T.astype(BF16)
    inv_freq = 1.0 / (ROPE_THETA ** (jnp.arange(0, MLA_ROPE, 2, dtype=F32) / MLA_ROPE))
    invf = jnp.concatenate([inv_freq, inv_freq, jnp.zeros((pad,), F32)]).reshape(1, LANES)
    sign = jnp.concatenate([-jnp.ones((half,), F32), jnp.ones((half,), F32), jnp.zeros((pad,), F32)]).reshape(1, LANES)
    tm = TOKEN_TILE
    full = lambda a: pl.BlockSpec(a.shape, lambda i: (0,) * a.ndim)
    row = lambda n: pl.BlockSpec((tm, n), lambda i: (i, 0))
    mod = pl.BlockSpec((1, 1, d), lambda i: (i // tiles_per_b, 0, 0))
    gq2, gkv2 = gq.reshape(1, q_lora), gkv.reshape(1, kv_lora)
    scale = (MLA_NOPE + MLA_ROPE) ** -0.5 * 1.4426950408889634
    tk = ATTN_TK
    kt_per_tile = tm // tk
    s = tiles_per_b * tm
    return pl.pallas_call(
        functools.partial(_mla_in_kernel, nh=nh, q_lora=q_lora, kv_lora=kv_lora, scale=scale, tk=tk),
        out_shape=[jax.ShapeDtypeStruct((t, nh * 2 * LANES), BF16), jax.ShapeDtypeStruct((t, nh * 2 * LANES), BF16),
                   jax.ShapeDtypeStruct((t // s, s // tk, nh * MLA_V, tk), BF16)],
        grid=(t // tm,),
        in_specs=[row(d), mod, mod, row(1), full(invf), full(sign), full(win_ext), full(gq2), full(gkv2),
                  full(wqa), full(wqb), full(wk), full(wvt)],
        out_specs=[row(nh * 2 * LANES), row(nh * 2 * LANES),
                   pl.BlockSpec((1, kt_per_tile, nh * MLA_V, tk),
                                lambda i: (i // tiles_per_b, i % tiles_per_b, 0, 0))],
        compiler_params=_cparams(("parallel",)),
        name="mla_in",
    )(x2, sc, sh, pos2, invf, sign, win_ext, gq2, gkv2, wqa, wqb, wk, wvt)


def _attn_kernel(q_ref, k_ref, vt_ref, o_ref, acc_ref, s0_ref, s1_ref, *, s, tq, tk, hb):
    kpq = tq // tk
    assert kpq == 2, "the pipeline below alternates two score buffers over pairs of key tiles"
    hw = 2 * LANES
    ones = jnp.ones((8, tk), BF16)

    def q_block(qi, carry):
        qrows = pl.ds(pl.multiple_of(qi * tq, tq), tq)
        acc_ref[...] = jnp.zeros_like(acc_ref)

        def scores(j, s_ref, c0=0, nc=tq):
            krows = pl.ds(pl.multiple_of(j * tk, tk), tk)
            cols = pl.ds(pl.multiple_of(qi * tq + c0, tk), nc)
            for h in range(hb):
                s_ref[h, :, c0:c0 + nc] = _dot_nt(k_ref[krows, h * hw:(h + 1) * hw], q_ref[cols, h * hw:(h + 1) * hw])

        def tile(j, s_ref, stats, masked, c0=0, nc=tq):
            def put(full, part):
                pieces = ([full[:, :c0]] if c0 else []) + [part] + ([full[:, c0 + nc:]] if c0 + nc < tq else [])
                return pieces[0] if len(pieces) == 1 else jnp.concatenate(pieces, axis=1)
            ps, alphas, out = [], [], []
            for h in range(hb):
                m, st = stats[2 * h][:, c0:c0 + nc], s_ref[h, :, c0:c0 + nc]
                if masked:
                    key = j * tk + lax.broadcasted_iota(I32, (tk, nc), 0)
                    qry = qi * tq + c0 + lax.broadcasted_iota(I32, (tk, nc), 1)
                    st = jnp.where(key <= qry, st, -jnp.inf)
                m_new = jnp.maximum(m, jnp.max(st, axis=0, keepdims=True))
                ps.append(jnp.exp2(st - m_new).astype(BF16))
                alphas.append(jnp.exp2(m - m_new))
                out.append(put(stats[2 * h], m_new))
            for h in range(hb):
                acc_ref[h, :, c0:c0 + nc] = (alphas[h] * acc_ref[h, :, c0:c0 + nc]
                                             + _dot(vt_ref[0, j, h * MLA_V:(h + 1) * MLA_V, :], ps[h]))
                l_new = alphas[h] * stats[2 * h + 1][:, c0:c0 + nc] + _dot(ones, ps[h])[0:1]
                out.insert(2 * h + 1, put(stats[2 * h + 1], l_new))
            return tuple(out)

        stats = (jnp.full((1, tq), -jnp.inf, F32), jnp.zeros((1, tq), F32)) * hb
        scores(0, s0_ref)

        def pair(jj, c):
            j = 2 * jj
            scores(j + 1, s1_ref)
            c = tile(j, s0_ref, c, False)
            scores(j + 2, s0_ref)
            return tile(j + 1, s1_ref, c, False)

        stats = lax.fori_loop(0, qi, pair, stats)
        scores(2 * qi + 1, s1_ref, tk, tk)
        stats = tile(2 * qi, s0_ref, stats, True, 0, tk)
        stats = tile(2 * qi, s0_ref, stats, False, tk, tk)
        stats = tile(2 * qi + 1, s1_ref, stats, True, tk, tk)
        for h in range(hb):
            o_ref[qrows, h * MLA_V:(h + 1) * MLA_V] = (acc_ref[h] / stats[2 * h + 1]).T.astype(o_ref.dtype)
        return carry

    lax.fori_loop(0, s // tq, q_block, 0)


def _attn(q, k, vt, bsz, s):
    nh = MLA_HEADS
    tq, tk = ATTN_TQ, ATTN_TK
    hb = ATTN_HEADS_PER_STEP
    hw = 2 * LANES
    return pl.pallas_call(
        functools.partial(_attn_kernel, s=s, tq=tq, tk=tk, hb=hb),
        out_shape=jax.ShapeDtypeStruct((bsz * s, nh * MLA_V), BF16),
        grid=(bsz, nh // hb),
        in_specs=[pl.BlockSpec((s, hb * hw), lambda b, h: (b, h)),
                  pl.BlockSpec((s, hb * hw), lambda b, h: (b, h)),
                  pl.BlockSpec((1, s // tk, hb * MLA_V, tk), lambda b, h: (b, 0, h, 0))],
        out_specs=pl.BlockSpec((s, hb * MLA_V), lambda b, h: (b, h)),
        scratch_shapes=[pltpu.VMEM((hb, MLA_V, tq), F32), pltpu.VMEM((hb, tk, tq), F32),
                        pltpu.VMEM((hb, tk, tq), F32)],
        compiler_params=_cparams(("parallel", "parallel")),
        name="attn",
    )(q, k, vt)


def _route_kernel(logits_ref, lp_ref, wt_ref, tm_ref, offs_ref, te_ref, na_ref,
                  upper_ref, carry_ref, *, tm, ne, ng, row_tile, nt_pad):
    i = pl.program_id(0)
    epg = ne // ng

    @pl.when(i == 0)
    def _():
        r = lax.broadcasted_iota(I32, (tm, tm), 0)
        c = lax.broadcasted_iota(I32, (tm, tm), 1)
        upper_ref[...] = jnp.where(r < c, 1.0, 0.0).astype(BF16)
        carry_ref[...] = jnp.zeros_like(carry_ref)

    lt = logits_ref[...].T
    lr = lt[0:ne]
    grow = lax.broadcasted_iota(I32, (8, tm), 0).astype(F32)
    lg = jnp.where(grow < ng, lt[ne:ne + 8], -jnp.inf)
    gmax = jnp.max(lg, axis=0, keepdims=True)
    g_idx = jnp.min(jnp.where(lg == gmax, grow, 1e9), axis=0, keepdims=True)
    g_w = 1.0 / jnp.sum(jnp.exp(lg - gmax), axis=0, keepdims=True)
    erow = lax.broadcasted_iota(I32, (ne, tm), 0).astype(F32)
    in_group = jnp.floor(erow * (1.0 / epg)) == g_idx
    sel = jnp.where(in_group, lr, -jnp.inf)
    v1 = jnp.max(sel, axis=0, keepdims=True)
    i1 = jnp.min(jnp.where(sel == v1, erow, 1e9), axis=0, keepdims=True)
    sel2 = jnp.where(erow == i1, -jnp.inf, sel)
    v2 = jnp.max(sel2, axis=0, keepdims=True)
    i2 = jnp.min(jnp.where(sel2 == v2, erow, 1e9), axis=0, keepdims=True)
    t = jnp.exp(v2 - v1)
    w1 = g_w / (1.0 + t)
    w2 = g_w * t / (1.0 + t)
    oh1 = erow == i1
    oh2 = erow == i2
    member = jnp.where(oh1 | oh2, 1.0, 0.0)
    lcnt = jnp.sum(member, axis=1, keepdims=True)
    lcnt = jnp.floor((lcnt + (RUN_ALIGN - 1)) * (1.0 / RUN_ALIGN)) * RUN_ALIGN
    er_ = lax.broadcasted_iota(I32, (ne, ne), 0)
    ec_ = lax.broadcasted_iota(I32, (ne, ne), 1)
    lstart = jnp.dot(jnp.where(ec_ < er_, 1.0, 0.0).astype(F32), jnp.broadcast_to(lcnt, (ne, LANES)),
                     precision=HIGHEST, preferred_element_type=F32)[:, 0:1]
    lrank = _dot(member.astype(BF16), upper_ref[...]) + lstart
    p1 = jnp.sum(jnp.where(oh1, lrank, 0.0), axis=0, keepdims=True)
    p2 = jnp.sum(jnp.where(oh2, lrank, 0.0), axis=0, keepdims=True)

    orow = lax.broadcasted_iota(I32, (8, tm), 0)
    lp_ref[...] = jnp.where(orow == 0, p1, jnp.where(orow == 1, p2, 0.0)).astype(I32)
    wrow = lax.broadcasted_iota(I32, (LANES, tm), 0)
    wt_ref[...] = jnp.where(wrow == 0, w1, jnp.where(wrow == 1, w2,
                                                     jnp.where(wrow == 2, p1, jnp.where(wrow == 3, p2, 0.0)))).T
    mr = lax.broadcasted_iota(I32, (ne, LANES), 0)
    mc = lax.broadcasted_iota(I32, (ne, LANES), 1)
    to_row = lambda col, lane0: jnp.sum(jnp.where(mr + lane0 == mc, col, 0.0), axis=0, keepdims=True)
    total = jnp.sum(lcnt, axis=0, keepdims=True)
    lane = lax.broadcasted_iota(I32, (1, LANES), 1)
    packed = (to_row(lstart, 0) + to_row(lcnt, ne) + to_row(carry_ref[...], 2 * ne)
              + jnp.where(lane == 3 * ne, total, 0.0))
    trow = lax.broadcasted_iota(I32, (8, LANES), 0)
    tm_ref[...] = jnp.where(trow == 0, packed, 0.0).astype(I32)
    carry_ref[...] = carry_ref[...] + lcnt

    @pl.when(i == pl.num_programs(0) - 1)
    def _():
        cnt = carry_ref[...]
        ntl = jnp.floor((cnt + (row_tile - 1)) * (1.0 / row_tile))
        incl = jnp.where(ec_ <= er_, 1.0, 0.0).astype(F32)
        ends = jnp.dot(incl, jnp.broadcast_to(ntl, (ne, LANES)), precision=HIGHEST,
                       preferred_element_type=F32)
        starts = ends - ntl
        offs_ref[...] = jnp.concatenate([starts * row_tile, ends * row_tile, jnp.broadcast_to(ntl, (ne, LANES)),
                                         jnp.zeros((8, LANES), F32)], axis=0).astype(I32)
        tile = lax.broadcasted_iota(I32, (ne, nt_pad), 1).astype(F32)
        te = jnp.sum(jnp.where(ends[:, 0:1] <= tile, 1.0, 0.0), axis=0, keepdims=True)
        te_ref[...] = jnp.broadcast_to(jnp.minimum(te, ne - 1.0), (8, nt_pad)).astype(I32)
        na_ref[...] = jnp.broadcast_to(ends[ne - 1:ne, :], (8, LANES)).astype(I32)


def _moe_tiles(t):
    rows = 2 * t + (RUN_ALIGN - 1) * MOE_EXPERTS * (t // TOKEN_TILE)
    nt_max = -(-rows // MOE_ROW_TILE) + MOE_EXPERTS
    nt_pad = -(-nt_max // LANES) * LANES
    return nt_max, nt_pad


def _sorted_rows(tm):
    return -(-(2 * tm + (RUN_ALIGN - 1) * MOE_EXPERTS) // LANES) * LANES


def _route(logits):
    t = logits.shape[0]
    ne, ng = MOE_EXPERTS, MOE_GROUPS
    tm = TOKEN_TILE
    _, nt_pad = _moe_tiles(t)
    const = lambda shp: pl.BlockSpec(shp, lambda i: (0,) * len(shp))
    return pl.pallas_call(
        functools.partial(_route_kernel, tm=tm, ne=ne, ng=ng, row_tile=MOE_ROW_TILE, nt_pad=nt_pad),
        out_shape=[jax.ShapeDtypeStruct((8, t), I32), jax.ShapeDtypeStruct((t, LANES), F32),
                   jax.ShapeDtypeStruct((8 * (t // tm), LANES), I32),
                   jax.ShapeDtypeStruct((3 * ne + 8, LANES), I32), jax.ShapeDtypeStruct((8, nt_pad), I32),
                   jax.ShapeDtypeStruct((8, LANES), I32)],
        grid=(t // tm,),
        in_specs=[pl.BlockSpec((tm, LANES), lambda i: (i, 0))],
        out_specs=[pl.BlockSpec((8, tm), lambda i: (0, i)), pl.BlockSpec((tm, LANES), lambda i: (i, 0)),
                   pl.BlockSpec((8, LANES), lambda i: (i, 0)),
                   const((3 * ne + 8, LANES)), const((8, nt_pad)), const((8, LANES))],
        scratch_shapes=[pltpu.VMEM((tm, tm), BF16), pltpu.VMEM((ne, 1), F32)],
        compiler_params=_cparams(("arbitrary",)),
        name="moe_route",
    )(logits)


def _start_runs(tm_ref, tile, ne, copy):
    for e in range(ne):
        lstart = tm_ref[tile, e]
        n = tm_ref[tile, ne + e]
        before = tm_ref[tile, 2 * ne + e]
        _binary_pieces(n, TOKEN_TILE, lambda off, rows: copy(
            e, pl.multiple_of(lstart + off, RUN_ALIGN), pl.multiple_of(before + off, RUN_ALIGN), rows))


def _binary_pieces(n, n_max, copy):
    del n_max
    nbig = n >> (RUN_PIECE.bit_length() - 1)

    def big_piece(c, carry):
        copy(c * RUN_PIECE, RUN_PIECE).start()
        return carry

    lax.fori_loop(0, nbig, big_piece, 0)
    off = nbig * RUN_PIECE
    p = RUN_PIECE // 2
    while p >= RUN_ALIGN:
        @pl.when((n & p) != 0)
        def _():
            copy(off, p).start()
        off = off + (n & p)
        p //= 2


def _dispatch_kernel(offs_ref, gend_ref, ntl_ref, tm_ref, x_ref, sc_ref, sh_ref, lp_ref, wt_ref, xs_ref,
                     h_ref, z_ref, sem_z, sem_r, *, tm, ne, row_tile, nt_max, ntile, srows):
    i = pl.program_id(0)
    tile = i - 1

    @pl.when(i == 0)
    def _():
        z_ref[...] = jnp.zeros_like(z_ref)
        for e in range(ne):
            @pl.when(ntl_ref[e] > 0)
            def _():
                start = pl.multiple_of(gend_ref[e] - row_tile, row_tile)
                cp = pltpu.make_async_copy(z_ref, xs_ref.at[pl.ds(start, row_tile), :], sem_z)
                cp.start()
                cp.wait()
        for back in range(1, nt_max - (2 * tm * ntile) // row_tile + 1):
            @pl.when(nt_max - back >= gend_ref[ne - 1] // row_tile)
            def _():
                cp = pltpu.make_async_copy(z_ref, xs_ref.at[pl.ds((nt_max - back) * row_tile, row_tile), :], sem_z)
                cp.start()
                cp.wait()
        for spill in range(2):
            cp = pltpu.make_async_copy(z_ref, xs_ref.at[pl.ds((nt_max + spill) * row_tile, row_tile), :], sem_z)
            cp.start()
            cp.wait()

    def wait_tile(t):
        slot = t % 2
        pltpu.make_async_copy(h_ref.at[slot], xs_ref.at[pl.ds(0, srows), :], sem_r.at[slot]).wait()

    @pl.when((tile >= 2) & (tile <= ntile))
    def _():
        wait_tile(tile - 2)

    @pl.when(tile == ntile)
    def _():
        wait_tile(tile - 1)

    @pl.when((tile >= 0) & (tile < ntile))
    def _():
        slot = tile % 2
        h = (x_ref[...] * (1.0 + sc_ref[0]) + sh_ref[0]).astype(BF16)
        row = lax.broadcasted_iota(I32, (srows, tm), 0)
        lp = lp_ref[...]
        signed = jnp.where(row == lp[0:1, :], 1.0, jnp.where(row == lp[1:2, :], -1.0, 0.0)).astype(BF16)
        lane = lax.broadcasted_iota(I32, (tm, LANES), 1)
        wt = wt_ref[...]
        terms = []
        for k in range(2):
            w = wt[:, k:k + 1]
            hi = w.astype(BF16).astype(F32)
            mid = (w - hi).astype(BF16).astype(F32)
            terms += [hi, mid, w - hi - mid]
        extra = jnp.where(lane == SIGN_LANE, 1.0, 0.0)
        for idx, term in enumerate(terms):
            extra = jnp.where(lane == idx, term, extra)
        h_ref[slot] = _dot(signed, jnp.concatenate([h, extra.astype(BF16)], axis=1)).astype(BF16)

        def copy(e, local_row, rows_before, rows):
            dst = pl.multiple_of(offs_ref[e] + rows_before, RUN_ALIGN)
            return pltpu.make_async_copy(h_ref.at[slot, pl.ds(local_row, rows), :], xs_ref.at[pl.ds(dst, rows), :],
                                         sem_r.at[slot])

        _start_runs(tm_ref, tile, ne, copy)
        used = tm_ref[tile, 3 * ne]
        _binary_pieces(srows - used, srows - 2 * tm, lambda off, rows: pltpu.make_async_copy(
            h_ref.at[slot, pl.ds(pl.multiple_of(used + off, RUN_ALIGN), rows), :],
            xs_ref.at[pl.ds(pl.multiple_of((nt_max + slot) * row_tile + off, RUN_ALIGN), rows), :], sem_r.at[slot]))


def _dispatch(x2, sc, sh, lp, wt, tmeta, offs, gend, ntl, tiles_per_b):
    t, d = x2.shape
    tm = TOKEN_TILE
    nt_max, _ = _moe_tiles(t)
    ntile = t // tm
    srows = _sorted_rows(tm)
    cur = lambda i: jnp.clip(i - 1, 0, ntile - 1)
    grid_spec = pltpu.PrefetchScalarGridSpec(
        num_scalar_prefetch=4,
        grid=(ntile + 2,),
        in_specs=[pl.BlockSpec((tm, d), lambda i, *_: (cur(i), 0)),
                  pl.BlockSpec((1, 1, d), lambda i, *_: (cur(i) // tiles_per_b, 0, 0)),
                  pl.BlockSpec((1, 1, d), lambda i, *_: (cur(i) // tiles_per_b, 0, 0)),
                  pl.BlockSpec((8, tm), lambda i, *_: (0, cur(i))),
                  pl.BlockSpec((tm, LANES), lambda i, *_: (cur(i), 0))],
        out_specs=pl.BlockSpec(memory_space=pl.ANY),
        scratch_shapes=[pltpu.VMEM((2, srows, d + LANES), BF16), pltpu.VMEM((MOE_ROW_TILE, d + LANES), BF16),
                        pltpu.SemaphoreType.DMA, pltpu.SemaphoreType.DMA((2,))])
    return pl.pallas_call(
        functools.partial(_dispatch_kernel, tm=tm, ne=MOE_EXPERTS, row_tile=MOE_ROW_TILE, nt_max=nt_max,
                          ntile=ntile, srows=srows),
        out_shape=jax.ShapeDtypeStruct(((nt_max + 2) * MOE_ROW_TILE, d + LANES), BF16),
        grid_spec=grid_spec,
        compiler_params=_cparams(("arbitrary",)),
        name="moe_dispatch",
    )(offs, gend, ntl, tmeta, x2, sc, sh, lp, wt)


def _gmm_kernel(te_ref, na_ref, xs_ref, wg_ref, wu_ref, wd_ref, ys_ref, wgu_buf, wd_buf, *, ff, tr):
    j = pl.program_id(0)

    @pl.when(j < na_ref[0])
    def _():
        changed = (j == 0) | (te_ref[j] != te_ref[jnp.maximum(j - 1, 0)])

        @pl.when(changed)
        def _():
            wgu_buf[:, :ff] = wg_ref[0, 0].astype(BF16)
            wgu_buf[:, ff:] = wu_ref[0, 0].astype(BF16)
            wd_buf[...] = wd_ref[0, 0].astype(BF16)

        halves = [slice(c * (tr // 2), (c + 1) * (tr // 2)) for c in range(2)]
        d = wgu_buf.shape[0]
        riders = [xs_ref[rows, d:].astype(F32) for rows in halves]
        signs = [r[:, SIGN_LANE:SIGN_LANE + 1] for r in riders]
        gus = [_dot(xs_ref[rows, :d] * sg.astype(BF16), wgu_buf[...]) for rows, sg in zip(halves, signs)]
        hids = [(gu[:, :ff] * _sigmoid(gu[:, :ff]) * gu[:, ff:]).astype(BF16) for gu in gus]
        for rows, hid, r, sg in zip(halves, hids, riders, signs):
            w = jnp.where(sg > 0.0, r[:, 0:1] + r[:, 1:2] + r[:, 2:3], -(r[:, 3:4] + r[:, 4:5] + r[:, 5:6]))
            ys_ref[rows, :] = (w * _dot(hid, wd_buf[...])).astype(ys_ref.dtype)

    @pl.when(j >= na_ref[0])
    def _():
        ys_ref[...] = jnp.zeros_like(ys_ref)


def _gmm(xs, te, na, w_gate, w_up, w_down, layer):
    d, ff = w_gate.shape[-2:]
    tr = MOE_ROW_TILE
    ns = te.shape[0] * tr
    act = lambda j, te_ref, na_ref: jnp.minimum(j, na_ref[0] - 1)
    grid_spec = pltpu.PrefetchScalarGridSpec(
        num_scalar_prefetch=2,
        grid=(ns // tr,),
        in_specs=[pl.BlockSpec((tr, d + LANES), lambda j, te_ref, na_ref: (act(j, te_ref, na_ref), 0)),
                  pl.BlockSpec((1, 1, d, ff), lambda j, te_ref, na_ref: (layer, te_ref[act(j, te_ref, na_ref)], 0, 0)),
                  pl.BlockSpec((1, 1, d, ff), lambda j, te_ref, na_ref: (layer, te_ref[act(j, te_ref, na_ref)], 0, 0)),
                  pl.BlockSpec((1, 1, ff, d), lambda j, te_ref, na_ref: (layer, te_ref[act(j, te_ref, na_ref)], 0, 0))],
        out_specs=pl.BlockSpec((tr, d), lambda j, te_ref, na_ref: (j, 0)),
        scratch_shapes=[pltpu.VMEM((d, 2 * ff), BF16), pltpu.VMEM((ff, d), BF16)])
    return pl.pallas_call(
        functools.partial(_gmm_kernel, ff=ff, tr=tr),
        out_shape=jax.ShapeDtypeStruct((ns, d), BF16),
        grid_spec=grid_spec,
        compiler_params=_cparams(("arbitrary",)),
        name="moe_gmm",
    )(te, na, xs, w_gate, w_up, w_down)


def _combine_kernel(offs_ref, tm_ref, wt_ref, x_ref, gate_ref, lg_ref, lb_ref, ys_ref, o_ref,
                    buf_ref, sem_r, *, tm, ne, ntile, srows, alpha):
    i = pl.program_id(0)

    def fetch(tile):
        slot = tile % 2

        def copy(e, local_row, rows_before, rows):
            src = pl.multiple_of(offs_ref[e] + rows_before, RUN_ALIGN)
            return pltpu.make_async_copy(ys_ref.at[pl.ds(src, rows), :], buf_ref.at[slot, pl.ds(local_row, rows), :],
                                         sem_r.at[slot])

        _start_runs(tm_ref, tile, ne, copy)
        used = tm_ref[tile, 3 * ne]
        _binary_pieces(srows - used, srows - 2 * tm, lambda off, rows: pltpu.make_async_copy(
            ys_ref.at[pl.ds(pl.multiple_of(off, RUN_ALIGN), rows), :],
            buf_ref.at[slot, pl.ds(pl.multiple_of(used + off, RUN_ALIGN), rows), :], sem_r.at[slot]))

    def wait(tile):
        slot = tile % 2
        pltpu.make_async_copy(ys_ref.at[pl.ds(0, srows), :], buf_ref.at[slot], sem_r.at[slot]).wait()

    def finish(tile):
        w = wt_ref[...]
        rows = buf_ref[tile % 2]
        col = lax.broadcasted_iota(I32, (tm, srows), 1).astype(F32)
        both = jnp.where(col == w[:, 2:3], 1.0, jnp.where(col == w[:, 3:4], 1.0, 0.0)).astype(BF16)
        ffn = _dot(both, rows)
        y = alpha * x_ref[...] + (1.0 + gate_ref[0]) * ffn
        o_ref[...] = _layer_norm(y, lg_ref[...], lb_ref[...])

    @pl.when(i == 0)
    def _():
        fetch(i)

    @pl.when((i > 0) & (i < ntile))
    def _():
        wait(i - 1)
        fetch(i)
        finish(i - 1)

    @pl.when(i == ntile)
    def _():
        wait(i - 1)
        finish(i - 1)


def _combine(ys, tmeta, wt, x2, gate, ln_g, ln_b, offs, tiles_per_b, alpha):
    t, d = x2.shape
    tm = TOKEN_TILE
    ntile = t // tm
    srows = _sorted_rows(tm)
    prev = lambda i: jnp.maximum(i - 1, 0)
    vec = pl.BlockSpec((1, d), lambda i, *_: (0, 0))
    grid_spec = pltpu.PrefetchScalarGridSpec(
        num_scalar_prefetch=2,
        grid=(ntile + 1,),
        in_specs=[pl.BlockSpec((tm, LANES), lambda i, *_: (prev(i), 0)),
                  pl.BlockSpec((tm, d), lambda i, *_: (prev(i), 0)),
                  pl.BlockSpec((1, 1, d), lambda i, *_: (prev(i) // tiles_per_b, 0, 0)),
                  vec, vec,
                  pl.BlockSpec(memory_space=pl.ANY)],
        out_specs=pl.BlockSpec((tm, d), lambda i, *_: (prev(i), 0)),
        scratch_shapes=[pltpu.VMEM((2, srows, d), BF16), pltpu.SemaphoreType.DMA((2,))])
    return pl.pallas_call(
        functools.partial(_combine_kernel, tm=tm, ne=MOE_EXPERTS, ntile=ntile, srows=srows, alpha=alpha),
        out_shape=jax.ShapeDtypeStruct((t, d), F32),
        grid_spec=grid_spec,
        compiler_params=_cparams(("arbitrary",)),
        name="moe_combine",
    )(offs, tmeta, wt, x2, gate, ln_g.reshape(1, d), ln_b.reshape(1, d), ys)


def _moe_block(x2, logits, sc, sh, gate, ln_g, ln_b, w_gate, w_up, w_down, layer, tiles_per_b, alpha):
    ne = MOE_EXPERTS
    nt_max, _ = _moe_tiles(x2.shape[0])
    lp, wt, tmeta, meta, te, na = _route(logits)
    offs, gend, ntl = meta[:ne, 0], meta[ne:2 * ne, 0], meta[2 * ne:3 * ne, 0]
    tmeta = tmeta.reshape(-1, SUBLANES, LANES)[:, 0, :]
    xs = _dispatch(x2, sc, sh, lp, wt, tmeta, offs, gend, ntl, tiles_per_b)
    ys = _gmm(xs, te[0, :nt_max], na[0, :1], w_gate, w_up, w_down, layer)
    return _combine(ys, tmeta, wt, x2, gate, ln_g, ln_b, offs, tiles_per_b, alpha)


def kernel(x, c, positions, ada_w, ada_b, ln_mix_g, ln_mix_b, ln_ffn_g, ln_ffn_b, ab_w_in, conv_w, conv_b, conv_ln_g, conv_ln_b, gla_gate_w, gla_gate_b, gla_norm_g, ab_w_out, mla_w_in, mla_q_norm_g, mla_kv_norm_g, mla_w_uq, mla_w_ukv, mla_w_out, moe_w_group, moe_b_group, moe_w_router, moe_b_router, moe_w_gate, moe_w_up, moe_w_down):
    bsz, s, d = x.shape
    depth = ada_w.shape[0]
    t = bsz * s
    tiles_per_b = s // TOKEN_TILE
    alpha = (2 * depth) ** 0.25
    mod = _ada(c, ada_w, ada_b).reshape(depth, bsz, 6, 1, d)
    x2 = x.reshape(t, d)
    for layer in range(depth):
        sh_m, sc_m, g_m, sh_f, sc_f, g_f = (mod[layer, :, n] for n in range(6))
        i = layer // 2
        if layer % 2 == 0:
            uc, q, k, v, r, gl = _ab_in(x2, sc_m, sh_m, ab_w_in[i], gla_gate_w[i], gla_gate_b[i], tiles_per_b)
            y_a = _conv(uc.reshape(bsz, s, -1), conv_w[i], conv_b[i], conv_ln_g[i], conv_ln_b[i])
            b3 = lambda a: a.reshape(bsz, s, -1)
            y_b = _gla(b3(q), b3(k), b3(v), b3(gl), b3(r), gla_norm_g[i])
            w_out = ab_w_out[i].astype(BF16)
            cc = y_a.shape[-1]
            acts = [y_a.reshape(t, cc), y_b.reshape(t, -1)]
            weights = [w_out[:cc], w_out[cc:]]
        else:
            qc, kc, vv = _mla_in(x2, sc_m, sh_m, positions.reshape(t, 1), mla_w_in[i], mla_q_norm_g[i],
                                 mla_kv_norm_g[i], mla_w_uq[i], mla_w_ukv[i], tiles_per_b)
            acts = [_attn(qc, kc, vv, bsz, s)]
            weights = [mla_w_out[i].astype(BF16)]
        w_route, b_route = _router_weights(moe_w_group[layer], moe_b_group[layer], moe_w_router[layer],
                                           moe_b_router[layer])
        x2, logits = _proj_ln(acts, weights, x2, g_m, ln_mix_g[layer], ln_mix_b[layer], sc_f, sh_f, w_route, b_route,
                              tiles_per_b, alpha)
        x2 = _moe_block(x2, logits, sc_f, sh_f, g_f, ln_ffn_g[layer], ln_ffn_b[layer], moe_w_gate, moe_w_up,
                        moe_w_down, layer, tiles_per_b, alpha)
    return x2.reshape(bsz, s, d)
```

```python
import functools

import jax
import jax.numpy as jnp
from jax import lax
from jax.experimental import pallas as pl
from jax.experimental.pallas import tpu as pltpu

F32 = jnp.float32
BF16 = jnp.bfloat16
I32 = jnp.int32
HIGHEST = lax.Precision.HIGHEST

LN_EPS = 1e-5
RMS_EPS = 1e-6
CONV_WIDTH = 31
GLA_HEADS = 4
GLA_GATE_TAU = 16.0
MLA_HEADS = 8
MLA_NOPE = 128
MLA_ROPE = 64
MLA_V = 128
ROPE_THETA = 10000.0
MOE_GROUPS = 4
MOE_EXPERTS_PER_GROUP = 8
MOE_EXPERTS = MOE_GROUPS * MOE_EXPERTS_PER_GROUP

LANES = 128
SUBLANES = 8
TOKEN_TILE = 512
GLA_CHUNK = 128
GLA_BLOCK = 512
GLA_BATCH_PER_STEP = 4
CONV_ROWS = 32
CONV_HALO = 32
ATTN_TQ = 512
ATTN_TK = 256
ATTN_HEADS_PER_STEP = 4
MOE_ROW_TILE = 512
RUN_PIECE = 64
RUN_ALIGN = 16
SIGN_LANE = 6
VMEM_LIMIT = 48 * 1024 * 1024


def _cparams(sem):
    return pltpu.CompilerParams(dimension_semantics=sem, vmem_limit_bytes=VMEM_LIMIT)


def _sigmoid(x):
    return 1.0 / (1.0 + jnp.exp(-x))


def _dot(a, b):
    return jnp.dot(a, b, preferred_element_type=F32)


def _dot_nt(a, b):
    return lax.dot_general(a, b, (((1,), (1,)), ((), ())), preferred_element_type=F32)


def _dot_tn(a, b):
    return lax.dot_general(a, b, (((0,), (0,)), ((), ())), preferred_element_type=F32)


def _split3(x):
    hi = x.astype(BF16)
    r1 = x - hi.astype(F32)
    mid = r1.astype(BF16)
    lo = (r1 - mid.astype(F32)).astype(BF16)
    return hi, mid, lo


def _dot_01_f32(a01, x):
    hi, mid, lo = _split3(x)
    return _dot(a01, hi) + (_dot(a01, mid) + _dot(a01, lo))


def _layer_norm(y, g, b):
    mu = jnp.mean(y, axis=-1, keepdims=True)
    d = y - mu
    var = jnp.mean(d * d, axis=-1, keepdims=True)
    return d * lax.rsqrt(var + LN_EPS) * g + b


def _ada_kernel(c_ref, w_ref, b_ref, o_ref):
    c = c_ref[...]
    o_ref[0] = jnp.dot(c * _sigmoid(c), w_ref[0], precision=HIGHEST, preferred_element_type=F32) + b_ref[0]


def _ada(c, ada_w, ada_b):
    depth, d, n = ada_w.shape
    bsz = c.shape[0]
    tn = 1536
    return pl.pallas_call(
        _ada_kernel,
        out_shape=jax.ShapeDtypeStruct((depth, bsz, n), F32),
        grid=(depth, n // tn),
        in_specs=[pl.BlockSpec((bsz, d), lambda l, j: (0, 0)),
                  pl.BlockSpec((1, d, tn), lambda l, j: (l, 0, j)),
                  pl.BlockSpec((1, 1, tn), lambda l, j: (l, 0, j))],
        out_specs=pl.BlockSpec((1, bsz, tn), lambda l, j: (l, 0, j)),
        compiler_params=_cparams(("parallel", "parallel")),
        name="ada",
    )(c, ada_w, ada_b.reshape(depth, 1, n))


def _ab_in_kernel(x_ref, sc_ref, sh_ref, wc_ref, wq_ref, wk_ref, wv_ref, wr_ref, wg_ref, gw_ref, gb_ref,
                  uc_ref, q_ref, k_ref, v_ref, r_ref, gl_ref):
    h = (x_ref[...] * (1.0 + sc_ref[0]) + sh_ref[0]).astype(BF16)
    uc_ref[...] = _dot(h, wc_ref[...]).astype(uc_ref.dtype)
    q_ref[...] = _dot(h, wq_ref[...])
    k_ref[...] = _dot(h, wk_ref[...])
    v_ref[...] = _dot(h, wv_ref[...]).astype(v_ref.dtype)
    r_ref[...] = _dot(h, wr_ref[...]).astype(r_ref.dtype)
    g_low = _dot(h, wg_ref[...])
    z = jnp.dot(g_low, gw_ref[...], precision=HIGHEST, preferred_element_type=F32) + gb_ref[...]
    gl_ref[...] = (jnp.minimum(z, 0.0) - jnp.log(1.0 + jnp.exp(-jnp.abs(z)))) * (1.0 / GLA_GATE_TAU)


def _ab_in(x2, sc, sh, w_in, gate_w, gate_b, tiles_per_b):
    t, d = x2.shape
    cc2 = d
    kw = d // 4
    vw = d // 2
    rank = gate_w.shape[0]
    splits = [cc2, cc2 + kw, cc2 + 2 * kw, cc2 + 2 * kw + vw, cc2 + 2 * kw + 2 * vw]
    wb = w_in.astype(BF16)
    ws = [wb[:, :splits[0]], wb[:, splits[0]:splits[1]], wb[:, splits[1]:splits[2]],
          wb[:, splits[2]:splits[3]], wb[:, splits[3]:splits[4]], wb[:, splits[4]:]]
    tm = TOKEN_TILE
    full = lambda a: pl.BlockSpec(a.shape, lambda i: (0,) * a.ndim)
    row = lambda n: pl.BlockSpec((tm, n), lambda i: (i, 0))
    mod = pl.BlockSpec((1, 1, d), lambda i: (i // tiles_per_b, 0, 0))
    gb2 = gate_b.reshape(1, kw)
    widths = [cc2, kw, kw, vw, vw, kw]
    return pl.pallas_call(
        _ab_in_kernel,
        out_shape=[jax.ShapeDtypeStruct((t, n), BF16 if idx in (0, 3, 4) else F32) for idx, n in enumerate(widths)],
        grid=(t // tm,),
        in_specs=[row(d), mod, mod] + [full(w) for w in ws] + [full(gate_w), full(gb2)],
        out_specs=[row(n) for n in widths],
        compiler_params=_cparams(("parallel",)),
        name="ab_in",
    )(x2, sc, sh, *ws, gate_w, gb2)


def _conv_kernel(u_ref, halo_ref, cw_ref, cb_ref, lg_ref, lb_ref, o_ref, hp_ref, wb_ref, *, ts, cc):
    j = pl.program_id(1)

    def glu(u):
        u = u.astype(F32)
        return u[:, :cc] * _sigmoid(u[:, cc:])

    hp_ref[0, 0:CONV_HALO, :] = jnp.where(j > 0, glu(halo_ref[0]), 0.0)
    hp_ref[0, CONV_HALO:CONV_HALO + ts, :] = glu(u_ref[0])
    nrow = CONV_HALO + ts
    hp0 = hp_ref[0]
    for b in range(1, SUBLANES):
        hp_ref[b] = pltpu.roll(hp0, nrow - b, axis=0)
    for tap in range(CONV_WIDTH):
        wb_ref[tap] = jnp.broadcast_to(cw_ref[tap:tap + 1, :], (SUBLANES, cc))
    shift = CONV_HALO - (CONV_WIDTH - 1)
    for rb in range(ts // CONV_ROWS):
        r0 = rb * CONV_ROWS
        acc = jnp.zeros((CONV_ROWS, cc), F32)
        for tap in range(CONV_WIDTH):
            lo = r0 + shift + tap
            base = lo // SUBLANES * SUBLANES
            w_rows = jnp.concatenate([wb_ref[tap]] * (CONV_ROWS // SUBLANES), axis=0)
            acc = acc + w_rows * hp_ref[lo - base, base:base + CONV_ROWS, :]
        y = _layer_norm(acc + cb_ref[...], lg_ref[...], lb_ref[...])
        o_ref[0, r0:r0 + CONV_ROWS, :] = (y * _sigmoid(y)).astype(o_ref.dtype)


def _conv(u3, conv_w, conv_b, ln_g, ln_b):
    bsz, s, cc2 = u3.shape
    cc = cc2 // 2
    ts = TOKEN_TILE
    hb = ts // CONV_HALO
    vec = lambda a: pl.BlockSpec((1, cc), lambda b, j: (0, 0))
    return pl.pallas_call(
        functools.partial(_conv_kernel, ts=ts, cc=cc),
        out_shape=jax.ShapeDtypeStruct((bsz, s, cc), BF16),
        grid=(bsz, s // ts),
        in_specs=[pl.BlockSpec((1, ts, cc2), lambda b, j: (b, j, 0)),
                  pl.BlockSpec((1, CONV_HALO, cc2), lambda b, j: (b, jnp.maximum(j * hb - 1, 0), 0)),
                  pl.BlockSpec((CONV_WIDTH, cc), lambda b, j: (0, 0)),
                  vec(conv_b), vec(ln_g), vec(ln_b)],
        out_specs=pl.BlockSpec((1, ts, cc), lambda b, j: (b, j, 0)),
        scratch_shapes=[pltpu.VMEM((SUBLANES, CONV_HALO + ts, cc), F32), pltpu.VMEM((CONV_WIDTH, SUBLANES, cc), F32)],
        compiler_params=_cparams(("parallel", "parallel")),
        name="conv",
    )(u3, u3, conv_w, conv_b.reshape(1, cc), ln_g.reshape(1, cc), ln_b.reshape(1, cc))


def _gla_kernel(q_ref, k_ref, v_ref, gl_ref, r_ref, ng_ref, o_ref, st_ref, *, nb, nh, dk, dv, gc, nchunks):
    @pl.when(pl.program_id(1) == 0)
    def _():
        st_ref[...] = jnp.zeros_like(st_ref)

    row = lax.broadcasted_iota(I32, (gc, gc), 0)
    col = lax.broadcasted_iota(I32, (gc, gc), 1)
    causal = col <= row
    tri = jnp.where(causal, 1.0, 0.0).astype(BF16)
    scale = dk ** -0.5

    ks = [slice(h * dk, (h + 1) * dk) for h in range(nh)]
    vs = [slice(h * dv, (h + 1) * dv) for h in range(nh)]
    streams = [(bb, h) for bb in range(nb) for h in range(nh)]

    def chunk(c, carry):
        r0 = pl.multiple_of(c * gc, gc)
        rows = pl.ds(r0, gc)
        bs = [_dot_01_f32(tri, gl_ref[bb, rows, :]) for bb in range(nb)]
        q_in, k_in, q_st, k_st, decay, v = [], [], [], [], [], []
        for bb in range(nb):
            b = bs[bb]
            b_last = b[gc - 1:gc, :]
            mid = 0.5 * b_last
            q = q_ref[bb, rows, :] * scale
            k = k_ref[bb, rows, :]
            v.append(v_ref[bb, rows, :].astype(BF16))
            q_in.append((q * jnp.exp(b - mid)).astype(BF16))
            k_in.append((k * jnp.exp(mid - b)).astype(BF16))
            q_st.append((q * jnp.exp(b)).astype(BF16))
            k_st.append((k * jnp.exp(b_last - b)).astype(BF16))
            decay.append(jnp.exp(b_last))
        sts = [st_ref[bb, h] for bb, h in streams]
        scores = [_dot_nt(q_in[bb][:, ks[h]], k_in[bb][:, ks[h]]) for bb, h in streams]
        inter = [_dot_nt(q_st[bb][:, ks[h]], st.astype(BF16)) for (bb, h), st in zip(streams, sts)]
        update = [_dot_tn(v[bb][:, vs[h]], k_st[bb][:, ks[h]]) for bb, h in streams]
        atts = [jnp.where(causal, sc, 0.0).astype(BF16) for sc in scores]
        outs = [_dot(att, v[bb][:, vs[h]]) + it for (bb, h), att, it in zip(streams, atts, inter)]
        for (bb, h), st, up, o in zip(streams, sts, update, outs):
            st_ref[bb, h] = st * decay[bb][:, ks[h]] + up
            r = r_ref[bb, rows, vs[h]].astype(F32)
            o = o * lax.rsqrt(jnp.mean(o * o, axis=-1, keepdims=True) + RMS_EPS) * ng_ref[:, vs[h]]
            o_ref[bb, rows, vs[h]] = (o * (r * _sigmoid(r))).astype(o_ref.dtype)
        return carry

    lax.fori_loop(0, nchunks, chunk, 0)


def _gla(q3, k3, v3, gl3, r3, norm_g):
    bsz, s, kw = q3.shape
    vw = v3.shape[-1]
    nh = GLA_HEADS
    dk, dv = kw // nh, vw // nh
    cb = GLA_BLOCK
    gc = GLA_CHUNK
    nb = GLA_BATCH_PER_STEP
    blk = lambda n: pl.BlockSpec((nb, cb, n), lambda b, j: (b, j, 0))
    return pl.pallas_call(
        functools.partial(_gla_kernel, nb=nb, nh=nh, dk=dk, dv=dv, gc=gc, nchunks=cb // gc),
        out_shape=jax.ShapeDtypeStruct((bsz, s, vw), BF16),
        grid=(bsz // nb, s // cb),
        in_specs=[blk(kw), blk(kw), blk(vw), blk(kw), blk(vw), pl.BlockSpec((1, vw), lambda b, j: (0, 0))],
        out_specs=blk(vw),
        scratch_shapes=[pltpu.VMEM((nb, nh, dv, dk), F32)],
        compiler_params=_cparams(("parallel", "arbitrary")),
        name="gla",
    )(q3, k3, v3, gl3, r3, norm_g.reshape(1, vw))


def _router_weights(w_group, b_group, w_router, b_router):
    d = w_router.shape[0]
    fill = LANES - MOE_EXPERTS - MOE_GROUPS
    wcat = jnp.concatenate([w_router, w_group, jnp.zeros((d, fill), F32)], axis=1)
    w_hi = wcat.astype(BF16)
    w_mid = (wcat - w_hi.astype(F32)).astype(BF16)
    w_lo = (wcat - w_hi.astype(F32) - w_mid.astype(F32)).astype(BF16)
    bcat = jnp.concatenate([b_router, b_group, jnp.zeros((fill,), F32)]).reshape(1, LANES)
    return jnp.concatenate([w_hi, w_mid, w_lo], axis=1), bcat


def _proj_ln_kernel(*refs, n_in, alpha):
    a_refs, w_refs = refs[:n_in], refs[n_in:2 * n_in]
    x_ref, gate_ref, lg_ref, lb_ref, sc_ref, sh_ref, wr_ref, br_ref, o_ref, lo_ref = refs[2 * n_in:]
    mix = _dot(a_refs[0][...], w_refs[0][...])
    for a_ref, w_ref in zip(a_refs[1:], w_refs[1:]):
        mix = mix + _dot(a_ref[...], w_ref[...])
    y = alpha * x_ref[...] + (1.0 + gate_ref[0]) * mix
    x1 = _layer_norm(y, lg_ref[...], lb_ref[...])
    o_ref[...] = x1
    h_hi, h_mid, h_lo = _split3(x1 * (1.0 + sc_ref[0]) + sh_ref[0])
    pa = _dot(h_hi, wr_ref[...])
    pb = _dot(h_mid, wr_ref[:, :2 * LANES])
    pc = _dot(h_lo, wr_ref[:, :LANES])
    small = (pa[:, 2 * LANES:] + pc) + pb[:, LANES:]
    lo_ref[...] = pa[:, :LANES] + ((pa[:, LANES:2 * LANES] + pb[:, :LANES]) + small) + br_ref[...]


def _proj_ln(acts, weights, x2, gate, ln_g, ln_b, sc_f, sh_f, w_route, b_route, tiles_per_b, alpha):
    t, d = x2.shape
    tm = TOKEN_TILE
    n_in = len(acts)
    row = lambda n: pl.BlockSpec((tm, n), lambda i: (i, 0))
    full = lambda a: pl.BlockSpec(a.shape, lambda i: (0,) * a.ndim)
    vec = pl.BlockSpec((1, d), lambda i: (0, 0))
    mod = pl.BlockSpec((1, 1, d), lambda i: (i // tiles_per_b, 0, 0))
    return pl.pallas_call(
        functools.partial(_proj_ln_kernel, n_in=n_in, alpha=alpha),
        out_shape=[jax.ShapeDtypeStruct((t, d), F32), jax.ShapeDtypeStruct((t, LANES), F32)],
        grid=(t // tm,),
        in_specs=[row(a.shape[1]) for a in acts] + [full(w) for w in weights]
                 + [row(d), mod, vec, vec, mod, mod, full(w_route), full(b_route)],
        out_specs=[row(d), row(LANES)],
        compiler_params=_cparams(("parallel",)),
        name="proj_ln",
    )(*acts, *weights, x2, gate, ln_g.reshape(1, d), ln_b.reshape(1, d), sc_f, sh_f, w_route, b_route)


def _mla_in_kernel(x_ref, sc_ref, sh_ref, pos_ref, invf_ref, sign_ref, win_ref, gq_ref, gkv_ref,
                   wqa_ref, wqb_ref, wk_ref, wvt_ref, q_ref, k_ref, vt_ref, *, nh, q_lora, kv_lora, scale, tk):
    h = (x_ref[...] * (1.0 + sc_ref[0]) + sh_ref[0]).astype(BF16)
    u = _dot(h, win_ref[...])
    cq = u[:, :q_lora]
    ckv = u[:, q_lora:q_lora + kv_lora]
    kr = u[:, q_lora + kv_lora:q_lora + kv_lora + LANES]
    kr_sw = u[:, q_lora + kv_lora + LANES:]
    cqn = (cq * lax.rsqrt(jnp.mean(cq * cq, axis=-1, keepdims=True) + RMS_EPS) * gq_ref[...]).astype(BF16)
    kvn = (ckv * lax.rsqrt(jnp.mean(ckv * ckv, axis=-1, keepdims=True) + RMS_EPS) * gkv_ref[...]).astype(BF16)
    ang = pos_ref[...].astype(F32) * invf_ref[...]
    cos = jnp.cos(ang)
    sin = jnp.sin(ang) * sign_ref[...]
    kr_rot = (kr * cos + kr_sw * sin).astype(BF16)
    qa = _dot(cqn, wqa_ref[...])
    qb = _dot(cqn, wqb_ref[...])
    kv = _dot(kvn, wk_ref[...])
    hw = 2 * LANES
    for hd in range(nh):
        q_ref[:, hd * hw:hd * hw + LANES] = (qa[:, hd * hw:hd * hw + LANES] * scale).astype(BF16)
        rope = qa[:, hd * hw + LANES:(hd + 1) * hw] * cos + qb[:, hd * LANES:(hd + 1) * LANES] * sin
        q_ref[:, hd * hw + LANES:(hd + 1) * hw] = (rope * scale).astype(BF16)
        k_ref[:, hd * hw:hd * hw + LANES] = kv[:, hd * LANES:(hd + 1) * LANES].astype(BF16)
        k_ref[:, hd * hw + LANES:(hd + 1) * hw] = kr_rot
    vt = _dot_nt(wvt_ref[...], kvn).astype(BF16)
    for c in range(vt.shape[1] // tk):
        vt_ref[0, c] = vt[:, c * tk:(c + 1) * tk]


def _mla_in(x2, sc, sh, pos2, w_in, gq, gkv, w_uq, w_ukv, tiles_per_b):
    t, d = x2.shape
    nh = MLA_HEADS
    q_lora, kv_lora = gq.shape[0], gkv.shape[0]
    half = MLA_ROPE // 2
    pad = LANES - MLA_ROPE
    kr_w = w_in[:, q_lora + kv_lora:]
    kr_sw = jnp.concatenate([kr_w[:, half:], kr_w[:, :half]], axis=1)
    zpad = jnp.zeros((d, pad), w_in.dtype)
    win_ext = jnp.concatenate([w_in[:, :q_lora + kv_lora], kr_w, zpad, kr_sw, zpad], axis=1).astype(BF16)
    wq = w_uq.reshape(q_lora, nh, MLA_NOPE + MLA_ROPE)
    q_nope, q_rope = wq[:, :, :MLA_NOPE], wq[:, :, MLA_NOPE:]
    q_rope_sw = jnp.concatenate([q_rope[:, :, half:], q_rope[:, :, :half]], axis=2)
    zq = jnp.zeros((q_lora, nh, pad), w_uq.dtype)
    wqa = jnp.concatenate([q_nope, q_rope, zq], axis=2).reshape(q_lora, nh * 2 * LANES).astype(BF16)
    wqb = jnp.concatenate([q_rope_sw, zq], axis=2).reshape(q_lora, nh * LANES).astype(BF16)
    wkv = w_ukv.reshape(kv_lora, nh, MLA_NOPE + MLA_V)
    wk = wkv[:, :, :MLA_NOPE].reshape(kv_lora, nh * MLA_NOPE).astype(BF16)
    wvt = wkv[:, :, MLA_NOPE:].reshape(kv_lora, nh * MLA_V).T.astype(BF16)
    inv_freq = 1.0 / (ROPE_THETA ** (jnp.arange(0, MLA_ROPE, 2, dtype=F32) / MLA_ROPE))
    invf = jnp.concatenate([inv_freq, inv_freq, jnp.zeros((pad,), F32)]).reshape(1, LANES)
    sign = jnp.concatenate([-jnp.ones((half,), F32), jnp.ones((half,), F32), jnp.zeros((pad,), F32)]).reshape(1, LANES)
    tm = TOKEN_TILE
    full = lambda a: pl.BlockSpec(a.shape, lambda i: (0,) * a.ndim)
    row = lambda n: pl.BlockSpec((tm, n), lambda i: (i, 0))
    mod = pl.BlockSpec((1, 1, d), lambda i: (i // tiles_per_b, 0, 0))
    gq2, gkv2 = gq.reshape(1, q_lora), gkv.reshape(1, kv_lora)
    scale = (MLA_NOPE + MLA_ROPE) ** -0.5 * 1.4426950408889634
    tk = ATTN_TK
    kt_per_tile = tm // tk
    s = tiles_per_b * tm
    return pl.pallas_call(
        functools.partial(_mla_in_kernel, nh=nh, q_lora=q_lora, kv_lora=kv_lora, scale=scale, tk=tk),
        out_shape=[jax.ShapeDtypeStruct((t, nh * 2 * LANES), BF16), jax.ShapeDtypeStruct((t, nh * 2 * LANES), BF16),
                   jax.ShapeDtypeStruct((t // s, s // tk, nh * MLA_V, tk), BF16)],
        grid=(t // tm,),
        in_specs=[row(d), mod, mod, row(1), full(invf), full(sign), full(win_ext), full(gq2), full(gkv2),
                  full(wqa), full(wqb), full(wk), full(wvt)],
        out_specs=[row(nh * 2 * LANES), row(nh * 2 * LANES),
                   pl.BlockSpec((1, kt_per_tile, nh * MLA_V, tk),
                                lambda i: (i // tiles_per_b, i % tiles_per_b, 0, 0))],
        compiler_params=_cparams(("parallel",)),
        name="mla_in",
    )(x2, sc, sh, pos2, invf, sign, win_ext, gq2, gkv2, wqa, wqb, wk, wvt)


def _attn_kernel(q_ref, k_ref, vt_ref, o_ref, acc_ref, s0_ref, s1_ref, *, s, tq, tk, hb):
    kpq = tq // tk
    assert kpq == 2, "the pipeline below alternates two score buffers over pairs of key tiles"
    hw = 2 * LANES
    ones = jnp.ones((8, tk), BF16)

    def q_block(qi, carry):
        qrows = pl.ds(pl.multiple_of(qi * tq, tq), tq)
        acc_ref[...] = jnp.zeros_like(acc_ref)

        def scores(j, s_ref, c0=0, nc=tq):
            krows = pl.ds(pl.multiple_of(j * tk, tk), tk)
            cols = pl.ds(pl.multiple_of(qi * tq + c0, tk), nc)
            for h in range(hb):
                s_ref[h, :, c0:c0 + nc] = _dot_nt(k_ref[krows, h * hw:(h + 1) * hw], q_ref[cols, h * hw:(h + 1) * hw])

        def tile(j, s_ref, stats, masked, c0=0, nc=tq):
            def put(full, part):
                pieces = ([full[:, :c0]] if c0 else []) + [part] + ([full[:, c0 + nc:]] if c0 + nc < tq else [])
                return pieces[0] if len(pieces) == 1 else jnp.concatenate(pieces, axis=1)
            ps, alphas, out = [], [], []
            for h in range(hb):
                m, st = stats[2 * h][:, c0:c0 + nc], s_ref[h, :, c0:c0 + nc]
                if masked:
                    key = j * tk + lax.broadcasted_iota(I32, (tk, nc), 0)
                    qry = qi * tq + c0 + lax.broadcasted_iota(I32, (tk, nc), 1)
                    st = jnp.where(key <= qry, st, -jnp.inf)
                m_new = jnp.maximum(m, jnp.max(st, axis=0, keepdims=True))
                ps.append(jnp.exp2(st - m_new).astype(BF16))
                alphas.append(jnp.exp2(m - m_new))
                out.append(put(stats[2 * h], m_new))
            for h in range(hb):
                acc_ref[h, :, c0:c0 + nc] = (alphas[h] * acc_ref[h, :, c0:c0 + nc]
                                             + _dot(vt_ref[0, j, h * MLA_V:(h + 1) * MLA_V, :], ps[h]))
                l_new = alphas[h] * stats[2 * h + 1][:, c0:c0 + nc] + _dot(ones, ps[h])[0:1]
                out.insert(2 * h + 1, put(stats[2 * h + 1], l_new))
            return tuple(out)

        stats = (jnp.full((1, tq), -jnp.inf, F32), jnp.zeros((1, tq), F32)) * hb
        scores(0, s0_ref)

        def pair(jj, c):
            j = 2 * jj
            scores(j + 1, s1_ref)
            c = tile(j, s0_ref, c, False)
            scores(j + 2, s0_ref)
            return tile(j + 1, s1_ref, c, False)

        stats = lax.fori_loop(0, qi, pair, stats)
        scores(2 * qi + 1, s1_ref, tk, tk)
        stats = tile(2 * qi, s0_ref, stats, True, 0, tk)
        stats = tile(2 * qi, s0_ref, stats, False, tk, tk)
        stats = tile(2 * qi + 1, s1_ref, stats, True, tk, tk)
        for h in range(hb):
            o_ref[qrows, h * MLA_V:(h + 1) * MLA_V] = (acc_ref[h] / stats[2 * h + 1]).T.astype(o_ref.dtype)
        return carry

    lax.fori_loop(0, s // tq, q_block, 0)


def _attn(q, k, vt, bsz, s):
    nh = MLA_HEADS
    tq, tk = ATTN_TQ, ATTN_TK
    hb = ATTN_HEADS_PER_STEP
    hw = 2 * LANES
    return pl.pallas_call(
        functools.partial(_attn_kernel, s=s, tq=tq, tk=tk, hb=hb),
        out_shape=jax.ShapeDtypeStruct((bsz * s, nh * MLA_V), BF16),
        grid=(bsz, nh // hb),
        in_specs=[pl.BlockSpec((s, hb * hw), lambda b, h: (b, h)),
                  pl.BlockSpec((s, hb * hw), lambda b, h: (b, h)),
                  pl.BlockSpec((1, s // tk, hb * MLA_V, tk), lambda b, h: (b, 0, h, 0))],
        out_specs=pl.BlockSpec((s, hb * MLA_V), lambda b, h: (b, h)),
        scratch_shapes=[pltpu.VMEM((hb, MLA_V, tq), F32), pltpu.VMEM((hb, tk, tq), F32),
                        pltpu.VMEM((hb, tk, tq), F32)],
        compiler_params=_cparams(("parallel", "parallel")),
        name="attn",
    )(q, k, vt)


def _route_kernel(logits_ref, lp_ref, wt_ref, tm_ref, offs_ref, te_ref, na_ref,
                  upper_ref, carry_ref, *, tm, ne, ng, row_tile, nt_pad):
    i = pl.program_id(0)
    epg = ne // ng

    @pl.when(i == 0)
    def _():
        r = lax.broadcasted_iota(I32, (tm, tm), 0)
        c = lax.broadcasted_iota(I32, (tm, tm), 1)
        upper_ref[...] = jnp.where(r < c, 1.0, 0.0).astype(BF16)
        carry_ref[...] = jnp.zeros_like(carry_ref)

    lt = logits_ref[...].T
    lr = lt[0:ne]
    grow = lax.broadcasted_iota(I32, (8, tm), 0).astype(F32)
    lg = jnp.where(grow < ng, lt[ne:ne + 8], -jnp.inf)
    gmax = jnp.max(lg, axis=0, keepdims=True)
    g_idx = jnp.min(jnp.where(lg == gmax, grow, 1e9), axis=0, keepdims=True)
    g_w = 1.0 / jnp.sum(jnp.exp(lg - gmax), axis=0, keepdims=True)
    erow = lax.broadcasted_iota(I32, (ne, tm), 0).astype(F32)
    in_group = jnp.floor(erow * (1.0 / epg)) == g_idx
    sel = jnp.where(in_group, lr, -jnp.inf)
    v1 = jnp.max(sel, axis=0, keepdims=True)
    i1 = jnp.min(jnp.where(sel == v1, erow, 1e9), axis=0, keepdims=True)
    sel2 = jnp.where(erow == i1, -jnp.inf, sel)
    v2 = jnp.max(sel2, axis=0, keepdims=True)
    i2 = jnp.min(jnp.where(sel2 == v2, erow, 1e9), axis=0, keepdims=True)
    t = jnp.exp(v2 - v1)
    w1 = g_w / (1.0 + t)
    w2 = g_w * t / (1.0 + t)
    oh1 = erow == i1
    oh2 = erow == i2
    member = jnp.where(oh1 | oh2, 1.0, 0.0)
    lcnt = jnp.sum(member, axis=1, keepdims=True)
    lcnt = jnp.floor((lcnt + (RUN_ALIGN - 1)) * (1.0 / RUN_ALIGN)) * RUN_ALIGN
    er_ = lax.broadcasted_iota(I32, (ne, ne), 0)
    ec_ = lax.broadcasted_iota(I32, (ne, ne), 1)
    lstart = jnp.dot(jnp.where(ec_ < er_, 1.0, 0.0).astype(F32), jnp.broadcast_to(lcnt, (ne, LANES)),
                     precision=HIGHEST, preferred_element_type=F32)[:, 0:1]
    lrank = _dot(member.astype(BF16), upper_ref[...]) + lstart
    p1 = jnp.sum(jnp.where(oh1, lrank, 0.0), axis=0, keepdims=True)
    p2 = jnp.sum(jnp.where(oh2, lrank, 0.0), axis=0, keepdims=True)

    orow = lax.broadcasted_iota(I32, (8, tm), 0)
    lp_ref[...] = jnp.where(orow == 0, p1, jnp.where(orow == 1, p2, 0.0)).astype(I32)
    wrow = lax.broadcasted_iota(I32, (LANES, tm), 0)
    wt_ref[...] = jnp.where(wrow == 0, w1, jnp.where(wrow == 1, w2,
                                                     jnp.where(wrow == 2, p1, jnp.where(wrow == 3, p2, 0.0)))).T
    mr = lax.broadcasted_iota(I32, (ne, LANES), 0)
    mc = lax.broadcasted_iota(I32, (ne, LANES), 1)
    to_row = lambda col, lane0: jnp.sum(jnp.where(mr + lane0 == mc, col, 0.0), axis=0, keepdims=True)
    total = jnp.sum(lcnt, axis=0, keepdims=True)
    lane = lax.broadcasted_iota(I32, (1, LANES), 1)
    packed = (to_row(lstart, 0) + to_row(lcnt, ne) + to_row(carry_ref[...], 2 * ne)
              + jnp.where(lane == 3 * ne, total, 0.0))
    trow = lax.broadcasted_iota(I32, (8, LANES), 0)
    tm_ref[...] = jnp.where(trow == 0, packed, 0.0).astype(I32)
    carry_ref[...] = carry_ref[...] + lcnt

    @pl.when(i == pl.num_programs(0) - 1)
    def _():
        cnt = carry_ref[...]
        ntl = jnp.floor((cnt + (row_tile - 1)) * (1.0 / row_tile))
        incl = jnp.where(ec_ <= er_, 1.0, 0.0).astype(F32)
        ends = jnp.dot(incl, jnp.broadcast_to(ntl, (ne, LANES)), precision=HIGHEST,
                       preferred_element_type=F32)
        starts = ends - ntl
        offs_ref[...] = jnp.concatenate([starts * row_tile, ends * row_tile, jnp.broadcast_to(ntl, (ne, LANES)),
                                         jnp.zeros((8, LANES), F32)], axis=0).astype(I32)
        tile = lax.broadcasted_iota(I32, (ne, nt_pad), 1).astype(F32)
        te = jnp.sum(jnp.where(ends[:, 0:1] <= tile, 1.0, 0.0), axis=0, keepdims=True)
        te_ref[...] = jnp.broadcast_to(jnp.minimum(te, ne - 1.0), (8, nt_pad)).astype(I32)
        na_ref[...] = jnp.broadcast_to(ends[ne - 1:ne, :], (8, LANES)).astype(I32)


def _moe_tiles(t):
    rows = 2 * t + (RUN_ALIGN - 1) * MOE_EXPERTS * (t // TOKEN_TILE)
    nt_max = -(-rows // MOE_ROW_TILE) + MOE_EXPERTS
    nt_pad = -(-nt_max // LANES) * LANES
    return nt_max, nt_pad


def _sorted_rows(tm):
    return -(-(2 * tm + (RUN_ALIGN - 1) * MOE_EXPERTS) // LANES) * LANES


def _route(logits):
    t = logits.shape[0]
    ne, ng = MOE_EXPERTS, MOE_GROUPS
    tm = TOKEN_TILE
    _, nt_pad = _moe_tiles(t)
    const = lambda shp: pl.BlockSpec(shp, lambda i: (0,) * len(shp))
    return pl.pallas_call(
        functools.partial(_route_kernel, tm=tm, ne=ne, ng=ng, row_tile=MOE_ROW_TILE, nt_pad=nt_pad),
        out_shape=[jax.ShapeDtypeStruct((8, t), I32), jax.ShapeDtypeStruct((t, LANES), F32),
                   jax.ShapeDtypeStruct((8 * (t // tm), LANES), I32),
                   jax.ShapeDtypeStruct((3 * ne + 8, LANES), I32), jax.ShapeDtypeStruct((8, nt_pad), I32),
                   jax.ShapeDtypeStruct((8, LANES), I32)],
        grid=(t // tm,),
        in_specs=[pl.BlockSpec((tm, LANES), lambda i: (i, 0))],
        out_specs=[pl.BlockSpec((8, tm), lambda i: (0, i)), pl.BlockSpec((tm, LANES), lambda i: (i, 0)),
                   pl.BlockSpec((8, LANES), lambda i: (i, 0)),
                   const((3 * ne + 8, LANES)), const((8, nt_pad)), const((8, LANES))],
        scratch_shapes=[pltpu.VMEM((tm, tm), BF16), pltpu.VMEM((ne, 1), F32)],
        compiler_params=_cparams(("arbitrary",)),
        name="moe_route",
    )(logits)


def _start_runs(tm_ref, tile, ne, copy):
    for e in range(ne):
        lstart = tm_ref[tile, e]
        n = tm_ref[tile, ne + e]
        before = tm_ref[tile, 2 * ne + e]
        _binary_pieces(n, TOKEN_TILE, lambda off, rows: copy(
            e, pl.multiple_of(lstart + off, RUN_ALIGN), pl.multiple_of(before + off, RUN_ALIGN), rows))


def _binary_pieces(n, n_max, copy):
    del n_max
    nbig = n >> (RUN_PIECE.bit_length() - 1)

    def big_piece(c, carry):
        copy(c * RUN_PIECE, RUN_PIECE).start()
        return carry

    lax.fori_loop(0, nbig, big_piece, 0)
    off = nbig * RUN_PIECE
    p = RUN_PIECE // 2
    while p >= RUN_ALIGN:
        @pl.when((n & p) != 0)
        def _():
            copy(off, p).start()
        off = off + (n & p)
        p //= 2


def _dispatch_kernel(offs_ref, gend_ref, ntl_ref, tm_ref, x_ref, sc_ref, sh_ref, lp_ref, wt_ref, xs_ref,
                     h_ref, z_ref, sem_z, sem_r, *, tm, ne, row_tile, nt_max, ntile, srows):
    i = pl.program_id(0)
    tile = i - 1

    @pl.when(i == 0)
    def _():
        z_ref[...] = jnp.zeros_like(z_ref)
        for e in range(ne):
            @pl.when(ntl_ref[e] > 0)
            def _():
                start = pl.multiple_of(gend_ref[e] - row_tile, row_tile)
                cp = pltpu.make_async_copy(z_ref, xs_ref.at[pl.ds(start, row_tile), :], sem_z)
                cp.start()
                cp.wait()
        for back in range(1, nt_max - (2 * tm * ntile) // row_tile + 1):
            @pl.when(nt_max - back >= gend_ref[ne - 1] // row_tile)
            def _():
                cp = pltpu.make_async_copy(z_ref, xs_ref.at[pl.ds((nt_max - back) * row_tile, row_tile), :], sem_z)
                cp.start()
                cp.wait()
        for spill in range(2):
            cp = pltpu.make_async_copy(z_ref, xs_ref.at[pl.ds((nt_max + spill) * row_tile, row_tile), :], sem_z)
            cp.start()
            cp.wait()

    def wait_tile(t):
        slot = t % 2
        pltpu.make_async_copy(h_ref.at[slot], xs_ref.at[pl.ds(0, srows), :], sem_r.at[slot]).wait()

    @pl.when((tile >= 2) & (tile <= ntile))
    def _():
        wait_tile(tile - 2)

    @pl.when(tile == ntile)
    def _():
        wait_tile(tile - 1)

    @pl.when((tile >= 0) & (tile < ntile))
    def _():
        slot = tile % 2
        h = (x_ref[...] * (1.0 + sc_ref[0]) + sh_ref[0]).astype(BF16)
        row = lax.broadcasted_iota(I32, (srows, tm), 0)
        lp = lp_ref[...]
        signed = jnp.where(row == lp[0:1, :], 1.0, jnp.where(row == lp[1:2, :], -1.0, 0.0)).astype(BF16)
        lane = lax.broadcasted_iota(I32, (tm, LANES), 1)
        wt = wt_ref[...]
        terms = []
        for k in range(2):
            w = wt[:, k:k + 1]
            hi = w.astype(BF16).astype(F32)
            mid = (w - hi).astype(BF16).astype(F32)
            terms += [hi, mid, w - hi - mid]
        extra = jnp.where(lane == SIGN_LANE, 1.0, 0.0)
        for idx, term in enumerate(terms):
            extra = jnp.where(lane == idx, term, extra)
        h_ref[slot] = _dot(signed, jnp.concatenate([h, extra.astype(BF16)], axis=1)).astype(BF16)

        def copy(e, local_row, rows_before, rows):
            dst = pl.multiple_of(offs_ref[e] + rows_before, RUN_ALIGN)
            return pltpu.make_async_copy(h_ref.at[slot, pl.ds(local_row, rows), :], xs_ref.at[pl.ds(dst, rows), :],
                                         sem_r.at[slot])

        _start_runs(tm_ref, tile, ne, copy)
        used = tm_ref[tile, 3 * ne]
        _binary_pieces(srows - used, srows - 2 * tm, lambda off, rows: pltpu.make_async_copy(
            h_ref.at[slot, pl.ds(pl.multiple_of(used + off, RUN_ALIGN), rows), :],
            xs_ref.at[pl.ds(pl.multiple_of((nt_max + slot) * row_tile + off, RUN_ALIGN), rows), :], sem_r.at[slot]))


def _dispatch(x2, sc, sh, lp, wt, tmeta, offs, gend, ntl, tiles_per_b):
    t, d = x2.shape
    tm = TOKEN_TILE
    nt_max, _ = _moe_tiles(t)
    ntile = t // tm
    srows = _sorted_rows(tm)
    cur = lambda i: jnp.clip(i - 1, 0, ntile - 1)
    grid_spec = pltpu.PrefetchScalarGridSpec(
        num_scalar_prefetch=4,
        grid=(ntile + 2,),
        in_specs=[pl.BlockSpec((tm, d), lambda i, *_: (cur(i), 0)),
                  pl.BlockSpec((1, 1, d), lambda i, *_: (cur(i) // tiles_per_b, 0, 0)),
                  pl.BlockSpec((1, 1, d), lambda i, *_: (cur(i) // tiles_per_b, 0, 0)),
                  pl.BlockSpec((8, tm), lambda i, *_: (0, cur(i))),
                  pl.BlockSpec((tm, LANES), lambda i, *_: (cur(i), 0))],
        out_specs=pl.BlockSpec(memory_space=pl.ANY),
        scratch_shapes=[pltpu.VMEM((2, srows, d + LANES), BF16), pltpu.VMEM((MOE_ROW_TILE, d + LANES), BF16),
                        pltpu.SemaphoreType.DMA, pltpu.SemaphoreType.DMA((2,))])
    return pl.pallas_call(
        functools.partial(_dispatch_kernel, tm=tm, ne=MOE_EXPERTS, row_tile=MOE_ROW_TILE, nt_max=nt_max,
                          ntile=ntile, srows=srows),
        out_shape=jax.ShapeDtypeStruct(((nt_max + 2) * MOE_ROW_TILE, d + LANES), BF16),
        grid_spec=grid_spec,
        compiler_params=_cparams(("arbitrary",)),
        name="moe_dispatch",
    )(offs, gend, ntl, tmeta, x2, sc, sh, lp, wt)


def _gmm_kernel(te_ref, na_ref, xs_ref, wg_ref, wu_ref, wd_ref, ys_ref, wgu_buf, wd_buf, *, ff, tr):
    j = pl.program_id(0)

    @pl.when(j < na_ref[0])
    def _():
        changed = (j == 0) | (te_ref[j] != te_ref[jnp.maximum(j - 1, 0)])

        @pl.when(changed)
        def _():
            wgu_buf[:, :ff] = wg_ref[0, 0].astype(BF16)
            wgu_buf[:, ff:] = wu_ref[0, 0].astype(BF16)
            wd_buf[...] = wd_ref[0, 0].astype(BF16)

        halves = [slice(c * (tr // 2), (c + 1) * (tr // 2)) for c in range(2)]
        d = wgu_buf.shape[0]
        riders = [xs_ref[rows, d:].astype(F32) for rows in halves]
        signs = [r[:, SIGN_LANE:SIGN_LANE + 1] for r in riders]
        gus = [_dot(xs_ref[rows, :d] * sg.astype(BF16), wgu_buf[...]) for rows, sg in zip(halves, signs)]
        hids = [(gu[:, :ff] * _sigmoid(gu[:, :ff]) * gu[:, ff:]).astype(BF16) for gu in gus]
        for rows, hid, r, sg in zip(halves, hids, riders, signs):
            w = jnp.where(sg > 0.0, r[:, 0:1] + r[:, 1:2] + r[:, 2:3], -(r[:, 3:4] + r[:, 4:5] + r[:, 5:6]))
            ys_ref[rows, :] = (w * _dot(hid, wd_buf[...])).astype(ys_ref.dtype)

    @pl.when(j >= na_ref[0])
    def _():
        ys_ref[...] = jnp.zeros_like(ys_ref)


def _gmm(xs, te, na, w_gate, w_up, w_down, layer):
    d, ff = w_gate.shape[-2:]
    tr = MOE_ROW_TILE
    ns = te.shape[0] * tr
    act = lambda j, te_ref, na_ref: jnp.minimum(j, na_ref[0] - 1)
    grid_spec = pltpu.PrefetchScalarGridSpec(
        num_scalar_prefetch=2,
        grid=(ns // tr,),
        in_specs=[pl.BlockSpec((tr, d + LANES), lambda j, te_ref, na_ref: (act(j, te_ref, na_ref), 0)),
                  pl.BlockSpec((1, 1, d, ff), lambda j, te_ref, na_ref: (layer, te_ref[act(j, te_ref, na_ref)], 0, 0)),
                  pl.BlockSpec((1, 1, d, ff), lambda j, te_ref, na_ref: (layer, te_ref[act(j, te_ref, na_ref)], 0, 0)),
                  pl.BlockSpec((1, 1, ff, d), lambda j, te_ref, na_ref: (layer, te_ref[act(j, te_ref, na_ref)], 0, 0))],
        out_specs=pl.BlockSpec((tr, d), lambda j, te_ref, na_ref: (j, 0)),
        scratch_shapes=[pltpu.VMEM((d, 2 * ff), BF16), pltpu.VMEM((ff, d), BF16)])
    return pl.pallas_call(
        functools.partial(_gmm_kernel, ff=ff, tr=tr),
        out_shape=jax.ShapeDtypeStruct((ns, d), BF16),
        grid_spec=grid_spec,
        compiler_params=_cparams(("arbitrary",)),
        name="moe_gmm",
    )(te, na, xs, w_gate, w_up, w_down)


def _combine_kernel(offs_ref, tm_ref, wt_ref, x_ref, gate_ref, lg_ref, lb_ref, ys_ref, o_ref,
                    buf_ref, sem_r, *, tm, ne, ntile, srows, alpha):
    i = pl.program_id(0)

    def fetch(tile):
        slot = tile % 2

        def copy(e, local_row, rows_before, rows):
            src = pl.multiple_of(offs_ref[e] + rows_before, RUN_ALIGN)
            return pltpu.make_async_copy(ys_ref.at[pl.ds(src, rows), :], buf_ref.at[slot, pl.ds(local_row, rows), :],
                                         sem_r.at[slot])

        _start_runs(tm_ref, tile, ne, copy)
        used = tm_ref[tile, 3 * ne]
        _binary_pieces(srows - used, srows - 2 * tm, lambda off, rows: pltpu.make_async_copy(
            ys_ref.at[pl.ds(pl.multiple_of(off, RUN_ALIGN), rows), :],
            buf_ref.at[slot, pl.ds(pl.multiple_of(used + off, RUN_ALIGN), rows), :], sem_r.at[slot]))

    def wait(tile):
        slot = tile % 2
        pltpu.make_async_copy(ys_ref.at[pl.ds(0, srows), :], buf_ref.at[slot], sem_r.at[slot]).wait()

    def finish(tile):
        w = wt_ref[...]
        rows = buf_ref[tile % 2]
        col = lax.broadcasted_iota(I32, (tm, srows), 1).astype(F32)
        both = jnp.where(col == w[:, 2:3], 1.0, jnp.where(col == w[:, 3:4], 1.0, 0.0)).astype(BF16)
        ffn = _dot(both, rows)
        y = alpha * x_ref[...] + (1.0 + gate_ref[0]) * ffn
        o_ref[...] = _layer_norm(y, lg_ref[...], lb_ref[...])

    @pl.when(i == 0)
    def _():
        fetch(i)

    @pl.when((i > 0) & (i < ntile))
    def _():
        wait(i - 1)
        fetch(i)
        finish(i - 1)

    @pl.when(i == ntile)
    def _():
        wait(i - 1)
        finish(i - 1)


def _combine(ys, tmeta, wt, x2, gate, ln_g, ln_b, offs, tiles_per_b, alpha):
    t, d = x2.shape
    tm = TOKEN_TILE
    ntile = t // tm
    srows = _sorted_rows(tm)
    prev = lambda i: jnp.maximum(i - 1, 0)
    vec = pl.BlockSpec((1, d), lambda i, *_: (0, 0))
    grid_spec = pltpu.PrefetchScalarGridSpec(
        num_scalar_prefetch=2,
        grid=(ntile + 1,),
        in_specs=[pl.BlockSpec((tm, LANES), lambda i, *_: (prev(i), 0)),
                  pl.BlockSpec((tm, d), lambda i, *_: (prev(i), 0)),
                  pl.BlockSpec((1, 1, d), lambda i, *_: (prev(i) // tiles_per_b, 0, 0)),
                  vec, vec,
                  pl.BlockSpec(memory_space=pl.ANY)],
        out_specs=pl.BlockSpec((tm, d), lambda i, *_: (prev(i), 0)),
        scratch_shapes=[pltpu.VMEM((2, srows, d), BF16), pltpu.SemaphoreType.DMA((2,))])
    return pl.pallas_call(
        functools.partial(_combine_kernel, tm=tm, ne=MOE_EXPERTS, ntile=ntile, srows=srows, alpha=alpha),
        out_shape=jax.ShapeDtypeStruct((t, d), F32),
        grid_spec=grid_spec,
        compiler_params=_cparams(("arbitrary",)),
        name="moe_combine",
    )(offs, tmeta, wt, x2, gate, ln_g.reshape(1, d), ln_b.reshape(1, d), ys)


def _moe_block(x2, logits, sc, sh, gate, ln_g, ln_b, w_gate, w_up, w_down, layer, tiles_per_b, alpha):
    ne = MOE_EXPERTS
    nt_max, _ = _moe_tiles(x2.shape[0])
    lp, wt, tmeta, meta, te, na = _route(logits)
    offs, gend, ntl = meta[:ne, 0], meta[ne:2 * ne, 0], meta[2 * ne:3 * ne, 0]
    tmeta = tmeta.reshape(-1, SUBLANES, LANES)[:, 0, :]
    xs = _dispatch(x2, sc, sh, lp, wt, tmeta, offs, gend, ntl, tiles_per_b)
    ys = _gmm(xs, te[0, :nt_max], na[0, :1], w_gate, w_up, w_down, layer)
    return _combine(ys, tmeta, wt, x2, gate, ln_g, ln_b, offs, tiles_per_b, alpha)


def kernel(x, c, positions, ada_w, ada_b, ln_mix_g, ln_mix_b, ln_ffn_g, ln_ffn_b, ab_w_in, conv_w, conv_b, conv_ln_g, conv_ln_b, gla_gate_w, gla_gate_b, gla_norm_g, ab_w_out, mla_w_in, mla_q_norm_g, mla_kv_norm_g, mla_w_uq, mla_w_ukv, mla_w_out, moe_w_group, moe_b_group, moe_w_router, moe_b_router, moe_w_gate, moe_w_up, moe_w_down):
    bsz, s, d = x.shape
    depth = ada_w.shape[0]
    t = bsz * s
    tiles_per_b = s // TOKEN_TILE
    alpha = (2 * depth) ** 0.25
    mod = _ada(c, ada_w, ada_b).reshape(depth, bsz, 6, 1, d)
    x2 = x.reshape(t, d)
    for layer in range(depth):
        sh_m, sc_m, g_m, sh_f, sc_f, g_f = (mod[layer, :, n] for n in range(6))
        i = layer // 2
        if layer % 2 == 0:
            uc, q, k, v, r, gl = _ab_in(x2, sc_m, sh_m, ab_w_in[i], gla_gate_w[i], gla_gate_b[i], tiles_per_b)
            y_a = _conv(uc.reshape(bsz, s, -1), conv_w[i], conv_b[i], conv_ln_g[i], conv_ln_b[i])
            b3 = lambda a: a.reshape(bsz, s, -1)
            y_b = _gla(b3(q), b3(k), b3(v), b3(gl), b3(r), gla_norm_g[i])
            w_out = ab_w_out[i].astype(BF16)
            cc = y_a.shape[-1]
            acts = [y_a.reshape(t, cc), y_b.reshape(t, -1)]
            weights = [w_out[:cc], w_out[cc:]]
        else:
            qc, kc, vv = _mla_in(x2, sc_m, sh_m, positions.reshape(t, 1), mla_w_in[i], mla_q_norm_g[i],
                                 mla_kv_norm_g[i], mla_w_uq[i], mla_w_ukv[i], tiles_per_b)
            acts = [_attn(qc, kc, vv, bsz, s)]
            weights = [mla_w_out[i].astype(BF16)]
        w_route, b_route = _router_weights(moe_w_group[layer], moe_b_group[layer], moe_w_router[layer],
                                           moe_b_router[layer])
        x2, logits = _proj_ln(acts, weights, x2, g_m, ln_mix_g[layer], ln_mix_b[layer], sc_f, sh_f, w_route, b_route,
                              tiles_per_b, alpha)
        x2 = _moe_block(x2, logits, sc_f, sh_f, g_f, ln_ffn_g[layer], ln_ffn_b[layer], moe_w_gate, moe_w_up,
                        moe_w_down, layer, tiles_per_b, alpha)
    return x2.reshape(bsz, s, d)
```

```python
import functools

import jax
import jax.numpy as jnp
from jax import lax
from jax.experimental import pallas as pl
from jax.experimental.pallas import tpu as pltpu

F32 = jnp.float32
BF16 = jnp.bfloat16
I32 = jnp.int32
HIGHEST = lax.Precision.HIGHEST

LN_EPS = 1e-5
RMS_EPS = 1e-6
CONV_WIDTH = 31
GLA_HEADS = 4
GLA_GATE_TAU = 16.0
MLA_HEADS = 8
MLA_NOPE = 128
MLA_ROPE = 64
MLA_V = 128
ROPE_THETA = 10000.0
MOE_GROUPS = 4
MOE_EXPERTS_PER_GROUP = 8
MOE_EXPERTS = MOE_GROUPS * MOE_EXPERTS_PER_GROUP

LANES = 128
SUBLANES = 8
TOKEN_TILE = 512
GLA_CHUNK = 128
GLA_BLOCK = 512
GLA_BATCH_PER_STEP = 4
CONV_ROWS = 32
CONV_HALO = 32
ATTN_TQ = 512
ATTN_TK = 256
ATTN_HEADS_PER_STEP = 4
MOE_ROW_TILE = 512
RUN_PIECE = 64
RUN_ALIGN = 16
SIGN_LANE = 6
VMEM_LIMIT = 48 * 1024 * 1024


def _cparams(sem):
    return pltpu.CompilerParams(dimension_semantics=sem, vmem_limit_bytes=VMEM_LIMIT)


def _sigmoid(x):
    return 1.0 / (1.0 + jnp.exp(-x))


def _dot(a, b):
    return jnp.dot(a, b, preferred_element_type=F32)


def _dot_nt(a, b):
    return lax.dot_general(a, b, (((1,), (1,)), ((), ())), preferred_element_type=F32)


def _dot_tn(a, b):
    return lax.dot_general(a, b, (((0,), (0,)), ((), ())), preferred_element_type=F32)


def _split3(x):
    hi = x.astype(BF16)
    r1 = x - hi.astype(F32)
    mid = r1.astype(BF16)
    lo = (r1 - mid.astype(F32)).astype(BF16)
    return hi, mid, lo


def _dot_01_f32(a01, x):
    hi, mid, lo = _split3(x)
    return _dot(a01, hi) + (_dot(a01, mid) + _dot(a01, lo))


def _layer_norm(y, g, b):
    mu = jnp.mean(y, axis=-1, keepdims=True)
    d = y - mu
    var = jnp.mean(d * d, axis=-1, keepdims=True)
    return d * lax.rsqrt(var + LN_EPS) * g + b


def _ada_kernel(c_ref, w_ref, b_ref, o_ref):
    c = c_ref[...]
    o_ref[0] = jnp.dot(c * _sigmoid(c), w_ref[0], precision=HIGHEST, preferred_element_type=F32) + b_ref[0]


def _ada(c, ada_w, ada_b):
    depth, d, n = ada_w.shape
    bsz = c.shape[0]
    tn = 1536
    return pl.pallas_call(
        _ada_kernel,
        out_shape=jax.ShapeDtypeStruct((depth, bsz, n), F32),
        grid=(depth, n // tn),
        in_specs=[pl.BlockSpec((bsz, d), lambda l, j: (0, 0)),
                  pl.BlockSpec((1, d, tn), lambda l, j: (l, 0, j)),
                  pl.BlockSpec((1, 1, tn), lambda l, j: (l, 0, j))],
        out_specs=pl.BlockSpec((1, bsz, tn), lambda l, j: (l, 0, j)),
        compiler_params=_cparams(("parallel", "parallel")),
        name="ada",
    )(c, ada_w, ada_b.reshape(depth, 1, n))


def _ab_in_kernel(x_ref, sc_ref, sh_ref, wc_ref, wq_ref, wk_ref, wv_ref, wr_ref, wg_ref, gw_ref, gb_ref,
                  uc_ref, q_ref, k_ref, v_ref, r_ref, gl_ref):
    h = (x_ref[...] * (1.0 + sc_ref[0]) + sh_ref[0]).astype(BF16)
    uc_ref[...] = _dot(h, wc_ref[...]).astype(uc_ref.dtype)
    q_ref[...] = _dot(h, wq_ref[...])
    k_ref[...] = _dot(h, wk_ref[...])
    v_ref[...] = _dot(h, wv_ref[...]).astype(v_ref.dtype)
    r_ref[...] = _dot(h, wr_ref[...]).astype(r_ref.dtype)
    g_low = _dot(h, wg_ref[...])
    z = jnp.dot(g_low, gw_ref[...], precision=HIGHEST, preferred_element_type=F32) + gb_ref[...]
    gl_ref[...] = (jnp.minimum(z, 0.0) - jnp.log(1.0 + jnp.exp(-jnp.abs(z)))) * (1.0 / GLA_GATE_TAU)


def _ab_in(x2, sc, sh, w_in, gate_w, gate_b, tiles_per_b):
    t, d = x2.shape
    cc2 = d
    kw = d // 4
    vw = d // 2
    rank = gate_w.shape[0]
    splits = [cc2, cc2 + kw, cc2 + 2 * kw, cc2 + 2 * kw + vw, cc2 + 2 * kw + 2 * vw]
    wb = w_in.astype(BF16)
    ws = [wb[:, :splits[0]], wb[:, splits[0]:splits[1]], wb[:, splits[1]:splits[2]],
          wb[:, splits[2]:splits[3]], wb[:, splits[3]:splits[4]], wb[:, splits[4]:]]
    tm = TOKEN_TILE
    full = lambda a: pl.BlockSpec(a.shape, lambda i: (0,) * a.ndim)
    row = lambda n: pl.BlockSpec((tm, n), lambda i: (i, 0))
    mod = pl.BlockSpec((1, 1, d), lambda i: (i // tiles_per_b, 0, 0))
    gb2 = gate_b.reshape(1, kw)
    widths = [cc2, kw, kw, vw, vw, kw]
    return pl.pallas_call(
        _ab_in_kernel,
        out_shape=[jax.ShapeDtypeStruct((t, n), BF16 if idx in (0, 3, 4) else F32) for idx, n in enumerate(widths)],
        grid=(t // tm,),
        in_specs=[row(d), mod, mod] + [full(w) for w in ws] + [full(gate_w), full(gb2)],
        out_specs=[row(n) for n in widths],
        compiler_params=_cparams(("parallel",)),
        name="ab_in",
    )(x2, sc, sh, *ws, gate_w, gb2)


def _conv_kernel(u_ref, halo_ref, cw_ref, cb_ref, lg_ref, lb_ref, o_ref, hp_ref, wb_ref, *, ts, cc):
    j = pl.program_id(1)

    def glu(u):
        u = u.astype(F32)
        return u[:, :cc] * _sigmoid(u[:, cc:])

    hp_ref[0, 0:CONV_HALO, :] = jnp.where(j > 0, glu(halo_ref[0]), 0.0)
    hp_ref[0, CONV_HALO:CONV_HALO + ts, :] = glu(u_ref[0])
    nrow = CONV_HALO + ts
    hp0 = hp_ref[0]
    for b in range(1, SUBLANES):
        hp_ref[b] = pltpu.roll(hp0, nrow - b, axis=0)
    for tap in range(CONV_WIDTH):
        wb_ref[tap] = jnp.broadcast_to(cw_ref[tap:tap + 1, :], (SUBLANES, cc))
    shift = CONV_HALO - (CONV_WIDTH - 1)
    for rb in range(ts // CONV_ROWS):
        r0 = rb * CONV_ROWS
        acc = jnp.zeros((CONV_ROWS, cc), F32)
        for tap in range(CONV_WIDTH):
            lo = r0 + shift + tap
            base = lo // SUBLANES * SUBLANES
            w_rows = jnp.concatenate([wb_ref[tap]] * (CONV_ROWS // SUBLANES), axis=0)
            acc = acc + w_rows * hp_ref[lo - base, base:base + CONV_ROWS, :]
        y = _layer_norm(acc + cb_ref[...], lg_ref[...], lb_ref[...])
        o_ref[0, r0:r0 + CONV_ROWS, :] = (y * _sigmoid(y)).astype(o_ref.dtype)


def _conv(u3, conv_w, conv_b, ln_g, ln_b):
    bsz, s, cc2 = u3.shape
    cc = cc2 // 2
    ts = TOKEN_TILE
    hb = ts // CONV_HALO
    vec = lambda a: pl.BlockSpec((1, cc), lambda b, j: (0, 0))
    return pl.pallas_call(
        functools.partial(_conv_kernel, ts=ts, cc=cc),
        out_shape=jax.ShapeDtypeStruct((bsz, s, cc), BF16),
        grid=(bsz, s // ts),
        in_specs=[pl.BlockSpec((1, ts, cc2), lambda b, j: (b, j, 0)),
                  pl.BlockSpec((1, CONV_HALO, cc2), lambda b, j: (b, jnp.maximum(j * hb - 1, 0), 0)),
                  pl.BlockSpec((CONV_WIDTH, cc), lambda b, j: (0, 0)),
                  vec(conv_b), vec(ln_g), vec(ln_b)],
        out_specs=pl.BlockSpec((1, ts, cc), lambda b, j: (b, j, 0)),
        scratch_shapes=[pltpu.VMEM((SUBLANES, CONV_HALO + ts, cc), F32), pltpu.VMEM((CONV_WIDTH, SUBLANES, cc), F32)],
        compiler_params=_cparams(("parallel", "parallel")),
        name="conv",
    )(u3, u3, conv_w, conv_b.reshape(1, cc), ln_g.reshape(1, cc), ln_b.reshape(1, cc))


def _gla_kernel(q_ref, k_ref, v_ref, gl_ref, r_ref, ng_ref, o_ref, st_ref, *, nb, nh, dk, dv, gc, nchunks):
    @pl.when(pl.program_id(1) == 0)
    def _():
        st_ref[...] = jnp.zeros_like(st_ref)

    row = lax.broadcasted_iota(I32, (gc, gc), 0)
    col = lax.broadcasted_iota(I32, (gc, gc), 1)
    causal = col <= row
    tri = jnp.where(causal, 1.0, 0.0).astype(BF16)
    scale = dk ** -0.5

    ks = [slice(h * dk, (h + 1) * dk) for h in range(nh)]
    vs = [slice(h * dv, (h + 1) * dv) for h in range(nh)]
    streams = [(bb, h) for bb in range(nb) for h in range(nh)]

    def chunk(c, carry):
        r0 = pl.multiple_of(c * gc, gc)
        rows = pl.ds(r0, gc)
        bs = [_dot_01_f32(tri, gl_ref[bb, rows, :]) for bb in range(nb)]
        q_in, k_in, q_st, k_st, decay, v = [], [], [], [], [], []
        for bb in range(nb):
            b = bs[bb]
            b_last = b[gc - 1:gc, :]
            mid = 0.5 * b_last
            q = q_ref[bb, rows, :] * scale
            k = k_ref[bb, rows, :]
            v.append(v_ref[bb, rows, :].astype(BF16))
            q_in.append((q * jnp.exp(b - mid)).astype(BF16))
            k_in.append((k * jnp.exp(mid - b)).astype(BF16))
            q_st.append((q * jnp.exp(b)).astype(BF16))
            k_st.append((k * jnp.exp(b_last - b)).astype(BF16))
            decay.append(jnp.exp(b_last))
        sts = [st_ref[bb, h] for bb, h in streams]
        scores = [_dot_nt(q_in[bb][:, ks[h]], k_in[bb][:, ks[h]]) for bb, h in streams]
        inter = [_dot_nt(q_st[bb][:, ks[h]], st.astype(BF16)) for (bb, h), st in zip(streams, sts)]
        update = [_dot_tn(v[bb][:, vs[h]], k_st[bb][:, ks[h]]) for bb, h in streams]
        atts = [jnp.where(causal, sc, 0.0).astype(BF16) for sc in scores]
        outs = [_dot(att, v[bb][:, vs[h]]) + it for (bb, h), att, it in zip(streams, atts, inter)]
        for (bb, h), st, up, o in zip(streams, sts, update, outs):
            st_ref[bb, h] = st * decay[bb][:, ks[h]] + up
            r = r_ref[bb, rows, vs[h]].astype(F32)
            o = o * lax.rsqrt(jnp.mean(o * o, axis=-1, keepdims=True) + RMS_EPS) * ng_ref[:, vs[h]]
            o_ref[bb, rows, vs[h]] = (o * (r * _sigmoid(r))).astype(o_ref.dtype)
        return carry

    lax.fori_loop(0, nchunks, chunk, 0)


def _gla(q3, k3, v3, gl3, r3, norm_g):
    bsz, s, kw = q3.shape
    vw = v3.shape[-1]
    nh = GLA_HEADS
    dk, dv = kw // nh, vw // nh
    cb = GLA_BLOCK
    gc = GLA_CHUNK
    nb = GLA_BATCH_PER_STEP
    blk = lambda n: pl.BlockSpec((nb, cb, n), lambda b, j: (b, j, 0))
    return pl.pallas_call(
        functools.partial(_gla_kernel, nb=nb, nh=nh, dk=dk, dv=dv, gc=gc, nchunks=cb // gc),
        out_shape=jax.ShapeDtypeStruct((bsz, s, vw), BF16),
        grid=(bsz // nb, s // cb),
        in_specs=[blk(kw), blk(kw), blk(vw), blk(kw), blk(vw), pl.BlockSpec((1, vw), lambda b, j: (0, 0))],
        out_specs=blk(vw),
        scratch_shapes=[pltpu.VMEM((nb, nh, dv, dk), F32)],
        compiler_params=_cparams(("parallel", "arbitrary")),
        name="gla",
    )(q3, k3, v3, gl3, r3, norm_g.reshape(1, vw))


def _router_weights(w_group, b_group, w_router, b_router):
    d = w_router.shape[0]
    fill = LANES - MOE_EXPERTS - MOE_GROUPS
    wcat = jnp.concatenate([w_router, w_group, jnp.zeros((d, fill), F32)], axis=1)
    w_hi = wcat.astype(BF16)
    w_mid = (wcat - w_hi.astype(F32)).astype(BF16)
    w_lo = (wcat - w_hi.astype(F32) - w_mid.astype(F32)).astype(BF16)
    bcat = jnp.concatenate([b_router, b_group, jnp.zeros((fill,), F32)]).reshape(1, LANES)
    return jnp.concatenate([w_hi, w_mid, w_lo], axis=1), bcat


def _proj_ln_kernel(*refs, n_in, alpha):
    a_refs, w_refs = refs[:n_in], refs[n_in:2 * n_in]
    x_ref, gate_ref, lg_ref, lb_ref, sc_ref, sh_ref, wr_ref, br_ref, o_ref, lo_ref = refs[2 * n_in:]
    mix = _dot(a_refs[0][...], w_refs[0][...])
    for a_ref, w_ref in zip(a_refs[1:], w_refs[1:]):
        mix = mix + _dot(a_ref[...], w_ref[...])
    y = alpha * x_ref[...] + (1.0 + gate_ref[0]) * mix
    x1 = _layer_norm(y, lg_ref[...], lb_ref[...])
    o_ref[...] = x1
    h_hi, h_mid, h_lo = _split3(x1 * (1.0 + sc_ref[0]) + sh_ref[0])
    pa = _dot(h_hi, wr_ref[...])
    pb = _dot(h_mid, wr_ref[:, :2 * LANES])
    pc = _dot(h_lo, wr_ref[:, :LANES])
    small = (pa[:, 2 * LANES:] + pc) + pb[:, LANES:]
    lo_ref[...] = pa[:, :LANES] + ((pa[:, LANES:2 * LANES] + pb[:, :LANES]) + small) + br_ref[...]


def _proj_ln(acts, weights, x2, gate, ln_g, ln_b, sc_f, sh_f, w_route, b_route, tiles_per_b, alpha):
    t, d = x2.shape
    tm = TOKEN_TILE
    n_in = len(acts)
    row = lambda n: pl.BlockSpec((tm, n), lambda i: (i, 0))
    full = lambda a: pl.BlockSpec(a.shape, lambda i: (0,) * a.ndim)
    vec = pl.BlockSpec((1, d), lambda i: (0, 0))
    mod = pl.BlockSpec((1, 1, d), lambda i: (i // tiles_per_b, 0, 0))
    return pl.pallas_call(
        functools.partial(_proj_ln_kernel, n_in=n_in, alpha=alpha),
        out_shape=[jax.ShapeDtypeStruct((t, d), F32), jax.ShapeDtypeStruct((t, LANES), F32)],
        grid=(t // tm,),
        in_specs=[row(a.shape[1]) for a in acts] + [full(w) for w in weights]
                 + [row(d), mod, vec, vec, mod, mod, full(w_route), full(b_route)],
        out_specs=[row(d), row(LANES)],
        compiler_params=_cparams(("parallel",)),
        name="proj_ln",
    )(*acts, *weights, x2, gate, ln_g.reshape(1, d), ln_b.reshape(1, d), sc_f, sh_f, w_route, b_route)


def _mla_in_kernel(x_ref, sc_ref, sh_ref, pos_ref, invf_ref, sign_ref, win_ref, gq_ref, gkv_ref,
                   wqa_ref, wqb_ref, wk_ref, wvt_ref, q_ref, k_ref, vt_ref, *, nh, q_lora, kv_lora, scale, tk):
    h = (x_ref[...] * (1.0 + sc_ref[0]) + sh_ref[0]).astype(BF16)
    u = _dot(h, win_ref[...])
    cq = u[:, :q_lora]
    ckv = u[:, q_lora:q_lora + kv_lora]
    kr = u[:, q_lora + kv_lora:q_lora + kv_lora + LANES]
    kr_sw = u[:, q_lora + kv_lora + LANES:]
    cqn = (cq * lax.rsqrt(jnp.mean(cq * cq, axis=-1, keepdims=True) + RMS_EPS) * gq_ref[...]).astype(BF16)
    kvn = (ckv * lax.rsqrt(jnp.mean(ckv * ckv, axis=-1, keepdims=True) + RMS_EPS) * gkv_ref[...]).astype(BF16)
    ang = pos_ref[...].astype(F32) * invf_ref[...]
    cos = jnp.cos(ang)
    sin = jnp.sin(ang) * sign_ref[...]
    kr_rot = (kr * cos + kr_sw * sin).astype(BF16)
    qa = _dot(cqn, wqa_ref[...])
    qb = _dot(cqn, wqb_ref[...])
    kv = _dot(kvn, wk_ref[...])
    hw = 2 * LANES
    for hd in range(nh):
        q_ref[:, hd * hw:hd * hw + LANES] = (qa[:, hd * hw:hd * hw + LANES] * scale).astype(BF16)
        rope = qa[:, hd * hw + LANES:(hd + 1) * hw] * cos + qb[:, hd * LANES:(hd + 1) * LANES] * sin
        q_ref[:, hd * hw + LANES:(hd + 1) * hw] = (rope * scale).astype(BF16)
        k_ref[:, hd * hw:hd * hw + LANES] = kv[:, hd * LANES:(hd + 1) * LANES].astype(BF16)
        k_ref[:, hd * hw + LANES:(hd + 1) * hw] = kr_rot
    vt = _dot_nt(wvt_ref[...], kvn).astype(BF16)
    for c in range(vt.shape[1] // tk):
        vt_ref[0, c] = vt[:, c * tk:(c + 1) * tk]


def _mla_in(x2, sc, sh, pos2, w_in, gq, gkv, w_uq, w_ukv, tiles_per_b):
    t, d = x2.shape
    nh = MLA_HEADS
    q_lora, kv_lora = gq.shape[0], gkv.shape[0]
    half = MLA_ROPE // 2
    pad = LANES - MLA_ROPE
    kr_w = w_in[:, q_lora + kv_lora:]
    kr_sw = jnp.concatenate([kr_w[:, half:], kr_w[:, :half]], axis=1)
    zpad = jnp.zeros((d, pad), w_in.dtype)
    win_ext = jnp.concatenate([w_in[:, :q_lora + kv_lora], kr_w, zpad, kr_sw, zpad], axis=1).astype(BF16)
    wq = w_uq.reshape(q_lora, nh, MLA_NOPE + MLA_ROPE)
    q_nope, q_rope = wq[:, :, :MLA_NOPE], wq[:, :, MLA_NOPE:]
    q_rope_sw = jnp.concatenate([q_rope[:, :, half:], q_rope[:, :, :half]], axis=2)
    zq = jnp.zeros((q_lora, nh, pad), w_uq.dtype)
    wqa = jnp.concatenate([q_nope, q_rope, zq], axis=2).reshape(q_lora, nh * 2 * LANES).astype(BF16)
    wqb = jnp.concatenate([q_rope_sw, zq], axis=2).reshape(q_lora, nh * LANES).astype(BF16)
    wkv = w_ukv.reshape(kv_lora, nh, MLA_NOPE + MLA_V)
    wk = wkv[:, :, :MLA_NOPE].reshape(kv_lora, nh * MLA_NOPE).astype(BF16)
    wvt = wkv[:, :, MLA_NOPE:].reshape(kv_lora, nh * MLA_V).T.astype(BF16)
    inv_freq = 1.0 / (ROPE_THETA ** (jnp.arange(0, MLA_ROPE, 2, dtype=F32) / MLA_ROPE))
    invf = jnp.concatenate([inv_freq, inv_freq, jnp.zeros((pad,), F32)]).reshape(1, LANES)
    sign = jnp.concatenate([-jnp.ones((half,), F32), jnp.ones((half,), F32), jnp.zeros((pad,), F32)]).reshape(1, LANES)
    tm = TOKEN_TILE
    full = lambda a: pl.BlockSpec(a.shape, lambda i: (0,) * a.ndim)
    row = lambda n: pl.BlockSpec((tm, n), lambda i: (i, 0))
    mod = pl.BlockSpec((1, 1, d), lambda i: (i // tiles_per_b, 0, 0))
    gq2, gkv2 = gq.reshape(1, q_lora), gkv.reshape(1, kv_lora)
    scale = (MLA_NOPE + MLA_ROPE) ** -0.5 * 1.4426950408889634
    tk = ATTN_TK
    kt_per_tile = tm // tk
    s = tiles_per_b * tm
    return pl.pallas_call(
        functools.partial(_mla_in_kernel, nh=nh, q_lora=q_lora, kv_lora=kv_lora, scale=scale, tk=tk),
        out_shape=[jax.ShapeDtypeStruct((t, nh * 2 * LANES), BF16), jax.ShapeDtypeStruct((t, nh * 2 * LANES), BF16),
                   jax.ShapeDtypeStruct((t // s, s // tk, nh * MLA_V, tk), BF16)],
        grid=(t // tm,),
        in_specs=[row(d), mod, mod, row(1), full(invf), full(sign), full(win_ext), full(gq2), full(gkv2),
                  full(wqa), full(wqb), full(wk), full(wvt)],
        out_specs=[row(nh * 2 * LANES), row(nh * 2 * LANES),
                   pl.BlockSpec((1, kt_per_tile, nh * MLA_V, tk),
                                lambda i: (i // tiles_per_b, i % tiles_per_b, 0, 0))],
        compiler_params=_cparams(("parallel",)),
        name="mla_in",
    )(x2, sc, sh, pos2, invf, sign, win_ext, gq2, gkv2, wqa, wqb, wk, wvt)


def _attn_kernel(q_ref, k_ref, vt_ref, o_ref, acc_ref, s0_ref, s1_ref, *, s, tq, tk, hb):
    kpq = tq // tk
    assert kpq == 2, "the pipeline below alternates two score buffers over pairs of key tiles"
    hw = 2 * LANES
    ones = jnp.ones((8, tk), BF16)

    def q_block(qi, carry):
        qrows = pl.ds(pl.multiple_of(qi * tq, tq), tq)
        acc_ref[...] = jnp.zeros_like(acc_ref)

        def scores(j, s_ref, c0=0, nc=tq):
            krows = pl.ds(pl.multiple_of(j * tk, tk), tk)
            cols = pl.ds(pl.multiple_of(qi * tq + c0, tk), nc)
            for h in range(hb):
                s_ref[h, :, c0:c0 + nc] = _dot_nt(k_ref[krows, h * hw:(h + 1) * hw], q_ref[cols, h * hw:(h + 1) * hw])

        def tile(j, s_ref, stats, masked, c0=0, nc=tq):
            def put(full, part):
                pieces = ([full[:, :c0]] if c0 else []) + [part] + ([full[:, c0 + nc:]] if c0 + nc < tq else [])
                return pieces[0] if len(pieces) == 1 else jnp.concatenate(pieces, axis=1)
            ps, alphas, out = [], [], []
            for h in range(hb):
                m, st = stats[2 * h][:, c0:c0 + nc], s_ref[h, :, c0:c0 + nc]
                if masked:
                    key = j * tk + lax.broadcasted_iota(I32, (tk, nc), 0)
                    qry = qi * tq + c0 + lax.broadcasted_iota(I32, (tk, nc), 1)
                    st = jnp.where(key <= qry, st, -jnp.inf)
                m_new = jnp.maximum(m, jnp.max(st, axis=0, keepdims=True))
                ps.append(jnp.exp2(st - m_new).astype(BF16))
                alphas.append(jnp.exp2(m - m_new))
                out.append(put(stats[2 * h], m_new))
            for h in range(hb):
                acc_ref[h, :, c0:c0 + nc] = (alphas[h] * acc_ref[h, :, c0:c0 + nc]
                                             + _dot(vt_ref[0, j, h * MLA_V:(h + 1) * MLA_V, :], ps[h]))
                l_new = alphas[h] * stats[2 * h + 1][:, c0:c0 + nc] + _dot(ones, ps[h])[0:1]
                out.insert(2 * h + 1, put(stats[2 * h + 1], l_new))
            return tuple(out)

        stats = (jnp.full((1, tq), -jnp.inf, F32), jnp.zeros((1, tq), F32)) * hb
        scores(0, s0_ref)

        def pair(jj, c):
            j = 2 * jj
            scores(j + 1, s1_ref)
            c = tile(j, s0_ref, c, False)
            scores(j + 2, s0_ref)
            return tile(j + 1, s1_ref, c, False)

        stats = lax.fori_loop(0, qi, pair, stats)
        scores(2 * qi + 1, s1_ref, tk, tk)
        stats = tile(2 * qi, s0_ref, stats, True, 0, tk)
        stats = tile(2 * qi, s0_ref, stats, False, tk, tk)
        stats = tile(2 * qi + 1, s1_ref, stats, True, tk, tk)
        for h in range(hb):
            o_ref[qrows, h * MLA_V:(h + 1) * MLA_V] = (acc_ref[h] / stats[2 * h + 1]).T.astype(o_ref.dtype)
        return carry

    lax.fori_loop(0, s // tq, q_block, 0)


def _attn(q, k, vt, bsz, s):
    nh = MLA_HEADS
    tq, tk = ATTN_TQ, ATTN_TK
    hb = ATTN_HEADS_PER_STEP
    hw = 2 * LANES
    return pl.pallas_call(
        functools.partial(_attn_kernel, s=s, tq=tq, tk=tk, hb=hb),
        out_shape=jax.ShapeDtypeStruct((bsz * s, nh * MLA_V), BF16),
        grid=(bsz, nh // hb),
        in_specs=[pl.BlockSpec((s, hb * hw), lambda b, h: (b, h)),
                  pl.BlockSpec((s, hb * hw), lambda b, h: (b, h)),
                  pl.BlockSpec((1, s // tk, hb * MLA_V, tk), lambda b, h: (b, 0, h, 0))],
        out_specs=pl.BlockSpec((s, hb * MLA_V), lambda b, h: (b, h)),
        scratch_shapes=[pltpu.VMEM((hb, MLA_V, tq), F32), pltpu.VMEM((hb, tk, tq), F32),
                        pltpu.VMEM((hb, tk, tq), F32)],
        compiler_params=_cparams(("parallel", "parallel")),
        name="attn",
    )(q, k, vt)


def _route_kernel(logits_ref, lp_ref, wt_ref, tm_ref, offs_ref, te_ref, na_ref,
                  upper_ref, carry_ref, *, tm, ne, ng, row_tile, nt_pad):
    i = pl.program_id(0)
    epg = ne // ng

    @pl.when(i == 0)
    def _():
        r = lax.broadcasted_iota(I32, (tm, tm), 0)
        c = lax.broadcasted_iota(I32, (tm, tm), 1)
        upper_ref[...] = jnp.where(r < c, 1.0, 0.0).astype(BF16)
        carry_ref[...] = jnp.zeros_like(carry_ref)

    lt = logits_ref[...].T
    lr = lt[0:ne]
    grow = lax.broadcasted_iota(I32, (8, tm), 0).astype(F32)
    lg = jnp.where(grow < ng, lt[ne:ne + 8], -jnp.inf)
    gmax = jnp.max(lg, axis=0, keepdims=True)
    g_idx = jnp.min(jnp.where(lg == gmax, grow, 1e9), axis=0, keepdims=True)
    g_w = 1.0 / jnp.sum(jnp.exp(lg - gmax), axis=0, keepdims=True)
    erow = lax.broadcasted_iota(I32, (ne, tm), 0).astype(F32)
    in_group = jnp.floor(erow * (1.0 / epg)) == g_idx
    sel = jnp.where(in_group, lr, -jnp.inf)
    v1 = jnp.max(sel, axis=0, keepdims=True)
    i1 = jnp.min(jnp.where(sel == v1, erow, 1e9), axis=0, keepdims=True)
    sel2 = jnp.where(erow == i1, -jnp.inf, sel)
    v2 = jnp.max(sel2, axis=0, keepdims=True)
    i2 = jnp.min(jnp.where(sel2 == v2, erow, 1e9), axis=0, keepdims=True)
    t = jnp.exp(v2 - v1)
    w1 = g_w / (1.0 + t)
    w2 = g_w * t / (1.0 + t)
    oh1 = erow == i1
    oh2 = erow == i2
    member = jnp.where(oh1 | oh2, 1.0, 0.0)
    lcnt = jnp.sum(member, axis=1, keepdims=True)
    lcnt = jnp.floor((lcnt + (RUN_ALIGN - 1)) * (1.0 / RUN_ALIGN)) * RUN_ALIGN
    er_ = lax.broadcasted_iota(I32, (ne, ne), 0)
    ec_ = lax.broadcasted_iota(I32, (ne, ne), 1)
    lstart = jnp.dot(jnp.where(ec_ < er_, 1.0, 0.0).astype(F32), jnp.broadcast_to(lcnt, (ne, LANES)),
                     precision=HIGHEST, preferred_element_type=F32)[:, 0:1]
    lrank = _dot(member.astype(BF16), upper_ref[...]) + lstart
    p1 = jnp.sum(jnp.where(oh1, lrank, 0.0), axis=0, keepdims=True)
    p2 = jnp.sum(jnp.where(oh2, lrank, 0.0), axis=0, keepdims=True)

    orow = lax.broadcasted_iota(I32, (8, tm), 0)
    lp_ref[...] = jnp.where(orow == 0, p1, jnp.where(orow == 1, p2, 0.0)).astype(I32)
    wrow = lax.broadcasted_iota(I32, (LANES, tm), 0)
    wt_ref[...] = jnp.where(wrow == 0, w1, jnp.where(wrow == 1, w2,
                                                     jnp.where(wrow == 2, p1, jnp.where(wrow == 3, p2, 0.0)))).T
    mr = lax.broadcasted_iota(I32, (ne, LANES), 0)
    mc = lax.broadcasted_iota(I32, (ne, LANES), 1)
    to_row = lambda col, lane0: jnp.sum(jnp.where(mr + lane0 == mc, col, 0.0), axis=0, keepdims=True)
    total = jnp.sum(lcnt, axis=0, keepdims=True)
    lane = lax.broadcasted_iota(I32, (1, LANES), 1)
    packed = (to_row(lstart, 0) + to_row(lcnt, ne) + to_row(carry_ref[...], 2 * ne)
              + jnp.where(lane == 3 * ne, total, 0.0))
    trow = lax.broadcasted_iota(I32, (8, LANES), 0)
    tm_ref[...] = jnp.where(trow == 0, packed, 0.0).astype(I32)
    carry_ref[...] = carry_ref[...] + lcnt

    @pl.when(i == pl.num_programs(0) - 1)
    def _():
        cnt = carry_ref[...]
        ntl = jnp.floor((cnt + (row_tile - 1)) * (1.0 / row_tile))
        incl = jnp.where(ec_ <= er_, 1.0, 0.0).astype(F32)
        ends = jnp.dot(incl, jnp.broadcast_to(ntl, (ne, LANES)), precision=HIGHEST,
                       preferred_element_type=F32)
        starts = ends - ntl
        offs_ref[...] = jnp.concatenate([starts * row_tile, ends * row_tile, jnp.broadcast_to(ntl, (ne, LANES)),
                                         jnp.zeros((8, LANES), F32)], axis=0).astype(I32)
        tile = lax.broadcasted_iota(I32, (ne, nt_pad), 1).astype(F32)
        te = jnp.sum(jnp.where(ends[:, 0:1] <= tile, 1.0, 0.0), axis=0, keepdims=True)
        te_ref[...] = jnp.broadcast_to(jnp.minimum(te, ne - 1.0), (8, nt_pad)).astype(I32)
        na_ref[...] = jnp.broadcast_to(ends[ne - 1:ne, :], (8, LANES)).astype(I32)


def _moe_tiles(t):
    rows = 2 * t + (RUN_ALIGN - 1) * MOE_EXPERTS * (t // TOKEN_TILE)
    nt_max = -(-rows // MOE_ROW_TILE) + MOE_EXPERTS
    nt_pad = -(-nt_max // LANES) * LANES
    return nt_max, nt_pad


def _sorted_rows(tm):
    return -(-(2 * tm + (RUN_ALIGN - 1) * MOE_EXPERTS) // LANES) * LANES


def _route(logits):
    t = logits.shape[0]
    ne, ng = MOE_EXPERTS, MOE_GROUPS
    tm = TOKEN_TILE
    _, nt_pad = _moe_tiles(t)
    const = lambda shp: pl.BlockSpec(shp, lambda i: (0,) * len(shp))
    return pl.pallas_call(
        functools.partial(_route_kernel, tm=tm, ne=ne, ng=ng, row_tile=MOE_ROW_TILE, nt_pad=nt_pad),
        out_shape=[jax.ShapeDtypeStruct((8, t), I32), jax.ShapeDtypeStruct((t, LANES), F32),
                   jax.ShapeDtypeStruct((8 * (t // tm), LANES), I32),
                   jax.ShapeDtypeStruct((3 * ne + 8, LANES), I32), jax.ShapeDtypeStruct((8, nt_pad), I32),
                   jax.ShapeDtypeStruct((8, LANES), I32)],
        grid=(t // tm,),
        in_specs=[pl.BlockSpec((tm, LANES), lambda i: (i, 0))],
        out_specs=[pl.BlockSpec((8, tm), lambda i: (0, i)), pl.BlockSpec((tm, LANES), lambda i: (i, 0)),
                   pl.BlockSpec((8, LANES), lambda i: (i, 0)),
                   const((3 * ne + 8, LANES)), const((8, nt_pad)), const((8, LANES))],
        scratch_shapes=[pltpu.VMEM((tm, tm), BF16), pltpu.VMEM((ne, 1), F32)],
        compiler_params=_cparams(("arbitrary",)),
        name="moe_route",
    )(logits)


def _start_runs(tm_ref, tile, ne, copy):
    for e in range(ne):
        lstart = tm_ref[tile, e]
        n = tm_ref[tile, ne + e]
        before = tm_ref[tile, 2 * ne + e]
        _binary_pieces(n, TOKEN_TILE, lambda off, rows: copy(
            e, pl.multiple_of(lstart + off, RUN_ALIGN), pl.multiple_of(before + off, RUN_ALIGN), rows))


def _binary_pieces(n, n_max, copy):
    del n_max
    nbig = n >> (RUN_PIECE.bit_length() - 1)

    def big_piece(c, carry):
        copy(c * RUN_PIECE, RUN_PIECE).start()
        return carry

    lax.fori_loop(0, nbig, big_piece, 0)
    off = nbig * RUN_PIECE
    p = RUN_PIECE // 2
    while p >= RUN_ALIGN:
        @pl.when((n & p) != 0)
        def _():
            copy(off, p).start()
        off = off + (n & p)
        p //= 2


def _dispatch_kernel(offs_ref, gend_ref, ntl_ref, tm_ref, x_ref, sc_ref, sh_ref, lp_ref, wt_ref, xs_ref,
                     h_ref, z_ref, sem_z, sem_r, *, tm, ne, row_tile, nt_max, ntile, srows):
    i = pl.program_id(0)
    tile = i - 1

    @pl.when(i == 0)
    def _():
        z_ref[...] = jnp.zeros_like(z_ref)
        zero_tile = lambda start: pltpu.make_async_copy(z_ref, xs_ref.at[pl.ds(start, row_tile), :], sem_z)
        for act in ("start", "wait"):
            for e in range(ne):
                @pl.when(ntl_ref[e] > 0)
                def _():
                    getattr(zero_tile(pl.multiple_of(gend_ref[e] - row_tile, row_tile)), act)()
            for back in range(1, nt_max - (2 * tm * ntile) // row_tile + 1):
                @pl.when(nt_max - back >= gend_ref[ne - 1] // row_tile)
                def _():
                    getattr(zero_tile((nt_max - back) * row_tile), act)()
            for spill in range(2):
                getattr(zero_tile((nt_max + spill) * row_tile), act)()

    def wait_tile(t):
        slot = t % 2
        pltpu.make_async_copy(h_ref.at[slot], xs_ref.at[pl.ds(0, srows), :], sem_r.at[slot]).wait()

    @pl.when((tile >= 2) & (tile <= ntile))
    def _():
        wait_tile(tile - 2)

    @pl.when(tile == ntile)
    def _():
        wait_tile(tile - 1)

    @pl.when((tile >= 0) & (tile < ntile))
    def _():
        slot = tile % 2
        h = (x_ref[...] * (1.0 + sc_ref[0]) + sh_ref[0]).astype(BF16)
        row = lax.broadcasted_iota(I32, (srows, tm), 0)
        lp = lp_ref[...]
        signed = jnp.where(row == lp[0:1, :], 1.0, jnp.where(row == lp[1:2, :], -1.0, 0.0)).astype(BF16)
        lane = lax.broadcasted_iota(I32, (tm, LANES), 1)
        wt = wt_ref[...]
        terms = []
        for k in range(2):
            w = wt[:, k:k + 1]
            hi = w.astype(BF16).astype(F32)
            mid = (w - hi).astype(BF16).astype(F32)
            terms += [hi, mid, w - hi - mid]
        extra = jnp.where(lane == SIGN_LANE, 1.0, 0.0)
        for idx, term in enumerate(terms):
            extra = jnp.where(lane == idx, term, extra)
        h_ref[slot] = _dot(signed, jnp.concatenate([h, extra.astype(BF16)], axis=1)).astype(BF16)

        def copy(e, local_row, rows_before, rows):
            dst = pl.multiple_of(offs_ref[e] + rows_before, RUN_ALIGN)
            return pltpu.make_async_copy(h_ref.at[slot, pl.ds(local_row, rows), :], xs_ref.at[pl.ds(dst, rows), :],
                                         sem_r.at[slot])

        _start_runs(tm_ref, tile, ne, copy)
        used = tm_ref[tile, 3 * ne]
        _binary_pieces(srows - used, srows - 2 * tm, lambda off, rows: pltpu.make_async_copy(
            h_ref.at[slot, pl.ds(pl.multiple_of(used + off, RUN_ALIGN), rows), :],
            xs_ref.at[pl.ds(pl.multiple_of((nt_max + slot) * row_tile + off, RUN_ALIGN), rows), :], sem_r.at[slot]))


def _dispatch(x2, sc, sh, lp, wt, tmeta, offs, gend, ntl, tiles_per_b):
    t, d = x2.shape
    tm = TOKEN_TILE
    nt_max, _ = _moe_tiles(t)
    ntile = t // tm
    srows = _sorted_rows(tm)
    cur = lambda i: jnp.clip(i - 1, 0, ntile - 1)
    grid_spec = pltpu.PrefetchScalarGridSpec(
        num_scalar_prefetch=4,
        grid=(ntile + 2,),
        in_specs=[pl.BlockSpec((tm, d), lambda i, *_: (cur(i), 0)),
                  pl.BlockSpec((1, 1, d), lambda i, *_: (cur(i) // tiles_per_b, 0, 0)),
                  pl.BlockSpec((1, 1, d), lambda i, *_: (cur(i) // tiles_per_b, 0, 0)),
                  pl.BlockSpec((8, tm), lambda i, *_: (0, cur(i))),
                  pl.BlockSpec((tm, LANES), lambda i, *_: (cur(i), 0))],
        out_specs=pl.BlockSpec(memory_space=pl.ANY),
        scratch_shapes=[pltpu.VMEM((2, srows, d + LANES), BF16), pltpu.VMEM((MOE_ROW_TILE, d + LANES), BF16),
                        pltpu.SemaphoreType.DMA, pltpu.SemaphoreType.DMA((2,))])
    return pl.pallas_call(
        functools.partial(_dispatch_kernel, tm=tm, ne=MOE_EXPERTS, row_tile=MOE_ROW_TILE, nt_max=nt_max,
                          ntile=ntile, srows=srows),
        out_shape=jax.ShapeDtypeStruct(((nt_max + 2) * MOE_ROW_TILE, d + LANES), BF16),
        grid_spec=grid_spec,
        compiler_params=_cparams(("arbitrary",)),
        name="moe_dispatch",
    )(offs, gend, ntl, tmeta, x2, sc, sh, lp, wt)


def _gmm_kernel(te_ref, na_ref, xs_ref, wg_ref, wu_ref, wd_ref, ys_ref, wgu_buf, wd_buf, *, ff, tr):
    j = pl.program_id(0)

    @pl.when(j < na_ref[0])
    def _():
        changed = (j == 0) | (te_ref[j] != te_ref[jnp.maximum(j - 1, 0)])

        @pl.when(changed)
        def _():
            wgu_buf[:, :ff] = wg_ref[0, 0].astype(BF16)
            wgu_buf[:, ff:] = wu_ref[0, 0].astype(BF16)
            wd_buf[...] = wd_ref[0, 0].astype(BF16)

        halves = [slice(c * (tr // 2), (c + 1) * (tr // 2)) for c in range(2)]
        d = wgu_buf.shape[0]
        riders = [xs_ref[rows, d:].astype(F32) for rows in halves]
        signs = [r[:, SIGN_LANE:SIGN_LANE + 1] for r in riders]
        gus = [_dot(xs_ref[rows, :d] * sg.astype(BF16), wgu_buf[...]) for rows, sg in zip(halves, signs)]
        hids = [(gu[:, :ff] * _sigmoid(gu[:, :ff]) * gu[:, ff:]).astype(BF16) for gu in gus]
        for rows, hid, r, sg in zip(halves, hids, riders, signs):
            w = jnp.where(sg > 0.0, r[:, 0:1] + r[:, 1:2] + r[:, 2:3], -(r[:, 3:4] + r[:, 4:5] + r[:, 5:6]))
            ys_ref[rows, :] = (w * _dot(hid, wd_buf[...])).astype(ys_ref.dtype)

    @pl.when(j >= na_ref[0])
    def _():
        ys_ref[...] = jnp.zeros_like(ys_ref)


def _gmm(xs, te, na, w_gate, w_up, w_down, layer):
    d, ff = w_gate.shape[-2:]
    tr = MOE_ROW_TILE
    ns = te.shape[0] * tr
    act = lambda j, te_ref, na_ref: jnp.minimum(j, na_ref[0] - 1)
    grid_spec = pltpu.PrefetchScalarGridSpec(
        num_scalar_prefetch=2,
        grid=(ns // tr,),
        in_specs=[pl.BlockSpec((tr, d + LANES), lambda j, te_ref, na_ref: (act(j, te_ref, na_ref), 0)),
                  pl.BlockSpec((1, 1, d, ff), lambda j, te_ref, na_ref: (layer, te_ref[act(j, te_ref, na_ref)], 0, 0)),
                  pl.BlockSpec((1, 1, d, ff), lambda j, te_ref, na_ref: (layer, te_ref[act(j, te_ref, na_ref)], 0, 0)),
                  pl.BlockSpec((1, 1, ff, d), lambda j, te_ref, na_ref: (layer, te_ref[act(j, te_ref, na_ref)], 0, 0))],
        out_specs=pl.BlockSpec((tr, d), lambda j, te_ref, na_ref: (j, 0)),
        scratch_shapes=[pltpu.VMEM((d, 2 * ff), BF16), pltpu.VMEM((ff, d), BF16)])
    return pl.pallas_call(
        functools.partial(_gmm_kernel, ff=ff, tr=tr),
        out_shape=jax.ShapeDtypeStruct((ns, d), BF16),
        grid_spec=grid_spec,
        compiler_params=_cparams(("arbitrary",)),
        name="moe_gmm",
    )(te, na, xs, w_gate, w_up, w_down)


def _combine_kernel(offs_ref, tm_ref, wt_ref, x_ref, gate_ref, lg_ref, lb_ref, ys_ref, o_ref,
                    buf_ref, sem_r, *, tm, ne, ntile, srows, alpha):
    i = pl.program_id(0)

    def fetch(tile):
        slot = tile % 2

        def copy(e, local_row, rows_before, rows):
            src = pl.multiple_of(offs_ref[e] + rows_before, RUN_ALIGN)
            return pltpu.make_async_copy(ys_ref.at[pl.ds(src, rows), :], buf_ref.at[slot, pl.ds(local_row, rows), :],
                                         sem_r.at[slot])

        _start_runs(tm_ref, tile, ne, copy)
        used = tm_ref[tile, 3 * ne]
        _binary_pieces(srows - used, srows - 2 * tm, lambda off, rows: pltpu.make_async_copy(
            ys_ref.at[pl.ds(pl.multiple_of(off, RUN_ALIGN), rows), :],
            buf_ref.at[slot, pl.ds(pl.multiple_of(used + off, RUN_ALIGN), rows), :], sem_r.at[slot]))

    def wait(tile):
        slot = tile % 2
        pltpu.make_async_copy(ys_ref.at[pl.ds(0, srows), :], buf_ref.at[slot], sem_r.at[slot]).wait()

    def finish(tile):
        w = wt_ref[...]
        rows = buf_ref[tile % 2]
        col = lax.broadcasted_iota(I32, (tm, srows), 1).astype(F32)
        both = jnp.where(col == w[:, 2:3], 1.0, jnp.where(col == w[:, 3:4], 1.0, 0.0)).astype(BF16)
        ffn = _dot(both, rows)
        y = alpha * x_ref[...] + (1.0 + gate_ref[0]) * ffn
        o_ref[...] = _layer_norm(y, lg_ref[...], lb_ref[...])

    @pl.when(i == 0)
    def _():
        fetch(i)

    @pl.when((i > 0) & (i < ntile))
    def _():
        wait(i - 1)
        fetch(i)
        finish(i - 1)

    @pl.when(i == ntile)
    def _():
        wait(i - 1)
        finish(i - 1)


def _combine(ys, tmeta, wt, x2, gate, ln_g, ln_b, offs, tiles_per_b, alpha):
    t, d = x2.shape
    tm = TOKEN_TILE
    ntile = t // tm
    srows = _sorted_rows(tm)
    prev = lambda i: jnp.maximum(i - 1, 0)
    vec = pl.BlockSpec((1, d), lambda i, *_: (0, 0))
    grid_spec = pltpu.PrefetchScalarGridSpec(
        num_scalar_prefetch=2,
        grid=(ntile + 1,),
        in_specs=[pl.BlockSpec((tm, LANES), lambda i, *_: (prev(i), 0)),
                  pl.BlockSpec((tm, d), lambda i, *_: (prev(i), 0)),
                  pl.BlockSpec((1, 1, d), lambda i, *_: (prev(i) // tiles_per_b, 0, 0)),
                  vec, vec,
                  pl.BlockSpec(memory_space=pl.ANY)],
        out_specs=pl.BlockSpec((tm, d), lambda i, *_: (prev(i), 0)),
        scratch_shapes=[pltpu.VMEM((2, srows, d), BF16), pltpu.SemaphoreType.DMA((2,))])
    return pl.pallas_call(
        functools.partial(_combine_kernel, tm=tm, ne=MOE_EXPERTS, ntile=ntile, srows=srows, alpha=alpha),
        out_shape=jax.ShapeDtypeStruct((t, d), F32),
        grid_spec=grid_spec,
        compiler_params=_cparams(("arbitrary",)),
        name="moe_combine",
    )(offs, tmeta, wt, x2, gate, ln_g.reshape(1, d), ln_b.reshape(1, d), ys)


def _moe_block(x2, logits, sc, sh, gate, ln_g, ln_b, w_gate, w_up, w_down, layer, tiles_per_b, alpha):
    ne = MOE_EXPERTS
    nt_max, _ = _moe_tiles(x2.shape[0])
    lp, wt, tmeta, meta, te, na = _route(logits)
    offs, gend, ntl = meta[:ne, 0], meta[ne:2 * ne, 0], meta[2 * ne:3 * ne, 0]
    tmeta = tmeta.reshape(-1, SUBLANES, LANES)[:, 0, :]
    xs = _dispatch(x2, sc, sh, lp, wt, tmeta, offs, gend, ntl, tiles_per_b)
    ys = _gmm(xs, te[0, :nt_max], na[0, :1], w_gate, w_up, w_down, layer)
    return _combine(ys, tmeta, wt, x2, gate, ln_g, ln_b, offs, tiles_per_b, alpha)


def kernel(x, c, positions, ada_w, ada_b, ln_mix_g, ln_mix_b, ln_ffn_g, ln_ffn_b, ab_w_in, conv_w, conv_b, conv_ln_g, conv_ln_b, gla_gate_w, gla_gate_b, gla_norm_g, ab_w_out, mla_w_in, mla_q_norm_g, mla_kv_norm_g, mla_w_uq, mla_w_ukv, mla_w_out, moe_w_group, moe_b_group, moe_w_router, moe_b_router, moe_w_gate, moe_w_up, moe_w_down):
    bsz, s, d = x.shape
    depth = ada_w.shape[0]
    t = bsz * s
    tiles_per_b = s // TOKEN_TILE
    alpha = (2 * depth) ** 0.25
    mod = _ada(c, ada_w, ada_b).reshape(depth, bsz, 6, 1, d)
    x2 = x.reshape(t, d)
    for layer in range(depth):
        sh_m, sc_m, g_m, sh_f, sc_f, g_f = (mod[layer, :, n] for n in range(6))
        i = layer // 2
        if layer % 2 == 0:
            uc, q, k, v, r, gl = _ab_in(x2, sc_m, sh_m, ab_w_in[i], gla_gate_w[i], gla_gate_b[i], tiles_per_b)
            y_a = _conv(uc.reshape(bsz, s, -1), conv_w[i], conv_b[i], conv_ln_g[i], conv_ln_b[i])
            b3 = lambda a: a.reshape(bsz, s, -1)
            y_b = _gla(b3(q), b3(k), b3(v), b3(gl), b3(r), gla_norm_g[i])
            w_out = ab_w_out[i].astype(BF16)
            cc = y_a.shape[-1]
            acts = [y_a.reshape(t, cc), y_b.reshape(t, -1)]
            weights = [w_out[:cc], w_out[cc:]]
        else:
            qc, kc, vv = _mla_in(x2, sc_m, sh_m, positions.reshape(t, 1), mla_w_in[i], mla_q_norm_g[i],
                                 mla_kv_norm_g[i], mla_w_uq[i], mla_w_ukv[i], tiles_per_b)
            acts = [_attn(qc, kc, vv, bsz, s)]
            weights = [mla_w_out[i].astype(BF16)]
        w_route, b_route = _router_weights(moe_w_group[layer], moe_b_group[layer], moe_w_router[layer],
                                           moe_b_router[layer])
        x2, logits = _proj_ln(acts, weights, x2, g_m, ln_mix_g[layer], ln_mix_b[layer], sc_f, sh_f, w_route, b_route,
                              tiles_per_b, alpha)
        x2 = _moe_block(x2, logits, sc_f, sh_f, g_f, ln_ffn_g[layer], ln_ffn_b[layer], moe_w_gate, moe_w_up,
                        moe_w_down, layer, tiles_per_b, alpha)
    return x2.reshape(bsz, s, d)
```

```python
import functools

import jax
import jax.numpy as jnp
from jax import lax
from jax.experimental import pallas as pl
from jax.experimental.pallas import tpu as pltpu

F32 = jnp.float32
BF16 = jnp.bfloat16
I32 = jnp.int32
HIGHEST = lax.Precision.HIGHEST

LN_EPS = 1e-5
RMS_EPS = 1e-6
CONV_WIDTH = 31
GLA_HEADS = 4
GLA_GATE_TAU = 16.0
MLA_HEADS = 8
MLA_NOPE = 128
MLA_ROPE = 64
MLA_V = 128
ROPE_THETA = 10000.0
MOE_GROUPS = 4
MOE_EXPERTS_PER_GROUP = 8
MOE_EXPERTS = MOE_GROUPS * MOE_EXPERTS_PER_GROUP

LANES = 128
SUBLANES = 8
TOKEN_TILE = 512
GLA_CHUNK = 128
GLA_BLOCK = 512
GLA_BATCH_PER_STEP = 4
CONV_ROWS = 32
CONV_HALO = 32
ATTN_TQ = 512
ATTN_TK = 256
ATTN_HEADS_PER_STEP = 4
MOE_ROW_TILE = 512
RUN_PIECE = 64
RUN_ALIGN = 16
SIGN_LANE = 6
VMEM_LIMIT = 48 * 1024 * 1024


def _cparams(sem):
    return pltpu.CompilerParams(dimension_semantics=sem, vmem_limit_bytes=VMEM_LIMIT)


def _sigmoid(x):
    return 1.0 / (1.0 + jnp.exp(-x))


def _dot(a, b):
    return jnp.dot(a, b, preferred_element_type=F32)


def _dot_nt(a, b):
    return lax.dot_general(a, b, (((1,), (1,)), ((), ())), preferred_element_type=F32)


def _dot_tn(a, b):
    return lax.dot_general(a, b, (((0,), (0,)), ((), ())), preferred_element_type=F32)


def _split3(x):
    hi = x.astype(BF16)
    r1 = x - hi.astype(F32)
    mid = r1.astype(BF16)
    lo = (r1 - mid.astype(F32)).astype(BF16)
    return hi, mid, lo


def _dot_01_f32(a01, x):
    hi, mid, lo = _split3(x)
    return _dot(a01, hi) + (_dot(a01, mid) + _dot(a01, lo))


def _layer_norm(y, g, b):
    mu = jnp.mean(y, axis=-1, keepdims=True)
    d = y - mu
    var = jnp.mean(d * d, axis=-1, keepdims=True)
    return d * lax.rsqrt(var + LN_EPS) * g + b


def _ada_kernel(c_ref, w_ref, b_ref, o_ref):
    c = c_ref[...]
    o_ref[0] = jnp.dot(c * _sigmoid(c), w_ref[0], precision=HIGHEST, preferred_element_type=F32) + b_ref[0]


def _ada(c, ada_w, ada_b):
    depth, d, n = ada_w.shape
    bsz = c.shape[0]
    tn = 1536
    return pl.pallas_call(
        _ada_kernel,
        out_shape=jax.ShapeDtypeStruct((depth, bsz, n), F32),
        grid=(depth, n // tn),
        in_specs=[pl.BlockSpec((bsz, d), lambda l, j: (0, 0)),
                  pl.BlockSpec((1, d, tn), lambda l, j: (l, 0, j)),
                  pl.BlockSpec((1, 1, tn), lambda l, j: (l, 0, j))],
        out_specs=pl.BlockSpec((1, bsz, tn), lambda l, j: (l, 0, j)),
        compiler_params=_cparams(("parallel", "parallel")),
        name="ada",
    )(c, ada_w, ada_b.reshape(depth, 1, n))


def _ab_in_kernel(x_ref, sc_ref, sh_ref, wc_ref, wq_ref, wk_ref, wv_ref, wr_ref, wg_ref, gw_ref, gb_ref,
                  uc_ref, q_ref, k_ref, v_ref, r_ref, gl_ref):
    h = (x_ref[...] * (1.0 + sc_ref[0]) + sh_ref[0]).astype(BF16)
    uc_ref[...] = _dot(h, wc_ref[...]).astype(uc_ref.dtype)
    q_ref[...] = _dot(h, wq_ref[...])
    k_ref[...] = _dot(h, wk_ref[...])
    v_ref[...] = _dot(h, wv_ref[...]).astype(v_ref.dtype)
    r_ref[...] = _dot(h, wr_ref[...]).astype(r_ref.dtype)
    g_low = _dot(h, wg_ref[...])
    z = jnp.dot(g_low, gw_ref[...], precision=HIGHEST, preferred_element_type=F32) + gb_ref[...]
    gl_ref[...] = (jnp.minimum(z, 0.0) - jnp.log(1.0 + jnp.exp(-jnp.abs(z)))) * (1.0 / GLA_GATE_TAU)


def _ab_in(x2, sc, sh, w_in, gate_w, gate_b, tiles_per_b):
    t, d = x2.shape
    cc2 = d
    kw = d // 4
    vw = d // 2
    rank = gate_w.shape[0]
    splits = [cc2, cc2 + kw, cc2 + 2 * kw, cc2 + 2 * kw + vw, cc2 + 2 * kw + 2 * vw]
    wb = w_in.astype(BF16)
    ws = [wb[:, :splits[0]], wb[:, splits[0]:splits[1]], wb[:, splits[1]:splits[2]],
          wb[:, splits[2]:splits[3]], wb[:, splits[3]:splits[4]], wb[:, splits[4]:]]
    tm = TOKEN_TILE
    full = lambda a: pl.BlockSpec(a.shape, lambda i: (0,) * a.ndim)
    row = lambda n: pl.BlockSpec((tm, n), lambda i: (i, 0))
    mod = pl.BlockSpec((1, 1, d), lambda i: (i // tiles_per_b, 0, 0))
    gb2 = gate_b.reshape(1, kw)
    widths = [cc2, kw, kw, vw, vw, kw]
    return pl.pallas_call(
        _ab_in_kernel,
        out_shape=[jax.ShapeDtypeStruct((t, n), BF16 if idx in (0, 3, 4) else F32) for idx, n in enumerate(widths)],
        grid=(t // tm,),
        in_specs=[row(d), mod, mod] + [full(w) for w in ws] + [full(gate_w), full(gb2)],
        out_specs=[row(n) for n in widths],
        compiler_params=_cparams(("parallel",)),
        name="ab_in",
    )(x2, sc, sh, *ws, gate_w, gb2)


def _conv_kernel(u_ref, halo_ref, cw_ref, cb_ref, lg_ref, lb_ref, o_ref, hp_ref, wb_ref, *, ts, cc):
    j = pl.program_id(1)

    def glu(u):
        u = u.astype(F32)
        return u[:, :cc] * _sigmoid(u[:, cc:])

    hp_ref[0, 0:CONV_HALO, :] = jnp.where(j > 0, glu(halo_ref[0]), 0.0)
    hp_ref[0, CONV_HALO:CONV_HALO + ts, :] = glu(u_ref[0])
    nrow = CONV_HALO + ts
    hp0 = hp_ref[0]
    for b in range(1, SUBLANES):
        hp_ref[b] = pltpu.roll(hp0, nrow - b, axis=0)
    for tap in range(CONV_WIDTH):
        wb_ref[tap] = jnp.broadcast_to(cw_ref[tap:tap + 1, :], (SUBLANES, cc))
    shift = CONV_HALO - (CONV_WIDTH - 1)
    for rb in range(ts // CONV_ROWS):
        r0 = rb * CONV_ROWS
        acc = jnp.zeros((CONV_ROWS, cc), F32)
        for tap in range(CONV_WIDTH):
            lo = r0 + shift + tap
            base = lo // SUBLANES * SUBLANES
            w_rows = jnp.concatenate([wb_ref[tap]] * (CONV_ROWS // SUBLANES), axis=0)
            acc = acc + w_rows * hp_ref[lo - base, base:base + CONV_ROWS, :]
        y = _layer_norm(acc + cb_ref[...], lg_ref[...], lb_ref[...])
        o_ref[0, r0:r0 + CONV_ROWS, :] = (y * _sigmoid(y)).astype(o_ref.dtype)


def _conv(u3, conv_w, conv_b, ln_g, ln_b):
    bsz, s, cc2 = u3.shape
    cc = cc2 // 2
    ts = TOKEN_TILE
    hb = ts // CONV_HALO
    vec = lambda a: pl.BlockSpec((1, cc), lambda b, j: (0, 0))
    return pl.pallas_call(
        functools.partial(_conv_kernel, ts=ts, cc=cc),
        out_shape=jax.ShapeDtypeStruct((bsz, s, cc), BF16),
        grid=(bsz, s // ts),
        in_specs=[pl.BlockSpec((1, ts, cc2), lambda b, j: (b, j, 0)),
                  pl.BlockSpec((1, CONV_HALO, cc2), lambda b, j: (b, jnp.maximum(j * hb - 1, 0), 0)),
                  pl.BlockSpec((CONV_WIDTH, cc), lambda b, j: (0, 0)),
                  vec(conv_b), vec(ln_g), vec(ln_b)],
        out_specs=pl.BlockSpec((1, ts, cc), lambda b, j: (b, j, 0)),
        scratch_shapes=[pltpu.VMEM((SUBLANES, CONV_HALO + ts, cc), F32), pltpu.VMEM((CONV_WIDTH, SUBLANES, cc), F32)],
        compiler_params=_cparams(("parallel", "parallel")),
        name="conv",
    )(u3, u3, conv_w, conv_b.reshape(1, cc), ln_g.reshape(1, cc), ln_b.reshape(1, cc))


def _gla_kernel(q_ref, k_ref, v_ref, gl_ref, r_ref, ng_ref, o_ref, st_ref, *, nb, nh, dk, dv, gc, nchunks):
    @pl.when(pl.program_id(1) == 0)
    def _():
        st_ref[...] = jnp.zeros_like(st_ref)

    row = lax.broadcasted_iota(I32, (gc, gc), 0)
    col = lax.broadcasted_iota(I32, (gc, gc), 1)
    causal = col <= row
    tri = jnp.where(causal, 1.0, 0.0).astype(BF16)
    scale = dk ** -0.5

    ks = [slice(h * dk, (h + 1) * dk) for h in range(nh)]
    vs = [slice(h * dv, (h + 1) * dv) for h in range(nh)]
    streams = [(bb, h) for bb in range(nb) for h in range(nh)]

    def chunk(c, carry):
        r0 = pl.multiple_of(c * gc, gc)
        rows = pl.ds(r0, gc)
        bs = [_dot_01_f32(tri, gl_ref[bb, rows, :]) for bb in range(nb)]
        q_in, k_in, q_st, k_st, decay, v = [], [], [], [], [], []
        for bb in range(nb):
            b = bs[bb]
            b_last = b[gc - 1:gc, :]
            mid = 0.5 * b_last
            q = q_ref[bb, rows, :] * scale
            k = k_ref[bb, rows, :]
            v.append(v_ref[bb, rows, :].astype(BF16))
            q_in.append((q * jnp.exp(b - mid)).astype(BF16))
            k_in.append((k * jnp.exp(mid - b)).astype(BF16))
            q_st.append((q * jnp.exp(b)).astype(BF16))
            k_st.append((k * jnp.exp(b_last - b)).astype(BF16))
            decay.append(jnp.exp(b_last))
        sts = [st_ref[bb, h] for bb, h in streams]
        scores = [_dot_nt(q_in[bb][:, ks[h]], k_in[bb][:, ks[h]]) for bb, h in streams]
        inter = [_dot_nt(q_st[bb][:, ks[h]], st.astype(BF16)) for (bb, h), st in zip(streams, sts)]
        update = [_dot_tn(v[bb][:, vs[h]], k_st[bb][:, ks[h]]) for bb, h in streams]
        atts = [jnp.where(causal, sc, 0.0).astype(BF16) for sc in scores]
        outs = [_dot(att, v[bb][:, vs[h]]) + it for (bb, h), att, it in zip(streams, atts, inter)]
        for (bb, h), st, up, o in zip(streams, sts, update, outs):
            st_ref[bb, h] = st * decay[bb][:, ks[h]] + up
            r = r_ref[bb, rows, vs[h]].astype(F32)
            o = o * lax.rsqrt(jnp.mean(o * o, axis=-1, keepdims=True) + RMS_EPS) * ng_ref[:, vs[h]]
            o_ref[bb, rows, vs[h]] = (o * (r * _sigmoid(r))).astype(o_ref.dtype)
        return carry

    lax.fori_loop(0, nchunks, chunk, 0)


def _gla(q3, k3, v3, gl3, r3, norm_g):
    bsz, s, kw = q3.shape
    vw = v3.shape[-1]
    nh = GLA_HEADS
    dk, dv = kw // nh, vw // nh
    cb = GLA_BLOCK
    gc = GLA_CHUNK
    nb = GLA_BATCH_PER_STEP
    blk = lambda n: pl.BlockSpec((nb, cb, n), lambda b, j: (b, j, 0))
    return pl.pallas_call(
        functools.partial(_gla_kernel, nb=nb, nh=nh, dk=dk, dv=dv, gc=gc, nchunks=cb // gc),
        out_shape=jax.ShapeDtypeStruct((bsz, s, vw), BF16),
        grid=(bsz // nb, s // cb),
        in_specs=[blk(kw), blk(kw), blk(vw), blk(kw), blk(vw), pl.BlockSpec((1, vw), lambda b, j: (0, 0))],
        out_specs=blk(vw),
        scratch_shapes=[pltpu.VMEM((nb, nh, dv, dk), F32)],
        compiler_params=_cparams(("parallel", "arbitrary")),
        name="gla",
    )(q3, k3, v3, gl3, r3, norm_g.reshape(1, vw))


def _router_weights(w_group, b_group, w_router, b_router):
    d = w_router.shape[0]
    fill = LANES - MOE_EXPERTS - MOE_GROUPS
    wcat = jnp.concatenate([w_router, w_group, jnp.zeros((d, fill), F32)], axis=1)
    w_hi = wcat.astype(BF16)
    w_mid = (wcat - w_hi.astype(F32)).astype(BF16)
    w_lo = (wcat - w_hi.astype(F32) - w_mid.astype(F32)).astype(BF16)
    bcat = jnp.concatenate([b_router, b_group, jnp.zeros((fill,), F32)]).reshape(1, LANES)
    return jnp.concatenate([w_hi, w_mid, w_lo], axis=1), bcat


def _proj_ln_kernel(*refs, n_in, alpha):
    a_refs, w_refs = refs[:n_in], refs[n_in:2 * n_in]
    x_ref, gate_ref, lg_ref, lb_ref, sc_ref, sh_ref, wr_ref, br_ref, o_ref, lo_ref = refs[2 * n_in:]
    mix = _dot(a_refs[0][...], w_refs[0][...])
    for a_ref, w_ref in zip(a_refs[1:], w_refs[1:]):
        mix = mix + _dot(a_ref[...], w_ref[...])
    y = alpha * x_ref[...] + (1.0 + gate_ref[0]) * mix
    x1 = _layer_norm(y, lg_ref[...], lb_ref[...])
    o_ref[...] = x1
    h_hi, h_mid, h_lo = _split3(x1 * (1.0 + sc_ref[0]) + sh_ref[0])
    pa = _dot(h_hi, wr_ref[...])
    pb = _dot(h_mid, wr_ref[:, :2 * LANES])
    pc = _dot(h_lo, wr_ref[:, :LANES])
    small = (pa[:, 2 * LANES:] + pc) + pb[:, LANES:]
    lo_ref[...] = pa[:, :LANES] + ((pa[:, LANES:2 * LANES] + pb[:, :LANES]) + small) + br_ref[...]


def _proj_ln(acts, weights, x2, gate, ln_g, ln_b, sc_f, sh_f, w_route, b_route, tiles_per_b, alpha):
    t, d = x2.shape
    tm = TOKEN_TILE
    n_in = len(acts)
    row = lambda n: pl.BlockSpec((tm, n), lambda i: (i, 0))
    full = lambda a: pl.BlockSpec(a.shape, lambda i: (0,) * a.ndim)
    vec = pl.BlockSpec((1, d), lambda i: (0, 0))
    mod = pl.BlockSpec((1, 1, d), lambda i: (i // tiles_per_b, 0, 0))
    return pl.pallas_call(
        functools.partial(_proj_ln_kernel, n_in=n_in, alpha=alpha),
        out_shape=[jax.ShapeDtypeStruct((t, d), F32), jax.ShapeDtypeStruct((t, LANES), F32)],
        grid=(t // tm,),
        in_specs=[row(a.shape[1]) for a in acts] + [full(w) for w in weights]
                 + [row(d), mod, vec, vec, mod, mod, full(w_route), full(b_route)],
        out_specs=[row(d), row(LANES)],
        compiler_params=_cparams(("parallel",)),
        name="proj_ln",
    )(*acts, *weights, x2, gate, ln_g.reshape(1, d), ln_b.reshape(1, d), sc_f, sh_f, w_route, b_route)


def _mla_in_kernel(x_ref, sc_ref, sh_ref, pos_ref, invf_ref, sign_ref, win_ref, gq_ref, gkv_ref,
                   wqa_ref, wqb_ref, wk_ref, wvt_ref, q_ref, k_ref, vt_ref, *, nh, q_lora, kv_lora, scale, tk):
    h = (x_ref[...] * (1.0 + sc_ref[0]) + sh_ref[0]).astype(BF16)
    u = _dot(h, win_ref[...])
    cq = u[:, :q_lora]
    ckv = u[:, q_lora:q_lora + kv_lora]
    kr = u[:, q_lora + kv_lora:q_lora + kv_lora + LANES]
    kr_sw = u[:, q_lora + kv_lora + LANES:]
    cqn = (cq * lax.rsqrt(jnp.mean(cq * cq, axis=-1, keepdims=True) + RMS_EPS) * gq_ref[...]).astype(BF16)
    kvn = (ckv * lax.rsqrt(jnp.mean(ckv * ckv, axis=-1, keepdims=True) + RMS_EPS) * gkv_ref[...]).astype(BF16)
    ang = pos_ref[...].astype(F32) * invf_ref[...]
    cos = jnp.cos(ang)
    sin = jnp.sin(ang) * sign_ref[...]
    kr_rot = (kr * cos + kr_sw * sin).astype(BF16)
    qa = _dot(cqn, wqa_ref[...])
    qb = _dot(cqn, wqb_ref[...])
    kv = _dot(kvn, wk_ref[...])
    hw = 2 * LANES
    for hd in range(nh):
        q_ref[:, hd * hw:hd * hw + LANES] = (qa[:, hd * hw:hd * hw + LANES] * scale).astype(BF16)
        rope = qa[:, hd * hw + LANES:(hd + 1) * hw] * cos + qb[:, hd * LANES:(hd + 1) * LANES] * sin
        q_ref[:, hd * hw + LANES:(hd + 1) * hw] = (rope * scale).astype(BF16)
        k_ref[:, hd * hw:hd * hw + LANES] = kv[:, hd * LANES:(hd + 1) * LANES].astype(BF16)
        k_ref[:, hd * hw + LANES:(hd + 1) * hw] = kr_rot
    vt = _dot_nt(wvt_ref[...], kvn).astype(BF16)
    for c in range(vt.shape[1] // tk):
        vt_ref[0, c] = vt[:, c * tk:(c + 1) * tk]


def _mla_in(x2, sc, sh, pos2, w_in, gq, gkv, w_uq, w_ukv, tiles_per_b):
    t, d = x2.shape
    nh = MLA_HEADS
    q_lora, kv_lora = gq.shape[0], gkv.shape[0]
    half = MLA_ROPE // 2
    pad = LANES - MLA_ROPE
    kr_w = w_in[:, q_lora + kv_lora:]
    kr_sw = jnp.concatenate([kr_w[:, half:], kr_w[:, :half]], axis=1)
    zpad = jnp.zeros((d, pad), w_in.dtype)
    win_ext = jnp.concatenate([w_in[:, :q_lora + kv_lora], kr_w, zpad, kr_sw, zpad], axis=1).astype(BF16)
    wq = w_uq.reshape(q_lora, nh, MLA_NOPE + MLA_ROPE)
    q_nope, q_rope = wq[:, :, :MLA_NOPE], wq[:, :, MLA_NOPE:]
    q_rope_sw = jnp.concatenate([q_rope[:, :, half:], q_rope[:, :, :half]], axis=2)
    zq = jnp.zeros((q_lora, nh, pad), w_uq.dtype)
    wqa = jnp.concatenate([q_nope, q_rope, zq], axis=2).reshape(q_lora, nh * 2 * LANES).astype(BF16)
    wqb = jnp.concatenate([q_rope_sw, zq], axis=2).reshape(q_lora, nh * LANES).astype(BF16)
    wkv = w_ukv.reshape(kv_lora, nh, MLA_NOPE + MLA_V)
    wk = wkv[:, :, :MLA_NOPE].reshape(kv_lora, nh * MLA_NOPE).astype(BF16)
    wvt = wkv[:, :, MLA_NOPE:].reshape(kv_lora, nh * MLA_V).T.astype(BF16)
    inv_freq = 1.0 / (ROPE_THETA ** (jnp.arange(0, MLA_ROPE, 2, dtype=F32) / MLA_ROPE))
    invf = jnp.concatenate([inv_freq, inv_freq, jnp.zeros((pad,), F32)]).reshape(1, LANES)
    sign = jnp.concatenate([-jnp.ones((half,), F32), jnp.ones((half,), F32), jnp.zeros((pad,), F32)]).reshape(1, LANES)
    tm = TOKEN_TILE
    full = lambda a: pl.BlockSpec(a.shape, lambda i: (0,) * a.ndim)
    row = lambda n: pl.BlockSpec((tm, n), lambda i: (i, 0))
    mod = pl.BlockSpec((1, 1, d), lambda i: (i // tiles_per_b, 0, 0))
    gq2, gkv2 = gq.reshape(1, q_lora), gkv.reshape(1, kv_lora)
    scale = (MLA_NOPE + MLA_ROPE) ** -0.5 * 1.4426950408889634
    tk = ATTN_TK
    kt_per_tile = tm // tk
    s = tiles_per_b * tm
    return pl.pallas_call(
        functools.partial(_mla_in_kernel, nh=nh, q_lora=q_lora, kv_lora=kv_lora, scale=scale, tk=tk),
        out_shape=[jax.ShapeDtypeStruct((t, nh * 2 * LANES), BF16), jax.ShapeDtypeStruct((t, nh * 2 * LANES), BF16),
                   jax.ShapeDtypeStruct((t // s, s // tk, nh * MLA_V, tk), BF16)],
        grid=(t // tm,),
        in_specs=[row(d), mod, mod, row(1), full(invf), full(sign), full(win_ext), full(gq2), full(gkv2),
                  full(wqa), full(wqb), full(wk), full(wvt)],
        out_specs=[row(nh * 2 * LANES), row(nh * 2 * LANES),
                   pl.BlockSpec((1, kt_per_tile, nh * MLA_V, tk),
                                lambda i: (i // tiles_per_b, i % tiles_per_b, 0, 0))],
        compiler_params=_cparams(("parallel",)),
        name="mla_in",
    )(x2, sc, sh, pos2, invf, sign, win_ext, gq2, gkv2, wqa, wqb, wk, wvt)


def _attn_kernel(q_ref, k_ref, vt_ref, o_ref, acc_ref, s0_ref, s1_ref, *, s, tq, tk, hb):
    kpq = tq // tk
    assert kpq == 2, "the pipeline below alternates two score buffers over pairs of key tiles"
    hw = 2 * LANES
    ones = jnp.ones((8, tk), BF16)

    def q_block(qi, carry):
        qrows = pl.ds(pl.multiple_of(qi * tq, tq), tq)
        acc_ref[...] = jnp.zeros_like(acc_ref)

        def scores(j, s_ref, c0=0, nc=tq, block=qi):
            krows = pl.ds(pl.multiple_of(j * tk, tk), tk)
            cols = pl.ds(pl.multiple_of(block * tq + c0, tk), nc)
            for h in range(hb):
                s_ref[h, :, c0:c0 + nc] = _dot_nt(k_ref[krows, h * hw:(h + 1) * hw], q_ref[cols, h * hw:(h + 1) * hw])

        def tile(j, s_ref, stats, masked, c0=0, nc=tq):
            def put(full, part):
                pieces = ([full[:, :c0]] if c0 else []) + [part] + ([full[:, c0 + nc:]] if c0 + nc < tq else [])
                return pieces[0] if len(pieces) == 1 else jnp.concatenate(pieces, axis=1)
            ps, alphas, out = [], [], []
            for h in range(hb):
                m, st = stats[2 * h][:, c0:c0 + nc], s_ref[h, :, c0:c0 + nc]
                if masked:
                    key = j * tk + lax.broadcasted_iota(I32, (tk, nc), 0)
                    qry = qi * tq + c0 + lax.broadcasted_iota(I32, (tk, nc), 1)
                    st = jnp.where(key <= qry, st, -jnp.inf)
                m_new = jnp.maximum(m, jnp.max(st, axis=0, keepdims=True))
                ps.append(jnp.exp2(st - m_new).astype(BF16))
                alphas.append(jnp.exp2(m - m_new))
                out.append(put(stats[2 * h], m_new))
            for h in range(hb):
                acc_ref[h, :, c0:c0 + nc] = (alphas[h] * acc_ref[h, :, c0:c0 + nc]
                                             + _dot(vt_ref[0, j, h * MLA_V:(h + 1) * MLA_V, :], ps[h]))
                l_new = alphas[h] * stats[2 * h + 1][:, c0:c0 + nc] + _dot(ones, ps[h])[0:1]
                out.insert(2 * h + 1, put(stats[2 * h + 1], l_new))
            return tuple(out)

        stats = (jnp.full((1, tq), -jnp.inf, F32), jnp.zeros((1, tq), F32)) * hb

        def pair(jj, c):
            j = 2 * jj
            scores(j + 1, s1_ref)
            c = tile(j, s0_ref, c, False)
            scores(j + 2, s0_ref)
            return tile(j + 1, s1_ref, c, False)

        stats = lax.fori_loop(0, qi, pair, stats)
        scores(2 * qi + 1, s1_ref, tk, tk)
        stats = tile(2 * qi, s0_ref, stats, True, 0, tk)
        stats = tile(2 * qi, s0_ref, stats, False, tk, tk)
        stats = tile(2 * qi + 1, s1_ref, stats, True, tk, tk)
        scores(0, s0_ref, block=jnp.minimum(qi + 1, nq - 1))
        for h in range(hb):
            o_ref[qrows, h * MLA_V:(h + 1) * MLA_V] = (acc_ref[h] / stats[2 * h + 1]).T.astype(o_ref.dtype)
        return carry

    nq = s // tq
    for h in range(hb):
        s0_ref[h] = _dot_nt(k_ref[0:tk, h * hw:(h + 1) * hw], q_ref[0:tq, h * hw:(h + 1) * hw])
    lax.fori_loop(0, nq, q_block, 0)


def _attn(q, k, vt, bsz, s):
    nh = MLA_HEADS
    tq, tk = ATTN_TQ, ATTN_TK
    hb = ATTN_HEADS_PER_STEP
    hw = 2 * LANES
    return pl.pallas_call(
        functools.partial(_attn_kernel, s=s, tq=tq, tk=tk, hb=hb),
        out_shape=jax.ShapeDtypeStruct((bsz * s, nh * MLA_V), BF16),
        grid=(bsz, nh // hb),
        in_specs=[pl.BlockSpec((s, hb * hw), lambda b, h: (b, h)),
                  pl.BlockSpec((s, hb * hw), lambda b, h: (b, h)),
                  pl.BlockSpec((1, s // tk, hb * MLA_V, tk), lambda b, h: (b, 0, h, 0))],
        out_specs=pl.BlockSpec((s, hb * MLA_V), lambda b, h: (b, h)),
        scratch_shapes=[pltpu.VMEM((hb, MLA_V, tq), F32), pltpu.VMEM((hb, tk, tq), F32),
                        pltpu.VMEM((hb, tk, tq), F32)],
        compiler_params=_cparams(("parallel", "parallel")),
        name="attn",
    )(q, k, vt)


def _route_kernel(logits_ref, lp_ref, wt_ref, tm_ref, offs_ref, te_ref, na_ref,
                  upper_ref, carry_ref, *, tm, ne, ng, row_tile, nt_pad):
    i = pl.program_id(0)
    epg = ne // ng

    @pl.when(i == 0)
    def _():
        r = lax.broadcasted_iota(I32, (tm, tm), 0)
        c = lax.broadcasted_iota(I32, (tm, tm), 1)
        upper_ref[...] = jnp.where(r < c, 1.0, 0.0).astype(BF16)
        carry_ref[...] = jnp.zeros_like(carry_ref)

    lt = logits_ref[...].T
    lr = lt[0:ne]
    grow = lax.broadcasted_iota(I32, (8, tm), 0).astype(F32)
    lg = jnp.where(grow < ng, lt[ne:ne + 8], -jnp.inf)
    gmax = jnp.max(lg, axis=0, keepdims=True)
    g_idx = jnp.min(jnp.where(lg == gmax, grow, 1e9), axis=0, keepdims=True)
    g_w = 1.0 / jnp.sum(jnp.exp(lg - gmax), axis=0, keepdims=True)
    erow = lax.broadcasted_iota(I32, (ne, tm), 0).astype(F32)
    in_group = jnp.floor(erow * (1.0 / epg)) == g_idx
    sel = jnp.where(in_group, lr, -jnp.inf)
    v1 = jnp.max(sel, axis=0, keepdims=True)
    i1 = jnp.min(jnp.where(sel == v1, erow, 1e9), axis=0, keepdims=True)
    sel2 = jnp.where(erow == i1, -jnp.inf, sel)
    v2 = jnp.max(sel2, axis=0, keepdims=True)
    i2 = jnp.min(jnp.where(sel2 == v2, erow, 1e9), axis=0, keepdims=True)
    t = jnp.exp(v2 - v1)
    w1 = g_w / (1.0 + t)
    w2 = g_w * t / (1.0 + t)
    oh1 = erow == i1
    oh2 = erow == i2
    member = jnp.where(oh1 | oh2, 1.0, 0.0)
    lcnt = jnp.sum(member, axis=1, keepdims=True)
    lcnt = jnp.floor((lcnt + (RUN_ALIGN - 1)) * (1.0 / RUN_ALIGN)) * RUN_ALIGN
    er_ = lax.broadcasted_iota(I32, (ne, ne), 0)
    ec_ = lax.broadcasted_iota(I32, (ne, ne), 1)
    lstart = jnp.dot(jnp.where(ec_ < er_, 1.0, 0.0).astype(F32), jnp.broadcast_to(lcnt, (ne, LANES)),
                     precision=HIGHEST, preferred_element_type=F32)[:, 0:1]
    lrank = _dot(member.astype(BF16), upper_ref[...]) + lstart
    p1 = jnp.sum(jnp.where(oh1, lrank, 0.0), axis=0, keepdims=True)
    p2 = jnp.sum(jnp.where(oh2, lrank, 0.0), axis=0, keepdims=True)

    orow = lax.broadcasted_iota(I32, (8, tm), 0)
    lp_ref[...] = jnp.where(orow == 0, p1, jnp.where(orow == 1, p2, 0.0)).astype(I32)
    wrow = lax.broadcasted_iota(I32, (LANES, tm), 0)
    wt_ref[...] = jnp.where(wrow == 0, w1, jnp.where(wrow == 1, w2,
                                                     jnp.where(wrow == 2, p1, jnp.where(wrow == 3, p2, 0.0)))).T
    mr = lax.broadcasted_iota(I32, (ne, LANES), 0)
    mc = lax.broadcasted_iota(I32, (ne, LANES), 1)
    to_row = lambda col, lane0: jnp.sum(jnp.where(mr + lane0 == mc, col, 0.0), axis=0, keepdims=True)
    total = jnp.sum(lcnt, axis=0, keepdims=True)
    lane = lax.broadcasted_iota(I32, (1, LANES), 1)
    packed = (to_row(lstart, 0) + to_row(lcnt, ne) + to_row(carry_ref[...], 2 * ne)
              + jnp.where(lane == 3 * ne, total, 0.0))
    trow = lax.broadcasted_iota(I32, (8, LANES), 0)
    tm_ref[...] = jnp.where(trow == 0, packed, 0.0).astype(I32)
    carry_ref[...] = carry_ref[...] + lcnt

    @pl.when(i == pl.num_programs(0) - 1)
    def _():
        cnt = carry_ref[...]
        ntl = jnp.floor((cnt + (row_tile - 1)) * (1.0 / row_tile))
        incl = jnp.where(ec_ <= er_, 1.0, 0.0).astype(F32)
        ends = jnp.dot(incl, jnp.broadcast_to(ntl, (ne, LANES)), precision=HIGHEST,
                       preferred_element_type=F32)
        starts = ends - ntl
        offs_ref[...] = jnp.concatenate([starts * row_tile, ends * row_tile, jnp.broadcast_to(ntl, (ne, LANES)),
                                         jnp.zeros((8, LANES), F32)], axis=0).astype(I32)
        tile = lax.broadcasted_iota(I32, (ne, nt_pad), 1).astype(F32)
        te = jnp.sum(jnp.where(ends[:, 0:1] <= tile, 1.0, 0.0), axis=0, keepdims=True)
        te_ref[...] = jnp.broadcast_to(jnp.minimum(te, ne - 1.0), (8, nt_pad)).astype(I32)
        na_ref[...] = jnp.broadcast_to(ends[ne - 1:ne, :], (8, LANES)).astype(I32)


def _moe_tiles(t):
    rows = 2 * t + (RUN_ALIGN - 1) * MOE_EXPERTS * (t // TOKEN_TILE)
    nt_max = -(-rows // MOE_ROW_TILE) + MOE_EXPERTS
    nt_pad = -(-nt_max // LANES) * LANES
    return nt_max, nt_pad


def _sorted_rows(tm):
    return -(-(2 * tm + (RUN_ALIGN - 1) * MOE_EXPERTS) // LANES) * LANES


def _route(logits):
    t = logits.shape[0]
    ne, ng = MOE_EXPERTS, MOE_GROUPS
    tm = TOKEN_TILE
    _, nt_pad = _moe_tiles(t)
    const = lambda shp: pl.BlockSpec(shp, lambda i: (0,) * len(shp))
    return pl.pallas_call(
        functools.partial(_route_kernel, tm=tm, ne=ne, ng=ng, row_tile=MOE_ROW_TILE, nt_pad=nt_pad),
        out_shape=[jax.ShapeDtypeStruct((8, t), I32), jax.ShapeDtypeStruct((t, LANES), F32),
                   jax.ShapeDtypeStruct((8 * (t // tm), LANES), I32),
                   jax.ShapeDtypeStruct((3 * ne + 8, LANES), I32), jax.ShapeDtypeStruct((8, nt_pad), I32),
                   jax.ShapeDtypeStruct((8, LANES), I32)],
        grid=(t // tm,),
        in_specs=[pl.BlockSpec((tm, LANES), lambda i: (i, 0))],
        out_specs=[pl.BlockSpec((8, tm), lambda i: (0, i)), pl.BlockSpec((tm, LANES), lambda i: (i, 0)),
                   pl.BlockSpec((8, LANES), lambda i: (i, 0)),
                   const((3 * ne + 8, LANES)), const((8, nt_pad)), const((8, LANES))],
        scratch_shapes=[pltpu.VMEM((tm, tm), BF16), pltpu.VMEM((ne, 1), F32)],
        compiler_params=_cparams(("arbitrary",)),
        name="moe_route",
    )(logits)


def _start_runs(tm_ref, tile, ne, copy):
    for e in range(ne):
        lstart = tm_ref[tile, e]
        n = tm_ref[tile, ne + e]
        before = tm_ref[tile, 2 * ne + e]
        _binary_pieces(n, TOKEN_TILE, lambda off, rows: copy(
            e, pl.multiple_of(lstart + off, RUN_ALIGN), pl.multiple_of(before + off, RUN_ALIGN), rows))


def _binary_pieces(n, n_max, copy):
    del n_max
    nbig = n >> (RUN_PIECE.bit_length() - 1)

    def big_piece(c, carry):
        copy(c * RUN_PIECE, RUN_PIECE).start()
        return carry

    lax.fori_loop(0, nbig, big_piece, 0)
    off = nbig * RUN_PIECE
    p = RUN_PIECE // 2
    while p >= RUN_ALIGN:
        @pl.when((n & p) != 0)
        def _():
            copy(off, p).start()
        off = off + (n & p)
        p //= 2


def _dispatch_kernel(offs_ref, gend_ref, ntl_ref, tm_ref, x_ref, sc_ref, sh_ref, lp_ref, wt_ref, xs_ref,
                     h_ref, z_ref, sem_z, sem_r, *, tm, ne, row_tile, nt_max, ntile, srows):
    i = pl.program_id(0)
    tile = i - 1

    @pl.when(i == 0)
    def _():
        z_ref[...] = jnp.zeros_like(z_ref)
        zero_tile = lambda start: pltpu.make_async_copy(z_ref, xs_ref.at[pl.ds(start, row_tile), :], sem_z)
        for act in ("start", "wait"):
            for e in range(ne):
                @pl.when(ntl_ref[e] > 0)
                def _():
                    getattr(zero_tile(pl.multiple_of(gend_ref[e] - row_tile, row_tile)), act)()
            for back in range(1, nt_max - (2 * tm * ntile) // row_tile + 1):
                @pl.when(nt_max - back >= gend_ref[ne - 1] // row_tile)
                def _():
                    getattr(zero_tile((nt_max - back) * row_tile), act)()
            for spill in range(2):
                getattr(zero_tile((nt_max + spill) * row_tile), act)()

    def wait_tile(t):
        slot = t % 2
        pltpu.make_async_copy(h_ref.at[slot], xs_ref.at[pl.ds(0, srows), :], sem_r.at[slot]).wait()

    @pl.when((tile >= 2) & (tile <= ntile))
    def _():
        wait_tile(tile - 2)

    @pl.when(tile == ntile)
    def _():
        wait_tile(tile - 1)

    @pl.when((tile >= 0) & (tile < ntile))
    def _():
        slot = tile % 2
        h = (x_ref[...] * (1.0 + sc_ref[0]) + sh_ref[0]).astype(BF16)
        row = lax.broadcasted_iota(I32, (srows, tm), 0)
        lp = lp_ref[...]
        signed = jnp.where(row == lp[0:1, :], 1.0, jnp.where(row == lp[1:2, :], -1.0, 0.0)).astype(BF16)
        lane = lax.broadcasted_iota(I32, (tm, LANES), 1)
        wt = wt_ref[...]
        terms = []
        for k in range(2):
            w = wt[:, k:k + 1]
            hi = w.astype(BF16).astype(F32)
            mid = (w - hi).astype(BF16).astype(F32)
            terms += [hi, mid, w - hi - mid]
        extra = jnp.where(lane == SIGN_LANE, 1.0, 0.0)
        for idx, term in enumerate(terms):
            extra = jnp.where(lane == idx, term, extra)
        h_ref[slot] = _dot(signed, jnp.concatenate([h, extra.astype(BF16)], axis=1)).astype(BF16)

        def copy(e, local_row, rows_before, rows):
            dst = pl.multiple_of(offs_ref[e] + rows_before, RUN_ALIGN)
            return pltpu.make_async_copy(h_ref.at[slot, pl.ds(local_row, rows), :], xs_ref.at[pl.ds(dst, rows), :],
                                         sem_r.at[slot])

        _start_runs(tm_ref, tile, ne, copy)
        used = tm_ref[tile, 3 * ne]
        _binary_pieces(srows - used, srows - 2 * tm, lambda off, rows: pltpu.make_async_copy(
            h_ref.at[slot, pl.ds(pl.multiple_of(used + off, RUN_ALIGN), rows), :],
            xs_ref.at[pl.ds(pl.multiple_of((nt_max + slot) * row_tile + off, RUN_ALIGN), rows), :], sem_r.at[slot]))


def _dispatch(x2, sc, sh, lp, wt, tmeta, offs, gend, ntl, tiles_per_b):
    t, d = x2.shape
    tm = TOKEN_TILE
    nt_max, _ = _moe_tiles(t)
    ntile = t // tm
    srows = _sorted_rows(tm)
    cur = lambda i: jnp.clip(i - 1, 0, ntile - 1)
    grid_spec = pltpu.PrefetchScalarGridSpec(
        num_scalar_prefetch=4,
        grid=(ntile + 2,),
        in_specs=[pl.BlockSpec((tm, d), lambda i, *_: (cur(i), 0)),
                  pl.BlockSpec((1, 1, d), lambda i, *_: (cur(i) // tiles_per_b, 0, 0)),
                  pl.BlockSpec((1, 1, d), lambda i, *_: (cur(i) // tiles_per_b, 0, 0)),
                  pl.BlockSpec((8, tm), lambda i, *_: (0, cur(i))),
                  pl.BlockSpec((tm, LANES), lambda i, *_: (cur(i), 0))],
        out_specs=pl.BlockSpec(memory_space=pl.ANY),
        scratch_shapes=[pltpu.VMEM((2, srows, d + LANES), BF16), pltpu.VMEM((MOE_ROW_TILE, d + LANES), BF16),
                        pltpu.SemaphoreType.DMA, pltpu.SemaphoreType.DMA((2,))])
    return pl.pallas_call(
        functools.partial(_dispatch_kernel, tm=tm, ne=MOE_EXPERTS, row_tile=MOE_ROW_TILE, nt_max=nt_max,
                          ntile=ntile, srows=srows),
        out_shape=jax.ShapeDtypeStruct(((nt_max + 2) * MOE_ROW_TILE, d + LANES), BF16),
        grid_spec=grid_spec,
        compiler_params=_cparams(("arbitrary",)),
        name="moe_dispatch",
    )(offs, gend, ntl, tmeta, x2, sc, sh, lp, wt)


def _gmm_kernel(te_ref, na_ref, xs_ref, wg_ref, wu_ref, wd_ref, ys_ref, wgu_buf, wd_buf, *, ff, tr):
    j = pl.program_id(0)

    @pl.when(j < na_ref[0])
    def _():
        changed = (j == 0) | (te_ref[j] != te_ref[jnp.maximum(j - 1, 0)])

        @pl.when(changed)
        def _():
            wgu_buf[:, :ff] = wg_ref[0, 0].astype(BF16)
            wgu_buf[:, ff:] = wu_ref[0, 0].astype(BF16)
            wd_buf[...] = wd_ref[0, 0].astype(BF16)

        halves = [slice(c * (tr // 2), (c + 1) * (tr // 2)) for c in range(2)]
        d = wgu_buf.shape[0]
        riders = [xs_ref[rows, d:].astype(F32) for rows in halves]
        signs = [r[:, SIGN_LANE:SIGN_LANE + 1] for r in riders]
        gus = [_dot(xs_ref[rows, :d] * sg.astype(BF16), wgu_buf[...]) for rows, sg in zip(halves, signs)]
        hids = [(gu[:, :ff] * _sigmoid(gu[:, :ff]) * gu[:, ff:]).astype(BF16) for gu in gus]
        for rows, hid, r, sg in zip(halves, hids, riders, signs):
            w = jnp.where(sg > 0.0, r[:, 0:1] + r[:, 1:2] + r[:, 2:3], -(r[:, 3:4] + r[:, 4:5] + r[:, 5:6]))
            ys_ref[rows, :] = (w * _dot(hid, wd_buf[...])).astype(ys_ref.dtype)

    @pl.when(j >= na_ref[0])
    def _():
        ys_ref[...] = jnp.zeros_like(ys_ref)


def _gmm(xs, te, na, w_gate, w_up, w_down, layer):
    d, ff = w_gate.shape[-2:]
    tr = MOE_ROW_TILE
    ns = te.shape[0] * tr
    act = lambda j, te_ref, na_ref: jnp.minimum(j, na_ref[0] - 1)
    grid_spec = pltpu.PrefetchScalarGridSpec(
        num_scalar_prefetch=2,
        grid=(ns // tr,),
        in_specs=[pl.BlockSpec((tr, d + LANES), lambda j, te_ref, na_ref: (act(j, te_ref, na_ref), 0)),
                  pl.BlockSpec((1, 1, d, ff), lambda j, te_ref, na_ref: (layer, te_ref[act(j, te_ref, na_ref)], 0, 0)),
                  pl.BlockSpec((1, 1, d, ff), lambda j, te_ref, na_ref: (layer, te_ref[act(j, te_ref, na_ref)], 0, 0)),
                  pl.BlockSpec((1, 1, ff, d), lambda j, te_ref, na_ref: (layer, te_ref[act(j, te_ref, na_ref)], 0, 0))],
        out_specs=pl.BlockSpec((tr, d), lambda j, te_ref, na_ref: (j, 0)),
        scratch_shapes=[pltpu.VMEM((d, 2 * ff), BF16), pltpu.VMEM((ff, d), BF16)])
    return pl.pallas_call(
        functools.partial(_gmm_kernel, ff=ff, tr=tr),
        out_shape=jax.ShapeDtypeStruct((ns, d), BF16),
        grid_spec=grid_spec,
        compiler_params=_cparams(("arbitrary",)),
        name="moe_gmm",
    )(te, na, xs, w_gate, w_up, w_down)


def _combine_kernel(offs_ref, tm_ref, wt_ref, x_ref, gate_ref, lg_ref, lb_ref, ys_ref, o_ref,
                    buf_ref, sem_r, *, tm, ne, ntile, srows, alpha):
    i = pl.program_id(0)

    def fetch(tile):
        slot = tile % 2

        def copy(e, local_row, rows_before, rows):
            src = pl.multiple_of(offs_ref[e] + rows_before, RUN_ALIGN)
            return pltpu.make_async_copy(ys_ref.at[pl.ds(src, rows), :], buf_ref.at[slot, pl.ds(local_row, rows), :],
                                         sem_r.at[slot])

        _start_runs(tm_ref, tile, ne, copy)
        used = tm_ref[tile, 3 * ne]
        _binary_pieces(srows - used, srows - 2 * tm, lambda off, rows: pltpu.make_async_copy(
            ys_ref.at[pl.ds(pl.multiple_of(off, RUN_ALIGN), rows), :],
            buf_ref.at[slot, pl.ds(pl.multiple_of(used + off, RUN_ALIGN), rows), :], sem_r.at[slot]))

    def wait(tile):
        slot = tile % 2
        pltpu.make_async_copy(ys_ref.at[pl.ds(0, srows), :], buf_ref.at[slot], sem_r.at[slot]).wait()

    def finish(tile):
        w = wt_ref[...]
        rows = buf_ref[tile % 2]
        col = lax.broadcasted_iota(I32, (tm, srows), 1).astype(F32)
        both = jnp.where(col == w[:, 2:3], 1.0, jnp.where(col == w[:, 3:4], 1.0, 0.0)).astype(BF16)
        ffn = _dot(both, rows)
        y = alpha * x_ref[...] + (1.0 + gate_ref[0]) * ffn
        o_ref[...] = _layer_norm(y, lg_ref[...], lb_ref[...])

    @pl.when(i == 0)
    def _():
        fetch(i)

    @pl.when((i > 0) & (i < ntile))
    def _():
        wait(i - 1)
        fetch(i)
        finish(i - 1)

    @pl.when(i == ntile)
    def _():
        wait(i - 1)
        finish(i - 1)


def _combine(ys, tmeta, wt, x2, gate, ln_g, ln_b, offs, tiles_per_b, alpha):
    t, d = x2.shape
    tm = TOKEN_TILE
    ntile = t // tm
    srows = _sorted_rows(tm)
    prev = lambda i: jnp.maximum(i - 1, 0)
    vec = pl.BlockSpec((1, d), lambda i, *_: (0, 0))
    grid_spec = pltpu.PrefetchScalarGridSpec(
        num_scalar_prefetch=2,
        grid=(ntile + 1,),
        in_specs=[pl.BlockSpec((tm, LANES), lambda i, *_: (prev(i), 0)),
                  pl.BlockSpec((tm, d), lambda i, *_: (prev(i), 0)),
                  pl.BlockSpec((1, 1, d), lambda i, *_: (prev(i) // tiles_per_b, 0, 0)),
                  vec, vec,
                  pl.BlockSpec(memory_space=pl.ANY)],
        out_specs=pl.BlockSpec((tm, d), lambda i, *_: (prev(i), 0)),
        scratch_shapes=[pltpu.VMEM((2, srows, d), BF16), pltpu.SemaphoreType.DMA((2,))])
    return pl.pallas_call(
        functools.partial(_combine_kernel, tm=tm, ne=MOE_EXPERTS, ntile=ntile, srows=srows, alpha=alpha),
        out_shape=jax.ShapeDtypeStruct((t, d), F32),
        grid_spec=grid_spec,
        compiler_params=_cparams(("arbitrary",)),
        name="moe_combine",
    )(offs, tmeta, wt, x2, gate, ln_g.reshape(1, d), ln_b.reshape(1, d), ys)


def _moe_block(x2, logits, sc, sh, gate, ln_g, ln_b, w_gate, w_up, w_down, layer, tiles_per_b, alpha):
    ne = MOE_EXPERTS
    nt_max, _ = _moe_tiles(x2.shape[0])
    lp, wt, tmeta, meta, te, na = _route(logits)
    offs, gend, ntl = meta[:ne, 0], meta[ne:2 * ne, 0], meta[2 * ne:3 * ne, 0]
    tmeta = tmeta.reshape(-1, SUBLANES, LANES)[:, 0, :]
    xs = _dispatch(x2, sc, sh, lp, wt, tmeta, offs, gend, ntl, tiles_per_b)
    ys = _gmm(xs, te[0, :nt_max], na[0, :1], w_gate, w_up, w_down, layer)
    return _combine(ys, tmeta, wt, x2, gate, ln_g, ln_b, offs, tiles_per_b, alpha)


def kernel(x, c, positions, ada_w, ada_b, ln_mix_g, ln_mix_b, ln_ffn_g, ln_ffn_b, ab_w_in, conv_w, conv_b, conv_ln_g, conv_ln_b, gla_gate_w, gla_gate_b, gla_norm_g, ab_w_out, mla_w_in, mla_q_norm_g, mla_kv_norm_g, mla_w_uq, mla_w_ukv, mla_w_out, moe_w_group, moe_b_group, moe_w_router, moe_b_router, moe_w_gate, moe_w_up, moe_w_down):
    bsz, s, d = x.shape
    depth = ada_w.shape[0]
    t = bsz * s
    tiles_per_b = s // TOKEN_TILE
    alpha = (2 * depth) ** 0.25
    mod = _ada(c, ada_w, ada_b).reshape(depth, bsz, 6, 1, d)
    x2 = x.reshape(t, d)
    for layer in range(depth):
        sh_m, sc_m, g_m, sh_f, sc_f, g_f = (mod[layer, :, n] for n in range(6))
        i = layer // 2
        if layer % 2 == 0:
            uc, q, k, v, r, gl = _ab_in(x2, sc_m, sh_m, ab_w_in[i], gla_gate_w[i], gla_gate_b[i], tiles_per_b)
            y_a = _conv(uc.reshape(bsz, s, -1), conv_w[i], conv_b[i], conv_ln_g[i], conv_ln_b[i])
            b3 = lambda a: a.reshape(bsz, s, -1)
            y_b = _gla(b3(q), b3(k), b3(v), b3(gl), b3(r), gla_norm_g[i])
            w_out = ab_w_out[i].astype(BF16)
            cc = y_a.shape[-1]
            acts = [y_a.reshape(t, cc), y_b.reshape(t, -1)]
            weights = [w_out[:cc], w_out[cc:]]
        else:
            qc, kc, vv = _mla_in(x2, sc_m, sh_m, positions.reshape(t, 1), mla_w_in[i], mla_q_norm_g[i],
                                 mla_kv_norm_g[i], mla_w_uq[i], mla_w_ukv[i], tiles_per_b)
            acts = [_attn(qc, kc, vv, bsz, s)]
            weights = [mla_w_out[i].astype(BF16)]
        w_route, b_route = _router_weights(moe_w_group[layer], moe_b_group[layer], moe_w_router[layer],
                                           moe_b_router[layer])
        x2, logits = _proj_ln(acts, weights, x2, g_m, ln_mix_g[layer], ln_mix_b[layer], sc_f, sh_f, w_route, b_route,
                              tiles_per_b, alpha)
        x2 = _moe_block(x2, logits, sc_f, sh_f, g_f, ln_ffn_g[layer], ln_ffn_b[layer], moe_w_gate, moe_w_up,
                        moe_w_down, layer, tiles_per_b, alpha)
    return x2.reshape(bsz, s, d)
```

```python
import functools

import jax
import jax.numpy as jnp
from jax import lax
from jax.experimental import pallas as pl
from jax.experimental.pallas import tpu as pltpu

F32 = jnp.float32
BF16 = jnp.bfloat16
I32 = jnp.int32
HIGHEST = lax.Precision.HIGHEST

LN_EPS = 1e-5
RMS_EPS = 1e-6
CONV_WIDTH = 31
GLA_HEADS = 4
GLA_GATE_TAU = 16.0
MLA_HEADS = 8
MLA_NOPE = 128
MLA_ROPE = 64
MLA_V = 128
ROPE_THETA = 10000.0
MOE_GROUPS = 4
MOE_EXPERTS_PER_GROUP = 8
MOE_EXPERTS = MOE_GROUPS * MOE_EXPERTS_PER_GROUP

LANES = 128
SUBLANES = 8
TOKEN_TILE = 512
GLA_CHUNK = 128
GLA_BLOCK = 512
GLA_BATCH_PER_STEP = 4
CONV_ROWS = 32
CONV_HALO = 32
ATTN_TQ = 512
ATTN_TK = 256
ATTN_HEADS_PER_STEP = 4
MOE_ROW_TILE = 512
RUN_PIECE = 64
RUN_ALIGN = 16
SIGN_LANE = 6
VMEM_LIMIT = 48 * 1024 * 1024


def _cparams(sem):
    return pltpu.CompilerParams(dimension_semantics=sem, vmem_limit_bytes=VMEM_LIMIT)


def _sigmoid(x):
    return 1.0 / (1.0 + jnp.exp(-x))


def _dot(a, b):
    return jnp.dot(a, b, preferred_element_type=F32)


def _dot_nt(a, b):
    return lax.dot_general(a, b, (((1,), (1,)), ((), ())), preferred_element_type=F32)


def _dot_tn(a, b):
    return lax.dot_general(a, b, (((0,), (0,)), ((), ())), preferred_element_type=F32)


def _split3(x):
    hi = x.astype(BF16)
    r1 = x - hi.astype(F32)
    mid = r1.astype(BF16)
    lo = (r1 - mid.astype(F32)).astype(BF16)
    return hi, mid, lo


def _dot_01_f32(a01, x):
    hi, mid, lo = _split3(x)
    return _dot(a01, hi) + (_dot(a01, mid) + _dot(a01, lo))


def _layer_norm(y, g, b):
    mu = jnp.mean(y, axis=-1, keepdims=True)
    d = y - mu
    var = jnp.mean(d * d, axis=-1, keepdims=True)
    return d * lax.rsqrt(var + LN_EPS) * g + b


def _ada_kernel(c_ref, w_ref, b_ref, o_ref):
    c = c_ref[...]
    n = c.shape[0]
    c_hi, c_mid, c_lo = _split3(c * _sigmoid(c))
    w_hi, w_mid, w_lo = _split3(w_ref[0])
    pa = _dot(jnp.concatenate([c_hi, c_mid, c_lo], axis=0), w_hi)
    pb = _dot(jnp.concatenate([c_hi, c_mid], axis=0), w_mid)
    pc = _dot(c_hi, w_lo)
    small = (pa[2 * n:] + pc) + pb[n:]
    o_ref[0] = pa[:n] + ((pa[n:2 * n] + pb[:n]) + small) + b_ref[0]


def _ada(c, ada_w, ada_b):
    depth, d, n = ada_w.shape
    bsz = c.shape[0]
    tn = 1536
    return pl.pallas_call(
        _ada_kernel,
        out_shape=jax.ShapeDtypeStruct((depth, bsz, n), F32),
        grid=(depth, n // tn),
        in_specs=[pl.BlockSpec((bsz, d), lambda l, j: (0, 0)),
                  pl.BlockSpec((1, d, tn), lambda l, j: (l, 0, j)),
                  pl.BlockSpec((1, 1, tn), lambda l, j: (l, 0, j))],
        out_specs=pl.BlockSpec((1, bsz, tn), lambda l, j: (l, 0, j)),
        compiler_params=_cparams(("parallel", "parallel")),
        name="ada",
    )(c, ada_w, ada_b.reshape(depth, 1, n))


def _ab_in_kernel(x_ref, sc_ref, sh_ref, wc_ref, wq_ref, wk_ref, wv_ref, wr_ref, wg_ref, gw_ref, gb_ref,
                  uc_ref, q_ref, k_ref, v_ref, r_ref, gl_ref):
    h = (x_ref[...] * (1.0 + sc_ref[0]) + sh_ref[0]).astype(BF16)
    uc_ref[...] = _dot(h, wc_ref[...]).astype(uc_ref.dtype)
    q_ref[...] = _dot(h, wq_ref[...])
    k_ref[...] = _dot(h, wk_ref[...])
    v_ref[...] = _dot(h, wv_ref[...]).astype(v_ref.dtype)
    r_ref[...] = _dot(h, wr_ref[...]).astype(r_ref.dtype)
    g_low = _dot(h, wg_ref[...])
    z = jnp.dot(g_low, gw_ref[...], precision=HIGHEST, preferred_element_type=F32) + gb_ref[...]
    gl_ref[...] = (jnp.minimum(z, 0.0) - jnp.log(1.0 + jnp.exp(-jnp.abs(z)))) * (1.0 / GLA_GATE_TAU)


def _ab_in(x2, sc, sh, w_in, gate_w, gate_b, tiles_per_b):
    t, d = x2.shape
    cc2 = d
    kw = d // 4
    vw = d // 2
    rank = gate_w.shape[0]
    splits = [cc2, cc2 + kw, cc2 + 2 * kw, cc2 + 2 * kw + vw, cc2 + 2 * kw + 2 * vw]
    wb = w_in.astype(BF16)
    ws = [wb[:, :splits[0]], wb[:, splits[0]:splits[1]], wb[:, splits[1]:splits[2]],
          wb[:, splits[2]:splits[3]], wb[:, splits[3]:splits[4]], wb[:, splits[4]:]]
    tm = TOKEN_TILE
    full = lambda a: pl.BlockSpec(a.shape, lambda i: (0,) * a.ndim)
    row = lambda n: pl.BlockSpec((tm, n), lambda i: (i, 0))
    mod = pl.BlockSpec((1, 1, d), lambda i: (i // tiles_per_b, 0, 0))
    gb2 = gate_b.reshape(1, kw)
    widths = [cc2, kw, kw, vw, vw, kw]
    return pl.pallas_call(
        _ab_in_kernel,
        out_shape=[jax.ShapeDtypeStruct((t, n), BF16 if idx in (0, 3, 4) else F32) for idx, n in enumerate(widths)],
        grid=(t // tm,),
        in_specs=[row(d), mod, mod] + [full(w) for w in ws] + [full(gate_w), full(gb2)],
        out_specs=[row(n) for n in widths],
        compiler_params=_cparams(("parallel",)),
        name="ab_in",
    )(x2, sc, sh, *ws, gate_w, gb2)


def _conv_kernel(u_ref, halo_ref, cw_ref, cb_ref, lg_ref, lb_ref, o_ref, hp_ref, wb_ref, *, ts, cc):
    j = pl.program_id(1)

    def glu(u):
        u = u.astype(F32)
        return u[:, :cc] * _sigmoid(u[:, cc:])

    hp_ref[0, 0:CONV_HALO, :] = jnp.where(j > 0, glu(halo_ref[0]), 0.0)
    hp_ref[0, CONV_HALO:CONV_HALO + ts, :] = glu(u_ref[0])
    nrow = CONV_HALO + ts
    hp0 = hp_ref[0]
    for b in range(1, SUBLANES):
        hp_ref[b] = pltpu.roll(hp0, nrow - b, axis=0)
    for tap in range(CONV_WIDTH):
        wb_ref[tap] = jnp.broadcast_to(cw_ref[tap:tap + 1, :], (SUBLANES, cc))
    shift = CONV_HALO - (CONV_WIDTH - 1)
    for rb in range(ts // CONV_ROWS):
        r0 = rb * CONV_ROWS
        acc = jnp.zeros((CONV_ROWS, cc), F32)
        for tap in range(CONV_WIDTH):
            lo = r0 + shift + tap
            base = lo // SUBLANES * SUBLANES
            w_rows = jnp.concatenate([wb_ref[tap]] * (CONV_ROWS // SUBLANES), axis=0)
            acc = acc + w_rows * hp_ref[lo - base, base:base + CONV_ROWS, :]
        y = _layer_norm(acc + cb_ref[...], lg_ref[...], lb_ref[...])
        o_ref[0, r0:r0 + CONV_ROWS, :] = (y * _sigmoid(y)).astype(o_ref.dtype)


def _conv(u3, conv_w, conv_b, ln_g, ln_b):
    bsz, s, cc2 = u3.shape
    cc = cc2 // 2
    ts = TOKEN_TILE
    hb = ts // CONV_HALO
    vec = lambda a: pl.BlockSpec((1, cc), lambda b, j: (0, 0))
    return pl.pallas_call(
        functools.partial(_conv_kernel, ts=ts, cc=cc),
        out_shape=jax.ShapeDtypeStruct((bsz, s, cc), BF16),
        grid=(bsz, s // ts),
        in_specs=[pl.BlockSpec((1, ts, cc2), lambda b, j: (b, j, 0)),
                  pl.BlockSpec((1, CONV_HALO, cc2), lambda b, j: (b, jnp.maximum(j * hb - 1, 0), 0)),
                  pl.BlockSpec((CONV_WIDTH, cc), lambda b, j: (0, 0)),
                  vec(conv_b), vec(ln_g), vec(ln_b)],
        out_specs=pl.BlockSpec((1, ts, cc), lambda b, j: (b, j, 0)),
        scratch_shapes=[pltpu.VMEM((SUBLANES, CONV_HALO + ts, cc), F32), pltpu.VMEM((CONV_WIDTH, SUBLANES, cc), F32)],
        compiler_params=_cparams(("parallel", "parallel")),
        name="conv",
    )(u3, u3, conv_w, conv_b.reshape(1, cc), ln_g.reshape(1, cc), ln_b.reshape(1, cc))


def _gla_kernel(q_ref, k_ref, v_ref, gl_ref, r_ref, ng_ref, o_ref, st_ref, *, nb, nh, dk, dv, gc, nchunks):
    @pl.when(pl.program_id(1) == 0)
    def _():
        st_ref[...] = jnp.zeros_like(st_ref)

    row = lax.broadcasted_iota(I32, (gc, gc), 0)
    col = lax.broadcasted_iota(I32, (gc, gc), 1)
    causal = col <= row
    tri = jnp.where(causal, 1.0, 0.0).astype(BF16)
    scale = dk ** -0.5

    ks = [slice(h * dk, (h + 1) * dk) for h in range(nh)]
    vs = [slice(h * dv, (h + 1) * dv) for h in range(nh)]
    streams = [(bb, h) for bb in range(nb) for h in range(nh)]

    def chunk(c, carry):
        r0 = pl.multiple_of(c * gc, gc)
        rows = pl.ds(r0, gc)
        bs = [_dot_01_f32(tri, gl_ref[bb, rows, :]) for bb in range(nb)]
        q_in, k_in, q_st, k_st, decay, v = [], [], [], [], [], []
        for bb in range(nb):
            b = bs[bb]
            b_last = b[gc - 1:gc, :]
            mid = 0.5 * b_last
            q = q_ref[bb, rows, :] * scale
            k = k_ref[bb, rows, :]
            v.append(v_ref[bb, rows, :].astype(BF16))
            q_in.append((q * jnp.exp(b - mid)).astype(BF16))
            k_in.append((k * jnp.exp(mid - b)).astype(BF16))
            q_st.append((q * jnp.exp(b)).astype(BF16))
            k_st.append((k * jnp.exp(b_last - b)).astype(BF16))
            decay.append(jnp.exp(b_last))
        sts = [st_ref[bb, h] for bb, h in streams]
        scores = [_dot_nt(q_in[bb][:, ks[h]], k_in[bb][:, ks[h]]) for bb, h in streams]
        inter = [_dot_nt(q_st[bb][:, ks[h]], st.astype(BF16)) for (bb, h), st in zip(streams, sts)]
        update = [_dot_tn(v[bb][:, vs[h]], k_st[bb][:, ks[h]]) for bb, h in streams]
        atts = [jnp.where(causal, sc, 0.0).astype(BF16) for sc in scores]
        outs = [_dot(att, v[bb][:, vs[h]]) + it for (bb, h), att, it in zip(streams, atts, inter)]
        for (bb, h), st, up, o in zip(streams, sts, update, outs):
            st_ref[bb, h] = st * decay[bb][:, ks[h]] + up
            r = r_ref[bb, rows, vs[h]].astype(F32)
            o = o * lax.rsqrt(jnp.mean(o * o, axis=-1, keepdims=True) + RMS_EPS) * ng_ref[:, vs[h]]
            o_ref[bb, rows, vs[h]] = (o * (r * _sigmoid(r))).astype(o_ref.dtype)
        return carry

    lax.fori_loop(0, nchunks, chunk, 0)


def _gla(q3, k3, v3, gl3, r3, norm_g):
    bsz, s, kw = q3.shape
    vw = v3.shape[-1]
    nh = GLA_HEADS
    dk, dv = kw // nh, vw // nh
    cb = GLA_BLOCK
    gc = GLA_CHUNK
    nb = GLA_BATCH_PER_STEP
    blk = lambda n: pl.BlockSpec((nb, cb, n), lambda b, j: (b, j, 0))
    return pl.pallas_call(
        functools.partial(_gla_kernel, nb=nb, nh=nh, dk=dk, dv=dv, gc=gc, nchunks=cb // gc),
        out_shape=jax.ShapeDtypeStruct((bsz, s, vw), BF16),
        grid=(bsz // nb, s // cb),
        in_specs=[blk(kw), blk(kw), blk(vw), blk(kw), blk(vw), pl.BlockSpec((1, vw), lambda b, j: (0, 0))],
        out_specs=blk(vw),
        scratch_shapes=[pltpu.VMEM((nb, nh, dv, dk), F32)],
        compiler_params=_cparams(("parallel", "arbitrary")),
        name="gla",
    )(q3, k3, v3, gl3, r3, norm_g.reshape(1, vw))


def _router_weights(w_group, b_group, w_router, b_router):
    d = w_router.shape[0]
    fill = LANES - MOE_EXPERTS - MOE_GROUPS
    wcat = jnp.concatenate([w_router, w_group, jnp.zeros((d, fill), F32)], axis=1)
    w_hi = wcat.astype(BF16)
    w_mid = (wcat - w_hi.astype(F32)).astype(BF16)
    w_lo = (wcat - w_hi.astype(F32) - w_mid.astype(F32)).astype(BF16)
    bcat = jnp.concatenate([b_router, b_group, jnp.zeros((fill,), F32)]).reshape(1, LANES)
    return jnp.concatenate([w_hi, w_mid, w_lo], axis=1), bcat


def _proj_ln_kernel(*refs, n_in, alpha):
    a_refs, w_refs = refs[:n_in], refs[n_in:2 * n_in]
    x_ref, gate_ref, lg_ref, lb_ref, sc_ref, sh_ref, wr_ref, br_ref, o_ref, lo_ref = refs[2 * n_in:]
    mix = _dot(a_refs[0][...], w_refs[0][...])
    for a_ref, w_ref in zip(a_refs[1:], w_refs[1:]):
        mix = mix + _dot(a_ref[...], w_ref[...])
    y = alpha * x_ref[...] + (1.0 + gate_ref[0]) * mix
    x1 = _layer_norm(y, lg_ref[...], lb_ref[...])
    o_ref[...] = x1
    h_hi, h_mid, h_lo = _split3(x1 * (1.0 + sc_ref[0]) + sh_ref[0])
    pa = _dot(h_hi, wr_ref[...])
    pb = _dot(h_mid, wr_ref[:, :2 * LANES])
    pc = _dot(h_lo, wr_ref[:, :LANES])
    small = (pa[:, 2 * LANES:] + pc) + pb[:, LANES:]
    lo_ref[...] = pa[:, :LANES] + ((pa[:, LANES:2 * LANES] + pb[:, :LANES]) + small) + br_ref[...]


def _proj_ln(acts, weights, x2, gate, ln_g, ln_b, sc_f, sh_f, w_route, b_route, tiles_per_b, alpha):
    t, d = x2.shape
    tm = TOKEN_TILE
    n_in = len(acts)
    row = lambda n: pl.BlockSpec((tm, n), lambda i: (i, 0))
    full = lambda a: pl.BlockSpec(a.shape, lambda i: (0,) * a.ndim)
    vec = pl.BlockSpec((1, d), lambda i: (0, 0))
    mod = pl.BlockSpec((1, 1, d), lambda i: (i // tiles_per_b, 0, 0))
    return pl.pallas_call(
        functools.partial(_proj_ln_kernel, n_in=n_in, alpha=alpha),
        out_shape=[jax.ShapeDtypeStruct((t, d), F32), jax.ShapeDtypeStruct((t, LANES), F32)],
        grid=(t // tm,),
        in_specs=[row(a.shape[1]) for a in acts] + [full(w) for w in weights]
                 + [row(d), mod, vec, vec, mod, mod, full(w_route), full(b_route)],
        out_specs=[row(d), row(LANES)],
        compiler_params=_cparams(("parallel",)),
        name="proj_ln",
    )(*acts, *weights, x2, gate, ln_g.reshape(1, d), ln_b.reshape(1, d), sc_f, sh_f, w_route, b_route)


def _mla_in_kernel(x_ref, sc_ref, sh_ref, pos_ref, invf_ref, sign_ref, win_ref, gq_ref, gkv_ref,
                   wqa_ref, wqb_ref, wk_ref, wvt_ref, q_ref, k_ref, vt_ref, *, nh, q_lora, kv_lora, scale, tk):
    h = (x_ref[...] * (1.0 + sc_ref[0]) + sh_ref[0]).astype(BF16)
    u = _dot(h, win_ref[...])
    cq = u[:, :q_lora]
    ckv = u[:, q_lora:q_lora + kv_lora]
    kr = u[:, q_lora + kv_lora:q_lora + kv_lora + LANES]
    kr_sw = u[:, q_lora + kv_lora + LANES:]
    cqn = (cq * lax.rsqrt(jnp.mean(cq * cq, axis=-1, keepdims=True) + RMS_EPS) * gq_ref[...]).astype(BF16)
    kvn = (ckv * lax.rsqrt(jnp.mean(ckv * ckv, axis=-1, keepdims=True) + RMS_EPS) * gkv_ref[...]).astype(BF16)
    ang = pos_ref[...].astype(F32) * invf_ref[...]
    cos = jnp.cos(ang)
    sin = jnp.sin(ang) * sign_ref[...]
    kr_rot = (kr * cos + kr_sw * sin).astype(BF16)
    qa = _dot(cqn, wqa_ref[...])
    qb = _dot(cqn, wqb_ref[...])
    kv = _dot(kvn, wk_ref[...])
    hw = 2 * LANES
    for hd in range(nh):
        q_ref[:, hd * hw:hd * hw + LANES] = (qa[:, hd * hw:hd * hw + LANES] * scale).astype(BF16)
        swapped = qb[:, (hd // 2) * LANES:(hd // 2 + 1) * LANES]
        if hd % 2:
            swapped = pltpu.roll(swapped, LANES // 2, axis=1)
        rope = qa[:, hd * hw + LANES:(hd + 1) * hw] * cos + swapped * sin
        q_ref[:, hd * hw + LANES:(hd + 1) * hw] = (rope * scale).astype(BF16)
        k_ref[:, hd * hw:hd * hw + LANES] = kv[:, hd * LANES:(hd + 1) * LANES].astype(BF16)
        k_ref[:, hd * hw + LANES:(hd + 1) * hw] = kr_rot
    vt = _dot_nt(wvt_ref[...], kvn).astype(BF16)
    for c in range(vt.shape[1] // tk):
        vt_ref[0, c] = vt[:, c * tk:(c + 1) * tk]


def _mla_in(x2, sc, sh, pos2, w_in, gq, gkv, w_uq, w_ukv, tiles_per_b):
    t, d = x2.shape
    nh = MLA_HEADS
    q_lora, kv_lora = gq.shape[0], gkv.shape[0]
    half = MLA_ROPE // 2
    pad = LANES - MLA_ROPE
    kr_w = w_in[:, q_lora + kv_lora:]
    kr_sw = jnp.concatenate([kr_w[:, half:], kr_w[:, :half]], axis=1)
    zpad = jnp.zeros((d, pad), w_in.dtype)
    win_ext = jnp.concatenate([w_in[:, :q_lora + kv_lora], kr_w, zpad, kr_sw, zpad], axis=1).astype(BF16)
    wq = w_uq.reshape(q_lora, nh, MLA_NOPE + MLA_ROPE)
    q_nope, q_rope = wq[:, :, :MLA_NOPE], wq[:, :, MLA_NOPE:]
    q_rope_sw = jnp.concatenate([q_rope[:, :, half:], q_rope[:, :, :half]], axis=2)
    zq = jnp.zeros((q_lora, nh, pad), w_uq.dtype)
    wqa = jnp.concatenate([q_nope, q_rope, zq], axis=2).reshape(q_lora, nh * 2 * LANES).astype(BF16)
    wqb = q_rope_sw.reshape(q_lora, nh * MLA_ROPE).astype(BF16)
    wkv = w_ukv.reshape(kv_lora, nh, MLA_NOPE + MLA_V)
    wk = wkv[:, :, :MLA_NOPE].reshape(kv_lora, nh * MLA_NOPE).astype(BF16)
    wvt = wkv[:, :, MLA_NOPE:].reshape(kv_lora, nh * MLA_V).T.astype(BF16)
    inv_freq = 1.0 / (ROPE_THETA ** (jnp.arange(0, MLA_ROPE, 2, dtype=F32) / MLA_ROPE))
    invf = jnp.concatenate([inv_freq, inv_freq, jnp.zeros((pad,), F32)]).reshape(1, LANES)
    sign = jnp.concatenate([-jnp.ones((half,), F32), jnp.ones((half,), F32), jnp.zeros((pad,), F32)]).reshape(1, LANES)
    tm = TOKEN_TILE
    full = lambda a: pl.BlockSpec(a.shape, lambda i: (0,) * a.ndim)
    row = lambda n: pl.BlockSpec((tm, n), lambda i: (i, 0))
    mod = pl.BlockSpec((1, 1, d), lambda i: (i // tiles_per_b, 0, 0))
    gq2, gkv2 = gq.reshape(1, q_lora), gkv.reshape(1, kv_lora)
    scale = (MLA_NOPE + MLA_ROPE) ** -0.5 * 1.4426950408889634
    tk = ATTN_TK
    kt_per_tile = tm // tk
    s = tiles_per_b * tm
    return pl.pallas_call(
        functools.partial(_mla_in_kernel, nh=nh, q_lora=q_lora, kv_lora=kv_lora, scale=scale, tk=tk),
        out_shape=[jax.ShapeDtypeStruct((t, nh * 2 * LANES), BF16), jax.ShapeDtypeStruct((t, nh * 2 * LANES), BF16),
                   jax.ShapeDtypeStruct((t // s, s // tk, nh * MLA_V, tk), BF16)],
        grid=(t // tm,),
        in_specs=[row(d), mod, mod, row(1), full(invf), full(sign), full(win_ext), full(gq2), full(gkv2),
                  full(wqa), full(wqb), full(wk), full(wvt)],
        out_specs=[row(nh * 2 * LANES), row(nh * 2 * LANES),
                   pl.BlockSpec((1, kt_per_tile, nh * MLA_V, tk),
                                lambda i: (i // tiles_per_b, i % tiles_per_b, 0, 0))],
        compiler_params=_cparams(("parallel",)),
        name="mla_in",
    )(x2, sc, sh, pos2, invf, sign, win_ext, gq2, gkv2, wqa, wqb, wk, wvt)


def _attn_kernel(q_ref, k_ref, vt_ref, o_ref, acc_ref, s0_ref, s1_ref, *, s, tq, tk, hb):
    kpq = tq // tk
    assert kpq == 2, "the pipeline below alternates two score buffers over pairs of key tiles"
    hw = 2 * LANES
    ones = jnp.ones((8, tk), BF16)

    def q_block(qi, carry):
        qrows = pl.ds(pl.multiple_of(qi * tq, tq), tq)
        acc_ref[...] = jnp.zeros_like(acc_ref)

        def scores(j, s_ref, c0=0, nc=tq, block=qi):
            krows = pl.ds(pl.multiple_of(j * tk, tk), tk)
            cols = pl.ds(pl.multiple_of(block * tq + c0, tk), nc)
            for h in range(hb):
                s_ref[h, :, c0:c0 + nc] = _dot_nt(k_ref[krows, h * hw:(h + 1) * hw], q_ref[cols, h * hw:(h + 1) * hw])

        def tile(j, s_ref, stats, masked, c0=0, nc=tq):
            def put(full, part):
                pieces = ([full[:, :c0]] if c0 else []) + [part] + ([full[:, c0 + nc:]] if c0 + nc < tq else [])
                return pieces[0] if len(pieces) == 1 else jnp.concatenate(pieces, axis=1)
            ps, alphas, out = [], [], []
            for h in range(hb):
                m, st = stats[2 * h][:, c0:c0 + nc], s_ref[h, :, c0:c0 + nc]
                if masked:
                    key = j * tk + lax.broadcasted_iota(I32, (tk, nc), 0)
                    qry = qi * tq + c0 + lax.broadcasted_iota(I32, (tk, nc), 1)
                    st = jnp.where(key <= qry, st, -jnp.inf)
                m_new = jnp.maximum(m, jnp.max(st, axis=0, keepdims=True))
                ps.append(jnp.exp2(st - m_new).astype(BF16))
                alphas.append(jnp.exp2(m - m_new))
                out.append(put(stats[2 * h], m_new))
            for h in range(hb):
                acc_ref[h, :, c0:c0 + nc] = (alphas[h] * acc_ref[h, :, c0:c0 + nc]
                                             + _dot(vt_ref[0, j, h * MLA_V:(h + 1) * MLA_V, :], ps[h]))
                l_new = alphas[h] * stats[2 * h + 1][:, c0:c0 + nc] + _dot(ones, ps[h])[0:1]
                out.insert(2 * h + 1, put(stats[2 * h + 1], l_new))
            return tuple(out)

        stats = (jnp.full((1, tq), -jnp.inf, F32), jnp.zeros((1, tq), F32)) * hb

        def pair(jj, c):
            j = 2 * jj
            scores(j + 1, s1_ref)
            c = tile(j, s0_ref, c, False)
            scores(j + 2, s0_ref)
            return tile(j + 1, s1_ref, c, False)

        stats = lax.fori_loop(0, qi, pair, stats)
        scores(2 * qi + 1, s1_ref, tk, tk)
        stats = tile(2 * qi, s0_ref, stats, True, 0, tk)
        stats = tile(2 * qi, s0_ref, stats, False, tk, tk)
        stats = tile(2 * qi + 1, s1_ref, stats, True, tk, tk)
        scores(0, s0_ref, block=jnp.minimum(qi + 1, nq - 1))
        for h in range(hb):
            o_ref[qrows, h * MLA_V:(h + 1) * MLA_V] = (acc_ref[h] / stats[2 * h + 1]).T.astype(o_ref.dtype)
        return carry

    nq = s // tq
    for h in range(hb):
        s0_ref[h] = _dot_nt(k_ref[0:tk, h * hw:(h + 1) * hw], q_ref[0:tq, h * hw:(h + 1) * hw])
    lax.fori_loop(0, nq, q_block, 0)


def _attn(q, k, vt, bsz, s):
    nh = MLA_HEADS
    tq, tk = ATTN_TQ, ATTN_TK
    hb = ATTN_HEADS_PER_STEP
    hw = 2 * LANES
    return pl.pallas_call(
        functools.partial(_attn_kernel, s=s, tq=tq, tk=tk, hb=hb),
        out_shape=jax.ShapeDtypeStruct((bsz * s, nh * MLA_V), BF16),
        grid=(bsz, nh // hb),
        in_specs=[pl.BlockSpec((s, hb * hw), lambda b, h: (b, h)),
                  pl.BlockSpec((s, hb * hw), lambda b, h: (b, h)),
                  pl.BlockSpec((1, s // tk, hb * MLA_V, tk), lambda b, h: (b, 0, h, 0))],
        out_specs=pl.BlockSpec((s, hb * MLA_V), lambda b, h: (b, h)),
        scratch_shapes=[pltpu.VMEM((hb, MLA_V, tq), F32), pltpu.VMEM((hb, tk, tq), F32),
                        pltpu.VMEM((hb, tk, tq), F32)],
        compiler_params=_cparams(("parallel", "parallel")),
        name="attn",
    )(q, k, vt)


def _route_kernel(logits_ref, lp_ref, wt_ref, tm_ref, offs_ref, te_ref, na_ref,
                  upper_ref, carry_ref, *, tm, ne, ng, row_tile, nt_pad):
    i = pl.program_id(0)
    epg = ne // ng

    @pl.when(i == 0)
    def _():
        r = lax.broadcasted_iota(I32, (tm, tm), 0)
        c = lax.broadcasted_iota(I32, (tm, tm), 1)
        upper_ref[...] = jnp.where(r < c, 1.0, 0.0).astype(BF16)
        carry_ref[...] = jnp.zeros_like(carry_ref)

    lt = logits_ref[...].T
    lr = lt[0:ne]
    grow = lax.broadcasted_iota(I32, (8, tm), 0).astype(F32)
    lg = jnp.where(grow < ng, lt[ne:ne + 8], -jnp.inf)
    gmax = jnp.max(lg, axis=0, keepdims=True)
    g_idx = jnp.min(jnp.where(lg == gmax, grow, 1e9), axis=0, keepdims=True)
    g_w = 1.0 / jnp.sum(jnp.exp(lg - gmax), axis=0, keepdims=True)
    erow = lax.broadcasted_iota(I32, (ne, tm), 0).astype(F32)
    in_group = jnp.floor(erow * (1.0 / epg)) == g_idx
    sel = jnp.where(in_group, lr, -jnp.inf)
    v1 = jnp.max(sel, axis=0, keepdims=True)
    i1 = jnp.min(jnp.where(sel == v1, erow, 1e9), axis=0, keepdims=True)
    sel2 = jnp.where(erow == i1, -jnp.inf, sel)
    v2 = jnp.max(sel2, axis=0, keepdims=True)
    i2 = jnp.min(jnp.where(sel2 == v2, erow, 1e9), axis=0, keepdims=True)
    t = jnp.exp(v2 - v1)
    w1 = g_w / (1.0 + t)
    w2 = g_w * t / (1.0 + t)
    oh1 = erow == i1
    oh2 = erow == i2
    member = jnp.where(oh1 | oh2, 1.0, 0.0)
    lcnt = jnp.sum(member, axis=1, keepdims=True)
    lcnt = jnp.floor((lcnt + (RUN_ALIGN - 1)) * (1.0 / RUN_ALIGN)) * RUN_ALIGN
    er_ = lax.broadcasted_iota(I32, (ne, ne), 0)
    ec_ = lax.broadcasted_iota(I32, (ne, ne), 1)
    lstart = jnp.dot(jnp.where(ec_ < er_, 1.0, 0.0).astype(F32), jnp.broadcast_to(lcnt, (ne, LANES)),
                     precision=HIGHEST, preferred_element_type=F32)[:, 0:1]
    lrank = _dot(member.astype(BF16), upper_ref[...]) + lstart
    p1 = jnp.sum(jnp.where(oh1, lrank, 0.0), axis=0, keepdims=True)
    p2 = jnp.sum(jnp.where(oh2, lrank, 0.0), axis=0, keepdims=True)

    orow = lax.broadcasted_iota(I32, (8, tm), 0)
    lp_ref[...] = jnp.where(orow == 0, p1, jnp.where(orow == 1, p2, 0.0)).astype(I32)
    wrow = lax.broadcasted_iota(I32, (LANES, tm), 0)
    wt_ref[...] = jnp.where(wrow == 0, w1, jnp.where(wrow == 1, w2,
                                                     jnp.where(wrow == 2, p1, jnp.where(wrow == 3, p2, 0.0)))).T
    mr = lax.broadcasted_iota(I32, (ne, LANES), 0)
    mc = lax.broadcasted_iota(I32, (ne, LANES), 1)
    to_row = lambda col, lane0: jnp.sum(jnp.where(mr + lane0 == mc, col, 0.0), axis=0, keepdims=True)
    total = jnp.sum(lcnt, axis=0, keepdims=True)
    lane = lax.broadcasted_iota(I32, (1, LANES), 1)
    packed = (to_row(lstart, 0) + to_row(lcnt, ne) + to_row(carry_ref[...], 2 * ne)
              + jnp.where(lane == 3 * ne, total, 0.0))
    trow = lax.broadcasted_iota(I32, (8, LANES), 0)
    tm_ref[...] = jnp.where(trow == 0, packed, 0.0).astype(I32)
    carry_ref[...] = carry_ref[...] + lcnt

    @pl.when(i == pl.num_programs(0) - 1)
    def _():
        cnt = carry_ref[...]
        ntl = jnp.floor((cnt + (row_tile - 1)) * (1.0 / row_tile))
        incl = jnp.where(ec_ <= er_, 1.0, 0.0).astype(F32)
        ends = jnp.dot(incl, jnp.broadcast_to(ntl, (ne, LANES)), precision=HIGHEST,
                       preferred_element_type=F32)
        starts = ends - ntl
        offs_ref[...] = jnp.concatenate([starts * row_tile, ends * row_tile, jnp.broadcast_to(ntl, (ne, LANES)),
                                         jnp.zeros((8, LANES), F32)], axis=0).astype(I32)
        tile = lax.broadcasted_iota(I32, (ne, nt_pad), 1).astype(F32)
        te = jnp.sum(jnp.where(ends[:, 0:1] <= tile, 1.0, 0.0), axis=0, keepdims=True)
        te_ref[...] = jnp.broadcast_to(jnp.minimum(te, ne - 1.0), (8, nt_pad)).astype(I32)
        na_ref[...] = jnp.broadcast_to(ends[ne - 1:ne, :], (8, LANES)).astype(I32)


def _moe_tiles(t):
    rows = 2 * t + (RUN_ALIGN - 1) * MOE_EXPERTS * (t // TOKEN_TILE)
    nt_max = -(-rows // MOE_ROW_TILE) + MOE_EXPERTS
    nt_pad = -(-nt_max // LANES) * LANES
    return nt_max, nt_pad


def _sorted_rows(tm):
    return -(-(2 * tm + (RUN_ALIGN - 1) * MOE_EXPERTS) // LANES) * LANES


def _route(logits):
    t = logits.shape[0]
    ne, ng = MOE_EXPERTS, MOE_GROUPS
    tm = TOKEN_TILE
    _, nt_pad = _moe_tiles(t)
    const = lambda shp: pl.BlockSpec(shp, lambda i: (0,) * len(shp))
    return pl.pallas_call(
        functools.partial(_route_kernel, tm=tm, ne=ne, ng=ng, row_tile=MOE_ROW_TILE, nt_pad=nt_pad),
        out_shape=[jax.ShapeDtypeStruct((8, t), I32), jax.ShapeDtypeStruct((t, LANES), F32),
                   jax.ShapeDtypeStruct((8 * (t // tm), LANES), I32),
                   jax.ShapeDtypeStruct((3 * ne + 8, LANES), I32), jax.ShapeDtypeStruct((8, nt_pad), I32),
                   jax.ShapeDtypeStruct((8, LANES), I32)],
        grid=(t // tm,),
        in_specs=[pl.BlockSpec((tm, LANES), lambda i: (i, 0))],
        out_specs=[pl.BlockSpec((8, tm), lambda i: (0, i)), pl.BlockSpec((tm, LANES), lambda i: (i, 0)),
                   pl.BlockSpec((8, LANES), lambda i: (i, 0)),
                   const((3 * ne + 8, LANES)), const((8, nt_pad)), const((8, LANES))],
        scratch_shapes=[pltpu.VMEM((tm, tm), BF16), pltpu.VMEM((ne, 1), F32)],
        compiler_params=_cparams(("arbitrary",)),
        name="moe_route",
    )(logits)


def _start_runs(tm_ref, tile, ne, copy):
    for e in range(ne):
        lstart = tm_ref[tile, e]
        n = tm_ref[tile, ne + e]
        before = tm_ref[tile, 2 * ne + e]
        _binary_pieces(n, TOKEN_TILE, lambda off, rows: copy(
            e, pl.multiple_of(lstart + off, RUN_ALIGN), pl.multiple_of(before + off, RUN_ALIGN), rows))


def _binary_pieces(n, n_max, copy):
    del n_max
    nbig = n >> (RUN_PIECE.bit_length() - 1)

    def big_piece(c, carry):
        copy(c * RUN_PIECE, RUN_PIECE).start()
        return carry

    lax.fori_loop(0, nbig, big_piece, 0)
    off = nbig * RUN_PIECE
    p = RUN_PIECE // 2
    while p >= RUN_ALIGN:
        @pl.when((n & p) != 0)
        def _():
            copy(off, p).start()
        off = off + (n & p)
        p //= 2


def _dispatch_kernel(offs_ref, gend_ref, ntl_ref, tm_ref, x_ref, sc_ref, sh_ref, lp_ref, wt_ref, xs_ref,
                     h_ref, z_ref, sem_z, sem_r, *, tm, ne, row_tile, nt_max, ntile, srows):
    i = pl.program_id(0)
    tile = i - 1

    @pl.when(i == 0)
    def _():
        z_ref[...] = jnp.zeros_like(z_ref)
        zero_tile = lambda start: pltpu.make_async_copy(z_ref, xs_ref.at[pl.ds(start, row_tile), :], sem_z)
        for act in ("start", "wait"):
            for e in range(ne):
                @pl.when(ntl_ref[e] > 0)
                def _():
                    getattr(zero_tile(pl.multiple_of(gend_ref[e] - row_tile, row_tile)), act)()
            for back in range(1, nt_max - (2 * tm * ntile) // row_tile + 1):
                @pl.when(nt_max - back >= gend_ref[ne - 1] // row_tile)
                def _():
                    getattr(zero_tile((nt_max - back) * row_tile), act)()
            for spill in range(2):
                getattr(zero_tile((nt_max + spill) * row_tile), act)()

    def wait_tile(t):
        slot = t % 2
        pltpu.make_async_copy(h_ref.at[slot], xs_ref.at[pl.ds(0, srows), :], sem_r.at[slot]).wait()

    @pl.when((tile >= 2) & (tile <= ntile))
    def _():
        wait_tile(tile - 2)

    @pl.when(tile == ntile)
    def _():
        wait_tile(tile - 1)

    @pl.when((tile >= 0) & (tile < ntile))
    def _():
        slot = tile % 2
        h = (x_ref[...] * (1.0 + sc_ref[0]) + sh_ref[0]).astype(BF16)
        row = lax.broadcasted_iota(I32, (srows, tm), 0)
        lp = lp_ref[...]
        signed = jnp.where(row == lp[0:1, :], 1.0, jnp.where(row == lp[1:2, :], -1.0, 0.0)).astype(BF16)
        lane = lax.broadcasted_iota(I32, (tm, LANES), 1)
        wt = wt_ref[...]
        terms = []
        for k in range(2):
            w = wt[:, k:k + 1]
            hi = w.astype(BF16).astype(F32)
            mid = (w - hi).astype(BF16).astype(F32)
            terms += [hi, mid, w - hi - mid]
        extra = jnp.where(lane == SIGN_LANE, 1.0, 0.0)
        for idx, term in enumerate(terms):
            extra = jnp.where(lane == idx, term, extra)
        h_ref[slot] = _dot(signed, jnp.concatenate([h, extra.astype(BF16)], axis=1)).astype(BF16)

        def copy(e, local_row, rows_before, rows):
            dst = pl.multiple_of(offs_ref[e] + rows_before, RUN_ALIGN)
            return pltpu.make_async_copy(h_ref.at[slot, pl.ds(local_row, rows), :], xs_ref.at[pl.ds(dst, rows), :],
                                         sem_r.at[slot])

        _start_runs(tm_ref, tile, ne, copy)
        used = tm_ref[tile, 3 * ne]
        _binary_pieces(srows - used, srows - 2 * tm, lambda off, rows: pltpu.make_async_copy(
            h_ref.at[slot, pl.ds(pl.multiple_of(used + off, RUN_ALIGN), rows), :],
            xs_ref.at[pl.ds(pl.multiple_of((nt_max + slot) * row_tile + off, RUN_ALIGN), rows), :], sem_r.at[slot]))


def _dispatch(x2, sc, sh, lp, wt, tmeta, offs, gend, ntl, tiles_per_b):
    t, d = x2.shape
    tm = TOKEN_TILE
    nt_max, _ = _moe_tiles(t)
    ntile = t // tm
    srows = _sorted_rows(tm)
    cur = lambda i: jnp.clip(i - 1, 0, ntile - 1)
    grid_spec = pltpu.PrefetchScalarGridSpec(
        num_scalar_prefetch=4,
        grid=(ntile + 2,),
        in_specs=[pl.BlockSpec((tm, d), lambda i, *_: (cur(i), 0)),
                  pl.BlockSpec((1, 1, d), lambda i, *_: (cur(i) // tiles_per_b, 0, 0)),
                  pl.BlockSpec((1, 1, d), lambda i, *_: (cur(i) // tiles_per_b, 0, 0)),
                  pl.BlockSpec((8, tm), lambda i, *_: (0, cur(i))),
                  pl.BlockSpec((tm, LANES), lambda i, *_: (cur(i), 0))],
        out_specs=pl.BlockSpec(memory_space=pl.ANY),
        scratch_shapes=[pltpu.VMEM((2, srows, d + LANES), BF16), pltpu.VMEM((MOE_ROW_TILE, d + LANES), BF16),
                        pltpu.SemaphoreType.DMA, pltpu.SemaphoreType.DMA((2,))])
    return pl.pallas_call(
        functools.partial(_dispatch_kernel, tm=tm, ne=MOE_EXPERTS, row_tile=MOE_ROW_TILE, nt_max=nt_max,
                          ntile=ntile, srows=srows),
        out_shape=jax.ShapeDtypeStruct(((nt_max + 2) * MOE_ROW_TILE, d + LANES), BF16),
        grid_spec=grid_spec,
        compiler_params=_cparams(("arbitrary",)),
        name="moe_dispatch",
    )(offs, gend, ntl, tmeta, x2, sc, sh, lp, wt)


def _gmm_kernel(te_ref, na_ref, xs_ref, wg_ref, wu_ref, wd_ref, ys_ref, wgu_buf, wd_buf, *, ff, tr):
    j = pl.program_id(0)

    @pl.when(j < na_ref[0])
    def _():
        changed = (j == 0) | (te_ref[j] != te_ref[jnp.maximum(j - 1, 0)])

        @pl.when(changed)
        def _():
            wgu_buf[:, :ff] = wg_ref[0, 0].astype(BF16)
            wgu_buf[:, ff:] = wu_ref[0, 0].astype(BF16)
            wd_buf[...] = wd_ref[0, 0].astype(BF16)

        halves = [slice(c * (tr // 2), (c + 1) * (tr // 2)) for c in range(2)]
        d = wgu_buf.shape[0]
        riders = [xs_ref[rows, d:].astype(F32) for rows in halves]
        signs = [r[:, SIGN_LANE:SIGN_LANE + 1] for r in riders]
        gus = [_dot(xs_ref[rows, :d] * sg.astype(BF16), wgu_buf[...]) for rows, sg in zip(halves, signs)]
        hids = [(gu[:, :ff] * _sigmoid(gu[:, :ff]) * gu[:, ff:]).astype(BF16) for gu in gus]
        for rows, hid, r, sg in zip(halves, hids, riders, signs):
            w = jnp.where(sg > 0.0, r[:, 0:1] + r[:, 1:2] + r[:, 2:3], -(r[:, 3:4] + r[:, 4:5] + r[:, 5:6]))
            ys_ref[rows, :] = (w * _dot(hid, wd_buf[...])).astype(ys_ref.dtype)

    @pl.when(j >= na_ref[0])
    def _():
        ys_ref[...] = jnp.zeros_like(ys_ref)


def _gmm(xs, te, na, w_gate, w_up, w_down, layer):
    d, ff = w_gate.shape[-2:]
    tr = MOE_ROW_TILE
    ns = te.shape[0] * tr
    act = lambda j, te_ref, na_ref: jnp.minimum(j, na_ref[0] - 1)
    grid_spec = pltpu.PrefetchScalarGridSpec(
        num_scalar_prefetch=2,
        grid=(ns // tr,),
        in_specs=[pl.BlockSpec((tr, d + LANES), lambda j, te_ref, na_ref: (act(j, te_ref, na_ref), 0)),
                  pl.BlockSpec((1, 1, d, ff), lambda j, te_ref, na_ref: (layer, te_ref[act(j, te_ref, na_ref)], 0, 0)),
                  pl.BlockSpec((1, 1, d, ff), lambda j, te_ref, na_ref: (layer, te_ref[act(j, te_ref, na_ref)], 0, 0)),
                  pl.BlockSpec((1, 1, ff, d), lambda j, te_ref, na_ref: (layer, te_ref[act(j, te_ref, na_ref)], 0, 0))],
        out_specs=pl.BlockSpec((tr, d), lambda j, te_ref, na_ref: (j, 0)),
        scratch_shapes=[pltpu.VMEM((d, 2 * ff), BF16), pltpu.VMEM((ff, d), BF16)])
    return pl.pallas_call(
        functools.partial(_gmm_kernel, ff=ff, tr=tr),
        out_shape=jax.ShapeDtypeStruct((ns, d), BF16),
        grid_spec=grid_spec,
        compiler_params=_cparams(("arbitrary",)),
        name="moe_gmm",
    )(te, na, xs, w_gate, w_up, w_down)


def _combine_kernel(offs_ref, tm_ref, wt_ref, x_ref, gate_ref, lg_ref, lb_ref, ys_ref, o_ref,
                    buf_ref, sem_r, *, tm, ne, ntile, srows, alpha):
    i = pl.program_id(0)

    def fetch(tile):
        slot = tile % 2

        def copy(e, local_row, rows_before, rows):
            src = pl.multiple_of(offs_ref[e] + rows_before, RUN_ALIGN)
            return pltpu.make_async_copy(ys_ref.at[pl.ds(src, rows), :], buf_ref.at[slot, pl.ds(local_row, rows), :],
                                         sem_r.at[slot])

        _start_runs(tm_ref, tile, ne, copy)
        used = tm_ref[tile, 3 * ne]
        _binary_pieces(srows - used, srows - 2 * tm, lambda off, rows: pltpu.make_async_copy(
            ys_ref.at[pl.ds(pl.multiple_of(off, RUN_ALIGN), rows), :],
            buf_ref.at[slot, pl.ds(pl.multiple_of(used + off, RUN_ALIGN), rows), :], sem_r.at[slot]))

    def wait(tile):
        slot = tile % 2
        pltpu.make_async_copy(ys_ref.at[pl.ds(0, srows), :], buf_ref.at[slot], sem_r.at[slot]).wait()

    def finish(tile):
        w = wt_ref[...]
        rows = buf_ref[tile % 2]
        col = lax.broadcasted_iota(I32, (tm, srows), 1).astype(F32)
        both = jnp.where(col == w[:, 2:3], 1.0, jnp.where(col == w[:, 3:4], 1.0, 0.0)).astype(BF16)
        ffn = _dot(both, rows)
        y = alpha * x_ref[...] + (1.0 + gate_ref[0]) * ffn
        o_ref[...] = _layer_norm(y, lg_ref[...], lb_ref[...])

    @pl.when(i == 0)
    def _():
        fetch(i)

    @pl.when((i > 0) & (i < ntile))
    def _():
        wait(i - 1)
        fetch(i)
        finish(i - 1)

    @pl.when(i == ntile)
    def _():
        wait(i - 1)
        finish(i - 1)


def _combine(ys, tmeta, wt, x2, gate, ln_g, ln_b, offs, tiles_per_b, alpha):
    t, d = x2.shape
    tm = TOKEN_TILE
    ntile = t // tm
    srows = _sorted_rows(tm)
    prev = lambda i: jnp.maximum(i - 1, 0)
    vec = pl.BlockSpec((1, d), lambda i, *_: (0, 0))
    grid_spec = pltpu.PrefetchScalarGridSpec(
        num_scalar_prefetch=2,
        grid=(ntile + 1,),
        in_specs=[pl.BlockSpec((tm, LANES), lambda i, *_: (prev(i), 0)),
                  pl.BlockSpec((tm, d), lambda i, *_: (prev(i), 0)),
                  pl.BlockSpec((1, 1, d), lambda i, *_: (prev(i) // tiles_per_b, 0, 0)),
                  vec, vec,
                  pl.BlockSpec(memory_space=pl.ANY)],
        out_specs=pl.BlockSpec((tm, d), lambda i, *_: (prev(i), 0)),
        scratch_shapes=[pltpu.VMEM((2, srows, d), BF16), pltpu.SemaphoreType.DMA((2,))])
    return pl.pallas_call(
        functools.partial(_combine_kernel, tm=tm, ne=MOE_EXPERTS, ntile=ntile, srows=srows, alpha=alpha),
        out_shape=jax.ShapeDtypeStruct((t, d), F32),
        grid_spec=grid_spec,
        compiler_params=_cparams(("arbitrary",)),
        name="moe_combine",
    )(offs, tmeta, wt, x2, gate, ln_g.reshape(1, d), ln_b.reshape(1, d), ys)


def _moe_block(x2, logits, sc, sh, gate, ln_g, ln_b, w_gate, w_up, w_down, layer, tiles_per_b, alpha):
    ne = MOE_EXPERTS
    nt_max, _ = _moe_tiles(x2.shape[0])
    lp, wt, tmeta, meta, te, na = _route(logits)
    offs, gend, ntl = meta[:ne, 0], meta[ne:2 * ne, 0], meta[2 * ne:3 * ne, 0]
    tmeta = tmeta.reshape(-1, SUBLANES, LANES)[:, 0, :]
    xs = _dispatch(x2, sc, sh, lp, wt, tmeta, offs, gend, ntl, tiles_per_b)
    ys = _gmm(xs, te[0, :nt_max], na[0, :1], w_gate, w_up, w_down, layer)
    return _combine(ys, tmeta, wt, x2, gate, ln_g, ln_b, offs, tiles_per_b, alpha)


def kernel(x, c, positions, ada_w, ada_b, ln_mix_g, ln_mix_b, ln_ffn_g, ln_ffn_b, ab_w_in, conv_w, conv_b, conv_ln_g, conv_ln_b, gla_gate_w, gla_gate_b, gla_norm_g, ab_w_out, mla_w_in, mla_q_norm_g, mla_kv_norm_g, mla_w_uq, mla_w_ukv, mla_w_out, moe_w_group, moe_b_group, moe_w_router, moe_b_router, moe_w_gate, moe_w_up, moe_w_down):
    bsz, s, d = x.shape
    depth = ada_w.shape[0]
    t = bsz * s
    tiles_per_b = s // TOKEN_TILE
    alpha = (2 * depth) ** 0.25
    mod = _ada(c, ada_w, ada_b).reshape(depth, bsz, 6, 1, d)
    x2 = x.reshape(t, d)
    for layer in range(depth):
        sh_m, sc_m, g_m, sh_f, sc_f, g_f = (mod[layer, :, n] for n in range(6))
        i = layer // 2
        if layer % 2 == 0:
            uc, q, k, v, r, gl = _ab_in(x2, sc_m, sh_m, ab_w_in[i], gla_gate_w[i], gla_gate_b[i], tiles_per_b)
            y_a = _conv(uc.reshape(bsz, s, -1), conv_w[i], conv_b[i], conv_ln_g[i], conv_ln_b[i])
            b3 = lambda a: a.reshape(bsz, s, -1)
            y_b = _gla(b3(q), b3(k), b3(v), b3(gl), b3(r), gla_norm_g[i])
            w_out = ab_w_out[i].astype(BF16)
            cc = y_a.shape[-1]
            acts = [y_a.reshape(t, cc), y_b.reshape(t, -1)]
            weights = [w_out[:cc], w_out[cc:]]
        else:
            qc, kc, vv = _mla_in(x2, sc_m, sh_m, positions.reshape(t, 1), mla_w_in[i], mla_q_norm_g[i],
                                 mla_kv_norm_g[i], mla_w_uq[i], mla_w_ukv[i], tiles_per_b)
            acts = [_attn(qc, kc, vv, bsz, s)]
            weights = [mla_w_out[i].astype(BF16)]
        w_route, b_route = _router_weights(moe_w_group[layer], moe_b_group[layer], moe_w_router[layer],
                                           moe_b_router[layer])
        x2, logits = _proj_ln(acts, weights, x2, g_m, ln_mix_g[layer], ln_mix_b[layer], sc_f, sh_f, w_route, b_route,
                              tiles_per_b, alpha)
        x2 = _moe_block(x2, logits, sc_f, sh_f, g_f, ln_ffn_g[layer], ln_ffn_b[layer], moe_w_gate, moe_w_up,
                        moe_w_down, layer, tiles_per_b, alpha)
    return x2.reshape(bsz, s, d)
```

```python
import functools

import jax
import jax.numpy as jnp
from jax import lax
from jax.experimental import pallas as pl
from jax.experimental.pallas import tpu as pltpu

F32 = jnp.float32
BF16 = jnp.bfloat16
I32 = jnp.int32
HIGHEST = lax.Precision.HIGHEST

LN_EPS = 1e-5
RMS_EPS = 1e-6
CONV_WIDTH = 31
GLA_HEADS = 4
GLA_GATE_TAU = 16.0
MLA_HEADS = 8
MLA_NOPE = 128
MLA_ROPE = 64
MLA_V = 128
ROPE_THETA = 10000.0
MOE_GROUPS = 4
MOE_EXPERTS_PER_GROUP = 8
MOE_EXPERTS = MOE_GROUPS * MOE_EXPERTS_PER_GROUP

LANES = 128
SUBLANES = 8
TOKEN_TILE = 512
GLA_CHUNK = 128
GLA_BLOCK = 512
GLA_BATCH_PER_STEP = 4
CONV_ROWS = 32
CONV_HALO = 32
ATTN_TQ = 512
ATTN_TK = 256
ATTN_HEADS_PER_STEP = 4
MOE_ROW_TILE = 512
RUN_PIECE = 64
RUN_ALIGN = 16
SIGN_LANE = 6
VMEM_LIMIT = 48 * 1024 * 1024


def _cparams(sem):
    return pltpu.CompilerParams(dimension_semantics=sem, vmem_limit_bytes=VMEM_LIMIT)


def _sigmoid(x):
    return 1.0 / (1.0 + jnp.exp(-x))


def _dot(a, b):
    return jnp.dot(a, b, preferred_element_type=F32)


def _dot_nt(a, b):
    return lax.dot_general(a, b, (((1,), (1,)), ((), ())), preferred_element_type=F32)


def _dot_tn(a, b):
    return lax.dot_general(a, b, (((0,), (0,)), ((), ())), preferred_element_type=F32)


def _split3(x):
    hi = x.astype(BF16)
    r1 = x - hi.astype(F32)
    mid = r1.astype(BF16)
    lo = (r1 - mid.astype(F32)).astype(BF16)
    return hi, mid, lo


def _dot_01_f32(a01, x):
    hi, mid, lo = _split3(x)
    return _dot(a01, hi) + (_dot(a01, mid) + _dot(a01, lo))


def _layer_norm(y, g, b):
    mu = jnp.mean(y, axis=-1, keepdims=True)
    d = y - mu
    var = jnp.mean(d * d, axis=-1, keepdims=True)
    return d * lax.rsqrt(var + LN_EPS) * g + b


def _ada_kernel(c_ref, w_ref, b_ref, o_ref):
    c = c_ref[...]
    n = c.shape[0]
    c_hi, c_mid, c_lo = _split3(c * _sigmoid(c))
    w_hi, w_mid, w_lo = _split3(w_ref[0])
    pa = _dot(jnp.concatenate([c_hi, c_mid, c_lo], axis=0), w_hi)
    pb = _dot(jnp.concatenate([c_hi, c_mid], axis=0), w_mid)
    pc = _dot(c_hi, w_lo)
    small = (pa[2 * n:] + pc) + pb[n:]
    o_ref[0] = pa[:n] + ((pa[n:2 * n] + pb[:n]) + small) + b_ref[0]


def _ada(c, ada_w, ada_b):
    depth, d, n = ada_w.shape
    bsz = c.shape[0]
    tn = 1536
    return pl.pallas_call(
        _ada_kernel,
        out_shape=jax.ShapeDtypeStruct((depth, bsz, n), F32),
        grid=(depth, n // tn),
        in_specs=[pl.BlockSpec((bsz, d), lambda l, j: (0, 0)),
                  pl.BlockSpec((1, d, tn), lambda l, j: (l, 0, j)),
                  pl.BlockSpec((1, 1, tn), lambda l, j: (l, 0, j))],
        out_specs=pl.BlockSpec((1, bsz, tn), lambda l, j: (l, 0, j)),
        compiler_params=_cparams(("parallel", "parallel")),
        name="ada",
    )(c, ada_w, ada_b.reshape(depth, 1, n))


def _ab_in_kernel(x_ref, sc_ref, sh_ref, wc_ref, wq_ref, wk_ref, wv_ref, wr_ref, wg_ref, gw_ref, gb_ref,
                  uc_ref, q_ref, k_ref, v_ref, r_ref, gl_ref):
    h = (x_ref[...] * (1.0 + sc_ref[0]) + sh_ref[0]).astype(BF16)
    uc_ref[...] = _dot(h, wc_ref[...]).astype(uc_ref.dtype)
    q_ref[...] = _dot(h, wq_ref[...])
    k_ref[...] = _dot(h, wk_ref[...])
    v_ref[...] = _dot(h, wv_ref[...]).astype(v_ref.dtype)
    r_ref[...] = _dot(h, wr_ref[...]).astype(r_ref.dtype)
    g_low = _dot(h, wg_ref[...])
    z = jnp.dot(g_low, gw_ref[...], precision=HIGHEST, preferred_element_type=F32) + gb_ref[...]
    gl_ref[...] = (jnp.minimum(z, 0.0) - jnp.log(1.0 + jnp.exp(-jnp.abs(z)))) * (1.0 / GLA_GATE_TAU)


def _ab_in(x2, sc, sh, w_in, gate_w, gate_b, tiles_per_b):
    t, d = x2.shape
    cc2 = d
    kw = d // 4
    vw = d // 2
    rank = gate_w.shape[0]
    splits = [cc2, cc2 + kw, cc2 + 2 * kw, cc2 + 2 * kw + vw, cc2 + 2 * kw + 2 * vw]
    wb = w_in.astype(BF16)
    ws = [wb[:, :splits[0]], wb[:, splits[0]:splits[1]], wb[:, splits[1]:splits[2]],
          wb[:, splits[2]:splits[3]], wb[:, splits[3]:splits[4]], wb[:, splits[4]:]]
    tm = TOKEN_TILE
    full = lambda a: pl.BlockSpec(a.shape, lambda i: (0,) * a.ndim)
    row = lambda n: pl.BlockSpec((tm, n), lambda i: (i, 0))
    mod = pl.BlockSpec((1, 1, d), lambda i: (i // tiles_per_b, 0, 0))
    gb2 = gate_b.reshape(1, kw)
    widths = [cc2, kw, kw, vw, vw, kw]
    return pl.pallas_call(
        _ab_in_kernel,
        out_shape=[jax.ShapeDtypeStruct((t, n), BF16 if idx in (0, 3, 4) else F32) for idx, n in enumerate(widths)],
        grid=(t // tm,),
        in_specs=[row(d), mod, mod] + [full(w) for w in ws] + [full(gate_w), full(gb2)],
        out_specs=[row(n) for n in widths],
        compiler_params=_cparams(("parallel",)),
        name="ab_in",
    )(x2, sc, sh, *ws, gate_w, gb2)


def _conv_kernel(u_ref, halo_ref, cw_ref, cb_ref, lg_ref, lb_ref, o_ref, hp_ref, wb_ref, *, ts, cc):
    j = pl.program_id(1)

    def glu(u):
        u = u.astype(F32)
        return u[:, :cc] * _sigmoid(u[:, cc:])

    hp_ref[0, 0:CONV_HALO, :] = jnp.where(j > 0, glu(halo_ref[0]), 0.0)
    hp_ref[0, CONV_HALO:CONV_HALO + ts, :] = glu(u_ref[0])
    nrow = CONV_HALO + ts
    hp0 = hp_ref[0]
    for b in range(1, SUBLANES):
        hp_ref[b] = pltpu.roll(hp0, nrow - b, axis=0)
    for tap in range(CONV_WIDTH):
        wb_ref[tap] = jnp.broadcast_to(cw_ref[tap:tap + 1, :], (SUBLANES, cc))
    shift = CONV_HALO - (CONV_WIDTH - 1)
    for rb in range(ts // CONV_ROWS):
        r0 = rb * CONV_ROWS
        acc = jnp.zeros((CONV_ROWS, cc), F32)
        for tap in range(CONV_WIDTH):
            lo = r0 + shift + tap
            base = lo // SUBLANES * SUBLANES
            w_rows = jnp.concatenate([wb_ref[tap]] * (CONV_ROWS // SUBLANES), axis=0)
            acc = acc + w_rows * hp_ref[lo - base, base:base + CONV_ROWS, :]
        y = _layer_norm(acc + cb_ref[...], lg_ref[...], lb_ref[...])
        o_ref[0, r0:r0 + CONV_ROWS, :] = (y * _sigmoid(y)).astype(o_ref.dtype)


def _conv(u3, conv_w, conv_b, ln_g, ln_b):
    bsz, s, cc2 = u3.shape
    cc = cc2 // 2
    ts = TOKEN_TILE
    hb = ts // CONV_HALO
    vec = lambda a: pl.BlockSpec((1, cc), lambda b, j: (0, 0))
    return pl.pallas_call(
        functools.partial(_conv_kernel, ts=ts, cc=cc),
        out_shape=jax.ShapeDtypeStruct((bsz, s, cc), BF16),
        grid=(bsz, s // ts),
        in_specs=[pl.BlockSpec((1, ts, cc2), lambda b, j: (b, j, 0)),
                  pl.BlockSpec((1, CONV_HALO, cc2), lambda b, j: (b, jnp.maximum(j * hb - 1, 0), 0)),
                  pl.BlockSpec((CONV_WIDTH, cc), lambda b, j: (0, 0)),
                  vec(conv_b), vec(ln_g), vec(ln_b)],
        out_specs=pl.BlockSpec((1, ts, cc), lambda b, j: (b, j, 0)),
        scratch_shapes=[pltpu.VMEM((SUBLANES, CONV_HALO + ts, cc), F32), pltpu.VMEM((CONV_WIDTH, SUBLANES, cc), F32)],
        compiler_params=_cparams(("parallel", "parallel")),
        name="conv",
    )(u3, u3, conv_w, conv_b.reshape(1, cc), ln_g.reshape(1, cc), ln_b.reshape(1, cc))


def _gla_kernel(q_ref, k_ref, v_ref, gl_ref, r_ref, ng_ref, o_ref, st_ref, *, nb, nh, dk, dv, gc, nchunks):
    @pl.when(pl.program_id(1) == 0)
    def _():
        st_ref[...] = jnp.zeros_like(st_ref)

    row = lax.broadcasted_iota(I32, (gc, gc), 0)
    col = lax.broadcasted_iota(I32, (gc, gc), 1)
    causal = col <= row
    tri = jnp.where(causal, 1.0, 0.0).astype(BF16)
    scale = dk ** -0.5

    ks = [slice(h * dk, (h + 1) * dk) for h in range(nh)]
    vs = [slice(h * dv, (h + 1) * dv) for h in range(nh)]
    streams = [(bb, h) for bb in range(nb) for h in range(nh)]

    def chunk(c, carry):
        r0 = pl.multiple_of(c * gc, gc)
        rows = pl.ds(r0, gc)
        bs = [_dot_01_f32(tri, gl_ref[bb, rows, :]) for bb in range(nb)]
        q_in, k_in, q_st, k_st, decay, v = [], [], [], [], [], []
        for bb in range(nb):
            b = bs[bb]
            b_last = b[gc - 1:gc, :]
            mid = 0.5 * b_last
            q = q_ref[bb, rows, :] * scale
            k = k_ref[bb, rows, :]
            v.append(v_ref[bb, rows, :].astype(BF16))
            q_in.append((q * jnp.exp(b - mid)).astype(BF16))
            k_in.append((k * jnp.exp(mid - b)).astype(BF16))
            q_st.append((q * jnp.exp(b)).astype(BF16))
            k_st.append((k * jnp.exp(b_last - b)).astype(BF16))
            decay.append(jnp.exp(b_last))
        sts = [st_ref[bb, h] for bb, h in streams]
        scores = [_dot_nt(q_in[bb][:, ks[h]], k_in[bb][:, ks[h]]) for bb, h in streams]
        inter = [_dot_nt(q_st[bb][:, ks[h]], st.astype(BF16)) for (bb, h), st in zip(streams, sts)]
        update = [_dot_tn(v[bb][:, vs[h]], k_st[bb][:, ks[h]]) for bb, h in streams]
        atts = [jnp.where(causal, sc, 0.0).astype(BF16) for sc in scores]
        outs = [_dot(att, v[bb][:, vs[h]]) + it for (bb, h), att, it in zip(streams, atts, inter)]
        for (bb, h), st, up, o in zip(streams, sts, update, outs):
            st_ref[bb, h] = st * decay[bb][:, ks[h]] + up
            r = r_ref[bb, rows, vs[h]].astype(F32)
            o = o * lax.rsqrt(jnp.mean(o * o, axis=-1, keepdims=True) + RMS_EPS) * ng_ref[:, vs[h]]
            o_ref[bb, rows, vs[h]] = (o * (r * _sigmoid(r))).astype(o_ref.dtype)
        return carry

    lax.fori_loop(0, nchunks, chunk, 0)


def _gla(q3, k3, v3, gl3, r3, norm_g):
    bsz, s, kw = q3.shape
    vw = v3.shape[-1]
    nh = GLA_HEADS
    dk, dv = kw // nh, vw // nh
    cb = GLA_BLOCK
    gc = GLA_CHUNK
    nb = GLA_BATCH_PER_STEP
    blk = lambda n: pl.BlockSpec((nb, cb, n), lambda b, j: (b, j, 0))
    return pl.pallas_call(
        functools.partial(_gla_kernel, nb=nb, nh=nh, dk=dk, dv=dv, gc=gc, nchunks=cb // gc),
        out_shape=jax.ShapeDtypeStruct((bsz, s, vw), BF16),
        grid=(bsz // nb, s // cb),
        in_specs=[blk(kw), blk(kw), blk(vw), blk(kw), blk(vw), pl.BlockSpec((1, vw), lambda b, j: (0, 0))],
        out_specs=blk(vw),
        scratch_shapes=[pltpu.VMEM((nb, nh, dv, dk), F32)],
        compiler_params=_cparams(("parallel", "arbitrary")),
        name="gla",
    )(q3, k3, v3, gl3, r3, norm_g.reshape(1, vw))


def _router_weights(w_group, b_group, w_router, b_router):
    d = w_router.shape[0]
    fill = LANES - MOE_EXPERTS - MOE_GROUPS
    wcat = jnp.concatenate([w_router, w_group, jnp.zeros((d, fill), F32)], axis=1)
    w_hi = wcat.astype(BF16)
    w_mid = (wcat - w_hi.astype(F32)).astype(BF16)
    w_lo = (wcat - w_hi.astype(F32) - w_mid.astype(F32)).astype(BF16)
    bcat = jnp.concatenate([b_router, b_group, jnp.zeros((fill,), F32)]).reshape(1, LANES)
    return jnp.concatenate([w_hi, w_mid, w_lo], axis=1), bcat


def _proj_ln_kernel(*refs, n_in, alpha):
    a_refs, w_refs = refs[:n_in], refs[n_in:2 * n_in]
    x_ref, gate_ref, lg_ref, lb_ref, sc_ref, sh_ref, wr_ref, br_ref, o_ref, lo_ref = refs[2 * n_in:]
    mix = _dot(a_refs[0][...], w_refs[0][...])
    for a_ref, w_ref in zip(a_refs[1:], w_refs[1:]):
        mix = mix + _dot(a_ref[...], w_ref[...])
    y = alpha * x_ref[...] + (1.0 + gate_ref[0]) * mix
    x1 = _layer_norm(y, lg_ref[...], lb_ref[...])
    o_ref[...] = x1
    h_hi, h_mid, h_lo = _split3(x1 * (1.0 + sc_ref[0]) + sh_ref[0])
    pa = _dot(h_hi, wr_ref[...])
    pb = _dot(h_mid, wr_ref[:, :2 * LANES])
    pc = _dot(h_lo, wr_ref[:, :LANES])
    small = (pa[:, 2 * LANES:] + pc) + pb[:, LANES:]
    lo_ref[...] = pa[:, :LANES] + ((pa[:, LANES:2 * LANES] + pb[:, :LANES]) + small) + br_ref[...]


def _proj_ln(acts, weights, x2, gate, ln_g, ln_b, sc_f, sh_f, w_route, b_route, tiles_per_b, alpha):
    t, d = x2.shape
    tm = TOKEN_TILE
    n_in = len(acts)
    row = lambda n: pl.BlockSpec((tm, n), lambda i: (i, 0))
    full = lambda a: pl.BlockSpec(a.shape, lambda i: (0,) * a.ndim)
    vec = pl.BlockSpec((1, d), lambda i: (0, 0))
    mod = pl.BlockSpec((1, 1, d), lambda i: (i // tiles_per_b, 0, 0))
    return pl.pallas_call(
        functools.partial(_proj_ln_kernel, n_in=n_in, alpha=alpha),
        out_shape=[jax.ShapeDtypeStruct((t, d), F32), jax.ShapeDtypeStruct((t, LANES), F32)],
        grid=(t // tm,),
        in_specs=[row(a.shape[1]) for a in acts] + [full(w) for w in weights]
                 + [row(d), mod, vec, vec, mod, mod, full(w_route), full(b_route)],
        out_specs=[row(d), row(LANES)],
        compiler_params=_cparams(("parallel",)),
        name="proj_ln",
    )(*acts, *weights, x2, gate, ln_g.reshape(1, d), ln_b.reshape(1, d), sc_f, sh_f, w_route, b_route)


def _mla_in_kernel(x_ref, sc_ref, sh_ref, pos_ref, invf_ref, sign_ref, win_ref, gq_ref, gkv_ref,
                   wqa_ref, wqb_ref, wk_ref, wvt_ref, q_ref, k_ref, vt_ref, *, nh, q_lora, kv_lora, scale, tk):
    h = (x_ref[...] * (1.0 + sc_ref[0]) + sh_ref[0]).astype(BF16)
    u = _dot(h, win_ref[...])
    cq = u[:, :q_lora]
    ckv = u[:, q_lora:q_lora + kv_lora]
    kr = u[:, q_lora + kv_lora:q_lora + kv_lora + LANES]
    kr_sw = u[:, q_lora + kv_lora + LANES:]
    cqn = (cq * lax.rsqrt(jnp.mean(cq * cq, axis=-1, keepdims=True) + RMS_EPS) * gq_ref[...]).astype(BF16)
    kvn = (ckv * lax.rsqrt(jnp.mean(ckv * ckv, axis=-1, keepdims=True) + RMS_EPS) * gkv_ref[...]).astype(BF16)
    ang = pos_ref[...].astype(F32) * invf_ref[...]
    cos = jnp.cos(ang)
    sin = jnp.sin(ang) * sign_ref[...]
    kr_rot = (kr * cos + kr_sw * sin).astype(BF16)
    qa = _dot(cqn, wqa_ref[...])
    qb = _dot(cqn, wqb_ref[...])
    kv = _dot(kvn, wk_ref[...])
    hw = 2 * LANES
    for hd in range(nh):
        q_ref[:, hd * hw:hd * hw + LANES] = (qa[:, hd * LANES:(hd + 1) * LANES] * scale).astype(BF16)
        pair = slice(nh * LANES + (hd // 2) * LANES, nh * LANES + (hd // 2 + 1) * LANES)
        plain, swapped = qa[:, pair], qb[:, (hd // 2) * LANES:(hd // 2 + 1) * LANES]
        if hd % 2:
            plain, swapped = pltpu.roll(plain, LANES // 2, axis=1), pltpu.roll(swapped, LANES // 2, axis=1)
        rope = plain * cos + swapped * sin
        q_ref[:, hd * hw + LANES:(hd + 1) * hw] = (rope * scale).astype(BF16)
        k_ref[:, hd * hw:hd * hw + LANES] = kv[:, hd * LANES:(hd + 1) * LANES].astype(BF16)
        k_ref[:, hd * hw + LANES:(hd + 1) * hw] = kr_rot
    vt = _dot_nt(wvt_ref[...], kvn).astype(BF16)
    for c in range(vt.shape[1] // tk):
        vt_ref[0, c] = vt[:, c * tk:(c + 1) * tk]


def _mla_in(x2, sc, sh, pos2, w_in, gq, gkv, w_uq, w_ukv, tiles_per_b):
    t, d = x2.shape
    nh = MLA_HEADS
    q_lora, kv_lora = gq.shape[0], gkv.shape[0]
    half = MLA_ROPE // 2
    pad = LANES - MLA_ROPE
    kr_w = w_in[:, q_lora + kv_lora:]
    kr_sw = jnp.concatenate([kr_w[:, half:], kr_w[:, :half]], axis=1)
    zpad = jnp.zeros((d, pad), w_in.dtype)
    win_ext = jnp.concatenate([w_in[:, :q_lora + kv_lora], kr_w, zpad, kr_sw, zpad], axis=1).astype(BF16)
    wq = w_uq.reshape(q_lora, nh, MLA_NOPE + MLA_ROPE)
    q_nope, q_rope = wq[:, :, :MLA_NOPE], wq[:, :, MLA_NOPE:]
    q_rope_sw = jnp.concatenate([q_rope[:, :, half:], q_rope[:, :, :half]], axis=2)
    wqa = jnp.concatenate([q_nope.reshape(q_lora, nh * MLA_NOPE), q_rope.reshape(q_lora, nh * MLA_ROPE)],
                          axis=1).astype(BF16)
    wqb = q_rope_sw.reshape(q_lora, nh * MLA_ROPE).astype(BF16)
    wkv = w_ukv.reshape(kv_lora, nh, MLA_NOPE + MLA_V)
    wk = wkv[:, :, :MLA_NOPE].reshape(kv_lora, nh * MLA_NOPE).astype(BF16)
    wvt = wkv[:, :, MLA_NOPE:].reshape(kv_lora, nh * MLA_V).T.astype(BF16)
    inv_freq = 1.0 / (ROPE_THETA ** (jnp.arange(0, MLA_ROPE, 2, dtype=F32) / MLA_ROPE))
    invf = jnp.concatenate([inv_freq, inv_freq, jnp.zeros((pad,), F32)]).reshape(1, LANES)
    sign = jnp.concatenate([-jnp.ones((half,), F32), jnp.ones((half,), F32), jnp.zeros((pad,), F32)]).reshape(1, LANES)
    tm = TOKEN_TILE
    full = lambda a: pl.BlockSpec(a.shape, lambda i: (0,) * a.ndim)
    row = lambda n: pl.BlockSpec((tm, n), lambda i: (i, 0))
    mod = pl.BlockSpec((1, 1, d), lambda i: (i // tiles_per_b, 0, 0))
    gq2, gkv2 = gq.reshape(1, q_lora), gkv.reshape(1, kv_lora)
    scale = (MLA_NOPE + MLA_ROPE) ** -0.5 * 1.4426950408889634
    tk = ATTN_TK
    kt_per_tile = tm // tk
    s = tiles_per_b * tm
    return pl.pallas_call(
        functools.partial(_mla_in_kernel, nh=nh, q_lora=q_lora, kv_lora=kv_lora, scale=scale, tk=tk),
        out_shape=[jax.ShapeDtypeStruct((t, nh * 2 * LANES), BF16), jax.ShapeDtypeStruct((t, nh * 2 * LANES), BF16),
                   jax.ShapeDtypeStruct((t // s, s // tk, nh * MLA_V, tk), BF16)],
        grid=(t // tm,),
        in_specs=[row(d), mod, mod, row(1), full(invf), full(sign), full(win_ext), full(gq2), full(gkv2),
                  full(wqa), full(wqb), full(wk), full(wvt)],
        out_specs=[row(nh * 2 * LANES), row(nh * 2 * LANES),
                   pl.BlockSpec((1, kt_per_tile, nh * MLA_V, tk),
                                lambda i: (i // tiles_per_b, i % tiles_per_b, 0, 0))],
        compiler_params=_cparams(("parallel",)),
        name="mla_in",
    )(x2, sc, sh, pos2, invf, sign, win_ext, gq2, gkv2, wqa, wqb, wk, wvt)


def _attn_kernel(q_ref, k_ref, vt_ref, o_ref, acc_ref, s0_ref, s1_ref, *, s, tq, tk, hb):
    kpq = tq // tk
    assert kpq == 2, "the pipeline below alternates two score buffers over pairs of key tiles"
    hw = 2 * LANES
    ones = jnp.ones((8, tk), BF16)

    def q_block(qi, carry):
        qrows = pl.ds(pl.multiple_of(qi * tq, tq), tq)
        acc_ref[...] = jnp.zeros_like(acc_ref)

        def scores(j, s_ref, c0=0, nc=tq, block=qi):
            krows = pl.ds(pl.multiple_of(j * tk, tk), tk)
            cols = pl.ds(pl.multiple_of(block * tq + c0, tk), nc)
            for h in range(hb):
                s_ref[h, :, c0:c0 + nc] = _dot_nt(k_ref[krows, h * hw:(h + 1) * hw], q_ref[cols, h * hw:(h + 1) * hw])

        def tile(j, s_ref, stats, masked, c0=0, nc=tq):
            def put(full, part):
                pieces = ([full[:, :c0]] if c0 else []) + [part] + ([full[:, c0 + nc:]] if c0 + nc < tq else [])
                return pieces[0] if len(pieces) == 1 else jnp.concatenate(pieces, axis=1)
            ps, alphas, out = [], [], []
            for h in range(hb):
                m, st = stats[2 * h][:, c0:c0 + nc], s_ref[h, :, c0:c0 + nc]
                if masked:
                    key = j * tk + lax.broadcasted_iota(I32, (tk, nc), 0)
                    qry = qi * tq + c0 + lax.broadcasted_iota(I32, (tk, nc), 1)
                    st = jnp.where(key <= qry, st, -jnp.inf)
                m_new = jnp.maximum(m, jnp.max(st, axis=0, keepdims=True))
                ps.append(jnp.exp2(st - m_new).astype(BF16))
                alphas.append(jnp.exp2(m - m_new))
                out.append(put(stats[2 * h], m_new))
            for h in range(hb):
                acc_ref[h, :, c0:c0 + nc] = (alphas[h] * acc_ref[h, :, c0:c0 + nc]
                                             + _dot(vt_ref[0, j, h * MLA_V:(h + 1) * MLA_V, :], ps[h]))
                l_new = alphas[h] * stats[2 * h + 1][:, c0:c0 + nc] + _dot(ones, ps[h])[0:1]
                out.insert(2 * h + 1, put(stats[2 * h + 1], l_new))
            return tuple(out)

        stats = (jnp.full((1, tq), -jnp.inf, F32), jnp.zeros((1, tq), F32)) * hb

        def pair(jj, c):
            j = 2 * jj
            scores(j + 1, s1_ref)
            c = tile(j, s0_ref, c, False)
            scores(j + 2, s0_ref)
            return tile(j + 1, s1_ref, c, False)

        stats = lax.fori_loop(0, qi, pair, stats)
        scores(2 * qi + 1, s1_ref, tk, tk)
        stats = tile(2 * qi, s0_ref, stats, True, 0, tk)
        stats = tile(2 * qi, s0_ref, stats, False, tk, tk)
        stats = tile(2 * qi + 1, s1_ref, stats, True, tk, tk)
        scores(0, s0_ref, block=jnp.minimum(qi + 1, nq - 1))
        for h in range(hb):
            o_ref[qrows, h * MLA_V:(h + 1) * MLA_V] = (acc_ref[h] / stats[2 * h + 1]).T.astype(o_ref.dtype)
        return carry

    nq = s // tq
    for h in range(hb):
        s0_ref[h] = _dot_nt(k_ref[0:tk, h * hw:(h + 1) * hw], q_ref[0:tq, h * hw:(h + 1) * hw])
    lax.fori_loop(0, nq, q_block, 0)


def _attn(q, k, vt, bsz, s):
    nh = MLA_HEADS
    tq, tk = ATTN_TQ, ATTN_TK
    hb = ATTN_HEADS_PER_STEP
    hw = 2 * LANES
    return pl.pallas_call(
        functools.partial(_attn_kernel, s=s, tq=tq, tk=tk, hb=hb),
        out_shape=jax.ShapeDtypeStruct((bsz * s, nh * MLA_V), BF16),
        grid=(bsz, nh // hb),
        in_specs=[pl.BlockSpec((s, hb * hw), lambda b, h: (b, h)),
                  pl.BlockSpec((s, hb * hw), lambda b, h: (b, h)),
                  pl.BlockSpec((1, s // tk, hb * MLA_V, tk), lambda b, h: (b, 0, h, 0))],
        out_specs=pl.BlockSpec((s, hb * MLA_V), lambda b, h: (b, h)),
        scratch_shapes=[pltpu.VMEM((hb, MLA_V, tq), F32), pltpu.VMEM((hb, tk, tq), F32),
                        pltpu.VMEM((hb, tk, tq), F32)],
        compiler_params=_cparams(("parallel", "parallel")),
        name="attn",
    )(q, k, vt)


def _route_kernel(logits_ref, lp_ref, wt_ref, tm_ref, offs_ref, te_ref, na_ref,
                  upper_ref, carry_ref, *, tm, ne, ng, row_tile, nt_pad):
    i = pl.program_id(0)
    epg = ne // ng

    @pl.when(i == 0)
    def _():
        r = lax.broadcasted_iota(I32, (tm, tm), 0)
        c = lax.broadcasted_iota(I32, (tm, tm), 1)
        upper_ref[...] = jnp.where(r < c, 1.0, 0.0).astype(BF16)
        carry_ref[...] = jnp.zeros_like(carry_ref)

    lt = logits_ref[...].T
    lr = lt[0:ne]
    grow = lax.broadcasted_iota(I32, (8, tm), 0).astype(F32)
    lg = jnp.where(grow < ng, lt[ne:ne + 8], -jnp.inf)
    gmax = jnp.max(lg, axis=0, keepdims=True)
    g_idx = jnp.min(jnp.where(lg == gmax, grow, 1e9), axis=0, keepdims=True)
    g_w = 1.0 / jnp.sum(jnp.exp(lg - gmax), axis=0, keepdims=True)
    erow = lax.broadcasted_iota(I32, (ne, tm), 0).astype(F32)
    in_group = jnp.floor(erow * (1.0 / epg)) == g_idx
    sel = jnp.where(in_group, lr, -jnp.inf)
    v1 = jnp.max(sel, axis=0, keepdims=True)
    i1 = jnp.min(jnp.where(sel == v1, erow, 1e9), axis=0, keepdims=True)
    sel2 = jnp.where(erow == i1, -jnp.inf, sel)
    v2 = jnp.max(sel2, axis=0, keepdims=True)
    i2 = jnp.min(jnp.where(sel2 == v2, erow, 1e9), axis=0, keepdims=True)
    t = jnp.exp(v2 - v1)
    w1 = g_w / (1.0 + t)
    w2 = g_w * t / (1.0 + t)
    oh1 = erow == i1
    oh2 = erow == i2
    member = jnp.where(oh1 | oh2, 1.0, 0.0)
    lcnt = jnp.sum(member, axis=1, keepdims=True)
    lcnt = jnp.floor((lcnt + (RUN_ALIGN - 1)) * (1.0 / RUN_ALIGN)) * RUN_ALIGN
    er_ = lax.broadcasted_iota(I32, (ne, ne), 0)
    ec_ = lax.broadcasted_iota(I32, (ne, ne), 1)
    lstart = jnp.dot(jnp.where(ec_ < er_, 1.0, 0.0).astype(F32), jnp.broadcast_to(lcnt, (ne, LANES)),
                     precision=HIGHEST, preferred_element_type=F32)[:, 0:1]
    lrank = _dot(member.astype(BF16), upper_ref[...]) + lstart
    p1 = jnp.sum(jnp.where(oh1, lrank, 0.0), axis=0, keepdims=True)
    p2 = jnp.sum(jnp.where(oh2, lrank, 0.0), axis=0, keepdims=True)

    orow = lax.broadcasted_iota(I32, (8, tm), 0)
    lp_ref[...] = jnp.where(orow == 0, p1, jnp.where(orow == 1, p2, 0.0)).astype(I32)
    wrow = lax.broadcasted_iota(I32, (LANES, tm), 0)
    wt_ref[...] = jnp.where(wrow == 0, w1, jnp.where(wrow == 1, w2,
                                                     jnp.where(wrow == 2, p1, jnp.where(wrow == 3, p2, 0.0)))).T
    mr = lax.broadcasted_iota(I32, (ne, LANES), 0)
    mc = lax.broadcasted_iota(I32, (ne, LANES), 1)
    to_row = lambda col, lane0: jnp.sum(jnp.where(mr + lane0 == mc, col, 0.0), axis=0, keepdims=True)
    total = jnp.sum(lcnt, axis=0, keepdims=True)
    lane = lax.broadcasted_iota(I32, (1, LANES), 1)
    packed = (to_row(lstart, 0) + to_row(lcnt, ne) + to_row(carry_ref[...], 2 * ne)
              + jnp.where(lane == 3 * ne, total, 0.0))
    trow = lax.broadcasted_iota(I32, (8, LANES), 0)
    tm_ref[...] = jnp.where(trow == 0, packed, 0.0).astype(I32)
    carry_ref[...] = carry_ref[...] + lcnt

    @pl.when(i == pl.num_programs(0) - 1)
    def _():
        cnt = carry_ref[...]
        ntl = jnp.floor((cnt + (row_tile - 1)) * (1.0 / row_tile))
        incl = jnp.where(ec_ <= er_, 1.0, 0.0).astype(F32)
        ends = jnp.dot(incl, jnp.broadcast_to(ntl, (ne, LANES)), precision=HIGHEST,
                       preferred_element_type=F32)
        starts = ends - ntl
        offs_ref[...] = jnp.concatenate([starts * row_tile, ends * row_tile, jnp.broadcast_to(ntl, (ne, LANES)),
                                         jnp.zeros((8, LANES), F32)], axis=0).astype(I32)
        tile = lax.broadcasted_iota(I32, (ne, nt_pad), 1).astype(F32)
        te = jnp.sum(jnp.where(ends[:, 0:1] <= tile, 1.0, 0.0), axis=0, keepdims=True)
        te_ref[...] = jnp.broadcast_to(jnp.minimum(te, ne - 1.0), (8, nt_pad)).astype(I32)
        na_ref[...] = jnp.broadcast_to(ends[ne - 1:ne, :], (8, LANES)).astype(I32)


def _moe_tiles(t):
    rows = 2 * t + (RUN_ALIGN - 1) * MOE_EXPERTS * (t // TOKEN_TILE)
    nt_max = -(-rows // MOE_ROW_TILE) + MOE_EXPERTS
    nt_pad = -(-nt_max // LANES) * LANES
    return nt_max, nt_pad


def _sorted_rows(tm):
    return -(-(2 * tm + (RUN_ALIGN - 1) * MOE_EXPERTS) // LANES) * LANES


def _route(logits):
    t = logits.shape[0]
    ne, ng = MOE_EXPERTS, MOE_GROUPS
    tm = TOKEN_TILE
    _, nt_pad = _moe_tiles(t)
    const = lambda shp: pl.BlockSpec(shp, lambda i: (0,) * len(shp))
    return pl.pallas_call(
        functools.partial(_route_kernel, tm=tm, ne=ne, ng=ng, row_tile=MOE_ROW_TILE, nt_pad=nt_pad),
        out_shape=[jax.ShapeDtypeStruct((8, t), I32), jax.ShapeDtypeStruct((t, LANES), F32),
                   jax.ShapeDtypeStruct((8 * (t // tm), LANES), I32),
                   jax.ShapeDtypeStruct((3 * ne + 8, LANES), I32), jax.ShapeDtypeStruct((8, nt_pad), I32),
                   jax.ShapeDtypeStruct((8, LANES), I32)],
        grid=(t // tm,),
        in_specs=[pl.BlockSpec((tm, LANES), lambda i: (i, 0))],
        out_specs=[pl.BlockSpec((8, tm), lambda i: (0, i)), pl.BlockSpec((tm, LANES), lambda i: (i, 0)),
                   pl.BlockSpec((8, LANES), lambda i: (i, 0)),
                   const((3 * ne + 8, LANES)), const((8, nt_pad)), const((8, LANES))],
        scratch_shapes=[pltpu.VMEM((tm, tm), BF16), pltpu.VMEM((ne, 1), F32)],
        compiler_params=_cparams(("arbitrary",)),
        name="moe_route",
    )(logits)


def _start_runs(tm_ref, tile, ne, copy):
    for e in range(ne):
        lstart = tm_ref[tile, e]
        n = tm_ref[tile, ne + e]
        before = tm_ref[tile, 2 * ne + e]
        _binary_pieces(n, TOKEN_TILE, lambda off, rows: copy(
            e, pl.multiple_of(lstart + off, RUN_ALIGN), pl.multiple_of(before + off, RUN_ALIGN), rows))


def _binary_pieces(n, n_max, copy):
    del n_max
    nbig = n >> (RUN_PIECE.bit_length() - 1)

    def big_piece(c, carry):
        copy(c * RUN_PIECE, RUN_PIECE).start()
        return carry

    lax.fori_loop(0, nbig, big_piece, 0)
    off = nbig * RUN_PIECE
    p = RUN_PIECE // 2
    while p >= RUN_ALIGN:
        @pl.when((n & p) != 0)
        def _():
            copy(off, p).start()
        off = off + (n & p)
        p //= 2


def _dispatch_kernel(offs_ref, gend_ref, ntl_ref, tm_ref, x_ref, sc_ref, sh_ref, lp_ref, wt_ref, xs_ref,
                     h_ref, z_ref, sem_z, sem_r, *, tm, ne, row_tile, nt_max, ntile, srows):
    i = pl.program_id(0)
    tile = i - 1

    @pl.when(i == 0)
    def _():
        z_ref[...] = jnp.zeros_like(z_ref)
        zero_tile = lambda start: pltpu.make_async_copy(z_ref, xs_ref.at[pl.ds(start, row_tile), :], sem_z)
        for act in ("start", "wait"):
            for e in range(ne):
                @pl.when(ntl_ref[e] > 0)
                def _():
                    getattr(zero_tile(pl.multiple_of(gend_ref[e] - row_tile, row_tile)), act)()
            for back in range(1, nt_max - (2 * tm * ntile) // row_tile + 1):
                @pl.when(nt_max - back >= gend_ref[ne - 1] // row_tile)
                def _():
                    getattr(zero_tile((nt_max - back) * row_tile), act)()
            for spill in range(2):
                getattr(zero_tile((nt_max + spill) * row_tile), act)()

    def wait_tile(t):
        slot = t % 2
        pltpu.make_async_copy(h_ref.at[slot], xs_ref.at[pl.ds(0, srows), :], sem_r.at[slot]).wait()

    @pl.when((tile >= 2) & (tile <= ntile))
    def _():
        wait_tile(tile - 2)

    @pl.when(tile == ntile)
    def _():
        wait_tile(tile - 1)

    @pl.when((tile >= 0) & (tile < ntile))
    def _():
        slot = tile % 2
        h = (x_ref[...] * (1.0 + sc_ref[0]) + sh_ref[0]).astype(BF16)
        row = lax.broadcasted_iota(I32, (srows, tm), 0)
        lp = lp_ref[...]
        signed = jnp.where(row == lp[0:1, :], 1.0, jnp.where(row == lp[1:2, :], -1.0, 0.0)).astype(BF16)
        lane = lax.broadcasted_iota(I32, (tm, LANES), 1)
        wt = wt_ref[...]
        terms = []
        for k in range(2):
            w = wt[:, k:k + 1]
            hi = w.astype(BF16).astype(F32)
            mid = (w - hi).astype(BF16).astype(F32)
            terms += [hi, mid, w - hi - mid]
        extra = jnp.where(lane == SIGN_LANE, 1.0, 0.0)
        for idx, term in enumerate(terms):
            extra = jnp.where(lane == idx, term, extra)
        h_ref[slot] = _dot(signed, jnp.concatenate([h, extra.astype(BF16)], axis=1)).astype(BF16)

        def copy(e, local_row, rows_before, rows):
            dst = pl.multiple_of(offs_ref[e] + rows_before, RUN_ALIGN)
            return pltpu.make_async_copy(h_ref.at[slot, pl.ds(local_row, rows), :], xs_ref.at[pl.ds(dst, rows), :],
                                         sem_r.at[slot])

        _start_runs(tm_ref, tile, ne, copy)
        used = tm_ref[tile, 3 * ne]
        _binary_pieces(srows - used, srows - 2 * tm, lambda off, rows: pltpu.make_async_copy(
            h_ref.at[slot, pl.ds(pl.multiple_of(used + off, RUN_ALIGN), rows), :],
            xs_ref.at[pl.ds(pl.multiple_of((nt_max + slot) * row_tile + off, RUN_ALIGN), rows), :], sem_r.at[slot]))


def _dispatch(x2, sc, sh, lp, wt, tmeta, offs, gend, ntl, tiles_per_b):
    t, d = x2.shape
    tm = TOKEN_TILE
    nt_max, _ = _moe_tiles(t)
    ntile = t // tm
    srows = _sorted_rows(tm)
    cur = lambda i: jnp.clip(i - 1, 0, ntile - 1)
    grid_spec = pltpu.PrefetchScalarGridSpec(
        num_scalar_prefetch=4,
        grid=(ntile + 2,),
        in_specs=[pl.BlockSpec((tm, d), lambda i, *_: (cur(i), 0)),
                  pl.BlockSpec((1, 1, d), lambda i, *_: (cur(i) // tiles_per_b, 0, 0)),
                  pl.BlockSpec((1, 1, d), lambda i, *_: (cur(i) // tiles_per_b, 0, 0)),
                  pl.BlockSpec((8, tm), lambda i, *_: (0, cur(i))),
                  pl.BlockSpec((tm, LANES), lambda i, *_: (cur(i), 0))],
        out_specs=pl.BlockSpec(memory_space=pl.ANY),
        scratch_shapes=[pltpu.VMEM((2, srows, d + LANES), BF16), pltpu.VMEM((MOE_ROW_TILE, d + LANES), BF16),
                        pltpu.SemaphoreType.DMA, pltpu.SemaphoreType.DMA((2,))])
    return pl.pallas_call(
        functools.partial(_dispatch_kernel, tm=tm, ne=MOE_EXPERTS, row_tile=MOE_ROW_TILE, nt_max=nt_max,
                          ntile=ntile, srows=srows),
        out_shape=jax.ShapeDtypeStruct(((nt_max + 2) * MOE_ROW_TILE, d + LANES), BF16),
        grid_spec=grid_spec,
        compiler_params=_cparams(("arbitrary",)),
        name="moe_dispatch",
    )(offs, gend, ntl, tmeta, x2, sc, sh, lp, wt)


def _gmm_kernel(te_ref, na_ref, xs_ref, wg_ref, wu_ref, wd_ref, ys_ref, wgu_buf, wd_buf, *, ff, tr):
    j = pl.program_id(0)

    @pl.when(j < na_ref[0])
    def _():
        changed = (j == 0) | (te_ref[j] != te_ref[jnp.maximum(j - 1, 0)])

        @pl.when(changed)
        def _():
            wgu_buf[:, :ff] = wg_ref[0, 0].astype(BF16)
            wgu_buf[:, ff:] = wu_ref[0, 0].astype(BF16)
            wd_buf[...] = wd_ref[0, 0].astype(BF16)

        halves = [slice(c * (tr // 2), (c + 1) * (tr // 2)) for c in range(2)]
        d = wgu_buf.shape[0]
        riders = [xs_ref[rows, d:].astype(F32) for rows in halves]
        signs = [r[:, SIGN_LANE:SIGN_LANE + 1] for r in riders]
        gus = [_dot(xs_ref[rows, :d] * sg.astype(BF16), wgu_buf[...]) for rows, sg in zip(halves, signs)]
        hids = [(gu[:, :ff] * _sigmoid(gu[:, :ff]) * gu[:, ff:]).astype(BF16) for gu in gus]
        for rows, hid, r, sg in zip(halves, hids, riders, signs):
            w = jnp.where(sg > 0.0, r[:, 0:1] + r[:, 1:2] + r[:, 2:3], -(r[:, 3:4] + r[:, 4:5] + r[:, 5:6]))
            ys_ref[rows, :] = (w * _dot(hid, wd_buf[...])).astype(ys_ref.dtype)

    @pl.when(j >= na_ref[0])
    def _():
        ys_ref[...] = jnp.zeros_like(ys_ref)


def _gmm(xs, te, na, w_gate, w_up, w_down, layer):
    d, ff = w_gate.shape[-2:]
    tr = MOE_ROW_TILE
    ns = te.shape[0] * tr
    act = lambda j, te_ref, na_ref: jnp.minimum(j, na_ref[0] - 1)
    grid_spec = pltpu.PrefetchScalarGridSpec(
        num_scalar_prefetch=2,
        grid=(ns // tr,),
        in_specs=[pl.BlockSpec((tr, d + LANES), lambda j, te_ref, na_ref: (act(j, te_ref, na_ref), 0)),
                  pl.BlockSpec((1, 1, d, ff), lambda j, te_ref, na_ref: (layer, te_ref[act(j, te_ref, na_ref)], 0, 0)),
                  pl.BlockSpec((1, 1, d, ff), lambda j, te_ref, na_ref: (layer, te_ref[act(j, te_ref, na_ref)], 0, 0)),
                  pl.BlockSpec((1, 1, ff, d), lambda j, te_ref, na_ref: (layer, te_ref[act(j, te_ref, na_ref)], 0, 0))],
        out_specs=pl.BlockSpec((tr, d), lambda j, te_ref, na_ref: (j, 0)),
        scratch_shapes=[pltpu.VMEM((d, 2 * ff), BF16), pltpu.VMEM((ff, d), BF16)])
    return pl.pallas_call(
        functools.partial(_gmm_kernel, ff=ff, tr=tr),
        out_shape=jax.ShapeDtypeStruct((ns, d), BF16),
        grid_spec=grid_spec,
        compiler_params=_cparams(("arbitrary",)),
        name="moe_gmm",
    )(te, na, xs, w_gate, w_up, w_down)


def _combine_kernel(offs_ref, tm_ref, wt_ref, x_ref, gate_ref, lg_ref, lb_ref, ys_ref, o_ref,
                    buf_ref, sem_r, *, tm, ne, ntile, srows, alpha):
    i = pl.program_id(0)

    def fetch(tile):
        slot = tile % 2

        def copy(e, local_row, rows_before, rows):
            src = pl.multiple_of(offs_ref[e] + rows_before, RUN_ALIGN)
            return pltpu.make_async_copy(ys_ref.at[pl.ds(src, rows), :], buf_ref.at[slot, pl.ds(local_row, rows), :],
                                         sem_r.at[slot])

        _start_runs(tm_ref, tile, ne, copy)
        used = tm_ref[tile, 3 * ne]
        _binary_pieces(srows - used, srows - 2 * tm, lambda off, rows: pltpu.make_async_copy(
            ys_ref.at[pl.ds(pl.multiple_of(off, RUN_ALIGN), rows), :],
            buf_ref.at[slot, pl.ds(pl.multiple_of(used + off, RUN_ALIGN), rows), :], sem_r.at[slot]))

    def wait(tile):
        slot = tile % 2
        pltpu.make_async_copy(ys_ref.at[pl.ds(0, srows), :], buf_ref.at[slot], sem_r.at[slot]).wait()

    def finish(tile):
        w = wt_ref[...]
        rows = buf_ref[tile % 2]
        col = lax.broadcasted_iota(I32, (tm, srows), 1).astype(F32)
        both = jnp.where(col == w[:, 2:3], 1.0, jnp.where(col == w[:, 3:4], 1.0, 0.0)).astype(BF16)
        ffn = _dot(both, rows)
        y = alpha * x_ref[...] + (1.0 + gate_ref[0]) * ffn
        o_ref[...] = _layer_norm(y, lg_ref[...], lb_ref[...])

    @pl.when(i == 0)
    def _():
        fetch(i)

    @pl.when((i > 0) & (i < ntile))
    def _():
        wait(i - 1)
        fetch(i)
        finish(i - 1)

    @pl.when(i == ntile)
    def _():
        wait(i - 1)
        finish(i - 1)


def _combine(ys, tmeta, wt, x2, gate, ln_g, ln_b, offs, tiles_per_b, alpha):
    t, d = x2.shape
    tm = TOKEN_TILE
    ntile = t // tm
    srows = _sorted_rows(tm)
    prev = lambda i: jnp.maximum(i - 1, 0)
    vec = pl.BlockSpec((1, d), lambda i, *_: (0, 0))
    grid_spec = pltpu.PrefetchScalarGridSpec(
        num_scalar_prefetch=2,
        grid=(ntile + 1,),
        in_specs=[pl.BlockSpec((tm, LANES), lambda i, *_: (prev(i), 0)),
                  pl.BlockSpec((tm, d), lambda i, *_: (prev(i), 0)),
                  pl.BlockSpec((1, 1, d), lambda i, *_: (prev(i) // tiles_per_b, 0, 0)),
                  vec, vec,
                  pl.BlockSpec(memory_space=pl.ANY)],
        out_specs=pl.BlockSpec((tm, d), lambda i, *_: (prev(i), 0)),
        scratch_shapes=[pltpu.VMEM((2, srows, d), BF16), pltpu.SemaphoreType.DMA((2,))])
    return pl.pallas_call(
        functools.partial(_combine_kernel, tm=tm, ne=MOE_EXPERTS, ntile=ntile, srows=srows, alpha=alpha),
        out_shape=jax.ShapeDtypeStruct((t, d), F32),
        grid_spec=grid_spec,
        compiler_params=_cparams(("arbitrary",)),
        name="moe_combine",
    )(offs, tmeta, wt, x2, gate, ln_g.reshape(1, d), ln_b.reshape(1, d), ys)


def _moe_block(x2, logits, sc, sh, gate, ln_g, ln_b, w_gate, w_up, w_down, layer, tiles_per_b, alpha):
    ne = MOE_EXPERTS
    nt_max, _ = _moe_tiles(x2.shape[0])
    lp, wt, tmeta, meta, te, na = _route(logits)
    offs, gend, ntl = meta[:ne, 0], meta[ne:2 * ne, 0], meta[2 * ne:3 * ne, 0]
    tmeta = tmeta.reshape(-1, SUBLANES, LANES)[:, 0, :]
    xs = _dispatch(x2, sc, sh, lp, wt, tmeta, offs, gend, ntl, tiles_per_b)
    ys = _gmm(xs, te[0, :nt_max], na[0, :1], w_gate, w_up, w_down, layer)
    return _combine(ys, tmeta, wt, x2, gate, ln_g, ln_b, offs, tiles_per_b, alpha)


def kernel(x, c, positions, ada_w, ada_b, ln_mix_g, ln_mix_b, ln_ffn_g, ln_ffn_b, ab_w_in, conv_w, conv_b, conv_ln_g, conv_ln_b, gla_gate_w, gla_gate_b, gla_norm_g, ab_w_out, mla_w_in, mla_q_norm_g, mla_kv_norm_g, mla_w_uq, mla_w_ukv, mla_w_out, moe_w_group, moe_b_group, moe_w_router, moe_b_router, moe_w_gate, moe_w_up, moe_w_down):
    bsz, s, d = x.shape
    depth = ada_w.shape[0]
    t = bsz * s
    tiles_per_b = s // TOKEN_TILE
    alpha = (2 * depth) ** 0.25
    mod = _ada(c, ada_w, ada_b).reshape(depth, bsz, 6, 1, d)
    x2 = x.reshape(t, d)
    for layer in range(depth):
        sh_m, sc_m, g_m, sh_f, sc_f, g_f = (mod[layer, :, n] for n in range(6))
        i = layer // 2
        if layer % 2 == 0:
            uc, q, k, v, r, gl = _ab_in(x2, sc_m, sh_m, ab_w_in[i], gla_gate_w[i], gla_gate_b[i], tiles_per_b)
            y_a = _conv(uc.reshape(bsz, s, -1), conv_w[i], conv_b[i], conv_ln_g[i], conv_ln_b[i])
            b3 = lambda a: a.reshape(bsz, s, -1)
            y_b = _gla(b3(q), b3(k), b3(v), b3(gl), b3(r), gla_norm_g[i])
            w_out = ab_w_out[i].astype(BF16)
            cc = y_a.shape[-1]
            acts = [y_a.reshape(t, cc), y_b.reshape(t, -1)]
            weights = [w_out[:cc], w_out[cc:]]
        else:
            qc, kc, vv = _mla_in(x2, sc_m, sh_m, positions.reshape(t, 1), mla_w_in[i], mla_q_norm_g[i],
                                 mla_kv_norm_g[i], mla_w_uq[i], mla_w_ukv[i], tiles_per_b)
            acts = [_attn(qc, kc, vv, bsz, s)]
            weights = [mla_w_out[i].astype(BF16)]
        w_route, b_route = _router_weights(moe_w_group[layer], moe_b_group[layer], moe_w_router[layer],
                                           moe_b_router[layer])
        x2, logits = _proj_ln(acts, weights, x2, g_m, ln_mix_g[layer], ln_mix_b[layer], sc_f, sh_f, w_route, b_route,
                              tiles_per_b, alpha)
        x2 = _moe_block(x2, logits, sc_f, sh_f, g_f, ln_ffn_g[layer], ln_ffn_b[layer], moe_w_gate, moe_w_up,
                        moe_w_down, layer, tiles_per_b, alpha)
    return x2.reshape(bsz, s, d)
```
